```python
import math
import jax, jax.numpy as jnp
from jax import lax
import numpy as np

D_MODEL = 1024
BATCH = 4
SEQ = 4096
DEPTH = 2
DEC_BATCH = 16
DEC_SEQ = 4096
PAST_LEN = 128

GRID_W = 64
MIX_W = D_MODEL
GROUP_W = MIX_W // 4
DA_HEADS = 4
DA_DV = GROUP_W // DA_HEADS
DA_DQK = DA_DV // 2
MLA_HEADS = 4
MLA_Q_RANK = 192
MLA_KV_RANK = 128
MLA_NOPE = 64
MLA_ROPE = 32
MLA_DV = GROUP_W // MLA_HEADS
MLA_ROPE_THETA = 10000.0
DIL_HEADS = 4
DIL_DH = GROUP_W // DIL_HEADS
DIL_PATTERNS = ((128, 1), (512, 4), (2048, 16))
DIL_QBLOCK = 64
NA_HEADS = 4
NA_DH = GROUP_W // NA_HEADS
NA_KR = 8
NA_KC = 16
ROPE_THETA = 500000.0
ROPE_FRACTION = 4
N_EXPERTS = 16
EC_FACTOR = 2
D_FF_EXPERT = 2816
DEEPNORM_ALPHA = (2.0 * DEPTH) ** 0.25
DEEPNORM_BETA = (8.0 * DEPTH) ** -0.25
IN_SIZES = (DA_HEADS * 2 * DA_DQK, DA_HEADS * 2 * DA_DQK, DA_HEADS * DA_DV,
            MLA_Q_RANK, MLA_KV_RANK, MLA_ROPE,
            DIL_HEADS * DIL_DH, DIL_HEADS * DIL_DH, DIL_HEADS * DIL_DH,
            NA_HEADS * NA_DH, NA_HEADS * NA_DH, NA_HEADS * NA_DH)
IN_COLS = sum(IN_SIZES)
Q_BLOCK = 128
NEG_INF = -1e30
LN_EPS = 1e-5
RMS_EPS = 1e-6
F32 = jnp.float32

kernel_name = 'hybrid_parallel_head_encoder'


def _split_points(sizes):
    return [int(s) for s in np.cumsum(np.array(sizes))[:-1]]


def _bhtd(a):
    return a.transpose(0, 2, 1, 3)


def _merge(a):
    B, H, T, d = a.shape
    return a.transpose(0, 2, 1, 3).reshape(B, T, H * d)


def layer_norm(x, g, b):
    xf = x.astype(F32)
    mu = jnp.mean(xf, -1, keepdims=True)
    var = jnp.mean(jnp.square(xf - mu), -1, keepdims=True)
    y = (xf - mu) * lax.rsqrt(var + LN_EPS)
    return (y * g.astype(F32) + b.astype(F32)).astype(x.dtype)


def rms_norm(x, g):
    xf = x.astype(F32)
    y = xf * lax.rsqrt(jnp.mean(jnp.square(xf), -1, keepdims=True) + RMS_EPS)
    return (y * g.astype(F32)).astype(x.dtype)


def rope(x, pos, rot_dim, theta):
    half = rot_dim // 2
    inv = theta ** (-jnp.arange(half, dtype=F32) / half)
    ang = pos.astype(F32)[:, None] * inv[None, :]
    cos = jnp.cos(ang)[:, None, :].astype(x.dtype)
    sin = jnp.sin(ang)[:, None, :].astype(x.dtype)
    x1 = x[..., :half]
    x2 = x[..., half:rot_dim]
    return jnp.concatenate([x1 * cos - x2 * sin, x2 * cos + x1 * sin, x[..., rot_dim:]], axis=-1)


def diff_attention(q1, q2, k1, k2, v, lam):
    B, H, T, dk = q1.shape
    dv = v.shape[-1]
    nq = T // Q_BLOCK
    scale = dk ** -0.5
    qs = jnp.stack([q1, q2], 0).reshape(2, B, H, nq, Q_BLOCK, dk).transpose(3, 0, 1, 2, 4, 5)
    ks = jnp.stack([k1, k2], 0)

    def block(qb):
        s = jnp.einsum('cbhqd,cbhkd->cbhqk', qb, ks, preferred_element_type=F32) * scale
        p = jax.nn.softmax(s, axis=-1)
        w = p[0] - lam * p[1]
        return jnp.einsum('bhqk,bhkd->bhqd', w.astype(v.dtype), v)

    o = lax.map(block, qs)
    return o.transpose(1, 2, 0, 3, 4).reshape(B, H, T, dv)


def mla_attention(q_nope, q_rope, k_nope, k_rope, v):
    B, H, T, dn = q_nope.shape
    dr = q_rope.shape[-1]
    dv = v.shape[-1]
    nq = T // Q_BLOCK
    scale = (dn + dr) ** -0.5
    qn = q_nope.reshape(B, H, nq, Q_BLOCK, dn).transpose(2, 0, 1, 3, 4)
    qr = q_rope.reshape(B, H, nq, Q_BLOCK, dr).transpose(2, 0, 1, 3, 4)

    def block(args):
        qn_b, qr_b = args
        s = (jnp.einsum('bhqd,bhkd->bhqk', qn_b, k_nope, preferred_element_type=F32)
             + jnp.einsum('bhqd,bkd->bhqk', qr_b, k_rope, preferred_element_type=F32)) * scale
        p = jax.nn.softmax(s, axis=-1)
        return jnp.einsum('bhqk,bhkd->bhqd', p.astype(v.dtype), v)

    o = lax.map(block, (qn, qr))
    return o.transpose(1, 2, 0, 3, 4).reshape(B, H, T, dv)


def dilated_branch(q, k, v, window, dil):
    B, H, T, dh = q.shape
    half = window // (2 * dil)
    L = T // dil
    qb = math.gcd(L, DIL_QBLOCK)
    nb = L // qb
    span = qb + 2 * half

    def sub(a):
        return a.reshape(B, H, L, dil, dh).transpose(0, 1, 3, 2, 4)

    qs = sub(q).reshape(B, H, dil, nb, qb, dh)
    pad = ((0, 0), (0, 0), (0, 0), (half, half), (0, 0))
    kp = jnp.pad(sub(k), pad)
    vp = jnp.pad(sub(v), pad)
    idx = jnp.arange(nb)[:, None] * qb + jnp.arange(span)[None, :]
    kb = kp[:, :, :, idx]
    vb = vp[:, :, :, idx]
    s = jnp.einsum('bhrnqe,bhrnke->bhrnqk', qs, kb, preferred_element_type=F32) * (dh ** -0.5)
    qpos = jnp.arange(nb)[:, None] * qb + jnp.arange(qb)[None, :]
    kpos = idx - half
    rel = kpos[:, None, :] - qpos[:, :, None]
    valid = (jnp.abs(rel) <= half) & (kpos[:, None, :] >= 0) & (kpos[:, None, :] < L)
    s = jnp.where(valid, s, NEG_INF)
    lse = jax.nn.logsumexp(s, axis=-1)
    p = jnp.exp(s - lse[..., None])
    o = jnp.einsum('bhrnqk,bhrnke->bhrnqe', p.astype(v.dtype), vb)
    o = o.reshape(B, H, dil, L, dh).transpose(0, 1, 3, 2, 4).reshape(B, H, T, dh)
    lse = lse.reshape(B, H, dil, L).transpose(0, 1, 3, 2).reshape(B, H, T)
    return o, lse


def dilated_attention(q, k, v):
    branches = [dilated_branch(q, k, v, w, d) for (w, d) in DIL_PATTERNS]
    outs = jnp.stack([o for (o, _) in branches], 0)
    wts = jax.nn.softmax(jnp.stack([s for (_, s) in branches], 0), axis=0)
    o = jnp.einsum('pbht,pbhtd->bhtd', wts, outs.astype(F32))
    return o.astype(q.dtype)


def neighbourhood_attention(q, k, v, rpb):
    B, H, T, dh = q.shape
    rows = T // GRID_W
    kr = min(NA_KR, rows)
    r = jnp.arange(rows)
    rs = jnp.clip(r - kr // 2, 0, rows - kr)
    krow = rs[:, None] + jnp.arange(kr)[None, :]
    c = jnp.arange(GRID_W)
    cs = jnp.clip(c - NA_KC // 2, 0, GRID_W - NA_KC)
    colmask = (c[None, :] >= cs[:, None]) & (c[None, :] < cs[:, None] + NA_KC)
    qg = q.reshape(B, H, rows, GRID_W, dh)
    kg = k.reshape(B, H, rows, GRID_W, dh)[:, :, krow]
    vg = v.reshape(B, H, rows, GRID_W, dh)[:, :, krow]
    s = jnp.einsum('bhrqd,bhrkcd->bhrqkc', qg, kg, preferred_element_type=F32) * (dh ** -0.5)
    ri = krow - r[:, None] + (NA_KR - 1)
    ci = jnp.clip(c[None, :] - c[:, None] + (NA_KC - 1), 0, 2 * NA_KC - 2)
    bias = rpb[:, ri[:, None, :, None], ci[None, :, None, :]].astype(F32)
    s = jnp.where(colmask[:, None, :], s + bias, NEG_INF)
    p = jax.nn.softmax(s.reshape(B, H, rows, GRID_W, kr * GRID_W), axis=-1)
    p = p.reshape(B, H, rows, GRID_W, kr, GRID_W)
    o = jnp.einsum('bhrqkc,bhrkcd->bhrqd', p.astype(v.dtype), vg)
    return o.reshape(B, H, T, dh)


def token_mixers(h, l, w_in, w_uq, w_ukv, q_norm_g, kv_norm_g, da_lq1, da_lk1, da_lq2, da_lk2,
                 da_subln_g, na_rpb, w_out):
    B, T, _ = h.shape
    pos = jnp.arange(T)
    u = h @ w_in[l]
    (a_q, a_k, a_v, b_cq, b_ckv, b_kr, c_q, c_k, c_v, d_q, d_k, d_v) = jnp.split(
        u, _split_points(IN_SIZES), axis=-1)

    a_rot = DA_DQK // ROPE_FRACTION
    aq = rope(a_q.reshape(B, T, 2 * DA_HEADS, DA_DQK), pos, a_rot, ROPE_THETA).reshape(B, T, DA_HEADS, 2, DA_DQK)
    ak = rope(a_k.reshape(B, T, 2 * DA_HEADS, DA_DQK), pos, a_rot, ROPE_THETA).reshape(B, T, DA_HEADS, 2, DA_DQK)
    av = a_v.reshape(B, T, DA_HEADS, DA_DV)
    lam_init = 0.8 - 0.6 * math.exp(-0.3 * l)
    lam = (jnp.exp(jnp.sum(da_lq1[l].astype(F32) * da_lk1[l].astype(F32)))
           - jnp.exp(jnp.sum(da_lq2[l].astype(F32) * da_lk2[l].astype(F32))) + lam_init)
    oa = diff_attention(_bhtd(aq[:, :, :, 0]), _bhtd(aq[:, :, :, 1]),
                        _bhtd(ak[:, :, :, 0]), _bhtd(ak[:, :, :, 1]), _bhtd(av), lam)
    oa = _merge(rms_norm(oa, da_subln_g[l]) * (1.0 - lam_init))

    qb = (rms_norm(b_cq, q_norm_g[l]) @ w_uq[l]).reshape(B, T, MLA_HEADS, MLA_NOPE + MLA_ROPE)
    qb_nope = qb[..., :MLA_NOPE]
    qb_rope = rope(qb[..., MLA_NOPE:], pos, MLA_ROPE, MLA_ROPE_THETA)
    kv = (rms_norm(b_ckv, kv_norm_g[l]) @ w_ukv[l]).reshape(B, T, MLA_HEADS, MLA_NOPE + MLA_DV)
    kb_nope = kv[..., :MLA_NOPE]
    vb = kv[..., MLA_NOPE:]
    kb_rope = rope(b_kr[:, :, None, :], pos, MLA_ROPE, MLA_ROPE_THETA)[:, :, 0]
    ob = _merge(mla_attention(_bhtd(qb_nope), _bhtd(qb_rope), _bhtd(kb_nope), kb_rope, _bhtd(vb)))

    c_rot = DIL_DH // ROPE_FRACTION
    qc = rope(c_q.reshape(B, T, DIL_HEADS, DIL_DH), pos, c_rot, ROPE_THETA)
    kc = rope(c_k.reshape(B, T, DIL_HEADS, DIL_DH), pos, c_rot, ROPE_THETA)
    vc = c_v.reshape(B, T, DIL_HEADS, DIL_DH)
    oc = _merge(dilated_attention(_bhtd(qc), _bhtd(kc), _bhtd(vc)))

    od = _merge(neighbourhood_attention(_bhtd(d_q.reshape(B, T, NA_HEADS, NA_DH)),
                                        _bhtd(d_k.reshape(B, T, NA_HEADS, NA_DH)),
                                        _bhtd(d_v.reshape(B, T, NA_HEADS, NA_DH)), na_rpb[l]))

    return jnp.concatenate([oa, ob, oc, od], axis=-1) @ w_out[l]


def expert_choice_ffn(h, w_router, w_gate, w_up, w_down):
    B, T, D = h.shape
    xf = h.reshape(B * T, D)
    n = B * T
    cap = EC_FACTOR * n // N_EXPERTS
    aff = jax.nn.softmax((xf @ w_router).astype(F32), axis=-1)
    gates, idx = lax.top_k(aff.T, cap)

    def expert(args):
        ix, gt, wg, wu, wd = args
        xe = xf[ix]
        ye = (jax.nn.silu(xe @ wg) * (xe @ wu)) @ wd
        return ye * gt[:, None].astype(ye.dtype)

    ye = lax.map(expert, (idx, gates, w_gate, w_up, w_down))
    out = jnp.zeros_like(xf).at[idx.reshape(-1)].add(ye.reshape(-1, D).astype(xf.dtype))
    return out.reshape(B, T, D)


def encoder_trunk(x, c, w_in, w_uq, w_ukv, q_norm_g, kv_norm_g, da_lq1, da_lk1, da_lq2, da_lk2,
                  da_subln_g, na_rpb, w_out, w_ada, b_ada, ln1_g, ln1_b, ln2_g, ln2_b,
                  w_router, w_e_gate, w_e_up, w_e_down):
    for l in range(DEPTH):
        mod = jax.nn.silu(c) @ w_ada[l] + b_ada[l]
        sh1, sc1, g1, sh2, sc2, g2 = jnp.split(mod[:, None, :], 6, axis=-1)
        h = x * (1.0 + sc1) + sh1
        m = token_mixers(h, l, w_in, w_uq, w_ukv, q_norm_g, kv_norm_g, da_lq1, da_lk1, da_lq2,
                         da_lk2, da_subln_g, na_rpb, w_out)
        x = layer_norm(DEEPNORM_ALPHA * x + (1.0 + g1) * m, ln1_g[l], ln1_b[l])
        h = x * (1.0 + sc2) + sh2
        f = expert_choice_ffn(h, w_router[l], w_e_gate[l], w_e_up[l], w_e_down[l])
        x = layer_norm(DEEPNORM_ALPHA * x + (1.0 + g2) * f, ln2_g[l], ln2_b[l])
    return x


def setup_inputs(seed: int = 0) -> dict:
    key = jax.random.key(seed)
    ks = jax.random.split(key, 32)

    def nrm(k, shape, scale):
        return jax.random.normal(k, shape, jnp.float32) * scale

    def gain(k, shape):
        return 1.0 + 0.02 * jax.random.normal(k, shape, jnp.float32)

    L = DEPTH
    return {
        'x_prompt': nrm(ks[0], (BATCH, SEQ, D_MODEL), 1.0),
        'x_sample': nrm(ks[1], (DEC_BATCH, DEC_SEQ, D_MODEL), 1.0),
        'c_prompt': nrm(ks[2], (BATCH, D_MODEL), 1.0),
        'c_sample': nrm(ks[3], (DEC_BATCH, D_MODEL), 1.0),
        'w_in': nrm(ks[4], (L, D_MODEL, IN_COLS), D_MODEL ** -0.5),
        'w_uq': nrm(ks[5], (L, MLA_Q_RANK, MLA_HEADS * (MLA_NOPE + MLA_ROPE)), MLA_Q_RANK ** -0.5),
        'w_ukv': nrm(ks[6], (L, MLA_KV_RANK, MLA_HEADS * (MLA_NOPE + MLA_DV)), MLA_KV_RANK ** -0.5),
        'q_norm_g': gain(ks[7], (L, MLA_Q_RANK)),
        'kv_norm_g': gain(ks[8], (L, MLA_KV_RANK)),
        'da_lq1': nrm(ks[9], (L, DA_DQK), 0.1),
        'da_lk1': nrm(ks[10], (L, DA_DQK), 0.1),
        'da_lq2': nrm(ks[11], (L, DA_DQK), 0.1),
        'da_lk2': nrm(ks[12], (L, DA_DQK), 0.1),
        'da_subln_g': gain(ks[13], (L, DA_DV)),
        'na_rpb': nrm(ks[14], (L, NA_HEADS, 2 * NA_KR - 1, 2 * NA_KC - 1), 0.02),
        'w_out': nrm(ks[15], (L, MIX_W, D_MODEL), MIX_W ** -0.5 * DEEPNORM_BETA),
        'w_ada': nrm(ks[16], (L, D_MODEL, 6 * D_MODEL), 0.1 * D_MODEL ** -0.5),
        'b_ada': nrm(ks[17], (L, 6 * D_MODEL), 0.02),
        'ln1_g': gain(ks[18], (L, D_MODEL)),
        'ln1_b': nrm(ks[19], (L, D_MODEL), 0.02),
        'ln2_g': gain(ks[20], (L, D_MODEL)),
        'ln2_b': nrm(ks[21], (L, D_MODEL), 0.02),
        'w_router': nrm(ks[22], (L, D_MODEL, N_EXPERTS), D_MODEL ** -0.5),
        'w_e_gate': nrm(ks[23], (L, N_EXPERTS, D_MODEL, D_FF_EXPERT), D_MODEL ** -0.5),
        'w_e_up': nrm(ks[24], (L, N_EXPERTS, D_MODEL, D_FF_EXPERT), D_MODEL ** -0.5),
        'w_e_down': nrm(ks[25], (L, N_EXPERTS, D_FF_EXPERT, D_MODEL), D_FF_EXPERT ** -0.5 * DEEPNORM_BETA),
    }


def reference(x_prompt, x_sample, c_prompt, c_sample, w_in, w_uq, w_ukv, q_norm_g, kv_norm_g,
              da_lq1, da_lk1, da_lq2, da_lk2, da_subln_g, na_rpb, w_out, w_ada, b_ada,
              ln1_g, ln1_b, ln2_g, ln2_b, w_router, w_e_gate, w_e_up, w_e_down):
    y_prompt = encoder_trunk(x_prompt, c_prompt, w_in, w_uq, w_ukv, q_norm_g, kv_norm_g,
                             da_lq1, da_lk1, da_lq2, da_lk2, da_subln_g, na_rpb, w_out,
                             w_ada, b_ada, ln1_g, ln1_b, ln2_g, ln2_b,
                             w_router, w_e_gate, w_e_up, w_e_down)
    y_sample = encoder_trunk(x_sample, c_sample, w_in, w_uq, w_ukv, q_norm_g, kv_norm_g,
                             da_lq1, da_lk1, da_lq2, da_lk2, da_subln_g, na_rpb, w_out,
                             w_ada, b_ada, ln1_g, ln1_b, ln2_g, ln2_b,
                             w_router, w_e_gate, w_e_up, w_e_down)
    return (y_prompt, y_sample)
```

```python
import functools
import math

import jax
import jax.numpy as jnp
import numpy as np
from jax import lax
from jax.experimental import pallas as pl
from jax.experimental.pallas import tpu as pltpu

F32 = jnp.float32
BF16 = jnp.bfloat16

D_MODEL = 1024
DEPTH = 2
GRID_W = 64
GROUP_W = 256
DA_HEADS, DA_DV, DA_DQK = 4, 64, 32
MLA_HEADS, MLA_Q_RANK, MLA_KV_RANK, MLA_NOPE, MLA_ROPE, MLA_DV = 4, 192, 128, 64, 32, 64
MLA_ROPE_THETA = 10000.0
DIL_HEADS, DIL_DH = 4, 64
DIL_PATTERNS = ((128, 1), (512, 4), (2048, 16))
NA_HEADS, NA_DH, NA_KR, NA_KC = 4, 64, 8, 16
ROPE_THETA = 500000.0
ROPE_FRACTION = 4
N_EXPERTS = 16
EC_FACTOR = 2
D_FF = 2816
DEEPNORM_ALPHA = (2.0 * DEPTH) ** 0.25
NEG_INF = -1e30
LN_EPS = 1e-5
RMS_EPS = 1e-6

LANES = 128
MXU_DIM = 256
VMEM_LIMIT = 56 * 1024 * 1024

S_AQ, S_AK, S_AV, S_BV, S_BQ0, S_BQ1, S_BK0, S_BK1, S_CQ, S_CK, S_CV, S_DQ, S_DK, S_DV = range(14)
N_SLOTS = 14
W_ALL_COLS = 9 * 256 + 256 + 128 + 128


def _cparams(sem):
    return pltpu.CompilerParams(dimension_semantics=sem, vmem_limit_bytes=VMEM_LIMIT)


def _split_bf16(a):
    hi = a.astype(BF16)
    lo = (a - hi.astype(F32)).astype(BF16)
    return hi, lo


def _mod_kernel(c_ref, w_ref, b_ref, o_ref):
    c = c_ref[...]
    a = c * (1.0 / (1.0 + jnp.exp(-c)))
    a_hi, a_lo = _split_bf16(a)
    w_hi, w_lo = _split_bf16(w_ref[...])
    acc = jnp.dot(a_hi, w_hi, preferred_element_type=F32)
    acc += jnp.dot(a_hi, w_lo, preferred_element_type=F32)
    acc += jnp.dot(a_lo, w_hi, preferred_element_type=F32)
    o_ref[...] = acc + b_ref[...]


def _modulation(c, w_ada, b_ada):
    nb, d = c.shape
    n_out = w_ada.shape[1]
    tn = 1536
    return pl.pallas_call(
        _mod_kernel,
        grid=(n_out // tn,),
        in_specs=[pl.BlockSpec((nb, d), lambda j: (0, 0)),
                  pl.BlockSpec((d, tn), lambda j: (0, j)),
                  pl.BlockSpec((1, tn), lambda j: (0, j))],
        out_specs=pl.BlockSpec((nb, tn), lambda j: (0, j)),
        out_shape=jax.ShapeDtypeStruct((nb, n_out), F32),
        compiler_params=_cparams(("arbitrary",)),
        name="adaln_mod",
    )(c, w_ada, b_ada.reshape(1, n_out))


def _rope_tables(t, width, group, rot, theta):
    half = rot // 2
    inv = theta ** (-jnp.arange(half, dtype=F32) / half)
    ang = jnp.arange(t, dtype=F32)[:, None] * inv[None, :]
    cos, sin = jnp.cos(ang), jnp.sin(ang)
    ones = jnp.ones((t, group - rot), F32)
    zeros = jnp.zeros((t, group - rot), F32)
    c_g = jnp.concatenate([cos, cos, ones], axis=1)
    s_g = jnp.concatenate([-sin, sin, zeros], axis=1)
    reps = width // group
    return jnp.tile(c_g, (1, reps)), jnp.tile(s_g, (1, reps))


def _apply_rope(x, c_tab, s_tab, group, rot):
    width = x.shape[-1]
    half = rot // 2
    lane = lax.broadcasted_iota(jnp.int32, (1, width), 1)
    first = (lane % group) < half
    fwd = pltpu.roll(x, width - half, 1)
    bwd = pltpu.roll(x, half, 1)
    return x * c_tab + s_tab * jnp.where(first, fwd, bwd)


def _inproj_kernel(x_ref, sc_ref, sh_ref, w_ref, wuq_ref, wukv_ref, gq_ref, gkv_ref,
                   ca_ref, sa_ref, cc_ref, scc_ref, cm_ref, sm_ref, o_ref):
    h = (x_ref[...] * (1.0 + sc_ref[...]) + sh_ref[...]).astype(BF16)

    def proj(col, width):
        return jnp.dot(h, w_ref[:, col:col + width], preferred_element_type=F32)

    sa_scale = DA_DQK ** -0.5
    sb_scale = (MLA_NOPE + MLA_ROPE) ** -0.5
    sc_scale = DIL_DH ** -0.5
    sd_scale = NA_DH ** -0.5
    a_rot = DA_DQK // ROPE_FRACTION
    c_rot = DIL_DH // ROPE_FRACTION

    ca, sa = ca_ref[...], sa_ref[...]
    o_ref[S_AQ] = (_apply_rope(proj(0, 256), ca, sa, DA_DQK, a_rot) * sa_scale).astype(BF16)
    o_ref[S_AK] = _apply_rope(proj(256, 256), ca, sa, DA_DQK, a_rot).astype(BF16)
    o_ref[S_AV] = proj(512, 256).astype(BF16)
    cc, scc = cc_ref[...], scc_ref[...]
    o_ref[S_CQ] = (_apply_rope(proj(768, 256), cc, scc, DIL_DH, c_rot) * sc_scale).astype(BF16)
    o_ref[S_CK] = _apply_rope(proj(1024, 256), cc, scc, DIL_DH, c_rot).astype(BF16)
    o_ref[S_CV] = proj(1280, 256).astype(BF16)
    o_ref[S_DQ] = (proj(1536, 256) * sd_scale).astype(BF16)
    o_ref[S_DK] = proj(1792, 256).astype(BF16)
    o_ref[S_DV] = proj(2048, 256).astype(BF16)

    cm, sm = cm_ref[...], sm_ref[...]
    cq = proj(2304, 256)
    cq = cq * lax.rsqrt(jnp.sum(cq * cq, -1, keepdims=True) * (1.0 / MLA_Q_RANK) + RMS_EPS)
    cq = (cq * gq_ref[...]).astype(BF16)
    q2 = jnp.dot(cq, wuq_ref[...], preferred_element_type=F32)
    for p in range(2):
        qp = q2[:, 256 * p:256 * (p + 1)]
        o_ref[S_BQ0 + p, :, 0:128] = (qp[:, 0:128] * sb_scale).astype(BF16)
        o_ref[S_BQ0 + p, :, 128:256] = (
            _apply_rope(qp[:, 128:256], cm, sm, MLA_ROPE, MLA_ROPE) * sb_scale).astype(BF16)
    ckv = proj(2560, 128)
    ckv = ckv * lax.rsqrt(jnp.mean(ckv * ckv, -1, keepdims=True) + RMS_EPS)
    ckv = (ckv * gkv_ref[...]).astype(BF16)
    kv = jnp.dot(ckv, wukv_ref[...], preferred_element_type=F32)
    kr = _apply_rope(proj(2688, 128), cm, sm, MLA_ROPE, MLA_ROPE).astype(BF16)
    for p in range(2):
        o_ref[S_BK0 + p, :, 0:128] = kv[:, 128 * p:128 * (p + 1)].astype(BF16)
        o_ref[S_BK0 + p, :, 128:256] = kr
    o_ref[S_BV] = kv[:, 256:512].astype(BF16)


def _inproj(x, sc, sh, w_all, wuq, wukv, gq, gkv, tabs, nb, t):
    tm = 512
    nt = t // tm
    ntok = nb * t
    ca, sa, cc, scc, cm, sm = tabs
    full = lambda shape: pl.BlockSpec(shape, lambda j, b: tuple(0 for _ in shape))
    tab = lambda w: pl.BlockSpec((tm, w), lambda j, b: (j, 0))
    return pl.pallas_call(
        _inproj_kernel,
        grid=(nt, nb),
        in_specs=[pl.BlockSpec((tm, D_MODEL), lambda j, b: (b * nt + j, 0)),
                  pl.BlockSpec((None, 1, D_MODEL), lambda j, b: (b, 0, 0)),
                  pl.BlockSpec((None, 1, D_MODEL), lambda j, b: (b, 0, 0)),
                  full((D_MODEL, W_ALL_COLS)), full((256, 512)), full((128, 512)),
                  full((1, 256)), full((1, 128)),
                  tab(256), tab(256), tab(256), tab(256), tab(128), tab(128)],
        out_specs=pl.BlockSpec((N_SLOTS, tm, 256), lambda j, b: (0, b * nt + j, 0)),
        out_shape=jax.ShapeDtypeStruct((N_SLOTS, ntok, 256), BF16),
        compiler_params=_cparams(("arbitrary", "arbitrary")),
        name="in_proj",
    )(x, sc, sh, w_all, wuq, wukv, gq, gkv, ca, sa, cc, scc, cm, sm)


def _lane_mask(width, ranges):
    lane = lax.broadcasted_iota(jnp.int32, (1, width), 1)
    m = None
    for lo, hi in ranges:
        r = (lane >= lo) & (lane < hi)
        m = r if m is None else (m | r)
    return m


def _online_attention(qm, k_ref, v_ref, t, kc):
    tq = qm.shape[0]

    def body(i, carry):
        m, l, acc = carry
        off = pl.multiple_of(i * kc, kc)
        k = k_ref[pl.ds(off, kc), :]
        s = lax.dot_general(qm, k, (((1,), (1,)), ((), ())), preferred_element_type=F32)
        m_new = jnp.maximum(m, jnp.max(s, -1, keepdims=True))
        alpha = jnp.exp(m - m_new)
        p = jnp.exp(s - m_new)
        l = alpha * l + jnp.sum(p, -1, keepdims=True)
        acc = alpha * acc + jnp.dot(p.astype(BF16), v_ref[pl.ds(off, kc), :],
                                    preferred_element_type=F32)
        return m_new, l, acc

    init = (jnp.full((tq, 1), NEG_INF, F32), jnp.zeros((tq, 1), F32),
            jnp.zeros((tq, v_ref.shape[-1]), F32))
    _, l, acc = lax.fori_loop(0, t // kc, body, init)
    return acc * (1.0 / l)


def _group_mean_sq(x, gmat):
    sq = x * x
    hi, lo = _split_bf16(sq)
    return (jnp.dot(hi, gmat, preferred_element_type=F32)
            + jnp.dot(lo, gmat, preferred_element_type=F32))


def _diff_attn_kernel(lam_ref, q_ref, k_ref, v_ref, g_ref, gmat_ref, o_ref, acc_ref, *, t, kc,
                      out_scale):
    q = q_ref[...]
    lam = lam_ref[0]
    lane = lax.broadcasted_iota(jnp.int32, (1, 256), 1)
    zero = jnp.zeros_like(q)

    def head(h, _):
        o_h = None
        for c in range(2):
            lo = (2 * h + c) * DA_DQK
            qm = jnp.where((lane >= lo) & (lane < lo + DA_DQK), q, zero)
            o_c = _online_attention(qm, k_ref, v_ref, t, kc)
            o_h = o_c if c == 0 else o_h - lam * o_c
        sel = (lane >= h * DA_DV) & (lane < (h + 1) * DA_DV)
        acc_ref[...] = jnp.where(sel, o_h, acc_ref[...])
        return 0

    acc_ref[...] = jnp.zeros_like(acc_ref)
    lax.fori_loop(0, DA_HEADS, head, 0)
    o = acc_ref[...]
    ms = _group_mean_sq(o, gmat_ref[...])
    o_ref[...] = (o * lax.rsqrt(ms + RMS_EPS) * g_ref[...] * out_scale).astype(o_ref.dtype)


def _diff_attention(proj, lam, g_tiled, gmat, nb, t, out_scale):
    tq, kc = 512, 512
    nq = t // tq
    kern = functools.partial(_diff_attn_kernel, t=t, kc=kc, out_scale=out_scale)
    return pl.pallas_call(
        kern,
        grid=(nb, nq),
        in_specs=[pl.BlockSpec(memory_space=pltpu.SMEM),
                  pl.BlockSpec((None, tq, 256), lambda b, i: (S_AQ, b * nq + i, 0)),
                  pl.BlockSpec((None, t, 256), lambda b, i: (S_AK, b, 0)),
                  pl.BlockSpec((None, t, 256), lambda b, i: (S_AV, b, 0)),
                  pl.BlockSpec((1, 256), lambda b, i: (0, 0)),
                  pl.BlockSpec((256, 256), lambda b, i: (0, 0))],
        out_specs=pl.BlockSpec((tq, 256), lambda b, i: (b * nq + i, 0)),
        out_shape=jax.ShapeDtypeStruct((nb * t, 256), BF16),
        scratch_shapes=[pltpu.VMEM((tq, 256), F32)],
        compiler_params=_cparams(("arbitrary", "arbitrary")),
        name="diff_attn",
    )(lam, proj, proj, proj, g_tiled, gmat)


def _mla_attn_kernel(q_ref, k_ref, v_ref, o_ref, acc_ref, *, t, kc):
    lane = lax.broadcasted_iota(jnp.int32, (1, 256), 1)

    def head(h, _):
        p = h // 2
        j = h % 2
        q = q_ref[p]
        nope = (lane >= j * MLA_NOPE) & (lane < (j + 1) * MLA_NOPE)
        rope = (lane >= 128 + j * MLA_ROPE) & (lane < 128 + (j + 1) * MLA_ROPE)
        qm = jnp.where(nope | rope, q, jnp.zeros_like(q))
        o_h = _online_attention(qm, k_ref.at[p], v_ref, t, kc)
        sel = (lane >= h * MLA_DV) & (lane < (h + 1) * MLA_DV)
        acc_ref[...] = jnp.where(sel, o_h, acc_ref[...])
        return 0

    acc_ref[...] = jnp.zeros_like(acc_ref)
    lax.fori_loop(0, MLA_HEADS, head, 0)
    o_ref[...] = acc_ref[...].astype(o_ref.dtype)


def _mla_attention(proj, nb, t):
    tq, kc = 512, 512
    nq = t // tq
    kern = functools.partial(_mla_attn_kernel, t=t, kc=kc)
    return pl.pallas_call(
        kern,
        grid=(nb, nq),
        in_specs=[pl.BlockSpec((2, tq, 256), lambda b, i: (S_BQ0 // 2, b * nq + i, 0)),
                  pl.BlockSpec((2, t, 256), lambda b, i: (S_BK0 // 2, b, 0)),
                  pl.BlockSpec((None, t, 256), lambda b, i: (S_BV, b, 0))],
        out_specs=pl.BlockSpec((tq, 256), lambda b, i: (b * nq + i, 0)),
        out_shape=jax.ShapeDtypeStruct((nb * t, 256), BF16),
        scratch_shapes=[pltpu.VMEM((tq, 256), F32)],
        compiler_params=_cparams(("arbitrary", "arbitrary")),
        name="mla_attn",
    )(proj, proj, proj)


DIL_REACH = max(w // 2 for w, _ in DIL_PATTERNS)


def _dil_attn_kernel(q_ref, k_ref, v_ref, o_ref, bias_ref, acc_ref, *, t, tq, band):
    i = pl.program_id(1)
    t0 = i * tq
    start = jnp.clip(t0 - DIL_REACH, 0, t - band)
    start = pl.multiple_of(start, tq)
    qi = lax.broadcasted_iota(jnp.int32, (tq, band), 0)
    kj = lax.broadcasted_iota(jnp.int32, (tq, band), 1)
    delta = kj - qi + (start - t0)
    ad = jnp.abs(delta)
    cnt = jnp.zeros((tq, band), F32)
    for window, dil in DIL_PATTERNS:
        ok = (ad <= window // 2) & ((delta & (dil - 1)) == 0)
        cnt = cnt + jnp.where(ok, 1.0, 0.0)
    bias_ref[...] = jnp.where(cnt > 2.5, math.log(3.0),
                              jnp.where(cnt > 1.5, math.log(2.0),
                                        jnp.where(cnt > 0.5, 0.0, NEG_INF)))
    q = q_ref[...]
    lane = lax.broadcasted_iota(jnp.int32, (1, 256), 1)
    kb = k_ref.at[pl.ds(start, band), :]
    vb = v_ref.at[pl.ds(start, band), :]

    def head(h, _):
        sel = (lane >= h * DIL_DH) & (lane < (h + 1) * DIL_DH)
        qm = jnp.where(sel, q, jnp.zeros_like(q))
        s = lax.dot_general(qm, kb[...], (((1,), (1,)), ((), ())), preferred_element_type=F32)
        s = s + bias_ref[...]
        m = jnp.max(s, -1, keepdims=True)
        p = jnp.exp(s - m)
        l = jnp.sum(p, -1, keepdims=True)
        o_h = jnp.dot(p.astype(BF16), vb[...], preferred_element_type=F32) * (1.0 / l)
        acc_ref[...] = jnp.where(sel, o_h, acc_ref[...])
        return 0

    acc_ref[...] = jnp.zeros_like(acc_ref)
    lax.fori_loop(0, DIL_HEADS, head, 0)
    o_ref[...] = acc_ref[...].astype(o_ref.dtype)


def _dil_attention(proj, nb, t):
    tq = 256
    band = min(t, tq + 2 * DIL_REACH)
    nq = t // tq
    kern = functools.partial(_dil_attn_kernel, t=t, tq=tq, band=band)
    return pl.pallas_call(
        kern,
        grid=(nb, nq),
        in_specs=[pl.BlockSpec((None, tq, 256), lambda b, i: (S_CQ, b * nq + i, 0)),
                  pl.BlockSpec((None, t, 256), lambda b, i: (S_CK, b, 0)),
                  pl.BlockSpec((None, t, 256), lambda b, i: (S_CV, b, 0))],
        out_specs=pl.BlockSpec((tq, 256), lambda b, i: (b * nq + i, 0)),
        out_shape=jax.ShapeDtypeStruct((nb * t, 256), BF16),
        scratch_shapes=[pltpu.VMEM((tq, band), F32), pltpu.VMEM((tq, 256), F32)],
        compiler_params=_cparams(("arbitrary", "arbitrary")),
        name="dil_attn",
    )(proj, proj, proj)


def _na_bias_table(rpb):
    c = jnp.arange(GRID_W)
    cs = jnp.clip(c - NA_KC // 2, 0, GRID_W - NA_KC)
    colmask = (c[None, :] >= cs[:, None]) & (c[None, :] < cs[:, None] + NA_KC)
    ci = jnp.clip(c[None, :] - c[:, None] + (NA_KC - 1), 0, 2 * NA_KC - 2)
    si = jnp.arange(NA_KR)
    j = jnp.arange(NA_KR)
    ri = si[:, None] + j[None, :]
    b = rpb[:, ri[:, None, :, None], ci[None, :, None, :]].astype(F32)
    b = jnp.where(colmask[None, None, :, None, :], b, NEG_INF)
    return b.reshape(rpb.shape[0], NA_KR, GRID_W, NA_KR * GRID_W)


def _na_attn_kernel(q_ref, k_ref, v_ref, tb_ref, o_ref, *, rows, rg):
    g = pl.program_id(1)
    lane = lax.broadcasted_iota(jnp.int32, (1, 256), 1)
    nk = NA_KR * GRID_W

    def row(r, _):
        grow = g * rg + r
        rs = jnp.clip(grow - NA_KR // 2, 0, rows - NA_KR)
        si = rs - grow + (NA_KR - 1)
        q = q_ref[pl.ds(pl.multiple_of(r * GRID_W, GRID_W), GRID_W), :]
        koff = pl.multiple_of(rs * GRID_W, GRID_W)
        kb = k_ref[pl.ds(koff, nk), :]
        vb = v_ref[pl.ds(koff, nk), :]
        out = jnp.zeros((GRID_W, 256), F32)
        for h in range(NA_HEADS):
            sel = (lane >= h * NA_DH) & (lane < (h + 1) * NA_DH)
            qm = jnp.where(sel, q, jnp.zeros_like(q))
            s = lax.dot_general(qm, kb, (((1,), (1,)), ((), ())), preferred_element_type=F32)
            s = s + tb_ref[h, si]
            m = jnp.max(s, -1, keepdims=True)
            p = jnp.exp(s - m)
            l = jnp.sum(p, -1, keepdims=True)
            o_h = jnp.dot(p.astype(BF16), vb, preferred_element_type=F32) * (1.0 / l)
            out = jnp.where(sel, o_h, out)
        o_ref[pl.ds(pl.multiple_of(r * GRID_W, GRID_W), GRID_W), :] = out.astype(o_ref.dtype)
        return 0

    lax.fori_loop(0, rg, row, 0)


def _na_attention(proj, tb, nb, t):
    rows = t // GRID_W
    assert rows >= NA_KR
    rg = 8
    ng = rows // rg
    tq = rg * GRID_W
    kern = functools.partial(_na_attn_kernel, rows=rows, rg=rg)
    return pl.pallas_call(
        kern,
        grid=(nb, ng),
        in_specs=[pl.BlockSpec((None, tq, 256), lambda b, i: (S_DQ, b * ng + i, 0)),
                  pl.BlockSpec((None, t, 256), lambda b, i: (S_DK, b, 0)),
                  pl.BlockSpec((None, t, 256), lambda b, i: (S_DV, b, 0)),
                  pl.BlockSpec(tb.shape, lambda b, i: (0, 0, 0, 0))],
        out_specs=pl.BlockSpec((tq, 256), lambda b, i: (b * ng + i, 0)),
        out_shape=jax.ShapeDtypeStruct((nb * t, 256), BF16),
        compiler_params=_cparams(("arbitrary", "arbitrary")),
        name="na_attn",
    )(proj, proj, proj, tb)


def _layer_norm(y, g, b):
    mu = jnp.mean(y, -1, keepdims=True)
    yc = y - mu
    var = jnp.mean(yc * yc, -1, keepdims=True)
    return yc * lax.rsqrt(var + LN_EPS) * g + b


def _outproj_kernel(oa_ref, ob_ref, oc_ref, od_ref, x_ref, g1_ref, sc2_ref, sh2_ref, w_ref,
                    lg_ref, lb_ref, wr_ref, x1_ref, h2_ref, aff_ref):
    m = jnp.dot(oa_ref[...], w_ref[0:256, :], preferred_element_type=F32)
    m += jnp.dot(ob_ref[...], w_ref[256:512, :], preferred_element_type=F32)
    m += jnp.dot(oc_ref[...], w_ref[512:768, :], preferred_element_type=F32)
    m += jnp.dot(od_ref[...], w_ref[768:1024, :], preferred_element_type=F32)
    y = DEEPNORM_ALPHA * x_ref[...] + (1.0 + g1_ref[...]) * m
    x1 = _layer_norm(y, lg_ref[...], lb_ref[...])
    x1_ref[...] = x1
    h2 = x1 * (1.0 + sc2_ref[...]) + sh2_ref[...]
    h2_ref[...] = h2.astype(BF16)
    h_hi, h_lo = _split_bf16(h2)
    w_hi, w_lo = _split_bf16(wr_ref[...])
    nt = (((1,), (1,)), ((), ()))
    lg = lax.dot_general(w_hi, h_hi, nt, preferred_element_type=F32)
    lg += lax.dot_general(w_hi, h_lo, nt, preferred_element_type=F32)
    lg += lax.dot_general(w_lo, h_hi, nt, preferred_element_type=F32)
    lg = lg - jnp.max(lg, 0, keepdims=True)
    e = jnp.exp(lg)
    aff_ref[...] = e / jnp.sum(e, 0, keepdims=True)


def _outproj(oa, ob, oc, od, x, g1, sc2, sh2, w_out, ln_g, ln_b, w_router_t, nb, t):
    tm = 512
    nt = t // tm
    ntok = nb * t
    tok = lambda w: pl.BlockSpec((tm, w), lambda i: (i, 0))
    per_b = pl.BlockSpec((None, 1, D_MODEL), lambda i: (i // nt, 0, 0))
    full = lambda shape: pl.BlockSpec(shape, lambda i: tuple(0 for _ in shape))
    return pl.pallas_call(
        _outproj_kernel,
        grid=(ntok // tm,),
        in_specs=[tok(256), tok(256), tok(256), tok(256), tok(D_MODEL), per_b, per_b, per_b,
                  full((D_MODEL, D_MODEL)), full((1, D_MODEL)), full((1, D_MODEL)),
                  full((N_EXPERTS, D_MODEL))],
        out_specs=[tok(D_MODEL), tok(D_MODEL),
                   pl.BlockSpec((N_EXPERTS, tm), lambda i: (0, i))],
        out_shape=[jax.ShapeDtypeStruct((ntok, D_MODEL), F32),
                   jax.ShapeDtypeStruct((ntok, D_MODEL), BF16),
                   jax.ShapeDtypeStruct((N_EXPERTS, ntok), F32)],
        compiler_params=_cparams(("arbitrary",)),
        name="out_proj",
    )(oa, ob, oc, od, x, g1, sc2, sh2, w_out, ln_g, ln_b, w_router_t)


def _expert_kernel(x_ref, gt_ref, wg_ref, wu_ref, wd_ref, o_ref, acc_ref):
    f = pl.program_id(2)
    x = x_ref[...]
    g = jnp.dot(x, wg_ref[...], preferred_element_type=F32)
    u = jnp.dot(x, wu_ref[...], preferred_element_type=F32)
    hmid = (g * (1.0 / (1.0 + jnp.exp(-g))) * u).astype(BF16)
    part = jnp.dot(hmid, wd_ref[...], preferred_element_type=F32)

    @pl.when(f == 0)
    def _():
        acc_ref[...] = part

    @pl.when(f > 0)
    def _():
        acc_ref[...] += part

    @pl.when(f == pl.num_programs(2) - 1)
    def _():
        o_ref[...] = acc_ref[...] * gt_ref[...]


def _experts(xe, gates, wg, wu, wd):
    ne, cap, d = xe.shape
    tm = min(cap, 1024)
    tf = 256
    nf = D_FF // tf
    return pl.pallas_call(
        _expert_kernel,
        grid=(ne, cap // tm, nf),
        in_specs=[pl.BlockSpec((None, tm, d), lambda e, m, f: (e, m, 0)),
                  pl.BlockSpec((None, tm, 1), lambda e, m, f: (e, m, 0)),
                  pl.BlockSpec((None, d, tf), lambda e, m, f: (e, 0, f)),
                  pl.BlockSpec((None, d, tf), lambda e, m, f: (e, 0, f)),
                  pl.BlockSpec((None, tf, d), lambda e, m, f: (e, f, 0))],
        out_specs=pl.BlockSpec((None, tm, d), lambda e, m, f: (e, m, 0)),
        out_shape=jax.ShapeDtypeStruct((ne, cap, d), F32),
        scratch_shapes=[pltpu.VMEM((tm, d), F32)],
        compiler_params=_cparams(("arbitrary", "arbitrary", "arbitrary")),
        name="expert_ffn",
    )(xe, gates, wg, wu, wd)


def _postnorm_kernel(x_ref, f_ref, g2_ref, lg_ref, lb_ref, o_ref):
    y = DEEPNORM_ALPHA * x_ref[...] + (1.0 + g2_ref[...]) * f_ref[...]
    o_ref[...] = _layer_norm(y, lg_ref[...], lb_ref[...])


def _postnorm(x1, f, g2, ln_g, ln_b, nb, t):
    tm = 512
    nt = t // tm
    ntok = nb * t
    tok = pl.BlockSpec((tm, D_MODEL), lambda i: (i, 0))
    per_b = pl.BlockSpec((None, 1, D_MODEL), lambda i: (i // nt, 0, 0))
    vec = pl.BlockSpec((1, D_MODEL), lambda i: (0, 0))
    return pl.pallas_call(
        _postnorm_kernel,
        grid=(ntok // tm,),
        in_specs=[tok, tok, per_b, vec, vec],
        out_specs=tok,
        out_shape=jax.ShapeDtypeStruct((ntok, D_MODEL), F32),
        compiler_params=_cparams(("arbitrary",)),
        name="post_norm",
    )(x1, f, g2, ln_g, ln_b)


def _prep_w_in(w_in_l):
    sizes = (256, 256, 256, MLA_Q_RANK, MLA_KV_RANK, MLA_ROPE, 256, 256, 256, 256, 256, 256)
    offs = np.concatenate([[0], np.cumsum(sizes)])
    part = [w_in_l[:, offs[i]:offs[i + 1]] for i in range(len(sizes))]
    a_q, a_k, a_v, b_cq, b_ckv, b_kr, c_q, c_k, c_v, d_q, d_k, d_v = part
    d = w_in_l.shape[0]
    zeros = lambda n: jnp.zeros((d, n), w_in_l.dtype)
    cols = [a_q, a_k, a_v, c_q, c_k, c_v, d_q, d_k, d_v,
            b_cq, zeros(256 - MLA_Q_RANK), b_ckv, b_kr, b_kr, zeros(128 - 2 * MLA_ROPE)]
    return jnp.concatenate(cols, axis=1).astype(BF16)


def _prep_w_uq(w_uq_l):
    hd = MLA_NOPE + MLA_ROPE
    nope = [w_uq_l[:, h * hd:h * hd + MLA_NOPE] for h in range(MLA_HEADS)]
    rope = [w_uq_l[:, h * hd + MLA_NOPE:(h + 1) * hd] for h in range(MLA_HEADS)]
    z = jnp.zeros((w_uq_l.shape[0], 256 - 2 * hd), w_uq_l.dtype)
    cols = []
    for p in range(2):
        cols += [nope[2 * p], nope[2 * p + 1], rope[2 * p], rope[2 * p + 1], z]
    w = jnp.concatenate(cols, axis=1)
    w = jnp.concatenate([w, jnp.zeros((256 - MLA_Q_RANK, w.shape[1]), w.dtype)], axis=0)
    return w.astype(BF16)


def _prep_w_ukv(w_ukv_l):
    hd = MLA_NOPE + MLA_DV
    kn = [w_ukv_l[:, h * hd:h * hd + MLA_NOPE] for h in range(MLA_HEADS)]
    vv = [w_ukv_l[:, h * hd + MLA_NOPE:(h + 1) * hd] for h in range(MLA_HEADS)]
    return jnp.concatenate(kn + vv, axis=1).astype(BF16)


def _moe_group(aff_t, h2, wg, wu, wd, off, n):
    cap = EC_FACTOR * n // N_EXPERTS
    aff_g = lax.slice_in_dim(aff_t, off, off + n, axis=1)
    gates, idx = lax.top_k(aff_g, cap)
    xe = jnp.take(lax.slice_in_dim(h2, off, off + n, axis=0), idx, axis=0)
    ye = _experts(xe, gates[..., None], wg, wu, wd)
    out = jnp.zeros((n, D_MODEL), F32).at[idx.reshape(-1)].add(ye.reshape(-1, D_MODEL))
    return out


def _trunk(x, c, group_tokens, nb, t, p):
    ntok = nb * t
    tabs = (_rope_tables(t, 256, DA_DQK, DA_DQK // ROPE_FRACTION, ROPE_THETA)
            + _rope_tables(t, 256, DIL_DH, DIL_DH // ROPE_FRACTION, ROPE_THETA)
            + _rope_tables(t, 128, MLA_ROPE, MLA_ROPE, MLA_ROPE_THETA))
    gmat = jnp.asarray(np.kron(np.eye(4), np.full((64, 64), 1.0 / 64)), BF16)
    for l in range(DEPTH):
        mod = _modulation(c, p['w_ada'][l], p['b_ada'][l])
        sh1, sc1, g1, sh2, sc2, g2 = [m.reshape(nb, 1, D_MODEL) for m in jnp.split(mod, 6, axis=-1)]
        gq = jnp.concatenate([p['q_norm_g'][l], jnp.zeros((256 - MLA_Q_RANK,), F32)]).reshape(1, 256)
        gkv = p['kv_norm_g'][l].reshape(1, 128)
        proj = _inproj(x, sc1, sh1, _prep_w_in(p['w_in'][l]), _prep_w_uq(p['w_uq'][l]),
                       _prep_w_ukv(p['w_ukv'][l]), gq, gkv, tabs, nb, t)
        lam_init = 0.8 - 0.6 * math.exp(-0.3 * l)
        lam = (jnp.exp(jnp.sum(p['da_lq1'][l] * p['da_lk1'][l]))
               - jnp.exp(jnp.sum(p['da_lq2'][l] * p['da_lk2'][l])) + lam_init).reshape(1)
        g_sub = jnp.tile(p['da_subln_g'][l], DA_HEADS).reshape(1, 256)
        oa = _diff_attention(proj, lam, g_sub, gmat, nb, t, 1.0 - lam_init)
        ob = _mla_attention(proj, nb, t)
        oc = _dil_attention(proj, nb, t)
        od = _na_attention(proj, _na_bias_table(p['na_rpb'][l]), nb, t)
        x1, h2, aff_t = _outproj(oa, ob, oc, od, x, g1, sc2, sh2, p['w_out'][l].astype(BF16),
                                 p['ln1_g'][l].reshape(1, -1), p['ln1_b'][l].reshape(1, -1),
                                 p['w_router'][l].T, nb, t)
        wg = p['w_e_gate'][l].astype(BF16)
        wu = p['w_e_up'][l].astype(BF16)
        wd = p['w_e_down'][l].astype(BF16)
        outs, off = [], 0
        for n in group_tokens:
            outs.append(_moe_group(aff_t, h2, wg, wu, wd, off, n))
            off += n
        f = jnp.concatenate(outs, axis=0)
        x = _postnorm(x1, f, g2, p['ln2_g'][l].reshape(1, -1), p['ln2_b'][l].reshape(1, -1), nb, t)
    return x


def kernel(x_prompt, x_sample, c_prompt, c_sample, w_in, w_uq, w_ukv, q_norm_g, kv_norm_g, da_lq1,
           da_lk1, da_lq2, da_lk2, da_subln_g, na_rpb, w_out, w_ada, b_ada, ln1_g, ln1_b, ln2_g,
           ln2_b, w_router, w_e_gate, w_e_up, w_e_down):
    p = dict(w_in=w_in, w_uq=w_uq, w_ukv=w_ukv, q_norm_g=q_norm_g, kv_norm_g=kv_norm_g,
             da_lq1=da_lq1, da_lk1=da_lk1, da_lq2=da_lq2, da_lk2=da_lk2, da_subln_g=da_subln_g,
             na_rpb=na_rpb, w_out=w_out, w_ada=w_ada, b_ada=b_ada, ln1_g=ln1_g, ln1_b=ln1_b,
             ln2_g=ln2_g, ln2_b=ln2_b, w_router=w_router, w_e_gate=w_e_gate, w_e_up=w_e_up,
             w_e_down=w_e_down)
    bp, t, d = x_prompt.shape
    bs = x_sample.shape[0]
    assert x_sample.shape[1] == t
    nb = bp + bs
    x = jnp.concatenate([x_prompt.reshape(bp * t, d), x_sample.reshape(bs * t, d)], axis=0)
    c = jnp.concatenate([c_prompt, c_sample], axis=0)
    y = _trunk(x, c, (bp * t, bs * t), nb, t, p)
    return y[:bp * t].reshape(bp, t, d), y[bp * t:].reshape(bs, t, d)
```

```python
import functools
import math

import jax
import jax.numpy as jnp
import numpy as np
from jax import lax
from jax.experimental import pallas as pl
from jax.experimental.pallas import tpu as pltpu

F32 = jnp.float32
BF16 = jnp.bfloat16

D_MODEL = 1024
DEPTH = 2
GRID_W = 64
GROUP_W = 256
DA_HEADS, DA_DV, DA_DQK = 4, 64, 32
MLA_HEADS, MLA_Q_RANK, MLA_KV_RANK, MLA_NOPE, MLA_ROPE, MLA_DV = 4, 192, 128, 64, 32, 64
MLA_ROPE_THETA = 10000.0
DIL_HEADS, DIL_DH = 4, 64
DIL_PATTERNS = ((128, 1), (512, 4), (2048, 16))
NA_HEADS, NA_DH, NA_KR, NA_KC = 4, 64, 8, 16
ROPE_THETA = 500000.0
ROPE_FRACTION = 4
N_EXPERTS = 16
EC_FACTOR = 2
D_FF = 2816
DEEPNORM_ALPHA = (2.0 * DEPTH) ** 0.25
NEG_INF = -1e30
LOG2E = math.log2(math.e)
LN_EPS = 1e-5
RMS_EPS = 1e-6

LANES = 128
MXU_DIM = 256
VMEM_LIMIT = 56 * 1024 * 1024

S_AQ, S_AK, S_AV, S_BV, S_BQ0, S_BQ1, S_BK0, S_BK1, S_CQ, S_CK, S_CV, S_DQ, S_DK, S_DV = range(14)
N_SLOTS = 14
W_ALL_COLS = 9 * 256 + 256 + 128 + 128


def _cparams(sem):
    return pltpu.CompilerParams(dimension_semantics=sem, vmem_limit_bytes=VMEM_LIMIT)


def _split_bf16(a):
    hi = a.astype(BF16)
    lo = (a - hi.astype(F32)).astype(BF16)
    return hi, lo


def _mod_kernel(c_ref, w_ref, b_ref, o_ref):
    c = c_ref[...]
    a = c * (1.0 / (1.0 + jnp.exp(-c)))
    a_hi, a_lo = _split_bf16(a)
    w_hi, w_lo = _split_bf16(w_ref[...])
    acc = jnp.dot(a_hi, w_hi, preferred_element_type=F32)
    acc += jnp.dot(a_hi, w_lo, preferred_element_type=F32)
    acc += jnp.dot(a_lo, w_hi, preferred_element_type=F32)
    o_ref[...] = acc + b_ref[...]


def _modulation(c, w_ada, b_ada):
    nb, d = c.shape
    n_out = w_ada.shape[1]
    tn = 1536
    return pl.pallas_call(
        _mod_kernel,
        grid=(n_out // tn,),
        in_specs=[pl.BlockSpec((nb, d), lambda j: (0, 0)),
                  pl.BlockSpec((d, tn), lambda j: (0, j)),
                  pl.BlockSpec((1, tn), lambda j: (0, j))],
        out_specs=pl.BlockSpec((nb, tn), lambda j: (0, j)),
        out_shape=jax.ShapeDtypeStruct((nb, n_out), F32),
        compiler_params=_cparams(("arbitrary",)),
        name="adaln_mod",
    )(c, w_ada, b_ada.reshape(1, n_out))


def _rope_tables(t, width, group, rot, theta):
    half = rot // 2
    inv = theta ** (-jnp.arange(half, dtype=F32) / half)
    ang = jnp.arange(t, dtype=F32)[:, None] * inv[None, :]
    cos, sin = jnp.cos(ang), jnp.sin(ang)
    ones = jnp.ones((t, group - rot), F32)
    zeros = jnp.zeros((t, group - rot), F32)
    c_g = jnp.concatenate([cos, cos, ones], axis=1)
    s_g = jnp.concatenate([-sin, sin, zeros], axis=1)
    reps = width // group
    return jnp.tile(c_g, (1, reps)), jnp.tile(s_g, (1, reps))


def _apply_rope(x, c_tab, s_tab, group, rot):
    width = x.shape[-1]
    half = rot // 2
    lane = lax.broadcasted_iota(jnp.int32, (1, width), 1)
    first = (lane % group) < half
    fwd = pltpu.roll(x, width - half, 1)
    bwd = pltpu.roll(x, half, 1)
    return x * c_tab + s_tab * jnp.where(first, fwd, bwd)


def _inproj_kernel(x_ref, sc_ref, sh_ref, w_ref, wuq_ref, wukv_ref, gq_ref, gkv_ref,
                   ca_ref, sa_ref, cc_ref, scc_ref, cm_ref, sm_ref, o_ref):
    h = (x_ref[...] * (1.0 + sc_ref[...]) + sh_ref[...]).astype(BF16)

    def proj(col, width):
        return jnp.dot(h, w_ref[:, col:col + width], preferred_element_type=F32)

    sa_scale = DA_DQK ** -0.5 * LOG2E
    sb_scale = (MLA_NOPE + MLA_ROPE) ** -0.5 * LOG2E
    sc_scale = DIL_DH ** -0.5 * LOG2E
    sd_scale = NA_DH ** -0.5 * LOG2E
    a_rot = DA_DQK // ROPE_FRACTION
    c_rot = DIL_DH // ROPE_FRACTION

    ca, sa = ca_ref[...], sa_ref[...]
    o_ref[S_AQ] = (_apply_rope(proj(0, 256), ca, sa, DA_DQK, a_rot) * sa_scale).astype(BF16)
    o_ref[S_AK] = _apply_rope(proj(256, 256), ca, sa, DA_DQK, a_rot).astype(BF16)
    o_ref[S_AV] = proj(512, 256).astype(BF16)
    cc, scc = cc_ref[...], scc_ref[...]
    o_ref[S_CQ] = (_apply_rope(proj(768, 256), cc, scc, DIL_DH, c_rot) * sc_scale).astype(BF16)
    o_ref[S_CK] = _apply_rope(proj(1024, 256), cc, scc, DIL_DH, c_rot).astype(BF16)
    o_ref[S_CV] = proj(1280, 256).astype(BF16)
    o_ref[S_DQ] = (proj(1536, 256) * sd_scale).astype(BF16)
    o_ref[S_DK] = proj(1792, 256).astype(BF16)
    o_ref[S_DV] = proj(2048, 256).astype(BF16)

    cm, sm = cm_ref[...], sm_ref[...]
    cq = proj(2304, 256)
    cq = cq * lax.rsqrt(jnp.sum(cq * cq, -1, keepdims=True) * (1.0 / MLA_Q_RANK) + RMS_EPS)
    cq = (cq * gq_ref[...]).astype(BF16)
    q2 = jnp.dot(cq, wuq_ref[...], preferred_element_type=F32)
    for p in range(2):
        qp = q2[:, 256 * p:256 * (p + 1)]
        o_ref[S_BQ0 + p, :, 0:128] = (qp[:, 0:128] * sb_scale).astype(BF16)
        o_ref[S_BQ0 + p, :, 128:256] = (
            _apply_rope(qp[:, 128:256], cm, sm, MLA_ROPE, MLA_ROPE) * sb_scale).astype(BF16)
    ckv = proj(2560, 128)
    ckv = ckv * lax.rsqrt(jnp.mean(ckv * ckv, -1, keepdims=True) + RMS_EPS)
    ckv = (ckv * gkv_ref[...]).astype(BF16)
    kv = jnp.dot(ckv, wukv_ref[...], preferred_element_type=F32)
    kr = _apply_rope(proj(2688, 128), cm, sm, MLA_ROPE, MLA_ROPE).astype(BF16)
    for p in range(2):
        o_ref[S_BK0 + p, :, 0:128] = kv[:, 128 * p:128 * (p + 1)].astype(BF16)
        o_ref[S_BK0 + p, :, 128:256] = kr
    o_ref[S_BV] = kv[:, 256:512].astype(BF16)


def _inproj(x, sc, sh, w_all, wuq, wukv, gq, gkv, tabs, nb, t):
    tm = 512
    nt = t // tm
    ntok = nb * t
    ca, sa, cc, scc, cm, sm = tabs
    full = lambda shape: pl.BlockSpec(shape, lambda j, b: tuple(0 for _ in shape))
    tab = lambda w: pl.BlockSpec((tm, w), lambda j, b: (j, 0))
    return pl.pallas_call(
        _inproj_kernel,
        grid=(nt, nb),
        in_specs=[pl.BlockSpec((tm, D_MODEL), lambda j, b: (b * nt + j, 0)),
                  pl.BlockSpec((None, 1, D_MODEL), lambda j, b: (b, 0, 0)),
                  pl.BlockSpec((None, 1, D_MODEL), lambda j, b: (b, 0, 0)),
                  full((D_MODEL, W_ALL_COLS)), full((256, 512)), full((128, 512)),
                  full((1, 256)), full((1, 128)),
                  tab(256), tab(256), tab(256), tab(256), tab(128), tab(128)],
        out_specs=pl.BlockSpec((N_SLOTS, tm, 256), lambda j, b: (0, b * nt + j, 0)),
        out_shape=jax.ShapeDtypeStruct((N_SLOTS, ntok, 256), BF16),
        compiler_params=_cparams(("arbitrary", "arbitrary")),
        name="in_proj",
    )(x, sc, sh, w_all, wuq, wukv, gq, gkv, ca, sa, cc, scc, cm, sm)


def _lane_mask(width, ranges):
    lane = lax.broadcasted_iota(jnp.int32, (1, width), 1)
    m = None
    for lo, hi in ranges:
        r = (lane >= lo) & (lane < hi)
        m = r if m is None else (m | r)
    return m


SOFTMAX_SLAB = 32


def _chain_scratch(tq, kc, width):
    return [pltpu.VMEM((tq, kc), F32), pltpu.VMEM((tq, kc), F32), pltpu.VMEM((tq, kc), BF16),
            pltpu.VMEM((tq, LANES), F32), pltpu.VMEM((tq, LANES), F32), pltpu.VMEM((tq, LANES), F32),
            pltpu.VMEM((tq, width), F32)]


CHAIN_REFS = 7


def _online_attention(qms, k_ref, v_ref, t, kc, chains):
    tq = qms[0].shape[0]
    width = v_ref.shape[-1]
    n_chunks = t // kc
    assert n_chunks % 2 == 0
    nt = (((1,), (1,)), ((), ()))
    for (_, _, _, m_ref, l_ref, _, acc_ref) in chains:
        m_ref[...] = jnp.full(m_ref.shape, NEG_INF, F32)
        l_ref[...] = jnp.zeros(l_ref.shape, F32)
        acc_ref[...] = jnp.zeros(acc_ref.shape, F32)

    def scores(chunk, slot):
        k = k_ref[pl.ds(pl.multiple_of(chunk * kc, kc), kc), :]
        for qm, chain in zip(qms, chains):
            chain[slot][...] = lax.dot_general(qm, k, nt, preferred_element_type=F32)

    def softmax_pv(chunk, slot):
        v = v_ref[pl.ds(pl.multiple_of(chunk * kc, kc), kc), :]
        for chain in chains:
            s_ref = chain[slot]
            (p_ref, m_ref, l_ref, a_ref, acc_ref) = chain[2:]
            for r in range(tq // SOFTMAX_SLAB):
                rows = slice(r * SOFTMAX_SLAB, (r + 1) * SOFTMAX_SLAB)
                s = s_ref[rows, :]
                m_prev = m_ref[rows, :]
                m_new = jnp.maximum(m_prev, jnp.max(s, -1, keepdims=True))
                p = jnp.exp2(s - jnp.concatenate([m_new] * (kc // LANES), axis=1))
                alpha = jnp.exp2(m_prev - m_new)
                l_ref[rows, :] = alpha * l_ref[rows, :] + jnp.sum(p, -1, keepdims=True)
                m_ref[rows, :] = m_new
                a_ref[rows, :] = alpha
                p_ref[rows, :] = p.astype(BF16)
            alpha = jnp.concatenate([a_ref[...]] * (width // LANES), axis=1)
            acc_ref[...] = alpha * acc_ref[...] + jnp.dot(p_ref[...], v,
                                                          preferred_element_type=F32)

    def body(j, _):
        scores(2 * j + 1, 1)
        softmax_pv(2 * j, 0)
        scores(jnp.minimum(2 * j + 2, n_chunks - 1), 0)
        softmax_pv(2 * j + 1, 1)
        return 0

    scores(0, 0)
    lax.fori_loop(0, n_chunks // 2, body, 0)
    outs = []
    for chain in chains:
        l_ref, acc_ref = chain[4], chain[6]
        inv = 1.0 / l_ref[...]
        outs.append(acc_ref[...] * jnp.concatenate([inv] * (width // LANES), axis=1))
    return outs


def _group_mean_sq(x, gmat):
    sq = x * x
    hi, lo = _split_bf16(sq)
    return (jnp.dot(hi, gmat, preferred_element_type=F32)
            + jnp.dot(lo, gmat, preferred_element_type=F32))


def _diff_attn_kernel(lam_ref, q_ref, k_ref, v_ref, g_ref, gmat_ref, o_ref, acc_ref, *scratch, t, kc,
                      out_scale):
    chains = (scratch[:CHAIN_REFS], scratch[CHAIN_REFS:])
    q = q_ref[...]
    lam = lam_ref[0]
    lane = lax.broadcasted_iota(jnp.int32, (1, 256), 1)
    zero = jnp.zeros_like(q)

    def head(h, _):
        qms = []
        for c in range(2):
            lo = (2 * h + c) * DA_DQK
            qms.append(jnp.where((lane >= lo) & (lane < lo + DA_DQK), q, zero))
        o_1, o_2 = _online_attention(qms, k_ref, v_ref, t, kc, chains)
        o_h = o_1 - lam * o_2
        sel = (lane >= h * DA_DV) & (lane < (h + 1) * DA_DV)
        acc_ref[...] = jnp.where(sel, o_h, acc_ref[...])
        return 0

    acc_ref[...] = jnp.zeros_like(acc_ref)
    lax.fori_loop(0, DA_HEADS, head, 0)
    o = acc_ref[...]
    ms = _group_mean_sq(o, gmat_ref[...])
    o_ref[...] = (o * lax.rsqrt(ms + RMS_EPS) * g_ref[...] * out_scale).astype(o_ref.dtype)


def _diff_attention(proj, lam, g_tiled, gmat, nb, t, out_scale):
    tq, kc = 512, 512
    nq = t // tq
    kern = functools.partial(_diff_attn_kernel, t=t, kc=kc, out_scale=out_scale)
    return pl.pallas_call(
        kern,
        grid=(nb, nq),
        in_specs=[pl.BlockSpec(memory_space=pltpu.SMEM),
                  pl.BlockSpec((None, tq, 256), lambda b, i: (S_AQ, b * nq + i, 0)),
                  pl.BlockSpec((None, t, 256), lambda b, i: (S_AK, b, 0)),
                  pl.BlockSpec((None, t, 256), lambda b, i: (S_AV, b, 0)),
                  pl.BlockSpec((1, 256), lambda b, i: (0, 0)),
                  pl.BlockSpec((256, 256), lambda b, i: (0, 0))],
        out_specs=pl.BlockSpec((tq, 256), lambda b, i: (b * nq + i, 0)),
        out_shape=jax.ShapeDtypeStruct((nb * t, 256), BF16),
        scratch_shapes=[pltpu.VMEM((tq, 256), F32)] + 2 * _chain_scratch(tq, kc, 256),
        compiler_params=_cparams(("arbitrary", "arbitrary")),
        name="diff_attn",
    )(lam, proj, proj, proj, g_tiled, gmat)


def _mla_attn_kernel(q_ref, k_ref, v_ref, o_ref, acc_ref, *scratch, t, kc):
    chains = (scratch[:CHAIN_REFS], scratch[CHAIN_REFS:])
    lane = lax.broadcasted_iota(jnp.int32, (1, 256), 1)

    def pair(p, _):
        q = q_ref[p]
        qms = []
        for j in range(2):
            nope = (lane >= j * MLA_NOPE) & (lane < (j + 1) * MLA_NOPE)
            rope = (lane >= 128 + j * MLA_ROPE) & (lane < 128 + (j + 1) * MLA_ROPE)
            qms.append(jnp.where(nope | rope, q, jnp.zeros_like(q)))
        outs = _online_attention(qms, k_ref.at[p], v_ref, t, kc, chains)
        out = acc_ref[...]
        for j in range(2):
            h = 2 * p + j
            sel = (lane >= h * MLA_DV) & (lane < (h + 1) * MLA_DV)
            out = jnp.where(sel, outs[j], out)
        acc_ref[...] = out
        return 0

    acc_ref[...] = jnp.zeros_like(acc_ref)
    lax.fori_loop(0, MLA_HEADS // 2, pair, 0)
    o_ref[...] = acc_ref[...].astype(o_ref.dtype)


def _mla_attention(proj, nb, t):
    tq, kc = 512, 512
    nq = t // tq
    kern = functools.partial(_mla_attn_kernel, t=t, kc=kc)
    return pl.pallas_call(
        kern,
        grid=(nb, nq),
        in_specs=[pl.BlockSpec((2, tq, 256), lambda b, i: (S_BQ0 // 2, b * nq + i, 0)),
                  pl.BlockSpec((2, t, 256), lambda b, i: (S_BK0 // 2, b, 0)),
                  pl.BlockSpec((None, t, 256), lambda b, i: (S_BV, b, 0))],
        out_specs=pl.BlockSpec((tq, 256), lambda b, i: (b * nq + i, 0)),
        out_shape=jax.ShapeDtypeStruct((nb * t, 256), BF16),
        scratch_shapes=[pltpu.VMEM((tq, 256), F32)] + 2 * _chain_scratch(tq, kc, 256),
        compiler_params=_cparams(("arbitrary", "arbitrary")),
        name="mla_attn",
    )(proj, proj, proj)


DIL_REACH = max(w // 2 for w, _ in DIL_PATTERNS)


def _dil_attn_kernel(q_ref, k_ref, v_ref, o_ref, bias_ref, acc_ref, *, t, tq, band):
    i = pl.program_id(1)

    def band_start(blk):
        return jnp.clip(blk * tq - DIL_REACH, 0, t - band)

    t0 = i * tq
    start = pl.multiple_of(band_start(i), tq)
    shift = start - t0
    prev_shift = band_start(i - 1) - (i - 1) * tq

    @pl.when((i == 0) | (shift != prev_shift))
    def _():
        qi = lax.broadcasted_iota(jnp.int32, (tq, band), 0)
        kj = lax.broadcasted_iota(jnp.int32, (tq, band), 1)
        delta = kj - qi + shift
        ad = jnp.abs(delta)
        cnt = jnp.zeros((tq, band), F32)
        for window, dil in DIL_PATTERNS:
            ok = (ad <= window // 2) & ((delta & (dil - 1)) == 0)
            cnt = cnt + jnp.where(ok, 1.0, 0.0)
        bias_ref[...] = jnp.where(cnt > 2.5, math.log2(3.0),
                                  jnp.where(cnt > 1.5, 1.0,
                                            jnp.where(cnt > 0.5, 0.0, NEG_INF)))

    q = q_ref[...]
    lane = lax.broadcasted_iota(jnp.int32, (1, 256), 1)
    kb = k_ref.at[pl.ds(start, band), :]
    vb = v_ref.at[pl.ds(start, band), :]

    def pair(hp, _):
        out = acc_ref[...]
        for j in range(2):
            h = 2 * hp + j
            sel = (lane >= h * DIL_DH) & (lane < (h + 1) * DIL_DH)
            qm = jnp.where(sel, q, jnp.zeros_like(q))
            s = lax.dot_general(qm, kb[...], (((1,), (1,)), ((), ())),
                                preferred_element_type=F32)
            s = s + bias_ref[...]
            m = jnp.max(s, -1, keepdims=True)
            p = jnp.exp2(s - m)
            l = jnp.sum(p, -1, keepdims=True)
            o_h = jnp.dot(p.astype(BF16), vb[...], preferred_element_type=F32) * (1.0 / l)
            out = jnp.where(sel, o_h, out)
        acc_ref[...] = out
        return 0

    acc_ref[...] = jnp.zeros_like(acc_ref)
    lax.fori_loop(0, DIL_HEADS // 2, pair, 0)
    o_ref[...] = acc_ref[...].astype(o_ref.dtype)


def _dil_attention(proj, nb, t):
    tq = 256
    band = min(t, tq + 2 * DIL_REACH)
    nq = t // tq
    kern = functools.partial(_dil_attn_kernel, t=t, tq=tq, band=band)
    return pl.pallas_call(
        kern,
        grid=(nb, nq),
        in_specs=[pl.BlockSpec((None, tq, 256), lambda b, i: (S_CQ, b * nq + i, 0)),
                  pl.BlockSpec((None, t, 256), lambda b, i: (S_CK, b, 0)),
                  pl.BlockSpec((None, t, 256), lambda b, i: (S_CV, b, 0))],
        out_specs=pl.BlockSpec((tq, 256), lambda b, i: (b * nq + i, 0)),
        out_shape=jax.ShapeDtypeStruct((nb * t, 256), BF16),
        scratch_shapes=[pltpu.VMEM((tq, band), F32), pltpu.VMEM((tq, 256), F32)],
        compiler_params=_cparams(("arbitrary", "arbitrary")),
        name="dil_attn",
    )(proj, proj, proj)


def _na_bias_table(rpb):
    c = jnp.arange(GRID_W)
    cs = jnp.clip(c - NA_KC // 2, 0, GRID_W - NA_KC)
    colmask = (c[None, :] >= cs[:, None]) & (c[None, :] < cs[:, None] + NA_KC)
    ci = jnp.clip(c[None, :] - c[:, None] + (NA_KC - 1), 0, 2 * NA_KC - 2)
    si = jnp.arange(NA_KR)
    j = jnp.arange(NA_KR)
    ri = si[:, None] + j[None, :]
    b = rpb[:, ri[:, None, :, None], ci[None, :, None, :]].astype(F32)
    b = jnp.where(colmask[None, None, :, None, :], b * LOG2E, NEG_INF)
    b = b.transpose(1, 0, 2, 3, 4)
    return b.reshape(NA_KR, rpb.shape[0] * GRID_W, NA_KR * GRID_W)


def _na_attn_kernel(q_ref, k_ref, v_ref, tb_ref, o_ref, *, rows, rg):
    g = pl.program_id(1)
    lane = lax.broadcasted_iota(jnp.int32, (1, 256), 1)
    nk = NA_KR * GRID_W
    sels = [(lane >= h * NA_DH) & (lane < (h + 1) * NA_DH) for h in range(NA_HEADS)]

    def row(r, _):
        grow = g * rg + r
        rs = jnp.clip(grow - NA_KR // 2, 0, rows - NA_KR)
        si = rs - grow + (NA_KR - 1)
        q = q_ref[pl.ds(pl.multiple_of(r * GRID_W, GRID_W), GRID_W), :]
        koff = pl.multiple_of(rs * GRID_W, GRID_W)
        kb = k_ref[pl.ds(koff, nk), :]
        vb = v_ref[pl.ds(koff, nk), :]
        q4 = jnp.concatenate([jnp.where(sel, q, jnp.zeros_like(q)) for sel in sels], axis=0)
        s = lax.dot_general(q4, kb, (((1,), (1,)), ((), ())), preferred_element_type=F32)
        s = s + tb_ref[si]
        m = jnp.max(s, -1, keepdims=True)
        p = jnp.exp2(s - m)
        l = jnp.sum(p, -1, keepdims=True)
        o4 = jnp.dot(p.astype(BF16), vb, preferred_element_type=F32) * (1.0 / l)
        out = jnp.zeros((GRID_W, 256), F32)
        for h, sel in enumerate(sels):
            out = jnp.where(sel, o4[h * GRID_W:(h + 1) * GRID_W], out)
        o_ref[pl.ds(pl.multiple_of(r * GRID_W, GRID_W), GRID_W), :] = out.astype(o_ref.dtype)
        return 0

    lax.fori_loop(0, rg, row, 0, unroll=2)


def _na_attention(proj, tb, nb, t):
    rows = t // GRID_W
    assert rows >= NA_KR
    rg = 8
    ng = rows // rg
    tq = rg * GRID_W
    kern = functools.partial(_na_attn_kernel, rows=rows, rg=rg)
    return pl.pallas_call(
        kern,
        grid=(nb, ng),
        in_specs=[pl.BlockSpec((None, tq, 256), lambda b, i: (S_DQ, b * ng + i, 0)),
                  pl.BlockSpec((None, t, 256), lambda b, i: (S_DK, b, 0)),
                  pl.BlockSpec((None, t, 256), lambda b, i: (S_DV, b, 0)),
                  pl.BlockSpec(tb.shape, lambda b, i: (0, 0, 0))],
        out_specs=pl.BlockSpec((tq, 256), lambda b, i: (b * ng + i, 0)),
        out_shape=jax.ShapeDtypeStruct((nb * t, 256), BF16),
        compiler_params=_cparams(("arbitrary", "arbitrary")),
        name="na_attn",
    )(proj, proj, proj, tb)


def _layer_norm(y, g, b):
    mu = jnp.mean(y, -1, keepdims=True)
    yc = y - mu
    var = jnp.mean(yc * yc, -1, keepdims=True)
    return yc * lax.rsqrt(var + LN_EPS) * g + b


def _outproj_kernel(oa_ref, ob_ref, oc_ref, od_ref, x_ref, g1_ref, sc2_ref, sh2_ref, w_ref,
                    lg_ref, lb_ref, wr_ref, x1_ref, h2_ref, aff_ref):
    m = jnp.dot(oa_ref[...], w_ref[0:256, :], preferred_element_type=F32)
    m += jnp.dot(ob_ref[...], w_ref[256:512, :], preferred_element_type=F32)
    m += jnp.dot(oc_ref[...], w_ref[512:768, :], preferred_element_type=F32)
    m += jnp.dot(od_ref[...], w_ref[768:1024, :], preferred_element_type=F32)
    y = DEEPNORM_ALPHA * x_ref[...] + (1.0 + g1_ref[...]) * m
    x1 = _layer_norm(y, lg_ref[...], lb_ref[...])
    x1_ref[...] = x1
    h2 = x1 * (1.0 + sc2_ref[...]) + sh2_ref[...]
    h2_ref[...] = h2.astype(BF16)
    h_hi, h_lo = _split_bf16(h2)
    w_hi, w_lo = _split_bf16(wr_ref[...])
    nt = (((1,), (1,)), ((), ()))
    lg = lax.dot_general(w_hi, h_hi, nt, preferred_element_type=F32)
    lg += lax.dot_general(w_hi, h_lo, nt, preferred_element_type=F32)
    lg += lax.dot_general(w_lo, h_hi, nt, preferred_element_type=F32)
    lg = lg - jnp.max(lg, 0, keepdims=True)
    e = jnp.exp(lg)
    aff_ref[...] = e / jnp.sum(e, 0, keepdims=True)


def _outproj(oa, ob, oc, od, x, g1, sc2, sh2, w_out, ln_g, ln_b, w_router_t, nb, t):
    tm = 512
    nt = t // tm
    ntok = nb * t
    tok = lambda w: pl.BlockSpec((tm, w), lambda i: (i, 0))
    per_b = pl.BlockSpec((None, 1, D_MODEL), lambda i: (i // nt, 0, 0))
    full = lambda shape: pl.BlockSpec(shape, lambda i: tuple(0 for _ in shape))
    return pl.pallas_call(
        _outproj_kernel,
        grid=(ntok // tm,),
        in_specs=[tok(256), tok(256), tok(256), tok(256), tok(D_MODEL), per_b, per_b, per_b,
                  full((D_MODEL, D_MODEL)), full((1, D_MODEL)), full((1, D_MODEL)),
                  full((N_EXPERTS, D_MODEL))],
        out_specs=[tok(D_MODEL), tok(D_MODEL),
                   pl.BlockSpec((N_EXPERTS, tm), lambda i: (0, i))],
        out_shape=[jax.ShapeDtypeStruct((ntok, D_MODEL), F32),
                   jax.ShapeDtypeStruct((ntok, D_MODEL), BF16),
                   jax.ShapeDtypeStruct((N_EXPERTS, ntok), F32)],
        compiler_params=_cparams(("arbitrary",)),
        name="out_proj",
    )(oa, ob, oc, od, x, g1, sc2, sh2, w_out, ln_g, ln_b, w_router_t)


def _expert_kernel(x_ref, gt_ref, wg_ref, wu_ref, wd_ref, o_ref, acc_ref):
    f = pl.program_id(2)
    x = x_ref[...]
    g = jnp.dot(x, wg_ref[...], preferred_element_type=F32)
    u = jnp.dot(x, wu_ref[...], preferred_element_type=F32)
    hmid = (g * (1.0 / (1.0 + jnp.exp(-g))) * u).astype(BF16)
    part = jnp.dot(hmid, wd_ref[...], preferred_element_type=F32)

    @pl.when(f == 0)
    def _():
        acc_ref[...] = part

    @pl.when(f > 0)
    def _():
        acc_ref[...] += part

    @pl.when(f == pl.num_programs(2) - 1)
    def _():
        o_ref[...] = acc_ref[...] * gt_ref[...]


def _experts(xe, gates, wg, wu, wd):
    ne, cap, d = xe.shape
    tm = min(cap, 1024)
    tf = 256
    nf = D_FF // tf
    return pl.pallas_call(
        _expert_kernel,
        grid=(ne, cap // tm, nf),
        in_specs=[pl.BlockSpec((None, tm, d), lambda e, m, f: (e, m, 0)),
                  pl.BlockSpec((None, tm, 1), lambda e, m, f: (e, m, 0)),
                  pl.BlockSpec((None, d, tf), lambda e, m, f: (e, 0, f)),
                  pl.BlockSpec((None, d, tf), lambda e, m, f: (e, 0, f)),
                  pl.BlockSpec((None, tf, d), lambda e, m, f: (e, f, 0))],
        out_specs=pl.BlockSpec((None, tm, d), lambda e, m, f: (e, m, 0)),
        out_shape=jax.ShapeDtypeStruct((ne, cap, d), F32),
        scratch_shapes=[pltpu.VMEM((tm, d), F32)],
        compiler_params=_cparams(("arbitrary", "arbitrary", "arbitrary")),
        name="expert_ffn",
    )(xe, gates, wg, wu, wd)


def _postnorm_kernel(x_ref, f_ref, g2_ref, lg_ref, lb_ref, o_ref):
    y = DEEPNORM_ALPHA * x_ref[...] + (1.0 + g2_ref[...]) * f_ref[...]
    o_ref[...] = _layer_norm(y, lg_ref[...], lb_ref[...])


def _postnorm(x1, f, g2, ln_g, ln_b, nb, t):
    tm = 512
    nt = t // tm
    ntok = nb * t
    tok = pl.BlockSpec((tm, D_MODEL), lambda i: (i, 0))
    per_b = pl.BlockSpec((None, 1, D_MODEL), lambda i: (i // nt, 0, 0))
    vec = pl.BlockSpec((1, D_MODEL), lambda i: (0, 0))
    return pl.pallas_call(
        _postnorm_kernel,
        grid=(ntok // tm,),
        in_specs=[tok, tok, per_b, vec, vec],
        out_specs=tok,
        out_shape=jax.ShapeDtypeStruct((ntok, D_MODEL), F32),
        compiler_params=_cparams(("arbitrary",)),
        name="post_norm",
    )(x1, f, g2, ln_g, ln_b)


def _prep_w_in(w_in_l):
    sizes = (256, 256, 256, MLA_Q_RANK, MLA_KV_RANK, MLA_ROPE, 256, 256, 256, 256, 256, 256)
    offs = np.concatenate([[0], np.cumsum(sizes)])
    part = [w_in_l[:, offs[i]:offs[i + 1]] for i in range(len(sizes))]
    a_q, a_k, a_v, b_cq, b_ckv, b_kr, c_q, c_k, c_v, d_q, d_k, d_v = part
    d = w_in_l.shape[0]
    zeros = lambda n: jnp.zeros((d, n), w_in_l.dtype)
    cols = [a_q, a_k, a_v, c_q, c_k, c_v, d_q, d_k, d_v,
            b_cq, zeros(256 - MLA_Q_RANK), b_ckv, b_kr, b_kr, zeros(128 - 2 * MLA_ROPE)]
    return jnp.concatenate(cols, axis=1).astype(BF16)


def _prep_w_uq(w_uq_l):
    hd = MLA_NOPE + MLA_ROPE
    nope = [w_uq_l[:, h * hd:h * hd + MLA_NOPE] for h in range(MLA_HEADS)]
    rope = [w_uq_l[:, h * hd + MLA_NOPE:(h + 1) * hd] for h in range(MLA_HEADS)]
    z = jnp.zeros((w_uq_l.shape[0], 256 - 2 * hd), w_uq_l.dtype)
    cols = []
    for p in range(2):
        cols += [nope[2 * p], nope[2 * p + 1], rope[2 * p], rope[2 * p + 1], z]
    w = jnp.concatenate(cols, axis=1)
    w = jnp.concatenate([w, jnp.zeros((256 - MLA_Q_RANK, w.shape[1]), w.dtype)], axis=0)
    return w.astype(BF16)


def _prep_w_ukv(w_ukv_l):
    hd = MLA_NOPE + MLA_DV
    kn = [w_ukv_l[:, h * hd:h * hd + MLA_NOPE] for h in range(MLA_HEADS)]
    vv = [w_ukv_l[:, h * hd + MLA_NOPE:(h + 1) * hd] for h in range(MLA_HEADS)]
    return jnp.concatenate(kn + vv, axis=1).astype(BF16)


def _moe_group(aff_t, h2, wg, wu, wd, off, n):
    cap = EC_FACTOR * n // N_EXPERTS
    aff_g = lax.slice_in_dim(aff_t, off, off + n, axis=1)
    gates, idx = lax.top_k(aff_g, cap)
    xe = jnp.take(lax.slice_in_dim(h2, off, off + n, axis=0), idx, axis=0)
    ye = _experts(xe, gates[..., None], wg, wu, wd)
    out = jnp.zeros((n, D_MODEL), F32).at[idx.reshape(-1)].add(ye.reshape(-1, D_MODEL))
    return out


def _trunk(x, c, group_tokens, nb, t, p):
    ntok = nb * t
    tabs = (_rope_tables(t, 256, DA_DQK, DA_DQK // ROPE_FRACTION, ROPE_THETA)
            + _rope_tables(t, 256, DIL_DH, DIL_DH // ROPE_FRACTION, ROPE_THETA)
            + _rope_tables(t, 128, MLA_ROPE, MLA_ROPE, MLA_ROPE_THETA))
    gmat = jnp.asarray(np.kron(np.eye(4), np.full((64, 64), 1.0 / 64)), BF16)
    for l in range(DEPTH):
        mod = _modulation(c, p['w_ada'][l], p['b_ada'][l])
        sh1, sc1, g1, sh2, sc2, g2 = [m.reshape(nb, 1, D_MODEL) for m in jnp.split(mod, 6, axis=-1)]
        gq = jnp.concatenate([p['q_norm_g'][l], jnp.zeros((256 - MLA_Q_RANK,), F32)]).reshape(1, 256)
        gkv = p['kv_norm_g'][l].reshape(1, 128)
        proj = _inproj(x, sc1, sh1, _prep_w_in(p['w_in'][l]), _prep_w_uq(p['w_uq'][l]),
                       _prep_w_ukv(p['w_ukv'][l]), gq, gkv, tabs, nb, t)
        lam_init = 0.8 - 0.6 * math.exp(-0.3 * l)
        lam = (jnp.exp(jnp.sum(p['da_lq1'][l] * p['da_lk1'][l]))
               - jnp.exp(jnp.sum(p['da_lq2'][l] * p['da_lk2'][l])) + lam_init).reshape(1)
        g_sub = jnp.tile(p['da_subln_g'][l], DA_HEADS).reshape(1, 256)
        oa = _diff_attention(proj, lam, g_sub, gmat, nb, t, 1.0 - lam_init)
        ob = _mla_attention(proj, nb, t)
        oc = _dil_attention(proj, nb, t)
        od = _na_attention(proj, _na_bias_table(p['na_rpb'][l]), nb, t)
        x1, h2, aff_t = _outproj(oa, ob, oc, od, x, g1, sc2, sh2, p['w_out'][l].astype(BF16),
                                 p['ln1_g'][l].reshape(1, -1), p['ln1_b'][l].reshape(1, -1),
                                 p['w_router'][l].T, nb, t)
        wg = p['w_e_gate'][l].astype(BF16)
        wu = p['w_e_up'][l].astype(BF16)
        wd = p['w_e_down'][l].astype(BF16)
        outs, off = [], 0
        for n in group_tokens:
            outs.append(_moe_group(aff_t, h2, wg, wu, wd, off, n))
            off += n
        f = jnp.concatenate(outs, axis=0)
        x = _postnorm(x1, f, g2, p['ln2_g'][l].reshape(1, -1), p['ln2_b'][l].reshape(1, -1), nb, t)
    return x


def kernel(x_prompt, x_sample, c_prompt, c_sample, w_in, w_uq, w_ukv, q_norm_g, kv_norm_g, da_lq1,
           da_lk1, da_lq2, da_lk2, da_subln_g, na_rpb, w_out, w_ada, b_ada, ln1_g, ln1_b, ln2_g,
           ln2_b, w_router, w_e_gate, w_e_up, w_e_down):
    p = dict(w_in=w_in, w_uq=w_uq, w_ukv=w_ukv, q_norm_g=q_norm_g, kv_norm_g=kv_norm_g,
             da_lq1=da_lq1, da_lk1=da_lk1, da_lq2=da_lq2, da_lk2=da_lk2, da_subln_g=da_subln_g,
             na_rpb=na_rpb, w_out=w_out, w_ada=w_ada, b_ada=b_ada, ln1_g=ln1_g, ln1_b=ln1_b,
             ln2_g=ln2_g, ln2_b=ln2_b, w_router=w_router, w_e_gate=w_e_gate, w_e_up=w_e_up,
             w_e_down=w_e_down)
    bp, t, d = x_prompt.shape
    bs = x_sample.shape[0]
    assert x_sample.shape[1] == t
    nb = bp + bs
    x = jnp.concatenate([x_prompt.reshape(bp * t, d), x_sample.reshape(bs * t, d)], axis=0)
    c = jnp.concatenate([c_prompt, c_sample], axis=0)
    y = _trunk(x, c, (bp * t, bs * t), nb, t, p)
    return y[:bp * t].reshape(bp, t, d), y[bp * t:].reshape(bs, t, d)
```

```python
import functools
import math

import jax
import jax.numpy as jnp
import numpy as np
from jax import lax
from jax.experimental import pallas as pl
from jax.experimental.pallas import tpu as pltpu

F32 = jnp.float32
BF16 = jnp.bfloat16

D_MODEL = 1024
DEPTH = 2
GRID_W = 64
GROUP_W = 256
DA_HEADS, DA_DV, DA_DQK = 4, 64, 32
MLA_HEADS, MLA_Q_RANK, MLA_KV_RANK, MLA_NOPE, MLA_ROPE, MLA_DV = 4, 192, 128, 64, 32, 64
MLA_ROPE_THETA = 10000.0
DIL_HEADS, DIL_DH = 4, 64
DIL_PATTERNS = ((128, 1), (512, 4), (2048, 16))
NA_HEADS, NA_DH, NA_KR, NA_KC = 4, 64, 8, 16
ROPE_THETA = 500000.0
ROPE_FRACTION = 4
N_EXPERTS = 16
EC_FACTOR = 2
D_FF = 2816
DEEPNORM_ALPHA = (2.0 * DEPTH) ** 0.25
NEG_INF = -1e30
LOG2E = math.log2(math.e)
LN_EPS = 1e-5
RMS_EPS = 1e-6

LANES = 128
MXU_DIM = 256
VMEM_LIMIT = 56 * 1024 * 1024

S_AQ, S_AK, S_AV, S_BV, S_BQ0, S_BQ1, S_BK0, S_BK1, S_CQ, S_CK, S_CV, S_DQ, S_DK, S_DV = range(14)
N_SLOTS = 14
W_ALL_COLS = 9 * 256 + 256 + 128 + 128


def _cparams(sem):
    return pltpu.CompilerParams(dimension_semantics=sem, vmem_limit_bytes=VMEM_LIMIT)


def _split_bf16(a):
    hi = a.astype(BF16)
    lo = (a - hi.astype(F32)).astype(BF16)
    return hi, lo


def _mod_kernel(c_ref, w_ref, b_ref, o_ref):
    c = c_ref[...]
    a = c * (1.0 / (1.0 + jnp.exp(-c)))
    a_hi, a_lo = _split_bf16(a)
    w_hi, w_lo = _split_bf16(w_ref[...])
    acc = jnp.dot(a_hi, w_hi, preferred_element_type=F32)
    acc += jnp.dot(a_hi, w_lo, preferred_element_type=F32)
    acc += jnp.dot(a_lo, w_hi, preferred_element_type=F32)
    o_ref[...] = acc + b_ref[...]


def _modulation(c, w_ada, b_ada):
    nb, d = c.shape
    n_out = w_ada.shape[1]
    tn = 1536
    return pl.pallas_call(
        _mod_kernel,
        grid=(n_out // tn,),
        in_specs=[pl.BlockSpec((nb, d), lambda j: (0, 0)),
                  pl.BlockSpec((d, tn), lambda j: (0, j)),
                  pl.BlockSpec((1, tn), lambda j: (0, j))],
        out_specs=pl.BlockSpec((nb, tn), lambda j: (0, j)),
        out_shape=jax.ShapeDtypeStruct((nb, n_out), F32),
        compiler_params=_cparams(("arbitrary",)),
        name="adaln_mod",
    )(c, w_ada, b_ada.reshape(1, n_out))


def _rope_tables(t, width, group, rot, theta):
    half = rot // 2
    inv = theta ** (-jnp.arange(half, dtype=F32) / half)
    ang = jnp.arange(t, dtype=F32)[:, None] * inv[None, :]
    cos, sin = jnp.cos(ang), jnp.sin(ang)
    ones = jnp.ones((t, group - rot), F32)
    zeros = jnp.zeros((t, group - rot), F32)
    c_g = jnp.concatenate([cos, cos, ones], axis=1)
    s_g = jnp.concatenate([-sin, sin, zeros], axis=1)
    reps = width // group
    return jnp.tile(c_g, (1, reps)), jnp.tile(s_g, (1, reps))


def _apply_rope(x, c_tab, s_tab, group, rot):
    width = x.shape[-1]
    half = rot // 2
    lane = lax.broadcasted_iota(jnp.int32, (1, width), 1)
    first = (lane % group) < half
    fwd = pltpu.roll(x, width - half, 1)
    bwd = pltpu.roll(x, half, 1)
    return x * c_tab + s_tab * jnp.where(first, fwd, bwd)


def _inproj_kernel(x_ref, sc_ref, sh_ref, w_ref, wuq_ref, wukv_ref, gq_ref, gkv_ref,
                   ca_ref, sa_ref, cc_ref, scc_ref, cm_ref, sm_ref, o_ref):
    h = (x_ref[...] * (1.0 + sc_ref[...]) + sh_ref[...]).astype(BF16)

    def proj(col, width):
        return jnp.dot(h, w_ref[:, col:col + width], preferred_element_type=F32)

    sa_scale = DA_DQK ** -0.5 * LOG2E
    sb_scale = (MLA_NOPE + MLA_ROPE) ** -0.5 * LOG2E
    sc_scale = DIL_DH ** -0.5 * LOG2E
    sd_scale = NA_DH ** -0.5 * LOG2E
    a_rot = DA_DQK // ROPE_FRACTION
    c_rot = DIL_DH // ROPE_FRACTION

    ca, sa = ca_ref[...], sa_ref[...]
    o_ref[S_AQ] = (_apply_rope(proj(0, 256), ca, sa, DA_DQK, a_rot) * sa_scale).astype(BF16)
    o_ref[S_AK] = _apply_rope(proj(256, 256), ca, sa, DA_DQK, a_rot).astype(BF16)
    o_ref[S_AV] = proj(512, 256).astype(BF16)
    cc, scc = cc_ref[...], scc_ref[...]
    o_ref[S_CQ] = (_apply_rope(proj(768, 256), cc, scc, DIL_DH, c_rot) * sc_scale).astype(BF16)
    o_ref[S_CK] = _apply_rope(proj(1024, 256), cc, scc, DIL_DH, c_rot).astype(BF16)
    o_ref[S_CV] = proj(1280, 256).astype(BF16)
    o_ref[S_DQ] = (proj(1536, 256) * sd_scale).astype(BF16)
    o_ref[S_DK] = proj(1792, 256).astype(BF16)
    o_ref[S_DV] = proj(2048, 256).astype(BF16)

    cm, sm = cm_ref[...], sm_ref[...]
    cq = proj(2304, 256)
    cq = cq * lax.rsqrt(jnp.sum(cq * cq, -1, keepdims=True) * (1.0 / MLA_Q_RANK) + RMS_EPS)
    cq = (cq * gq_ref[...]).astype(BF16)
    q2 = jnp.dot(cq, wuq_ref[...], preferred_element_type=F32)
    for p in range(2):
        qp = q2[:, 256 * p:256 * (p + 1)]
        o_ref[S_BQ0 + p, :, 0:128] = (qp[:, 0:128] * sb_scale).astype(BF16)
        o_ref[S_BQ0 + p, :, 128:256] = (
            _apply_rope(qp[:, 128:256], cm, sm, MLA_ROPE, MLA_ROPE) * sb_scale).astype(BF16)
    ckv = proj(2560, 128)
    ckv = ckv * lax.rsqrt(jnp.mean(ckv * ckv, -1, keepdims=True) + RMS_EPS)
    ckv = (ckv * gkv_ref[...]).astype(BF16)
    kv = jnp.dot(ckv, wukv_ref[...], preferred_element_type=F32)
    kr = _apply_rope(proj(2688, 128), cm, sm, MLA_ROPE, MLA_ROPE).astype(BF16)
    for p in range(2):
        o_ref[S_BK0 + p, :, 0:128] = kv[:, 128 * p:128 * (p + 1)].astype(BF16)
        o_ref[S_BK0 + p, :, 128:256] = kr
    o_ref[S_BV] = kv[:, 256:512].astype(BF16)


def _inproj(x, sc, sh, w_all, wuq, wukv, gq, gkv, tabs, nb, t):
    tm = 512
    nt = t // tm
    ntok = nb * t
    ca, sa, cc, scc, cm, sm = tabs
    full = lambda shape: pl.BlockSpec(shape, lambda j, b: tuple(0 for _ in shape))
    tab = lambda w: pl.BlockSpec((tm, w), lambda j, b: (j, 0))
    return pl.pallas_call(
        _inproj_kernel,
        grid=(nt, nb),
        in_specs=[pl.BlockSpec((tm, D_MODEL), lambda j, b: (b * nt + j, 0)),
                  pl.BlockSpec((None, 1, D_MODEL), lambda j, b: (b, 0, 0)),
                  pl.BlockSpec((None, 1, D_MODEL), lambda j, b: (b, 0, 0)),
                  full((D_MODEL, W_ALL_COLS)), full((256, 512)), full((128, 512)),
                  full((1, 256)), full((1, 128)),
                  tab(256), tab(256), tab(256), tab(256), tab(128), tab(128)],
        out_specs=pl.BlockSpec((N_SLOTS, tm, 256), lambda j, b: (0, b * nt + j, 0)),
        out_shape=jax.ShapeDtypeStruct((N_SLOTS, ntok, 256), BF16),
        compiler_params=_cparams(("arbitrary", "arbitrary")),
        name="in_proj",
    )(x, sc, sh, w_all, wuq, wukv, gq, gkv, ca, sa, cc, scc, cm, sm)


def _lane_mask(width, ranges):
    lane = lax.broadcasted_iota(jnp.int32, (1, width), 1)
    m = None
    for lo, hi in ranges:
        r = (lane >= lo) & (lane < hi)
        m = r if m is None else (m | r)
    return m


SOFTMAX_SLAB = 32


def _chain_scratch(tq, kc, width):
    return [pltpu.VMEM((tq, kc), F32), pltpu.VMEM((tq, kc), F32), pltpu.VMEM((tq, kc), BF16),
            pltpu.VMEM((tq, LANES), F32), pltpu.VMEM((tq, LANES), F32), pltpu.VMEM((tq, LANES), F32),
            pltpu.VMEM((tq, width), F32)]


CHAIN_REFS = 7


def _online_attention(qms, k_ref, v_ref, t, kc, chains):
    tq = qms[0].shape[0]
    width = v_ref.shape[-1]
    n_chunks = t // kc
    assert n_chunks % 2 == 0
    nt = (((1,), (1,)), ((), ()))
    for (_, _, _, m_ref, l_ref, _, acc_ref) in chains:
        m_ref[...] = jnp.full(m_ref.shape, NEG_INF, F32)
        l_ref[...] = jnp.zeros(l_ref.shape, F32)
        acc_ref[...] = jnp.zeros(acc_ref.shape, F32)

    def scores(chunk, slot):
        k = k_ref[pl.ds(pl.multiple_of(chunk * kc, kc), kc), :]
        for qm, chain in zip(qms, chains):
            chain[slot][...] = lax.dot_general(qm, k, nt, preferred_element_type=F32)

    def softmax_pv(chunk, slot):
        v = v_ref[pl.ds(pl.multiple_of(chunk * kc, kc), kc), :]
        for chain in chains:
            s_ref = chain[slot]
            (p_ref, m_ref, l_ref, a_ref, acc_ref) = chain[2:]
            for r in range(tq // SOFTMAX_SLAB):
                rows = slice(r * SOFTMAX_SLAB, (r + 1) * SOFTMAX_SLAB)
                s = s_ref[rows, :]
                m_prev = m_ref[rows, :]
                m_new = jnp.maximum(m_prev, jnp.max(s, -1, keepdims=True))
                p = jnp.exp2(s - jnp.concatenate([m_new] * (kc // LANES), axis=1))
                alpha = jnp.exp2(m_prev - m_new)
                l_ref[rows, :] = alpha * l_ref[rows, :] + jnp.sum(p, -1, keepdims=True)
                m_ref[rows, :] = m_new
                a_ref[rows, :] = alpha
                p_ref[rows, :] = p.astype(BF16)
            alpha = jnp.concatenate([a_ref[...]] * (width // LANES), axis=1)
            acc_ref[...] = alpha * acc_ref[...] + jnp.dot(p_ref[...], v,
                                                          preferred_element_type=F32)

    def body(j, _):
        scores(2 * j + 1, 1)
        softmax_pv(2 * j, 0)
        scores(jnp.minimum(2 * j + 2, n_chunks - 1), 0)
        softmax_pv(2 * j + 1, 1)
        return 0

    scores(0, 0)
    lax.fori_loop(0, n_chunks // 2, body, 0)
    outs = []
    for chain in chains:
        l_ref, acc_ref = chain[4], chain[6]
        inv = 1.0 / l_ref[...]
        outs.append(acc_ref[...] * jnp.concatenate([inv] * (width // LANES), axis=1))
    return outs


def _group_mean_sq(x, gmat):
    sq = x * x
    hi, lo = _split_bf16(sq)
    return (jnp.dot(hi, gmat, preferred_element_type=F32)
            + jnp.dot(lo, gmat, preferred_element_type=F32))


def _diff_attn_kernel(lam_ref, q_ref, k_ref, v_ref, g_ref, gmat_ref, o_ref, acc_ref, *scratch, t, kc,
                      out_scale):
    chains = (scratch[:CHAIN_REFS], scratch[CHAIN_REFS:])
    q = q_ref[...]
    lam = lam_ref[0]
    lane = lax.broadcasted_iota(jnp.int32, (1, 256), 1)
    zero = jnp.zeros_like(q)

    def head(h, _):
        qms = []
        for c in range(2):
            lo = (2 * h + c) * DA_DQK
            qms.append(jnp.where((lane >= lo) & (lane < lo + DA_DQK), q, zero))
        o_1, o_2 = _online_attention(qms, k_ref, v_ref, t, kc, chains)
        o_h = o_1 - lam * o_2
        sel = (lane >= h * DA_DV) & (lane < (h + 1) * DA_DV)
        acc_ref[...] = jnp.where(sel, o_h, acc_ref[...])
        return 0

    acc_ref[...] = jnp.zeros_like(acc_ref)
    lax.fori_loop(0, DA_HEADS, head, 0)
    o = acc_ref[...]
    ms = _group_mean_sq(o, gmat_ref[...])
    o_ref[...] = (o * lax.rsqrt(ms + RMS_EPS) * g_ref[...] * out_scale).astype(o_ref.dtype)


def _diff_attention(proj, lam, g_tiled, gmat, nb, t, out_scale):
    tq, kc = 512, 512
    nq = t // tq
    kern = functools.partial(_diff_attn_kernel, t=t, kc=kc, out_scale=out_scale)
    return pl.pallas_call(
        kern,
        grid=(nb, nq),
        in_specs=[pl.BlockSpec(memory_space=pltpu.SMEM),
                  pl.BlockSpec((None, tq, 256), lambda b, i: (S_AQ, b * nq + i, 0)),
                  pl.BlockSpec((None, t, 256), lambda b, i: (S_AK, b, 0)),
                  pl.BlockSpec((None, t, 256), lambda b, i: (S_AV, b, 0)),
                  pl.BlockSpec((1, 256), lambda b, i: (0, 0)),
                  pl.BlockSpec((256, 256), lambda b, i: (0, 0))],
        out_specs=pl.BlockSpec((tq, 256), lambda b, i: (b * nq + i, 0)),
        out_shape=jax.ShapeDtypeStruct((nb * t, 256), BF16),
        scratch_shapes=[pltpu.VMEM((tq, 256), F32)] + 2 * _chain_scratch(tq, kc, 256),
        compiler_params=_cparams(("arbitrary", "arbitrary")),
        name="diff_attn",
    )(lam, proj, proj, proj, g_tiled, gmat)


def _mla_attn_kernel(q_ref, k_ref, v_ref, o_ref, acc_ref, *scratch, t, kc):
    chains = (scratch[:CHAIN_REFS], scratch[CHAIN_REFS:])
    lane = lax.broadcasted_iota(jnp.int32, (1, 256), 1)

    def pair(p, _):
        q = q_ref[p]
        qms = []
        for j in range(2):
            nope = (lane >= j * MLA_NOPE) & (lane < (j + 1) * MLA_NOPE)
            rope = (lane >= 128 + j * MLA_ROPE) & (lane < 128 + (j + 1) * MLA_ROPE)
            qms.append(jnp.where(nope | rope, q, jnp.zeros_like(q)))
        outs = _online_attention(qms, k_ref.at[p], v_ref, t, kc, chains)
        out = acc_ref[...]
        for j in range(2):
            h = 2 * p + j
            sel = (lane >= h * MLA_DV) & (lane < (h + 1) * MLA_DV)
            out = jnp.where(sel, outs[j], out)
        acc_ref[...] = out
        return 0

    acc_ref[...] = jnp.zeros_like(acc_ref)
    lax.fori_loop(0, MLA_HEADS // 2, pair, 0)
    o_ref[...] = acc_ref[...].astype(o_ref.dtype)


def _mla_attention(proj, nb, t):
    tq, kc = 512, 512
    nq = t // tq
    kern = functools.partial(_mla_attn_kernel, t=t, kc=kc)
    return pl.pallas_call(
        kern,
        grid=(nb, nq),
        in_specs=[pl.BlockSpec((2, tq, 256), lambda b, i: (S_BQ0 // 2, b * nq + i, 0)),
                  pl.BlockSpec((2, t, 256), lambda b, i: (S_BK0 // 2, b, 0)),
                  pl.BlockSpec((None, t, 256), lambda b, i: (S_BV, b, 0))],
        out_specs=pl.BlockSpec((tq, 256), lambda b, i: (b * nq + i, 0)),
        out_shape=jax.ShapeDtypeStruct((nb * t, 256), BF16),
        scratch_shapes=[pltpu.VMEM((tq, 256), F32)] + 2 * _chain_scratch(tq, kc, 256),
        compiler_params=_cparams(("arbitrary", "arbitrary")),
        name="mla_attn",
    )(proj, proj, proj)


DIL_REACH = max(w // 2 for w, _ in DIL_PATTERNS)


def _dil_attn_kernel(q_ref, k_ref, v_ref, o_ref, bias_ref, acc_ref, *, t, tq, band):
    i = pl.program_id(1)

    def band_start(blk):
        return jnp.clip(blk * tq - DIL_REACH, 0, t - band)

    t0 = i * tq
    start = pl.multiple_of(band_start(i), tq)
    shift = start - t0
    prev_shift = band_start(i - 1) - (i - 1) * tq

    @pl.when((i == 0) | (shift != prev_shift))
    def _():
        qi = lax.broadcasted_iota(jnp.int32, (tq, band), 0)
        kj = lax.broadcasted_iota(jnp.int32, (tq, band), 1)
        delta = kj - qi + shift
        ad = jnp.abs(delta)
        cnt = jnp.zeros((tq, band), F32)
        for window, dil in DIL_PATTERNS:
            ok = (ad <= window // 2) & ((delta & (dil - 1)) == 0)
            cnt = cnt + jnp.where(ok, 1.0, 0.0)
        bias_ref[...] = jnp.where(cnt > 2.5, math.log2(3.0),
                                  jnp.where(cnt > 1.5, 1.0,
                                            jnp.where(cnt > 0.5, 0.0, NEG_INF)))

    q = q_ref[...]
    lane = lax.broadcasted_iota(jnp.int32, (1, 256), 1)
    kb = k_ref.at[pl.ds(start, band), :]
    vb = v_ref.at[pl.ds(start, band), :]

    def pair(hp, _):
        out = acc_ref[...]
        for j in range(2):
            h = 2 * hp + j
            sel = (lane >= h * DIL_DH) & (lane < (h + 1) * DIL_DH)
            qm = jnp.where(sel, q, jnp.zeros_like(q))
            s = lax.dot_general(qm, kb[...], (((1,), (1,)), ((), ())),
                                preferred_element_type=F32)
            s = s + bias_ref[...]
            m = jnp.max(s, -1, keepdims=True)
            p = jnp.exp2(s - m)
            l = jnp.sum(p, -1, keepdims=True)
            o_h = jnp.dot(p.astype(BF16), vb[...], preferred_element_type=F32) * (1.0 / l)
            out = jnp.where(sel, o_h, out)
        acc_ref[...] = out
        return 0

    acc_ref[...] = jnp.zeros_like(acc_ref)
    lax.fori_loop(0, DIL_HEADS // 2, pair, 0)
    o_ref[...] = acc_ref[...].astype(o_ref.dtype)


def _dil_attention(proj, nb, t):
    tq = 256
    band = min(t, tq + 2 * DIL_REACH)
    nq = t // tq
    kern = functools.partial(_dil_attn_kernel, t=t, tq=tq, band=band)
    return pl.pallas_call(
        kern,
        grid=(nb, nq),
        in_specs=[pl.BlockSpec((None, tq, 256), lambda b, i: (S_CQ, b * nq + i, 0)),
                  pl.BlockSpec((None, t, 256), lambda b, i: (S_CK, b, 0)),
                  pl.BlockSpec((None, t, 256), lambda b, i: (S_CV, b, 0))],
        out_specs=pl.BlockSpec((tq, 256), lambda b, i: (b * nq + i, 0)),
        out_shape=jax.ShapeDtypeStruct((nb * t, 256), BF16),
        scratch_shapes=[pltpu.VMEM((tq, band), F32), pltpu.VMEM((tq, 256), F32)],
        compiler_params=_cparams(("arbitrary", "arbitrary")),
        name="dil_attn",
    )(proj, proj, proj)


def _na_bias_table(rpb):
    c = jnp.arange(GRID_W)
    cs = jnp.clip(c - NA_KC // 2, 0, GRID_W - NA_KC)
    colmask = (c[None, :] >= cs[:, None]) & (c[None, :] < cs[:, None] + NA_KC)
    ci = jnp.clip(c[None, :] - c[:, None] + (NA_KC - 1), 0, 2 * NA_KC - 2)
    si = jnp.arange(NA_KR)
    j = jnp.arange(NA_KR)
    ri = si[:, None] + j[None, :]
    b = rpb[:, ri[:, None, :, None], ci[None, :, None, :]].astype(F32)
    b = jnp.where(colmask[None, None, :, None, :], b * LOG2E, NEG_INF)
    b = b.transpose(1, 0, 2, 3, 4)
    return b.reshape(NA_KR, rpb.shape[0] * GRID_W, NA_KR * GRID_W)


def _na_attn_kernel(q_ref, k_ref, v_ref, tb_ref, o_ref, *, rows, rg):
    g = pl.program_id(1)
    lane = lax.broadcasted_iota(jnp.int32, (1, 256), 1)
    nk = NA_KR * GRID_W
    sels = [(lane >= h * NA_DH) & (lane < (h + 1) * NA_DH) for h in range(NA_HEADS)]

    def row(r, _):
        grow = g * rg + r
        rs = jnp.clip(grow - NA_KR // 2, 0, rows - NA_KR)
        si = rs - grow + (NA_KR - 1)
        q = q_ref[pl.ds(pl.multiple_of(r * GRID_W, GRID_W), GRID_W), :]
        koff = pl.multiple_of(rs * GRID_W, GRID_W)
        kb = k_ref[pl.ds(koff, nk), :]
        vb = v_ref[pl.ds(koff, nk), :]
        q4 = jnp.concatenate([jnp.where(sel, q, jnp.zeros_like(q)) for sel in sels], axis=0)
        s = lax.dot_general(q4, kb, (((1,), (1,)), ((), ())), preferred_element_type=F32)
        s = s + tb_ref[si]
        m = jnp.max(s, -1, keepdims=True)
        p = jnp.exp2(s - m)
        l = jnp.sum(p, -1, keepdims=True)
        o4 = jnp.dot(p.astype(BF16), vb, preferred_element_type=F32) * (1.0 / l)
        out = jnp.zeros((GRID_W, 256), F32)
        for h, sel in enumerate(sels):
            out = jnp.where(sel, o4[h * GRID_W:(h + 1) * GRID_W], out)
        o_ref[pl.ds(pl.multiple_of(r * GRID_W, GRID_W), GRID_W), :] = out.astype(o_ref.dtype)
        return 0

    lax.fori_loop(0, rg, row, 0, unroll=2)


def _na_attention(proj, tb, nb, t):
    rows = t // GRID_W
    assert rows >= NA_KR
    rg = 8
    ng = rows // rg
    tq = rg * GRID_W
    kern = functools.partial(_na_attn_kernel, rows=rows, rg=rg)
    return pl.pallas_call(
        kern,
        grid=(nb, ng),
        in_specs=[pl.BlockSpec((None, tq, 256), lambda b, i: (S_DQ, b * ng + i, 0)),
                  pl.BlockSpec((None, t, 256), lambda b, i: (S_DK, b, 0)),
                  pl.BlockSpec((None, t, 256), lambda b, i: (S_DV, b, 0)),
                  pl.BlockSpec(tb.shape, lambda b, i: (0, 0, 0))],
        out_specs=pl.BlockSpec((tq, 256), lambda b, i: (b * ng + i, 0)),
        out_shape=jax.ShapeDtypeStruct((nb * t, 256), BF16),
        compiler_params=_cparams(("arbitrary", "arbitrary")),
        name="na_attn",
    )(proj, proj, proj, tb)


def _layer_norm(y, g, b):
    mu = jnp.mean(y, -1, keepdims=True)
    yc = y - mu
    var = jnp.mean(yc * yc, -1, keepdims=True)
    return yc * lax.rsqrt(var + LN_EPS) * g + b


def _outproj_kernel(oa_ref, ob_ref, oc_ref, od_ref, x_ref, g1_ref, sc2_ref, sh2_ref, w_ref,
                    lg_ref, lb_ref, wr_ref, x1_ref, h2_ref, aff_ref):
    m = jnp.dot(oa_ref[...], w_ref[0:256, :], preferred_element_type=F32)
    m += jnp.dot(ob_ref[...], w_ref[256:512, :], preferred_element_type=F32)
    m += jnp.dot(oc_ref[...], w_ref[512:768, :], preferred_element_type=F32)
    m += jnp.dot(od_ref[...], w_ref[768:1024, :], preferred_element_type=F32)
    y = DEEPNORM_ALPHA * x_ref[...] + (1.0 + g1_ref[...]) * m
    x1 = _layer_norm(y, lg_ref[...], lb_ref[...])
    x1_ref[...] = x1
    h2 = x1 * (1.0 + sc2_ref[...]) + sh2_ref[...]
    h2_ref[...] = h2.astype(BF16)
    h_hi, h_lo = _split_bf16(h2)
    w_hi, w_lo = _split_bf16(wr_ref[...])
    nt = (((1,), (1,)), ((), ()))
    lg = lax.dot_general(w_hi, h_hi, nt, preferred_element_type=F32)
    lg += lax.dot_general(w_hi, h_lo, nt, preferred_element_type=F32)
    lg += lax.dot_general(w_lo, h_hi, nt, preferred_element_type=F32)
    lg = lg - jnp.max(lg, 0, keepdims=True)
    e = jnp.exp(lg)
    aff_ref[...] = e / jnp.sum(e, 0, keepdims=True)


def _outproj(oa, ob, oc, od, x, g1, sc2, sh2, w_out, ln_g, ln_b, w_router_t, nb, t):
    tm = 512
    nt = t // tm
    ntok = nb * t
    tok = lambda w: pl.BlockSpec((tm, w), lambda i: (i, 0))
    per_b = pl.BlockSpec((None, 1, D_MODEL), lambda i: (i // nt, 0, 0))
    full = lambda shape: pl.BlockSpec(shape, lambda i: tuple(0 for _ in shape))
    return pl.pallas_call(
        _outproj_kernel,
        grid=(ntok // tm,),
        in_specs=[tok(256), tok(256), tok(256), tok(256), tok(D_MODEL), per_b, per_b, per_b,
                  full((D_MODEL, D_MODEL)), full((1, D_MODEL)), full((1, D_MODEL)),
                  full((N_EXPERTS, D_MODEL))],
        out_specs=[tok(D_MODEL), tok(D_MODEL),
                   pl.BlockSpec((N_EXPERTS, tm), lambda i: (0, i))],
        out_shape=[jax.ShapeDtypeStruct((ntok, D_MODEL), F32),
                   jax.ShapeDtypeStruct((ntok, D_MODEL), BF16),
                   jax.ShapeDtypeStruct((N_EXPERTS, ntok), F32)],
        compiler_params=_cparams(("arbitrary",)),
        name="out_proj",
    )(oa, ob, oc, od, x, g1, sc2, sh2, w_out, ln_g, ln_b, w_router_t)


def _expert_kernel(x_ref, wg_ref, wu_ref, wd_ref, o_ref, acc_ref):
    f = pl.program_id(2)
    x = x_ref[...]
    g = jnp.dot(x, wg_ref[...], preferred_element_type=F32)
    u = jnp.dot(x, wu_ref[...], preferred_element_type=F32)
    hmid = (g * (1.0 / (1.0 + jnp.exp(-g))) * u).astype(BF16)
    part = jnp.dot(hmid, wd_ref[...], preferred_element_type=F32)

    @pl.when(f == 0)
    def _():
        acc_ref[...] = part

    @pl.when(f > 0)
    def _():
        acc_ref[...] += part

    @pl.when(f == pl.num_programs(2) - 1)
    def _():
        o_ref[...] = acc_ref[...].astype(o_ref.dtype)


def _experts(xe, slots, wg, wu, wd):
    ne, _, d = xe.shape
    tm = math.gcd(slots, 1024)
    tf = 256
    nf = D_FF // tf
    return pl.pallas_call(
        _expert_kernel,
        grid=(ne, slots // tm, nf),
        in_specs=[pl.BlockSpec((None, tm, d), lambda e, m, f: (e, m, 0)),
                  pl.BlockSpec((None, d, tf), lambda e, m, f: (e, 0, f)),
                  pl.BlockSpec((None, d, tf), lambda e, m, f: (e, 0, f)),
                  pl.BlockSpec((None, tf, d), lambda e, m, f: (e, f, 0))],
        out_specs=pl.BlockSpec((None, tm, d), lambda e, m, f: (e, m, 0)),
        out_shape=jax.ShapeDtypeStruct((ne, slots, d), BF16),
        scratch_shapes=[pltpu.VMEM((tm, d), F32)],
        compiler_params=_cparams(("arbitrary", "arbitrary", "arbitrary")),
        name="expert_ffn",
    )(xe, wg, wu, wd)


MOE_TM = 512
MOE_WIN = 128
ROW_ALIGN = 16


def _select_kernel(aff_ref, o_ref, *, cap):
    a = aff_ref[...]
    keys = lax.bitcast_convert_type(a, jnp.int32)
    ne, n = a.shape
    capf = float(cap)

    def count(mask):
        return jnp.sum(jnp.where(mask, 1.0, 0.0), axis=1, keepdims=True)

    def key_bit(b, thr):
        cand = thr | lax.shift_left(jnp.int32(1), 30 - b)
        return jnp.where(count(keys >= cand) >= capf, cand, thr)

    thr = lax.fori_loop(0, 31, key_bit, jnp.zeros((ne, 1), jnp.int32))
    above = keys > thr
    need = capf - count(above)
    idx = lax.broadcasted_iota(jnp.int32, (ne, n), 1)
    tie_idx = jnp.where(keys == thr, idx, jnp.int32(2 ** 30))
    nbits = max(1, (n - 1).bit_length())

    def idx_bit(b, j):
        cand = j | lax.shift_left(jnp.int32(1), nbits - 1 - b)
        return jnp.where(count(tie_idx < cand) < need, cand, j)

    j = lax.fori_loop(0, nbits, idx_bit, jnp.zeros((ne, 1), jnp.int32))
    sel = above | (tie_idx <= j)
    o_ref[...] = jnp.where(sel, a, -1.0)


def _select(aff, cap):
    ne, n = aff.shape
    return pl.pallas_call(
        functools.partial(_select_kernel, cap=cap),
        out_shape=jax.ShapeDtypeStruct((ne, n), F32),
        compiler_params=pltpu.CompilerParams(vmem_limit_bytes=VMEM_LIMIT),
        name="ec_select",
    )(aff)


def _routing_tables(gs, group_tokens):
    ne = gs.shape[0]
    a_l, off_l, cnt_l, lim_l = [], [], [], []
    tok0, slot0 = 0, 0
    for n in group_tokens:
        cap = EC_FACTOR * n // N_EXPERTS
        nt = n // MOE_TM
        sel = lax.slice_in_dim(gs, tok0, tok0 + n, axis=1) >= 0
        counts = jnp.sum(sel.reshape(ne, nt, MOE_TM), axis=-1, dtype=jnp.int32)
        s0 = slot0 + jnp.cumsum(counts, axis=1) - counts
        a = (s0 // ROW_ALIGN) * ROW_ALIGN
        a_l.append(a)
        off_l.append(s0 - a)
        cnt_l.append(counts)
        lim_l.append(jnp.full((nt,), slot0 + cap - MOE_WIN, jnp.int32))
        tok0 += n
        slot0 += cap
    a = jnp.concatenate(a_l, axis=1).T
    off = jnp.concatenate(off_l, axis=1).T
    end = off + jnp.concatenate(cnt_l, axis=1).T
    rounds = (end + MOE_WIN - 1) // MOE_WIN
    gp = (end // ROW_ALIGN) * ROW_ALIGN
    return dict(a=a.reshape(-1).astype(jnp.int32), nr=rounds.reshape(-1).astype(jnp.int32),
                nrounds=jnp.max(rounds, axis=1).astype(jnp.int32), lim=jnp.concatenate(lim_l),
                off_col=off.astype(F32)[:, :, None], off_row=off.astype(F32)[:, None, :],
                off16=jnp.repeat(off.astype(F32), ROW_ALIGN, axis=1)[:, :, None],
                gp16=jnp.repeat(gp.astype(F32), ROW_ALIGN, axis=1)[:, :, None],
                total=slot0)


def _dispatch_kernel(a_tab, nrounds, nr_tab, gs_ref, off_ref, off16_ref, gp16_ref, x_ref, u_ref, xe_ref,
                     pos_ref, c_ref, c2_ref, stage_ref, carry_ref, sem, rc_ref):
    j = pl.program_id(0)
    ne, tm = gs_ref.shape
    d = x_ref.shape[1]
    win = MOE_WIN

    @pl.when(j == 0)
    def _():
        carry_ref[...] = jnp.zeros_like(carry_ref)
        rc_ref[0] = 0

    sel = gs_ref[...] >= 0.0
    rank = jnp.dot(jnp.where(sel, 1.0, 0.0).astype(BF16), u_ref[...], preferred_element_type=F32)
    pos_ref[...] = jnp.where(sel, rank + off_ref[...], -1.0)

    def wait_round(slot):
        def one(_, c):
            pltpu.make_async_copy(stage_ref.at[slot, pl.ds(0, win)], xe_ref.at[0, pl.ds(0, win)],
                                  sem.at[slot]).wait()
            return c
        lax.fori_loop(0, rc_ref[1 + slot], one, 0)

    def round_body(r, _):
        k = lax.broadcasted_iota(jnp.int32, (win, tm), 0).astype(F32) + (r * win).astype(F32)
        for e in range(ne):
            c_ref[e * win:(e + 1) * win, :] = jnp.where(pos_ref[e:e + 1, :] == k, 1.0, 0.0).astype(BF16)
        slot = rc_ref[0] % 2
        for nb in range(d // MXU_DIM):
            cols = slice(nb * MXU_DIM, (nb + 1) * MXU_DIM)
            stage_ref[slot, :, cols] = jnp.dot(c_ref[...], x_ref[:, cols],
                                               preferred_element_type=F32).astype(BF16)

        @pl.when(r == 0)
        def _():
            k16 = lax.broadcasted_iota(jnp.int32, (ROW_ALIGN, 1), 0).astype(F32)
            for e in range(ne):
                keep = k16 < off16_ref[e * ROW_ALIGN:(e + 1) * ROW_ALIGN, :]
                rows = pl.ds(e * win, ROW_ALIGN)
                stage_ref[slot, rows, :] = jnp.where(
                    keep, carry_ref[e * ROW_ALIGN:(e + 1) * ROW_ALIGN, :], stage_ref[slot, rows, :])

        @pl.when(rc_ref[0] > 0)
        def _():
            wait_round(1 - slot)

        rc_ref[1 + slot] = 0
        for e in range(ne):
            @pl.when(r < nr_tab[j * ne + e])
            def _():
                dst = pl.multiple_of(a_tab[j * ne + e] + r * win, ROW_ALIGN)
                pltpu.make_async_copy(stage_ref.at[slot, pl.ds(e * win, win)],
                                      xe_ref.at[e, pl.ds(dst, win)], sem.at[slot]).start()
                rc_ref[1 + slot] = rc_ref[1 + slot] + 1
        rc_ref[0] = rc_ref[0] + 1
        return 0

    lax.fori_loop(0, nrounds[j], round_body, 0)

    k16 = lax.broadcasted_iota(jnp.int32, (ROW_ALIGN, 1), 0).astype(F32)
    for e in range(ne):
        rows = slice(e * ROW_ALIGN, (e + 1) * ROW_ALIGN)
        c2_ref[rows, :] = jnp.where(pos_ref[e:e + 1, :] == gp16_ref[rows, :] + k16, 1.0, 0.0).astype(BF16)
    kk = jnp.concatenate([k16] * ne, axis=0)
    keep_old = (gp16_ref[...] == 0.0) & (kk < off16_ref[...])
    for nb in range(d // MXU_DIM):
        cols = slice(nb * MXU_DIM, (nb + 1) * MXU_DIM)
        new = jnp.dot(c2_ref[...], x_ref[:, cols], preferred_element_type=F32).astype(BF16)
        carry_ref[:, cols] = jnp.where(keep_old, carry_ref[:, cols], new)

    @pl.when((j == pl.num_programs(0) - 1) & (rc_ref[0] > 0))
    def _():
        wait_round((rc_ref[0] - 1) % 2)


def _dispatch(gs, h2, tabs, u_mat):
    ne, ntok = gs.shape
    d = h2.shape[1]
    nt = ntok // MOE_TM
    rows = tabs['total'] + MOE_WIN
    grid_spec = pltpu.PrefetchScalarGridSpec(
        num_scalar_prefetch=3,
        grid=(nt,),
        in_specs=[pl.BlockSpec((ne, MOE_TM), lambda j, *_: (0, j)),
                  pl.BlockSpec((None, ne, 1), lambda j, *_: (j, 0, 0)),
                  pl.BlockSpec((None, ne * ROW_ALIGN, 1), lambda j, *_: (j, 0, 0)),
                  pl.BlockSpec((None, ne * ROW_ALIGN, 1), lambda j, *_: (j, 0, 0)),
                  pl.BlockSpec((MOE_TM, d), lambda j, *_: (j, 0)),
                  pl.BlockSpec((MOE_TM, MOE_TM), lambda j, *_: (0, 0))],
        out_specs=pl.BlockSpec(memory_space=pl.ANY),
        scratch_shapes=[pltpu.VMEM((ne, MOE_TM), F32),
                        pltpu.VMEM((ne * MOE_WIN, MOE_TM), BF16),
                        pltpu.VMEM((ne * ROW_ALIGN, MOE_TM), BF16),
                        pltpu.VMEM((2, ne * MOE_WIN, d), BF16),
                        pltpu.VMEM((ne * ROW_ALIGN, d), BF16),
                        pltpu.SemaphoreType.DMA((2,)),
                        pltpu.SMEM((3,), jnp.int32)])
    return pl.pallas_call(
        _dispatch_kernel,
        grid_spec=grid_spec,
        out_shape=jax.ShapeDtypeStruct((ne, rows, d), BF16),
        compiler_params=_cparams(("arbitrary",)),
        name="ec_dispatch",
    )(tabs['a'], tabs['nrounds'], tabs['nr'], gs, tabs['off_col'], tabs['off16'], tabs['gp16'], h2,
      u_mat)


def _combine_kernel(a_tab, nrounds, lim_tab, gs_ref, off_ref, l_ref, ye_ref, x1_ref, g2_ref, lg_ref,
                    lb_ref, o_ref, p_ref, y_ref, acc_ref, sem):
    j = pl.program_id(0)
    tm, ne = gs_ref.shape
    d = x1_ref.shape[1]
    win = MOE_WIN
    gs = gs_ref[...]
    sel = gs >= 0.0
    rank = jnp.dot(l_ref[...], jnp.where(sel, 1.0, 0.0).astype(BF16), preferred_element_type=F32)
    pos = jnp.where(sel, rank + off_ref[...], -1.0)
    gate = jnp.where(sel, gs, 0.0)
    acc_ref[...] = jnp.zeros_like(acc_ref)
    lim = lim_tab[j]

    def round_body(r, _):
        copies = []
        shifts = []
        for e in range(ne):
            want = a_tab[j * ne + e] + r * win
            src = pl.multiple_of(jnp.minimum(want, lim), ROW_ALIGN)
            shifts.append((want - src).astype(F32))
            cp = pltpu.make_async_copy(ye_ref.at[e, pl.ds(src, win)], y_ref.at[pl.ds(e * win, win)],
                                       sem.at[0])
            cp.start()
            copies.append(cp)
        base = (r * win).astype(F32)
        k = lax.broadcasted_iota(jnp.int32, (tm, win), 1).astype(F32) + base
        for e in range(ne):
            pe = pos[:, e:e + 1]
            pe = jnp.where(pe >= base, pe + shifts[e], -1.0)
            pcol = jnp.broadcast_to(pe, (tm, win))
            gcol = jnp.broadcast_to(gate[:, e:e + 1], (tm, win))
            p_ref[:, e * win:(e + 1) * win] = jnp.where(pcol == k, gcol, 0.0).astype(BF16)
        for cp in copies:
            cp.wait()
        for nb in range(d // MXU_DIM):
            cols = slice(nb * MXU_DIM, (nb + 1) * MXU_DIM)
            acc_ref[:, cols] += jnp.dot(p_ref[...], y_ref[:, cols], preferred_element_type=F32)
        return 0

    lax.fori_loop(0, nrounds[j], round_body, 0)
    y = DEEPNORM_ALPHA * x1_ref[...] + (1.0 + g2_ref[...]) * acc_ref[...]
    o_ref[...] = _layer_norm(y, lg_ref[...], lb_ref[...])


def _combine_postnorm(gs_tok, ye, tabs, l_mat, x1, g2, ln_g, ln_b, t):
    ntok, ne = gs_tok.shape
    d = x1.shape[1]
    nt = ntok // MOE_TM
    tiles_per_seq = t // MOE_TM
    grid_spec = pltpu.PrefetchScalarGridSpec(
        num_scalar_prefetch=3,
        grid=(nt,),
        in_specs=[pl.BlockSpec((MOE_TM, ne), lambda j, *_: (j, 0)),
                  pl.BlockSpec((None, 1, ne), lambda j, *_: (j, 0, 0)),
                  pl.BlockSpec((MOE_TM, MOE_TM), lambda j, *_: (0, 0)),
                  pl.BlockSpec(memory_space=pl.ANY),
                  pl.BlockSpec((MOE_TM, d), lambda j, *_: (j, 0)),
                  pl.BlockSpec((None, 1, d), lambda j, *_: (j // tiles_per_seq, 0, 0)),
                  pl.BlockSpec((1, d), lambda j, *_: (0, 0)),
                  pl.BlockSpec((1, d), lambda j, *_: (0, 0))],
        out_specs=pl.BlockSpec((MOE_TM, d), lambda j, *_: (j, 0)),
        scratch_shapes=[pltpu.VMEM((MOE_TM, ne * MOE_WIN), BF16),
                        pltpu.VMEM((ne * MOE_WIN, d), BF16),
                        pltpu.VMEM((MOE_TM, d), F32),
                        pltpu.SemaphoreType.DMA((1,))])
    return pl.pallas_call(
        _combine_kernel,
        grid_spec=grid_spec,
        out_shape=jax.ShapeDtypeStruct((ntok, d), F32),
        compiler_params=_cparams(("arbitrary",)),
        name="ec_combine",
    )(tabs['a'], tabs['nrounds'], tabs['lim'], gs_tok, tabs['off_row'], l_mat, ye, x1, g2, ln_g, ln_b)


def _prep_w_in(w_in_l):
    sizes = (256, 256, 256, MLA_Q_RANK, MLA_KV_RANK, MLA_ROPE, 256, 256, 256, 256, 256, 256)
    offs = np.concatenate([[0], np.cumsum(sizes)])
    part = [w_in_l[:, offs[i]:offs[i + 1]] for i in range(len(sizes))]
    a_q, a_k, a_v, b_cq, b_ckv, b_kr, c_q, c_k, c_v, d_q, d_k, d_v = part
    d = w_in_l.shape[0]
    zeros = lambda n: jnp.zeros((d, n), w_in_l.dtype)
    cols = [a_q, a_k, a_v, c_q, c_k, c_v, d_q, d_k, d_v,
            b_cq, zeros(256 - MLA_Q_RANK), b_ckv, b_kr, b_kr, zeros(128 - 2 * MLA_ROPE)]
    return jnp.concatenate(cols, axis=1).astype(BF16)


def _prep_w_uq(w_uq_l):
    hd = MLA_NOPE + MLA_ROPE
    nope = [w_uq_l[:, h * hd:h * hd + MLA_NOPE] for h in range(MLA_HEADS)]
    rope = [w_uq_l[:, h * hd + MLA_NOPE:(h + 1) * hd] for h in range(MLA_HEADS)]
    z = jnp.zeros((w_uq_l.shape[0], 256 - 2 * hd), w_uq_l.dtype)
    cols = []
    for p in range(2):
        cols += [nope[2 * p], nope[2 * p + 1], rope[2 * p], rope[2 * p + 1], z]
    w = jnp.concatenate(cols, axis=1)
    w = jnp.concatenate([w, jnp.zeros((256 - MLA_Q_RANK, w.shape[1]), w.dtype)], axis=0)
    return w.astype(BF16)


def _prep_w_ukv(w_ukv_l):
    hd = MLA_NOPE + MLA_DV
    kn = [w_ukv_l[:, h * hd:h * hd + MLA_NOPE] for h in range(MLA_HEADS)]
    vv = [w_ukv_l[:, h * hd + MLA_NOPE:(h + 1) * hd] for h in range(MLA_HEADS)]
    return jnp.concatenate(kn + vv, axis=1).astype(BF16)


def _select_groups(aff_t, group_tokens):
    parts, off = [], 0
    for n in group_tokens:
        parts.append(_select(lax.slice_in_dim(aff_t, off, off + n, axis=1),
                             EC_FACTOR * n // N_EXPERTS))
        off += n
    return jnp.concatenate(parts, axis=1)


def _trunk(x, c, group_tokens, nb, t, p):
    ntok = nb * t
    tabs = (_rope_tables(t, 256, DA_DQK, DA_DQK // ROPE_FRACTION, ROPE_THETA)
            + _rope_tables(t, 256, DIL_DH, DIL_DH // ROPE_FRACTION, ROPE_THETA)
            + _rope_tables(t, 128, MLA_ROPE, MLA_ROPE, MLA_ROPE_THETA))
    gmat = jnp.asarray(np.kron(np.eye(4), np.full((64, 64), 1.0 / 64)), BF16)
    ti = jnp.arange(MOE_TM)
    u_mat = (ti[:, None] < ti[None, :]).astype(BF16)
    l_mat = (ti[None, :] < ti[:, None]).astype(BF16)
    for l in range(DEPTH):
        mod = _modulation(c, p['w_ada'][l], p['b_ada'][l])
        sh1, sc1, g1, sh2, sc2, g2 = [m.reshape(nb, 1, D_MODEL) for m in jnp.split(mod, 6, axis=-1)]
        gq = jnp.concatenate([p['q_norm_g'][l], jnp.zeros((256 - MLA_Q_RANK,), F32)]).reshape(1, 256)
        gkv = p['kv_norm_g'][l].reshape(1, 128)
        proj = _inproj(x, sc1, sh1, _prep_w_in(p['w_in'][l]), _prep_w_uq(p['w_uq'][l]),
                       _prep_w_ukv(p['w_ukv'][l]), gq, gkv, tabs, nb, t)
        lam_init = 0.8 - 0.6 * math.exp(-0.3 * l)
        lam = (jnp.exp(jnp.sum(p['da_lq1'][l] * p['da_lk1'][l]))
               - jnp.exp(jnp.sum(p['da_lq2'][l] * p['da_lk2'][l])) + lam_init).reshape(1)
        g_sub = jnp.tile(p['da_subln_g'][l], DA_HEADS).reshape(1, 256)
        oa = _diff_attention(proj, lam, g_sub, gmat, nb, t, 1.0 - lam_init)
        ob = _mla_attention(proj, nb, t)
        oc = _dil_attention(proj, nb, t)
        od = _na_attention(proj, _na_bias_table(p['na_rpb'][l]), nb, t)
        x1, h2, aff_t = _outproj(oa, ob, oc, od, x, g1, sc2, sh2, p['w_out'][l].astype(BF16),
                                 p['ln1_g'][l].reshape(1, -1), p['ln1_b'][l].reshape(1, -1),
                                 p['w_router'][l].T, nb, t)
        wg = p['w_e_gate'][l].astype(BF16)
        wu = p['w_e_up'][l].astype(BF16)
        wd = p['w_e_down'][l].astype(BF16)
        gs = _select_groups(aff_t, group_tokens)
        rt = _routing_tables(gs, group_tokens)
        xe = _dispatch(gs, h2, rt, u_mat)
        ye = _experts(xe, rt['total'], wg, wu, wd)
        x = _combine_postnorm(gs.T, ye, rt, l_mat, x1, g2, p['ln2_g'][l].reshape(1, -1),
                              p['ln2_b'][l].reshape(1, -1), t)
    return x


def kernel(x_prompt, x_sample, c_prompt, c_sample, w_in, w_uq, w_ukv, q_norm_g, kv_norm_g, da_lq1,
           da_lk1, da_lq2, da_lk2, da_subln_g, na_rpb, w_out, w_ada, b_ada, ln1_g, ln1_b, ln2_g,
           ln2_b, w_router, w_e_gate, w_e_up, w_e_down):
    p = dict(w_in=w_in, w_uq=w_uq, w_ukv=w_ukv, q_norm_g=q_norm_g, kv_norm_g=kv_norm_g,
             da_lq1=da_lq1, da_lk1=da_lk1, da_lq2=da_lq2, da_lk2=da_lk2, da_subln_g=da_subln_g,
             na_rpb=na_rpb, w_out=w_out, w_ada=w_ada, b_ada=b_ada, ln1_g=ln1_g, ln1_b=ln1_b,
             ln2_g=ln2_g, ln2_b=ln2_b, w_router=w_router, w_e_gate=w_e_gate, w_e_up=w_e_up,
             w_e_down=w_e_down)
    bp, t, d = x_prompt.shape
    bs = x_sample.shape[0]
    assert x_sample.shape[1] == t
    nb = bp + bs
    x = jnp.concatenate([x_prompt.reshape(bp * t, d), x_sample.reshape(bs * t, d)], axis=0)
    c = jnp.concatenate([c_prompt, c_sample], axis=0)
    y = _trunk(x, c, (bp * t, bs * t), nb, t, p)
    return y[:bp * t].reshape(bp, t, d), y[bp * t:].reshape(bs, t, d)
```

```python
import functools
import math

import jax
import jax.numpy as jnp
import numpy as np
from jax import lax
from jax.experimental import pallas as pl
from jax.experimental.pallas import tpu as pltpu

F32 = jnp.float32
BF16 = jnp.bfloat16

D_MODEL = 1024
DEPTH = 2
GRID_W = 64
GROUP_W = 256
DA_HEADS, DA_DV, DA_DQK = 4, 64, 32
MLA_HEADS, MLA_Q_RANK, MLA_KV_RANK, MLA_NOPE, MLA_ROPE, MLA_DV = 4, 192, 128, 64, 32, 64
MLA_ROPE_THETA = 10000.0
DIL_HEADS, DIL_DH = 4, 64
DIL_PATTERNS = ((128, 1), (512, 4), (2048, 16))
NA_HEADS, NA_DH, NA_KR, NA_KC = 4, 64, 8, 16
ROPE_THETA = 500000.0
ROPE_FRACTION = 4
N_EXPERTS = 16
EC_FACTOR = 2
D_FF = 2816
DEEPNORM_ALPHA = (2.0 * DEPTH) ** 0.25
NEG_INF = -1e30
LOG2E = math.log2(math.e)
LN_EPS = 1e-5
RMS_EPS = 1e-6

LANES = 128
MXU_DIM = 256
VMEM_LIMIT = 56 * 1024 * 1024

S_AQ, S_AK, S_AV, S_BV, S_BQ0, S_BQ1, S_BK0, S_BK1, S_CQ, S_CK, S_CV, S_DQ, S_DK, S_DV = range(14)
N_SLOTS = 14
W_ALL_COLS = 9 * 256 + 256 + 128 + 128


def _cparams(sem):
    return pltpu.CompilerParams(dimension_semantics=sem, vmem_limit_bytes=VMEM_LIMIT)


def _split_bf16(a):
    hi = a.astype(BF16)
    lo = (a - hi.astype(F32)).astype(BF16)
    return hi, lo


def _mod_kernel(c_ref, w_ref, b_ref, o_ref):
    c = c_ref[...]
    a = c * (1.0 / (1.0 + jnp.exp(-c)))
    a_hi, a_lo = _split_bf16(a)
    w_hi, w_lo = _split_bf16(w_ref[...])
    acc = jnp.dot(a_hi, w_hi, preferred_element_type=F32)
    acc += jnp.dot(a_hi, w_lo, preferred_element_type=F32)
    acc += jnp.dot(a_lo, w_hi, preferred_element_type=F32)
    o_ref[...] = acc + b_ref[...]


def _modulation(c, w_ada, b_ada):
    nb, d = c.shape
    n_out = w_ada.shape[1]
    tn = 1536
    return pl.pallas_call(
        _mod_kernel,
        grid=(n_out // tn,),
        in_specs=[pl.BlockSpec((nb, d), lambda j: (0, 0)),
                  pl.BlockSpec((d, tn), lambda j: (0, j)),
                  pl.BlockSpec((1, tn), lambda j: (0, j))],
        out_specs=pl.BlockSpec((nb, tn), lambda j: (0, j)),
        out_shape=jax.ShapeDtypeStruct((nb, n_out), F32),
        compiler_params=_cparams(("arbitrary",)),
        name="adaln_mod",
    )(c, w_ada, b_ada.reshape(1, n_out))


def _rope_tables(t, width, group, rot, theta):
    half = rot // 2
    inv = theta ** (-jnp.arange(half, dtype=F32) / half)
    ang = jnp.arange(t, dtype=F32)[:, None] * inv[None, :]
    cos, sin = jnp.cos(ang), jnp.sin(ang)
    ones = jnp.ones((t, group - rot), F32)
    zeros = jnp.zeros((t, group - rot), F32)
    c_g = jnp.concatenate([cos, cos, ones], axis=1)
    s_g = jnp.concatenate([-sin, sin, zeros], axis=1)
    reps = width // group
    return jnp.tile(c_g, (1, reps)), jnp.tile(s_g, (1, reps))


def _apply_rope(x, c_tab, s_tab, group, rot):
    width = x.shape[-1]
    half = rot // 2
    lane = lax.broadcasted_iota(jnp.int32, (1, width), 1)
    first = (lane % group) < half
    fwd = pltpu.roll(x, width - half, 1)
    bwd = pltpu.roll(x, half, 1)
    return x * c_tab + s_tab * jnp.where(first, fwd, bwd)


def _inproj_kernel(x_ref, sc_ref, sh_ref, w_ref, wuq_ref, wukv_ref, gq_ref, gkv_ref,
                   ca_ref, sa_ref, cc_ref, scc_ref, cm_ref, sm_ref, o_ref):
    h = (x_ref[...] * (1.0 + sc_ref[...]) + sh_ref[...]).astype(BF16)

    def proj(col, width):
        return jnp.dot(h, w_ref[:, col:col + width], preferred_element_type=F32)

    sa_scale = DA_DQK ** -0.5 * LOG2E
    sb_scale = (MLA_NOPE + MLA_ROPE) ** -0.5 * LOG2E
    sc_scale = DIL_DH ** -0.5 * LOG2E
    sd_scale = NA_DH ** -0.5 * LOG2E
    a_rot = DA_DQK // ROPE_FRACTION
    c_rot = DIL_DH // ROPE_FRACTION

    ca, sa = ca_ref[...], sa_ref[...]
    o_ref[S_AQ] = (_apply_rope(proj(0, 256), ca, sa, DA_DQK, a_rot) * sa_scale).astype(BF16)
    o_ref[S_AK] = _apply_rope(proj(256, 256), ca, sa, DA_DQK, a_rot).astype(BF16)
    o_ref[S_AV] = proj(512, 256).astype(BF16)
    cc, scc = cc_ref[...], scc_ref[...]
    o_ref[S_CQ] = (_apply_rope(proj(768, 256), cc, scc, DIL_DH, c_rot) * sc_scale).astype(BF16)
    o_ref[S_CK] = _apply_rope(proj(1024, 256), cc, scc, DIL_DH, c_rot).astype(BF16)
    o_ref[S_CV] = proj(1280, 256).astype(BF16)
    o_ref[S_DQ] = (proj(1536, 256) * sd_scale).astype(BF16)
    o_ref[S_DK] = proj(1792, 256).astype(BF16)
    o_ref[S_DV] = proj(2048, 256).astype(BF16)

    cm, sm = cm_ref[...], sm_ref[...]
    cq = proj(2304, 256)
    cq = cq * lax.rsqrt(jnp.sum(cq * cq, -1, keepdims=True) * (1.0 / MLA_Q_RANK) + RMS_EPS)
    cq = (cq * gq_ref[...]).astype(BF16)
    q2 = jnp.dot(cq, wuq_ref[...], preferred_element_type=F32)
    for p in range(2):
        qp = q2[:, 256 * p:256 * (p + 1)]
        o_ref[S_BQ0 + p, :, 0:128] = (qp[:, 0:128] * sb_scale).astype(BF16)
        o_ref[S_BQ0 + p, :, 128:256] = (
            _apply_rope(qp[:, 128:256], cm, sm, MLA_ROPE, MLA_ROPE) * sb_scale).astype(BF16)
    ckv = proj(2560, 128)
    ckv = ckv * lax.rsqrt(jnp.mean(ckv * ckv, -1, keepdims=True) + RMS_EPS)
    ckv = (ckv * gkv_ref[...]).astype(BF16)
    kv = jnp.dot(ckv, wukv_ref[...], preferred_element_type=F32)
    kr = _apply_rope(proj(2688, 128), cm, sm, MLA_ROPE, MLA_ROPE).astype(BF16)
    for p in range(2):
        o_ref[S_BK0 + p, :, 0:128] = kv[:, 128 * p:128 * (p + 1)].astype(BF16)
        o_ref[S_BK0 + p, :, 128:256] = kr
    o_ref[S_BV] = kv[:, 256:512].astype(BF16)


def _inproj(x, sc, sh, w_all, wuq, wukv, gq, gkv, tabs, nb, t):
    tm = 512
    nt = t // tm
    ntok = nb * t
    ca, sa, cc, scc, cm, sm = tabs
    full = lambda shape: pl.BlockSpec(shape, lambda j, b: tuple(0 for _ in shape))
    tab = lambda w: pl.BlockSpec((tm, w), lambda j, b: (j, 0))
    return pl.pallas_call(
        _inproj_kernel,
        grid=(nt, nb),
        in_specs=[pl.BlockSpec((tm, D_MODEL), lambda j, b: (b * nt + j, 0)),
                  pl.BlockSpec((None, 1, D_MODEL), lambda j, b: (b, 0, 0)),
                  pl.BlockSpec((None, 1, D_MODEL), lambda j, b: (b, 0, 0)),
                  full((D_MODEL, W_ALL_COLS)), full((256, 512)), full((128, 512)),
                  full((1, 256)), full((1, 128)),
                  tab(256), tab(256), tab(256), tab(256), tab(128), tab(128)],
        out_specs=pl.BlockSpec((N_SLOTS, tm, 256), lambda j, b: (0, b * nt + j, 0)),
        out_shape=jax.ShapeDtypeStruct((N_SLOTS, ntok, 256), BF16),
        compiler_params=_cparams(("arbitrary", "arbitrary")),
        name="in_proj",
    )(x, sc, sh, w_all, wuq, wukv, gq, gkv, ca, sa, cc, scc, cm, sm)


def _lane_mask(width, ranges):
    lane = lax.broadcasted_iota(jnp.int32, (1, width), 1)
    m = None
    for lo, hi in ranges:
        r = (lane >= lo) & (lane < hi)
        m = r if m is None else (m | r)
    return m


SOFTMAX_SLAB = 32


def _chain_scratch(tq, kc, width):
    return [pltpu.VMEM((tq, kc), F32), pltpu.VMEM((tq, kc), F32), pltpu.VMEM((tq, kc), BF16),
            pltpu.VMEM((tq, LANES), F32), pltpu.VMEM((tq, LANES), F32), pltpu.VMEM((tq, LANES), F32),
            pltpu.VMEM((tq, width), F32)]


CHAIN_REFS = 7


def _online_attention(qms, k_ref, v_ref, t, kc, chains):
    tq = qms[0].shape[0]
    width = v_ref.shape[-1]
    n_chunks = t // kc
    assert n_chunks % 2 == 0
    nt = (((1,), (1,)), ((), ()))
    for (_, _, _, m_ref, l_ref, _, acc_ref) in chains:
        m_ref[...] = jnp.full(m_ref.shape, NEG_INF, F32)
        l_ref[...] = jnp.zeros(l_ref.shape, F32)
        acc_ref[...] = jnp.zeros(acc_ref.shape, F32)

    def scores(chunk, slot):
        k = k_ref[pl.ds(pl.multiple_of(chunk * kc, kc), kc), :]
        for qm, chain in zip(qms, chains):
            chain[slot][...] = lax.dot_general(qm, k, nt, preferred_element_type=F32)

    def softmax_pv(chunk, slot):
        v = v_ref[pl.ds(pl.multiple_of(chunk * kc, kc), kc), :]
        for chain in chains:
            s_ref = chain[slot]
            (p_ref, m_ref, l_ref, a_ref, acc_ref) = chain[2:]
            for r in range(tq // SOFTMAX_SLAB):
                rows = slice(r * SOFTMAX_SLAB, (r + 1) * SOFTMAX_SLAB)
                s = s_ref[rows, :]
                m_prev = m_ref[rows, :]
                m_new = jnp.maximum(m_prev, jnp.max(s, -1, keepdims=True))
                p = jnp.exp2(s - jnp.concatenate([m_new] * (kc // LANES), axis=1))
                alpha = jnp.exp2(m_prev - m_new)
                l_ref[rows, :] = alpha * l_ref[rows, :] + jnp.sum(p, -1, keepdims=True)
                m_ref[rows, :] = m_new
                a_ref[rows, :] = alpha
                p_ref[rows, :] = p.astype(BF16)
            alpha = jnp.concatenate([a_ref[...]] * (width // LANES), axis=1)
            acc_ref[...] = alpha * acc_ref[...] + jnp.dot(p_ref[...], v,
                                                          preferred_element_type=F32)

    def body(j, _):
        scores(2 * j + 1, 1)
        softmax_pv(2 * j, 0)
        scores(jnp.minimum(2 * j + 2, n_chunks - 1), 0)
        softmax_pv(2 * j + 1, 1)
        return 0

    scores(0, 0)
    lax.fori_loop(0, n_chunks // 2, body, 0)
    outs = []
    for chain in chains:
        l_ref, acc_ref = chain[4], chain[6]
        inv = 1.0 / l_ref[...]
        outs.append(acc_ref[...] * jnp.concatenate([inv] * (width // LANES), axis=1))
    return outs


def _group_mean_sq(x, gmat):
    sq = x * x
    hi, lo = _split_bf16(sq)
    return (jnp.dot(hi, gmat, preferred_element_type=F32)
            + jnp.dot(lo, gmat, preferred_element_type=F32))


def _diff_attn_kernel(lam_ref, q_ref, k_ref, v_ref, g_ref, gmat_ref, o_ref, acc_ref, *scratch, t, kc,
                      out_scale):
    chains = (scratch[:CHAIN_REFS], scratch[CHAIN_REFS:])
    q = q_ref[...]
    lam = lam_ref[0]
    lane = lax.broadcasted_iota(jnp.int32, (1, 256), 1)
    zero = jnp.zeros_like(q)

    def head(h, _):
        qms = []
        for c in range(2):
            lo = (2 * h + c) * DA_DQK
            qms.append(jnp.where((lane >= lo) & (lane < lo + DA_DQK), q, zero))
        o_1, o_2 = _online_attention(qms, k_ref, v_ref, t, kc, chains)
        o_h = o_1 - lam * o_2
        sel = (lane >= h * DA_DV) & (lane < (h + 1) * DA_DV)
        acc_ref[...] = jnp.where(sel, o_h, acc_ref[...])
        return 0

    acc_ref[...] = jnp.zeros_like(acc_ref)
    lax.fori_loop(0, DA_HEADS, head, 0)
    o = acc_ref[...]
    ms = _group_mean_sq(o, gmat_ref[...])
    o_ref[...] = (o * lax.rsqrt(ms + RMS_EPS) * g_ref[...] * out_scale).astype(o_ref.dtype)


def _diff_attention(proj, lam, g_tiled, gmat, nb, t, out_scale):
    tq, kc = 512, 512
    nq = t // tq
    kern = functools.partial(_diff_attn_kernel, t=t, kc=kc, out_scale=out_scale)
    return pl.pallas_call(
        kern,
        grid=(nb, nq),
        in_specs=[pl.BlockSpec(memory_space=pltpu.SMEM),
                  pl.BlockSpec((None, tq, 256), lambda b, i: (S_AQ, b * nq + i, 0)),
                  pl.BlockSpec((None, t, 256), lambda b, i: (S_AK, b, 0)),
                  pl.BlockSpec((None, t, 256), lambda b, i: (S_AV, b, 0)),
                  pl.BlockSpec((1, 256), lambda b, i: (0, 0)),
                  pl.BlockSpec((256, 256), lambda b, i: (0, 0))],
        out_specs=pl.BlockSpec((tq, 256), lambda b, i: (b * nq + i, 0)),
        out_shape=jax.ShapeDtypeStruct((nb * t, 256), BF16),
        scratch_shapes=[pltpu.VMEM((tq, 256), F32)] + 2 * _chain_scratch(tq, kc, 256),
        compiler_params=_cparams(("arbitrary", "arbitrary")),
        name="diff_attn",
    )(lam, proj, proj, proj, g_tiled, gmat)


def _mla_attn_kernel(q_ref, k_ref, v_ref, o_ref, acc_ref, *scratch, t, kc):
    chains = (scratch[:CHAIN_REFS], scratch[CHAIN_REFS:])
    lane = lax.broadcasted_iota(jnp.int32, (1, 256), 1)

    def pair(p, _):
        q = q_ref[p]
        qms = []
        for j in range(2):
            nope = (lane >= j * MLA_NOPE) & (lane < (j + 1) * MLA_NOPE)
            rope = (lane >= 128 + j * MLA_ROPE) & (lane < 128 + (j + 1) * MLA_ROPE)
            qms.append(jnp.where(nope | rope, q, jnp.zeros_like(q)))
        outs = _online_attention(qms, k_ref.at[p], v_ref, t, kc, chains)
        out = acc_ref[...]
        for j in range(2):
            h = 2 * p + j
            sel = (lane >= h * MLA_DV) & (lane < (h + 1) * MLA_DV)
            out = jnp.where(sel, outs[j], out)
        acc_ref[...] = out
        return 0

    acc_ref[...] = jnp.zeros_like(acc_ref)
    lax.fori_loop(0, MLA_HEADS // 2, pair, 0)
    o_ref[...] = acc_ref[...].astype(o_ref.dtype)


def _mla_attention(proj, nb, t):
    tq, kc = 512, 512
    nq = t // tq
    kern = functools.partial(_mla_attn_kernel, t=t, kc=kc)
    return pl.pallas_call(
        kern,
        grid=(nb, nq),
        in_specs=[pl.BlockSpec((2, tq, 256), lambda b, i: (S_BQ0 // 2, b * nq + i, 0)),
                  pl.BlockSpec((2, t, 256), lambda b, i: (S_BK0 // 2, b, 0)),
                  pl.BlockSpec((None, t, 256), lambda b, i: (S_BV, b, 0))],
        out_specs=pl.BlockSpec((tq, 256), lambda b, i: (b * nq + i, 0)),
        out_shape=jax.ShapeDtypeStruct((nb * t, 256), BF16),
        scratch_shapes=[pltpu.VMEM((tq, 256), F32)] + 2 * _chain_scratch(tq, kc, 256),
        compiler_params=_cparams(("arbitrary", "arbitrary")),
        name="mla_attn",
    )(proj, proj, proj)


DIL_REACH = max(w // 2 for w, _ in DIL_PATTERNS)


def _dil_attn_kernel(q_ref, k_ref, v_ref, o_ref, bias_ref, acc_ref, *, t, tq, band):
    i = pl.program_id(1)

    def band_start(blk):
        return jnp.clip(blk * tq - DIL_REACH, 0, t - band)

    t0 = i * tq
    start = pl.multiple_of(band_start(i), tq)
    shift = start - t0
    prev_shift = band_start(i - 1) - (i - 1) * tq

    @pl.when((i == 0) | (shift != prev_shift))
    def _():
        qi = lax.broadcasted_iota(jnp.int32, (tq, band), 0)
        kj = lax.broadcasted_iota(jnp.int32, (tq, band), 1)
        delta = kj - qi + shift
        ad = jnp.abs(delta)
        cnt = jnp.zeros((tq, band), F32)
        for window, dil in DIL_PATTERNS:
            ok = (ad <= window // 2) & ((delta & (dil - 1)) == 0)
            cnt = cnt + jnp.where(ok, 1.0, 0.0)
        bias_ref[...] = jnp.where(cnt > 2.5, math.log2(3.0),
                                  jnp.where(cnt > 1.5, 1.0,
                                            jnp.where(cnt > 0.5, 0.0, NEG_INF)))

    q = q_ref[...]
    lane = lax.broadcasted_iota(jnp.int32, (1, 256), 1)
    kb = k_ref.at[pl.ds(start, band), :]
    vb = v_ref.at[pl.ds(start, band), :]

    def pair(hp, _):
        out = acc_ref[...]
        for j in range(2):
            h = 2 * hp + j
            sel = (lane >= h * DIL_DH) & (lane < (h + 1) * DIL_DH)
            qm = jnp.where(sel, q, jnp.zeros_like(q))
            s = lax.dot_general(qm, kb[...], (((1,), (1,)), ((), ())),
                                preferred_element_type=F32)
            s = s + bias_ref[...]
            m = jnp.max(s, -1, keepdims=True)
            p = jnp.exp2(s - m)
            l = jnp.sum(p, -1, keepdims=True)
            o_h = jnp.dot(p.astype(BF16), vb[...], preferred_element_type=F32) * (1.0 / l)
            out = jnp.where(sel, o_h, out)
        acc_ref[...] = out
        return 0

    acc_ref[...] = jnp.zeros_like(acc_ref)
    lax.fori_loop(0, DIL_HEADS // 2, pair, 0)
    o_ref[...] = acc_ref[...].astype(o_ref.dtype)


def _dil_attention(proj, nb, t):
    tq = 256
    band = min(t, tq + 2 * DIL_REACH)
    nq = t // tq
    kern = functools.partial(_dil_attn_kernel, t=t, tq=tq, band=band)
    return pl.pallas_call(
        kern,
        grid=(nb, nq),
        in_specs=[pl.BlockSpec((None, tq, 256), lambda b, i: (S_CQ, b * nq + i, 0)),
                  pl.BlockSpec((None, t, 256), lambda b, i: (S_CK, b, 0)),
                  pl.BlockSpec((None, t, 256), lambda b, i: (S_CV, b, 0))],
        out_specs=pl.BlockSpec((tq, 256), lambda b, i: (b * nq + i, 0)),
        out_shape=jax.ShapeDtypeStruct((nb * t, 256), BF16),
        scratch_shapes=[pltpu.VMEM((tq, band), F32), pltpu.VMEM((tq, 256), F32)],
        compiler_params=_cparams(("arbitrary", "arbitrary")),
        name="dil_attn",
    )(proj, proj, proj)


def _na_bias_table(rpb):
    c = np.arange(GRID_W)
    cs = np.clip(c - NA_KC // 2, 0, GRID_W - NA_KC)
    colmask = (c[None, :] >= cs[:, None]) & (c[None, :] < cs[:, None] + NA_KC)
    rows = jnp.stack([rpb[:, si:si + NA_KR, :] for si in range(NA_KR)], axis=1).astype(F32)
    edge = GRID_W - NA_KC
    padded = jnp.concatenate([jnp.repeat(rows[..., :1], edge, axis=-1), rows,
                              jnp.repeat(rows[..., -1:], edge, axis=-1)], axis=-1)
    b = jnp.stack([padded[..., GRID_W - 1 - qc:2 * GRID_W - 1 - qc] for qc in range(GRID_W)],
                  axis=2)
    b = jnp.where(colmask[None, None, :, None, :], b * LOG2E, NEG_INF)
    b = b.transpose(1, 0, 2, 3, 4)
    return b.reshape(NA_KR, rpb.shape[0] * GRID_W, NA_KR * GRID_W)


def _na_attn_kernel(q_ref, k_ref, v_ref, tb_ref, o_ref, *, rows, rg):
    g = pl.program_id(1)
    lane = lax.broadcasted_iota(jnp.int32, (1, 256), 1)
    nk = NA_KR * GRID_W
    sels = [(lane >= h * NA_DH) & (lane < (h + 1) * NA_DH) for h in range(NA_HEADS)]

    def row(r, _):
        grow = g * rg + r
        rs = jnp.clip(grow - NA_KR // 2, 0, rows - NA_KR)
        si = rs - grow + (NA_KR - 1)
        q = q_ref[pl.ds(pl.multiple_of(r * GRID_W, GRID_W), GRID_W), :]
        koff = pl.multiple_of(rs * GRID_W, GRID_W)
        kb = k_ref[pl.ds(koff, nk), :]
        vb = v_ref[pl.ds(koff, nk), :]
        q4 = jnp.concatenate([jnp.where(sel, q, jnp.zeros_like(q)) for sel in sels], axis=0)
        s = lax.dot_general(q4, kb, (((1,), (1,)), ((), ())), preferred_element_type=F32)
        s = s + tb_ref[si]
        m = jnp.max(s, -1, keepdims=True)
        p = jnp.exp2(s - m)
        l = jnp.sum(p, -1, keepdims=True)
        o4 = jnp.dot(p.astype(BF16), vb, preferred_element_type=F32) * (1.0 / l)
        out = jnp.zeros((GRID_W, 256), F32)
        for h, sel in enumerate(sels):
            out = jnp.where(sel, o4[h * GRID_W:(h + 1) * GRID_W], out)
        o_ref[pl.ds(pl.multiple_of(r * GRID_W, GRID_W), GRID_W), :] = out.astype(o_ref.dtype)
        return 0

    lax.fori_loop(0, rg, row, 0, unroll=2)


def _na_attention(proj, tb, nb, t):
    rows = t // GRID_W
    assert rows >= NA_KR
    rg = 8
    ng = rows // rg
    tq = rg * GRID_W
    kern = functools.partial(_na_attn_kernel, rows=rows, rg=rg)
    return pl.pallas_call(
        kern,
        grid=(nb, ng),
        in_specs=[pl.BlockSpec((None, tq, 256), lambda b, i: (S_DQ, b * ng + i, 0)),
                  pl.BlockSpec((None, t, 256), lambda b, i: (S_DK, b, 0)),
                  pl.BlockSpec((None, t, 256), lambda b, i: (S_DV, b, 0)),
                  pl.BlockSpec(tb.shape, lambda b, i: (0, 0, 0))],
        out_specs=pl.BlockSpec((tq, 256), lambda b, i: (b * ng + i, 0)),
        out_shape=jax.ShapeDtypeStruct((nb * t, 256), BF16),
        compiler_params=_cparams(("arbitrary", "arbitrary")),
        name="na_attn",
    )(proj, proj, proj, tb)


def _layer_norm(y, g, b):
    mu = jnp.mean(y, -1, keepdims=True)
    yc = y - mu
    var = jnp.mean(yc * yc, -1, keepdims=True)
    return yc * lax.rsqrt(var + LN_EPS) * g + b


def _outproj_kernel(oa_ref, ob_ref, oc_ref, od_ref, x_ref, g1_ref, sc2_ref, sh2_ref, w_ref,
                    lg_ref, lb_ref, wr_ref, x1_ref, h2_ref, aff_ref):
    m = jnp.dot(oa_ref[...], w_ref[0:256, :], preferred_element_type=F32)
    m += jnp.dot(ob_ref[...], w_ref[256:512, :], preferred_element_type=F32)
    m += jnp.dot(oc_ref[...], w_ref[512:768, :], preferred_element_type=F32)
    m += jnp.dot(od_ref[...], w_ref[768:1024, :], preferred_element_type=F32)
    y = DEEPNORM_ALPHA * x_ref[...] + (1.0 + g1_ref[...]) * m
    x1 = _layer_norm(y, lg_ref[...], lb_ref[...])
    x1_ref[...] = x1
    h2 = x1 * (1.0 + sc2_ref[...]) + sh2_ref[...]
    h2_ref[...] = h2.astype(BF16)
    h_hi, h_lo = _split_bf16(h2)
    w_hi, w_lo = _split_bf16(wr_ref[...])
    nt = (((1,), (1,)), ((), ()))
    lg = lax.dot_general(w_hi, h_hi, nt, preferred_element_type=F32)
    lg += lax.dot_general(w_hi, h_lo, nt, preferred_element_type=F32)
    lg += lax.dot_general(w_lo, h_hi, nt, preferred_element_type=F32)
    lg = lg - jnp.max(lg, 0, keepdims=True)
    e = jnp.exp(lg)
    aff_ref[...] = e / jnp.sum(e, 0, keepdims=True)


def _outproj(oa, ob, oc, od, x, g1, sc2, sh2, w_out, ln_g, ln_b, w_router_t, nb, t):
    tm = 512
    nt = t // tm
    ntok = nb * t
    tok = lambda w: pl.BlockSpec((tm, w), lambda i: (i, 0))
    per_b = pl.BlockSpec((None, 1, D_MODEL), lambda i: (i // nt, 0, 0))
    full = lambda shape: pl.BlockSpec(shape, lambda i: tuple(0 for _ in shape))
    return pl.pallas_call(
        _outproj_kernel,
        grid=(ntok // tm,),
        in_specs=[tok(256), tok(256), tok(256), tok(256), tok(D_MODEL), per_b, per_b, per_b,
                  full((D_MODEL, D_MODEL)), full((1, D_MODEL)), full((1, D_MODEL)),
                  full((N_EXPERTS, D_MODEL))],
        out_specs=[tok(D_MODEL), tok(D_MODEL),
                   pl.BlockSpec((N_EXPERTS, tm), lambda i: (0, i))],
        out_shape=[jax.ShapeDtypeStruct((ntok, D_MODEL), F32),
                   jax.ShapeDtypeStruct((ntok, D_MODEL), BF16),
                   jax.ShapeDtypeStruct((N_EXPERTS, ntok), F32)],
        compiler_params=_cparams(("arbitrary",)),
        name="out_proj",
    )(oa, ob, oc, od, x, g1, sc2, sh2, w_out, ln_g, ln_b, w_router_t)


def _expert_kernel(x_ref, wg_ref, wu_ref, wd_ref, o_ref, acc_ref):
    f = pl.program_id(2)

    @pl.when(f == 0)
    def _():
        acc_ref[...] = jnp.zeros_like(acc_ref)

    x = x_ref[...]
    g = jnp.dot(x, wg_ref[...], preferred_element_type=F32)
    u = jnp.dot(x, wu_ref[...], preferred_element_type=F32)
    hmid = (g * (1.0 / (1.0 + jnp.exp(-g))) * u).astype(BF16)
    acc_ref[...] += jnp.dot(hmid, wd_ref[...], preferred_element_type=F32)

    @pl.when(f == pl.num_programs(2) - 1)
    def _():
        o_ref[...] = acc_ref[...].astype(o_ref.dtype)


def _experts(xe, slots, wg, wu, wd):
    ne, _, d = xe.shape
    tm = math.gcd(slots, 1024)
    tf = 256
    nf = D_FF // tf
    return pl.pallas_call(
        _expert_kernel,
        grid=(ne, slots // tm, nf),
        in_specs=[pl.BlockSpec((None, tm, d), lambda e, m, f: (e, m, 0)),
                  pl.BlockSpec((None, d, tf), lambda e, m, f: (e, 0, f)),
                  pl.BlockSpec((None, d, tf), lambda e, m, f: (e, 0, f)),
                  pl.BlockSpec((None, tf, d), lambda e, m, f: (e, f, 0))],
        out_specs=pl.BlockSpec((None, tm, d), lambda e, m, f: (e, m, 0)),
        out_shape=jax.ShapeDtypeStruct((ne, slots, d), BF16),
        scratch_shapes=[pltpu.VMEM((tm, d), F32)],
        compiler_params=_cparams(("arbitrary", "arbitrary", "arbitrary")),
        name="expert_ffn",
    )(xe, wg, wu, wd)


MOE_TM = 512
MOE_WIN = 128
ROW_ALIGN = 16


def _select_kernel(aff_ref, o_ref, *, cap):
    a = aff_ref[...]
    keys = lax.bitcast_convert_type(a, jnp.int32)
    ne, n = a.shape
    capf = float(cap)

    def count(mask):
        return jnp.sum(jnp.where(mask, 1.0, 0.0), axis=1, keepdims=True)

    def key_bit(b, thr):
        cand = thr | lax.shift_left(jnp.int32(1), 30 - b)
        return jnp.where(count(keys >= cand) >= capf, cand, thr)

    thr = lax.fori_loop(0, 31, key_bit, jnp.zeros((ne, 1), jnp.int32))
    above = keys > thr
    need = capf - count(above)
    idx = lax.broadcasted_iota(jnp.int32, (ne, n), 1)
    tie_idx = jnp.where(keys == thr, idx, jnp.int32(2 ** 30))
    nbits = max(1, (n - 1).bit_length())

    def idx_bit(b, j):
        cand = j | lax.shift_left(jnp.int32(1), nbits - 1 - b)
        return jnp.where(count(tie_idx < cand) < need, cand, j)

    j = lax.fori_loop(0, nbits, idx_bit, jnp.zeros((ne, 1), jnp.int32))
    sel = above | (tie_idx <= j)
    o_ref[...] = jnp.where(sel, a, -1.0)


def _select(aff, cap):
    ne, n = aff.shape
    return pl.pallas_call(
        functools.partial(_select_kernel, cap=cap),
        out_shape=jax.ShapeDtypeStruct((ne, n), F32),
        compiler_params=pltpu.CompilerParams(vmem_limit_bytes=VMEM_LIMIT),
        name="ec_select",
    )(aff)


def _routing_tables(gs, group_tokens):
    ne = gs.shape[0]
    a_l, off_l, cnt_l, lim_l = [], [], [], []
    tok0, slot0 = 0, 0
    for n in group_tokens:
        cap = EC_FACTOR * n // N_EXPERTS
        nt = n // MOE_TM
        sel = lax.slice_in_dim(gs, tok0, tok0 + n, axis=1) >= 0
        counts = jnp.sum(sel.reshape(ne, nt, MOE_TM), axis=-1, dtype=jnp.int32)
        s0 = slot0 + jnp.cumsum(counts, axis=1) - counts
        a = (s0 // ROW_ALIGN) * ROW_ALIGN
        a_l.append(a)
        off_l.append(s0 - a)
        cnt_l.append(counts)
        lim_l.append(jnp.full((nt,), slot0 + cap - MOE_WIN, jnp.int32))
        tok0 += n
        slot0 += cap
    a = jnp.concatenate(a_l, axis=1).T
    off = jnp.concatenate(off_l, axis=1).T
    end = off + jnp.concatenate(cnt_l, axis=1).T
    rounds = (end + MOE_WIN - 1) // MOE_WIN
    gp = (end // ROW_ALIGN) * ROW_ALIGN
    return dict(a=a.reshape(-1).astype(jnp.int32), nr=rounds.reshape(-1).astype(jnp.int32),
                nrounds=jnp.max(rounds, axis=1).astype(jnp.int32), lim=jnp.concatenate(lim_l),
                off_col=off.astype(F32)[:, :, None], off_row=off.astype(F32)[:, None, :],
                off16=jnp.repeat(off.astype(F32), ROW_ALIGN, axis=1)[:, :, None],
                gp16=jnp.repeat(gp.astype(F32), ROW_ALIGN, axis=1)[:, :, None],
                total=slot0)


def _dispatch_kernel(a_tab, nrounds, nr_tab, gs_ref, off_ref, off16_ref, gp16_ref, x_ref, u_ref, xe_ref,
                     pos_ref, c_ref, c2_ref, stage_ref, carry_ref, sem, rc_ref):
    j = pl.program_id(0)
    ne, tm = gs_ref.shape
    d = x_ref.shape[1]
    win = MOE_WIN

    @pl.when(j == 0)
    def _():
        carry_ref[...] = jnp.zeros_like(carry_ref)
        rc_ref[0] = 0

    sel = gs_ref[...] >= 0.0
    rank = jnp.dot(jnp.where(sel, 1.0, 0.0).astype(BF16), u_ref[...], preferred_element_type=F32)
    pos_ref[...] = jnp.where(sel, rank + off_ref[...], -1.0)

    def wait_round(slot):
        def one(_, c):
            pltpu.make_async_copy(stage_ref.at[slot, pl.ds(0, win)], xe_ref.at[0, pl.ds(0, win)],
                                  sem.at[slot]).wait()
            return c
        lax.fori_loop(0, rc_ref[1 + slot], one, 0)

    def round_body(r, _):
        k = lax.broadcasted_iota(jnp.int32, (win, tm), 0).astype(F32) + (r * win).astype(F32)
        for e in range(ne):
            c_ref[e * win:(e + 1) * win, :] = jnp.where(pos_ref[e:e + 1, :] == k, 1.0, 0.0).astype(BF16)
        slot = rc_ref[0] % 2
        for nb in range(d // MXU_DIM):
            cols = slice(nb * MXU_DIM, (nb + 1) * MXU_DIM)
            stage_ref[slot, :, cols] = jnp.dot(c_ref[...], x_ref[:, cols],
                                               preferred_element_type=F32).astype(BF16)

        @pl.when(r == 0)
        def _():
            k16 = lax.broadcasted_iota(jnp.int32, (ROW_ALIGN, 1), 0).astype(F32)
            for e in range(ne):
                keep = k16 < off16_ref[e * ROW_ALIGN:(e + 1) * ROW_ALIGN, :]
                rows = pl.ds(e * win, ROW_ALIGN)
                stage_ref[slot, rows, :] = jnp.where(
                    keep, carry_ref[e * ROW_ALIGN:(e + 1) * ROW_ALIGN, :], stage_ref[slot, rows, :])

        @pl.when(rc_ref[0] > 0)
        def _():
            wait_round(1 - slot)

        rc_ref[1 + slot] = 0
        for e in range(ne):
            @pl.when(r < nr_tab[j * ne + e])
            def _():
                dst = pl.multiple_of(a_tab[j * ne + e] + r * win, ROW_ALIGN)
                pltpu.make_async_copy(stage_ref.at[slot, pl.ds(e * win, win)],
                                      xe_ref.at[e, pl.ds(dst, win)], sem.at[slot]).start()
                rc_ref[1 + slot] = rc_ref[1 + slot] + 1
        rc_ref[0] = rc_ref[0] + 1
        return 0

    lax.fori_loop(0, nrounds[j], round_body, 0)

    k16 = lax.broadcasted_iota(jnp.int32, (ROW_ALIGN, 1), 0).astype(F32)
    for e in range(ne):
        rows = slice(e * ROW_ALIGN, (e + 1) * ROW_ALIGN)
        c2_ref[rows, :] = jnp.where(pos_ref[e:e + 1, :] == gp16_ref[rows, :] + k16, 1.0, 0.0).astype(BF16)
    kk = jnp.concatenate([k16] * ne, axis=0)
    keep_old = (gp16_ref[...] == 0.0) & (kk < off16_ref[...])
    for nb in range(d // MXU_DIM):
        cols = slice(nb * MXU_DIM, (nb + 1) * MXU_DIM)
        new = jnp.dot(c2_ref[...], x_ref[:, cols], preferred_element_type=F32).astype(BF16)
        carry_ref[:, cols] = jnp.where(keep_old, carry_ref[:, cols], new)

    @pl.when((j == pl.num_programs(0) - 1) & (rc_ref[0] > 0))
    def _():
        wait_round((rc_ref[0] - 1) % 2)


def _dispatch(gs, h2, tabs, u_mat):
    ne, ntok = gs.shape
    d = h2.shape[1]
    nt = ntok // MOE_TM
    rows = tabs['total'] + MOE_WIN
    grid_spec = pltpu.PrefetchScalarGridSpec(
        num_scalar_prefetch=3,
        grid=(nt,),
        in_specs=[pl.BlockSpec((ne, MOE_TM), lambda j, *_: (0, j)),
                  pl.BlockSpec((None, ne, 1), lambda j, *_: (j, 0, 0)),
                  pl.BlockSpec((None, ne * ROW_ALIGN, 1), lambda j, *_: (j, 0, 0)),
                  pl.BlockSpec((None, ne * ROW_ALIGN, 1), lambda j, *_: (j, 0, 0)),
                  pl.BlockSpec((MOE_TM, d), lambda j, *_: (j, 0)),
                  pl.BlockSpec((MOE_TM, MOE_TM), lambda j, *_: (0, 0))],
        out_specs=pl.BlockSpec(memory_space=pl.ANY),
        scratch_shapes=[pltpu.VMEM((ne, MOE_TM), F32),
                        pltpu.VMEM((ne * MOE_WIN, MOE_TM), BF16),
                        pltpu.VMEM((ne * ROW_ALIGN, MOE_TM), BF16),
                        pltpu.VMEM((2, ne * MOE_WIN, d), BF16),
                        pltpu.VMEM((ne * ROW_ALIGN, d), BF16),
                        pltpu.SemaphoreType.DMA((2,)),
                        pltpu.SMEM((3,), jnp.int32)])
    return pl.pallas_call(
        _dispatch_kernel,
        grid_spec=grid_spec,
        out_shape=jax.ShapeDtypeStruct((ne, rows, d), BF16),
        compiler_params=_cparams(("arbitrary",)),
        name="ec_dispatch",
    )(tabs['a'], tabs['nrounds'], tabs['nr'], gs, tabs['off_col'], tabs['off16'], tabs['gp16'], h2,
      u_mat)


def _combine_kernel(a_tab, nrounds, lim_tab, gs_ref, off_ref, l_ref, ye_ref, x1_ref, g2_ref, lg_ref,
                    lb_ref, o_ref, p_ref, y_ref, acc_ref, sem):
    j = pl.program_id(0)
    tm, ne = gs_ref.shape
    d = x1_ref.shape[1]
    win = MOE_WIN
    gs = gs_ref[...]
    sel = gs >= 0.0
    rank = jnp.dot(l_ref[...], jnp.where(sel, 1.0, 0.0).astype(BF16), preferred_element_type=F32)
    pos = jnp.where(sel, rank + off_ref[...], -1.0)
    gate = jnp.where(sel, gs, 0.0)
    acc_ref[...] = jnp.zeros_like(acc_ref)
    lim = lim_tab[j]

    def round_body(r, _):
        copies = []
        shifts = []
        for e in range(ne):
            want = a_tab[j * ne + e] + r * win
            src = pl.multiple_of(jnp.minimum(want, lim), ROW_ALIGN)
            shifts.append((want - src).astype(F32))
            cp = pltpu.make_async_copy(ye_ref.at[e, pl.ds(src, win)], y_ref.at[pl.ds(e * win, win)],
                                       sem.at[0])
            cp.start()
            copies.append(cp)
        base = (r * win).astype(F32)
        k = lax.broadcasted_iota(jnp.int32, (tm, win), 1).astype(F32) + base
        for e in range(ne):
            pe = pos[:, e:e + 1]
            pe = jnp.where(pe >= base, pe + shifts[e], -1.0)
            pcol = jnp.broadcast_to(pe, (tm, win))
            gcol = jnp.broadcast_to(gate[:, e:e + 1], (tm, win))
            p_ref[:, e * win:(e + 1) * win] = jnp.where(pcol == k, gcol, 0.0).astype(BF16)
        for cp in copies:
            cp.wait()
        for nb in range(d // MXU_DIM):
            cols = slice(nb * MXU_DIM, (nb + 1) * MXU_DIM)
            acc_ref[:, cols] += jnp.dot(p_ref[...], y_ref[:, cols], preferred_element_type=F32)
        return 0

    lax.fori_loop(0, nrounds[j], round_body, 0)
    y = DEEPNORM_ALPHA * x1_ref[...] + (1.0 + g2_ref[...]) * acc_ref[...]
    o_ref[...] = _layer_norm(y, lg_ref[...], lb_ref[...])


def _combine_postnorm(gs_tok, ye, tabs, l_mat, x1, g2, ln_g, ln_b, t):
    ntok, ne = gs_tok.shape
    d = x1.shape[1]
    nt = ntok // MOE_TM
    tiles_per_seq = t // MOE_TM
    grid_spec = pltpu.PrefetchScalarGridSpec(
        num_scalar_prefetch=3,
        grid=(nt,),
        in_specs=[pl.BlockSpec((MOE_TM, ne), lambda j, *_: (j, 0)),
                  pl.BlockSpec((None, 1, ne), lambda j, *_: (j, 0, 0)),
                  pl.BlockSpec((MOE_TM, MOE_TM), lambda j, *_: (0, 0)),
                  pl.BlockSpec(memory_space=pl.ANY),
                  pl.BlockSpec((MOE_TM, d), lambda j, *_: (j, 0)),
                  pl.BlockSpec((None, 1, d), lambda j, *_: (j // tiles_per_seq, 0, 0)),
                  pl.BlockSpec((1, d), lambda j, *_: (0, 0)),
                  pl.BlockSpec((1, d), lambda j, *_: (0, 0))],
        out_specs=pl.BlockSpec((MOE_TM, d), lambda j, *_: (j, 0)),
        scratch_shapes=[pltpu.VMEM((MOE_TM, ne * MOE_WIN), BF16),
                        pltpu.VMEM((ne * MOE_WIN, d), BF16),
                        pltpu.VMEM((MOE_TM, d), F32),
                        pltpu.SemaphoreType.DMA((1,))])
    return pl.pallas_call(
        _combine_kernel,
        grid_spec=grid_spec,
        out_shape=jax.ShapeDtypeStruct((ntok, d), F32),
        compiler_params=_cparams(("arbitrary",)),
        name="ec_combine",
    )(tabs['a'], tabs['nrounds'], tabs['lim'], gs_tok, tabs['off_row'], l_mat, ye, x1, g2, ln_g, ln_b)


def _prep_w_in(w_in_l):
    sizes = (256, 256, 256, MLA_Q_RANK, MLA_KV_RANK, MLA_ROPE, 256, 256, 256, 256, 256, 256)
    offs = np.concatenate([[0], np.cumsum(sizes)])
    part = [w_in_l[:, offs[i]:offs[i + 1]] for i in range(len(sizes))]
    a_q, a_k, a_v, b_cq, b_ckv, b_kr, c_q, c_k, c_v, d_q, d_k, d_v = part
    d = w_in_l.shape[0]
    zeros = lambda n: jnp.zeros((d, n), w_in_l.dtype)
    cols = [a_q, a_k, a_v, c_q, c_k, c_v, d_q, d_k, d_v,
            b_cq, zeros(256 - MLA_Q_RANK), b_ckv, b_kr, b_kr, zeros(128 - 2 * MLA_ROPE)]
    return jnp.concatenate(cols, axis=1).astype(BF16)


def _prep_w_uq(w_uq_l):
    hd = MLA_NOPE + MLA_ROPE
    nope = [w_uq_l[:, h * hd:h * hd + MLA_NOPE] for h in range(MLA_HEADS)]
    rope = [w_uq_l[:, h * hd + MLA_NOPE:(h + 1) * hd] for h in range(MLA_HEADS)]
    z = jnp.zeros((w_uq_l.shape[0], 256 - 2 * hd), w_uq_l.dtype)
    cols = []
    for p in range(2):
        cols += [nope[2 * p], nope[2 * p + 1], rope[2 * p], rope[2 * p + 1], z]
    w = jnp.concatenate(cols, axis=1)
    w = jnp.concatenate([w, jnp.zeros((256 - MLA_Q_RANK, w.shape[1]), w.dtype)], axis=0)
    return w.astype(BF16)


def _prep_w_ukv(w_ukv_l):
    hd = MLA_NOPE + MLA_DV
    kn = [w_ukv_l[:, h * hd:h * hd + MLA_NOPE] for h in range(MLA_HEADS)]
    vv = [w_ukv_l[:, h * hd + MLA_NOPE:(h + 1) * hd] for h in range(MLA_HEADS)]
    return jnp.concatenate(kn + vv, axis=1).astype(BF16)


def _select_groups(aff_t, group_tokens):
    parts, off = [], 0
    for n in group_tokens:
        parts.append(_select(lax.slice_in_dim(aff_t, off, off + n, axis=1),
                             EC_FACTOR * n // N_EXPERTS))
        off += n
    return jnp.concatenate(parts, axis=1)


def _trunk(x, c, group_tokens, nb, t, p):
    ntok = nb * t
    tabs = (_rope_tables(t, 256, DA_DQK, DA_DQK // ROPE_FRACTION, ROPE_THETA)
            + _rope_tables(t, 256, DIL_DH, DIL_DH // ROPE_FRACTION, ROPE_THETA)
            + _rope_tables(t, 128, MLA_ROPE, MLA_ROPE, MLA_ROPE_THETA))
    gmat = jnp.asarray(np.kron(np.eye(4), np.full((64, 64), 1.0 / 64)), BF16)
    ti = jnp.arange(MOE_TM)
    u_mat = (ti[:, None] < ti[None, :]).astype(BF16)
    l_mat = (ti[None, :] < ti[:, None]).astype(BF16)
    for l in range(DEPTH):
        mod = _modulation(c, p['w_ada'][l], p['b_ada'][l])
        sh1, sc1, g1, sh2, sc2, g2 = [m.reshape(nb, 1, D_MODEL) for m in jnp.split(mod, 6, axis=-1)]
        gq = jnp.concatenate([p['q_norm_g'][l], jnp.zeros((256 - MLA_Q_RANK,), F32)]).reshape(1, 256)
        gkv = p['kv_norm_g'][l].reshape(1, 128)
        proj = _inproj(x, sc1, sh1, _prep_w_in(p['w_in'][l]), _prep_w_uq(p['w_uq'][l]),
                       _prep_w_ukv(p['w_ukv'][l]), gq, gkv, tabs, nb, t)
        lam_init = 0.8 - 0.6 * math.exp(-0.3 * l)
        lam = (jnp.exp(jnp.sum(p['da_lq1'][l] * p['da_lk1'][l]))
               - jnp.exp(jnp.sum(p['da_lq2'][l] * p['da_lk2'][l])) + lam_init).reshape(1)
        g_sub = jnp.tile(p['da_subln_g'][l], DA_HEADS).reshape(1, 256)
        oa = _diff_attention(proj, lam, g_sub, gmat, nb, t, 1.0 - lam_init)
        ob = _mla_attention(proj, nb, t)
        oc = _dil_attention(proj, nb, t)
        od = _na_attention(proj, _na_bias_table(p['na_rpb'][l]), nb, t)
        x1, h2, aff_t = _outproj(oa, ob, oc, od, x, g1, sc2, sh2, p['w_out'][l].astype(BF16),
                                 p['ln1_g'][l].reshape(1, -1), p['ln1_b'][l].reshape(1, -1),
                                 p['w_router'][l].T, nb, t)
        wg = p['w_e_gate'][l].astype(BF16)
        wu = p['w_e_up'][l].astype(BF16)
        wd = p['w_e_down'][l].astype(BF16)
        gs = _select_groups(aff_t, group_tokens)
        rt = _routing_tables(gs, group_tokens)
        xe = _dispatch(gs, h2, rt, u_mat)
        ye = _experts(xe, rt['total'], wg, wu, wd)
        x = _combine_postnorm(gs.T, ye, rt, l_mat, x1, g2, p['ln2_g'][l].reshape(1, -1),
                              p['ln2_b'][l].reshape(1, -1), t)
    return x


def kernel(x_prompt, x_sample, c_prompt, c_sample, w_in, w_uq, w_ukv, q_norm_g, kv_norm_g, da_lq1,
           da_lk1, da_lq2, da_lk2, da_subln_g, na_rpb, w_out, w_ada, b_ada, ln1_g, ln1_b, ln2_g,
           ln2_b, w_router, w_e_gate, w_e_up, w_e_down):
    p = dict(w_in=w_in, w_uq=w_uq, w_ukv=w_ukv, q_norm_g=q_norm_g, kv_norm_g=kv_norm_g,
             da_lq1=da_lq1, da_lk1=da_lk1, da_lq2=da_lq2, da_lk2=da_lk2, da_subln_g=da_subln_g,
             na_rpb=na_rpb, w_out=w_out, w_ada=w_ada, b_ada=b_ada, ln1_g=ln1_g, ln1_b=ln1_b,
             ln2_g=ln2_g, ln2_b=ln2_b, w_router=w_router, w_e_gate=w_e_gate, w_e_up=w_e_up,
             w_e_down=w_e_down)
    bp, t, d = x_prompt.shape
    bs = x_sample.shape[0]
    assert x_sample.shape[1] == t
    nb = bp + bs
    x = jnp.concatenate([x_prompt.reshape(bp * t, d), x_sample.reshape(bs * t, d)], axis=0)
    c = jnp.concatenate([c_prompt, c_sample], axis=0)
    y = _trunk(x, c, (bp * t, bs * t), nb, t, p)
    return y[:bp * t].reshape(bp, t, d), y[bp * t:].reshape(bs, t, d)
```

```python
import functools
import math

import jax
import jax.numpy as jnp
import numpy as np
from jax import lax
from jax.experimental import pallas as pl
from jax.experimental.pallas import tpu as pltpu

F32 = jnp.float32
BF16 = jnp.bfloat16

D_MODEL = 1024
DEPTH = 2
GRID_W = 64
GROUP_W = 256
DA_HEADS, DA_DV, DA_DQK = 4, 64, 32
MLA_HEADS, MLA_Q_RANK, MLA_KV_RANK, MLA_NOPE, MLA_ROPE, MLA_DV = 4, 192, 128, 64, 32, 64
MLA_ROPE_THETA = 10000.0
DIL_HEADS, DIL_DH = 4, 64
DIL_PATTERNS = ((128, 1), (512, 4), (2048, 16))
NA_HEADS, NA_DH, NA_KR, NA_KC = 4, 64, 8, 16
ROPE_THETA = 500000.0
ROPE_FRACTION = 4
N_EXPERTS = 16
EC_FACTOR = 2
D_FF = 2816
DEEPNORM_ALPHA = (2.0 * DEPTH) ** 0.25
NEG_INF = -1e30
LOG2E = math.log2(math.e)
LN_EPS = 1e-5
RMS_EPS = 1e-6

LANES = 128
MXU_DIM = 256
VMEM_LIMIT = 56 * 1024 * 1024

S_AQ, S_AK, S_BQ0, S_BQ1, S_BK0, S_BK1, S_CQ, S_CK, S_CV, S_DQ, S_DK, S_DV = range(12)
N_SLOTS = 12
V_A, V_B = 0, 4
N_VSLOTS = 8
V_ONE_LANE = 64
W_ALL_COLS = 12 * 256


def _cparams(sem):
    return pltpu.CompilerParams(dimension_semantics=sem, vmem_limit_bytes=VMEM_LIMIT)


def _split_bf16(a):
    hi = a.astype(BF16)
    lo = (a - hi.astype(F32)).astype(BF16)
    return hi, lo


def _mod_kernel(c_ref, w_ref, b_ref, o_ref):
    c = c_ref[...]
    a = c * (1.0 / (1.0 + jnp.exp(-c)))
    a_hi, a_lo = _split_bf16(a)
    w_hi, w_lo = _split_bf16(w_ref[...])
    acc = jnp.dot(a_hi, w_hi, preferred_element_type=F32)
    acc += jnp.dot(a_hi, w_lo, preferred_element_type=F32)
    acc += jnp.dot(a_lo, w_hi, preferred_element_type=F32)
    o_ref[...] = acc + b_ref[...]


def _modulation(c, w_ada, b_ada):
    nb, d = c.shape
    n_out = w_ada.shape[1]
    tn = 1536
    return pl.pallas_call(
        _mod_kernel,
        grid=(n_out // tn,),
        in_specs=[pl.BlockSpec((nb, d), lambda j: (0, 0)),
                  pl.BlockSpec((d, tn), lambda j: (0, j)),
                  pl.BlockSpec((1, tn), lambda j: (0, j))],
        out_specs=pl.BlockSpec((nb, tn), lambda j: (0, j)),
        out_shape=jax.ShapeDtypeStruct((nb, n_out), F32),
        compiler_params=_cparams(("arbitrary",)),
        name="adaln_mod",
    )(c, w_ada, b_ada.reshape(1, n_out))


def _rope_tables(t, width, group, rot, theta):
    half = rot // 2
    inv = theta ** (-jnp.arange(half, dtype=F32) / half)
    ang = jnp.arange(t, dtype=F32)[:, None] * inv[None, :]
    cos, sin = jnp.cos(ang), jnp.sin(ang)
    ones = jnp.ones((t, group - rot), F32)
    zeros = jnp.zeros((t, group - rot), F32)
    c_g = jnp.concatenate([cos, cos, ones], axis=1)
    s_g = jnp.concatenate([-sin, sin, zeros], axis=1)
    reps = width // group
    return jnp.tile(c_g, (1, reps)), jnp.tile(s_g, (1, reps))


def _apply_rope(x, c_tab, s_tab, group, rot):
    width = x.shape[-1]
    half = rot // 2
    lane = lax.broadcasted_iota(jnp.int32, (1, width), 1)
    first = (lane % group) < half
    fwd = pltpu.roll(x, width - half, 1)
    bwd = pltpu.roll(x, half, 1)
    return x * c_tab + s_tab * jnp.where(first, fwd, bwd)


def _inproj_kernel(x_ref, sc_ref, sh_ref, w_ref, wuq_ref, wukv_ref, gq_ref, gkv_ref,
                   ca_ref, sa_ref, cc_ref, scc_ref, cm_ref, sm_ref, o_ref, ov_ref):
    h = (x_ref[...] * (1.0 + sc_ref[...]) + sh_ref[...]).astype(BF16)

    def proj(col, width):
        return jnp.dot(h, w_ref[:, col:col + width], preferred_element_type=F32)

    one_lane = lax.broadcasted_iota(jnp.int32, (1, LANES), 1) == V_ONE_LANE

    def put_values(first_slot, vals):
        for hh in range(4):
            v = vals[:, LANES * hh:LANES * (hh + 1)]
            ov_ref[first_slot + hh] = jnp.where(one_lane, 1.0, v).astype(BF16)

    sa_scale = DA_DQK ** -0.5 * LOG2E
    sb_scale = (MLA_NOPE + MLA_ROPE) ** -0.5 * LOG2E
    sc_scale = DIL_DH ** -0.5 * LOG2E
    sd_scale = NA_DH ** -0.5 * LOG2E
    a_rot = DA_DQK // ROPE_FRACTION
    c_rot = DIL_DH // ROPE_FRACTION

    ca, sa = ca_ref[...], sa_ref[...]
    o_ref[S_AQ] = (_apply_rope(proj(0, 256), ca, sa, DA_DQK, a_rot) * sa_scale).astype(BF16)
    o_ref[S_AK] = _apply_rope(proj(256, 256), ca, sa, DA_DQK, a_rot).astype(BF16)
    put_values(V_A, proj(512, 512))
    cc, scc = cc_ref[...], scc_ref[...]
    o_ref[S_CQ] = (_apply_rope(proj(1024, 256), cc, scc, DIL_DH, c_rot) * sc_scale).astype(BF16)
    o_ref[S_CK] = _apply_rope(proj(1280, 256), cc, scc, DIL_DH, c_rot).astype(BF16)
    o_ref[S_CV] = proj(1536, 256).astype(BF16)
    o_ref[S_DQ] = (proj(1792, 256) * sd_scale).astype(BF16)
    o_ref[S_DK] = proj(2048, 256).astype(BF16)
    o_ref[S_DV] = proj(2304, 256).astype(BF16)

    cm, sm = cm_ref[...], sm_ref[...]
    cq = proj(2560, 256)
    cq = cq * lax.rsqrt(jnp.sum(cq * cq, -1, keepdims=True) * (1.0 / MLA_Q_RANK) + RMS_EPS)
    cq = (cq * gq_ref[...]).astype(BF16)
    q2 = jnp.dot(cq, wuq_ref[...], preferred_element_type=F32)
    for p in range(2):
        qp = q2[:, 256 * p:256 * (p + 1)]
        o_ref[S_BQ0 + p, :, 0:128] = (qp[:, 0:128] * sb_scale).astype(BF16)
        o_ref[S_BQ0 + p, :, 128:256] = (
            _apply_rope(qp[:, 128:256], cm, sm, MLA_ROPE, MLA_ROPE) * sb_scale).astype(BF16)
    ckv = proj(2816, 128)
    ckv = ckv * lax.rsqrt(jnp.mean(ckv * ckv, -1, keepdims=True) + RMS_EPS)
    ckv = (ckv * gkv_ref[...]).astype(BF16)
    kv = jnp.dot(ckv, wukv_ref[...], preferred_element_type=F32)
    kr = _apply_rope(proj(2944, 128), cm, sm, MLA_ROPE, MLA_ROPE).astype(BF16)
    for p in range(2):
        o_ref[S_BK0 + p, :, 0:128] = kv[:, 128 * p:128 * (p + 1)].astype(BF16)
        o_ref[S_BK0 + p, :, 128:256] = kr
    put_values(V_B, kv[:, 256:768])


def _inproj(x, sc, sh, w_all, wuq, wukv, gq, gkv, tabs, nb, t):
    tm = 512
    nt = t // tm
    ntok = nb * t
    ca, sa, cc, scc, cm, sm = tabs
    full = lambda shape: pl.BlockSpec(shape, lambda j, b: tuple(0 for _ in shape))
    tab = lambda w: pl.BlockSpec((tm, w), lambda j, b: (j, 0))
    return pl.pallas_call(
        _inproj_kernel,
        grid=(nt, nb),
        in_specs=[pl.BlockSpec((tm, D_MODEL), lambda j, b: (b * nt + j, 0)),
                  pl.BlockSpec((None, 1, D_MODEL), lambda j, b: (b, 0, 0)),
                  pl.BlockSpec((None, 1, D_MODEL), lambda j, b: (b, 0, 0)),
                  full((D_MODEL, W_ALL_COLS)), full((256, 512)), full((128, 768)),
                  full((1, 256)), full((1, 128)),
                  tab(256), tab(256), tab(256), tab(256), tab(128), tab(128)],
        out_specs=[pl.BlockSpec((N_SLOTS, tm, 256), lambda j, b: (0, b * nt + j, 0)),
                   pl.BlockSpec((N_VSLOTS, tm, LANES), lambda j, b: (0, b * nt + j, 0))],
        out_shape=[jax.ShapeDtypeStruct((N_SLOTS, ntok, 256), BF16),
                   jax.ShapeDtypeStruct((N_VSLOTS, ntok, LANES), BF16)],
        compiler_params=_cparams(("arbitrary", "arbitrary")),
        name="in_proj",
    )(x, sc, sh, w_all, wuq, wukv, gq, gkv, ca, sa, cc, scc, cm, sm)


def _lane_mask(width, ranges):
    lane = lax.broadcasted_iota(jnp.int32, (1, width), 1)
    m = None
    for lo, hi in ranges:
        r = (lane >= lo) & (lane < hi)
        m = r if m is None else (m | r)
    return m


SOFTMAX_SLAB = 32


def _chain_scratch(tq, kc):
    return [pltpu.VMEM((tq, kc), F32), pltpu.VMEM((tq, kc), F32), pltpu.VMEM((tq, kc), BF16),
            pltpu.VMEM((tq, LANES), F32), pltpu.VMEM((tq, LANES), F32), pltpu.VMEM((tq, LANES), F32)]


CHAIN_REFS = 6


def _online_attention(qms, k_ref, v_refs, t, kc, chains):
    tq = qms[0].shape[0]
    n_chunks = t // kc
    assert n_chunks % 2 == 0
    nt = (((1,), (1,)), ((), ()))
    for (_, _, _, m_ref, _, acc_ref) in chains:
        m_ref[...] = jnp.full(m_ref.shape, NEG_INF, F32)
        acc_ref[...] = jnp.zeros(acc_ref.shape, F32)

    def scores(chunk, slot):
        k = k_ref[pl.ds(pl.multiple_of(chunk * kc, kc), kc), :]
        for qm, chain in zip(qms, chains):
            chain[slot][...] = lax.dot_general(qm, k, nt, preferred_element_type=F32)

    def softmax_pv(chunk, slot):
        for v_ref, chain in zip(v_refs, chains):
            v = v_ref[pl.ds(pl.multiple_of(chunk * kc, kc), kc), :]
            s_ref = chain[slot]
            (p_ref, m_ref, a_ref, acc_ref) = chain[2:]
            for r in range(tq // SOFTMAX_SLAB):
                rows = slice(r * SOFTMAX_SLAB, (r + 1) * SOFTMAX_SLAB)
                s = s_ref[rows, :]
                m_prev = m_ref[rows, :]
                m_new = jnp.maximum(m_prev, jnp.max(s, -1, keepdims=True))
                d = s - jnp.concatenate([m_new] * (kc // LANES), axis=1)
                p_ref[rows, :] = jnp.exp2(d.astype(BF16))
                a_ref[rows, :] = jnp.exp2(m_prev - m_new)
                m_ref[rows, :] = m_new
            acc_ref[...] = a_ref[...] * acc_ref[...] + jnp.dot(p_ref[...], v,
                                                              preferred_element_type=F32)

    def body(j, _):
        scores(2 * j + 1, 1)
        softmax_pv(2 * j, 0)
        scores(jnp.minimum(2 * j + 2, n_chunks - 1), 0)
        softmax_pv(2 * j + 1, 1)
        return 0

    scores(0, 0)
    lax.fori_loop(0, n_chunks // 2, body, 0)
    outs = []
    for chain in chains:
        acc = chain[5][...]
        den = jnp.broadcast_to(acc[:, V_ONE_LANE:V_ONE_LANE + 1], acc.shape)
        outs.append(acc * (1.0 / den))
    return outs


def _place_head(pair_ref, o, j):
    lane = lax.broadcasted_iota(jnp.int32, (1, LANES), 1)
    low = lane < V_ONE_LANE
    shifted = pltpu.roll(o, V_ONE_LANE, 1)
    cur = pair_ref[...]
    pair_ref[...] = jnp.where(low, jnp.where(j == 0, o, cur), jnp.where(j == 1, shifted, cur))


def _group_mean_sq(x, gmat):
    sq = x * x
    hi, lo = _split_bf16(sq)
    return (jnp.dot(hi, gmat, preferred_element_type=F32)
            + jnp.dot(lo, gmat, preferred_element_type=F32))


def _diff_attn_kernel(lam_ref, q_ref, k_ref, v_ref, g_ref, gmat_ref, o_ref, acc_ref, *scratch, t, kc,
                      out_scale):
    chains = (scratch[:CHAIN_REFS], scratch[CHAIN_REFS:])
    q = q_ref[...]
    lam = lam_ref[0]
    lane = lax.broadcasted_iota(jnp.int32, (1, 256), 1)
    zero = jnp.zeros_like(q)

    def head(h, _):
        qms = []
        for c in range(2):
            lo = (2 * h + c) * DA_DQK
            qms.append(jnp.where((lane >= lo) & (lane < lo + DA_DQK), q, zero))
        v_h = v_ref.at[h]
        o_1, o_2 = _online_attention(qms, k_ref, [v_h, v_h], t, kc, chains)
        _place_head(acc_ref.at[h // 2], o_1 - lam * o_2, h % 2)
        return 0

    acc_ref[...] = jnp.zeros_like(acc_ref)
    lax.fori_loop(0, DA_HEADS, head, 0)
    o = jnp.concatenate([acc_ref[0], acc_ref[1]], axis=1)
    ms = _group_mean_sq(o, gmat_ref[...])
    o_ref[...] = (o * lax.rsqrt(ms + RMS_EPS) * g_ref[...] * out_scale).astype(o_ref.dtype)


def _diff_attention(proj, projv, lam, g_tiled, gmat, nb, t, out_scale):
    tq, kc = 512, 512
    nq = t // tq
    kern = functools.partial(_diff_attn_kernel, t=t, kc=kc, out_scale=out_scale)
    return pl.pallas_call(
        kern,
        grid=(nb, nq),
        in_specs=[pl.BlockSpec(memory_space=pltpu.SMEM),
                  pl.BlockSpec((None, tq, 256), lambda b, i: (S_AQ, b * nq + i, 0)),
                  pl.BlockSpec((None, t, 256), lambda b, i: (S_AK, b, 0)),
                  pl.BlockSpec((DA_HEADS, t, LANES), lambda b, i: (V_A // DA_HEADS, b, 0)),
                  pl.BlockSpec((1, 256), lambda b, i: (0, 0)),
                  pl.BlockSpec((256, 256), lambda b, i: (0, 0))],
        out_specs=pl.BlockSpec((tq, 256), lambda b, i: (b * nq + i, 0)),
        out_shape=jax.ShapeDtypeStruct((nb * t, 256), BF16),
        scratch_shapes=[pltpu.VMEM((2, tq, LANES), F32)] + 2 * _chain_scratch(tq, kc),
        compiler_params=_cparams(("arbitrary", "arbitrary")),
        name="diff_attn",
    )(lam, proj, proj, projv, g_tiled, gmat)


def _mla_attn_kernel(q_ref, k_ref, v_ref, o_ref, acc_ref, *scratch, t, kc):
    chains = (scratch[:CHAIN_REFS], scratch[CHAIN_REFS:])
    lane = lax.broadcasted_iota(jnp.int32, (1, 256), 1)

    def pair(p, _):
        q = q_ref[p]
        qms = []
        for j in range(2):
            nope = (lane >= j * MLA_NOPE) & (lane < (j + 1) * MLA_NOPE)
            rope = (lane >= 128 + j * MLA_ROPE) & (lane < 128 + (j + 1) * MLA_ROPE)
            qms.append(jnp.where(nope | rope, q, jnp.zeros_like(q)))
        outs = _online_attention(qms, k_ref.at[p], [v_ref.at[2 * p], v_ref.at[2 * p + 1]], t, kc,
                                 chains)
        for j in range(2):
            _place_head(acc_ref.at[p], outs[j], j)
        return 0

    acc_ref[...] = jnp.zeros_like(acc_ref)
    lax.fori_loop(0, MLA_HEADS // 2, pair, 0)
    o_ref[...] = jnp.concatenate([acc_ref[0], acc_ref[1]], axis=1).astype(o_ref.dtype)


def _mla_attention(proj, projv, nb, t):
    tq, kc = 512, 512
    nq = t // tq
    kern = functools.partial(_mla_attn_kernel, t=t, kc=kc)
    return pl.pallas_call(
        kern,
        grid=(nb, nq),
        in_specs=[pl.BlockSpec((2, tq, 256), lambda b, i: (S_BQ0 // 2, b * nq + i, 0)),
                  pl.BlockSpec((2, t, 256), lambda b, i: (S_BK0 // 2, b, 0)),
                  pl.BlockSpec((MLA_HEADS, t, LANES), lambda b, i: (V_B // MLA_HEADS, b, 0))],
        out_specs=pl.BlockSpec((tq, 256), lambda b, i: (b * nq + i, 0)),
        out_shape=jax.ShapeDtypeStruct((nb * t, 256), BF16),
        scratch_shapes=[pltpu.VMEM((2, tq, LANES), F32)] + 2 * _chain_scratch(tq, kc),
        compiler_params=_cparams(("arbitrary", "arbitrary")),
        name="mla_attn",
    )(proj, proj, projv)


DIL_REACH = max(w // 2 for w, _ in DIL_PATTERNS)


def _dil_attn_kernel(q_ref, k_ref, v_ref, o_ref, bias_ref, acc_ref, *, t, tq, band):
    i = pl.program_id(1)

    def band_start(blk):
        return jnp.clip(blk * tq - DIL_REACH, 0, t - band)

    t0 = i * tq
    start = pl.multiple_of(band_start(i), tq)
    shift = start - t0
    prev_shift = band_start(i - 1) - (i - 1) * tq

    @pl.when((i == 0) | (shift != prev_shift))
    def _():
        qi = lax.broadcasted_iota(jnp.int32, (tq, band), 0)
        kj = lax.broadcasted_iota(jnp.int32, (tq, band), 1)
        delta = kj - qi + shift
        ad = jnp.abs(delta)
        cnt = jnp.zeros((tq, band), F32)
        for window, dil in DIL_PATTERNS:
            ok = (ad <= window // 2) & ((delta & (dil - 1)) == 0)
            cnt = cnt + jnp.where(ok, 1.0, 0.0)
        bias_ref[...] = jnp.where(cnt > 2.5, math.log2(3.0),
                                  jnp.where(cnt > 1.5, 1.0,
                                            jnp.where(cnt > 0.5, 0.0, NEG_INF)))

    q = q_ref[...]
    lane = lax.broadcasted_iota(jnp.int32, (1, 256), 1)
    kb = k_ref.at[pl.ds(start, band), :]
    vb = v_ref.at[pl.ds(start, band), :]

    def pair(hp, _):
        out = acc_ref[...]
        for j in range(2):
            h = 2 * hp + j
            sel = (lane >= h * DIL_DH) & (lane < (h + 1) * DIL_DH)
            qm = jnp.where(sel, q, jnp.zeros_like(q))
            s = lax.dot_general(qm, kb[...], (((1,), (1,)), ((), ())),
                                preferred_element_type=F32)
            s = s + bias_ref[...]
            m = jnp.max(s, -1, keepdims=True)
            p = jnp.exp2(s - m)
            l = jnp.sum(p, -1, keepdims=True)
            o_h = jnp.dot(p.astype(BF16), vb[...], preferred_element_type=F32) * (1.0 / l)
            out = jnp.where(sel, o_h, out)
        acc_ref[...] = out
        return 0

    acc_ref[...] = jnp.zeros_like(acc_ref)
    lax.fori_loop(0, DIL_HEADS // 2, pair, 0)
    o_ref[...] = acc_ref[...].astype(o_ref.dtype)


def _dil_attention(proj, nb, t):
    tq = 256
    band = min(t, tq + 2 * DIL_REACH)
    nq = t // tq
    kern = functools.partial(_dil_attn_kernel, t=t, tq=tq, band=band)
    return pl.pallas_call(
        kern,
        grid=(nb, nq),
        in_specs=[pl.BlockSpec((None, tq, 256), lambda b, i: (S_CQ, b * nq + i, 0)),
                  pl.BlockSpec((None, t, 256), lambda b, i: (S_CK, b, 0)),
                  pl.BlockSpec((None, t, 256), lambda b, i: (S_CV, b, 0))],
        out_specs=pl.BlockSpec((tq, 256), lambda b, i: (b * nq + i, 0)),
        out_shape=jax.ShapeDtypeStruct((nb * t, 256), BF16),
        scratch_shapes=[pltpu.VMEM((tq, band), F32), pltpu.VMEM((tq, 256), F32)],
        compiler_params=_cparams(("arbitrary", "arbitrary")),
        name="dil_attn",
    )(proj, proj, proj)


def _na_bias_table(rpb):
    c = np.arange(GRID_W)
    cs = np.clip(c - NA_KC // 2, 0, GRID_W - NA_KC)
    colmask = (c[None, :] >= cs[:, None]) & (c[None, :] < cs[:, None] + NA_KC)
    rows = jnp.stack([rpb[:, si:si + NA_KR, :] for si in range(NA_KR)], axis=1).astype(F32)
    edge = GRID_W - NA_KC
    padded = jnp.concatenate([jnp.repeat(rows[..., :1], edge, axis=-1), rows,
                              jnp.repeat(rows[..., -1:], edge, axis=-1)], axis=-1)
    b = jnp.stack([padded[..., GRID_W - 1 - qc:2 * GRID_W - 1 - qc] for qc in range(GRID_W)],
                  axis=2)
    b = jnp.where(colmask[None, None, :, None, :], b * LOG2E, NEG_INF)
    b = b.transpose(1, 0, 2, 3, 4)
    return b.reshape(NA_KR, rpb.shape[0] * GRID_W, NA_KR * GRID_W)


def _na_attn_kernel(q_ref, k_ref, v_ref, tb_ref, o_ref, *, rows, rg):
    g = pl.program_id(1)
    lane = lax.broadcasted_iota(jnp.int32, (1, 256), 1)
    nk = NA_KR * GRID_W
    sels = [(lane >= h * NA_DH) & (lane < (h + 1) * NA_DH) for h in range(NA_HEADS)]

    def row(r, _):
        grow = g * rg + r
        rs = jnp.clip(grow - NA_KR // 2, 0, rows - NA_KR)
        si = rs - grow + (NA_KR - 1)
        q = q_ref[pl.ds(pl.multiple_of(r * GRID_W, GRID_W), GRID_W), :]
        koff = pl.multiple_of(rs * GRID_W, GRID_W)
        kb = k_ref[pl.ds(koff, nk), :]
        vb = v_ref[pl.ds(koff, nk), :]
        q4 = jnp.concatenate([jnp.where(sel, q, jnp.zeros_like(q)) for sel in sels], axis=0)
        s = lax.dot_general(q4, kb, (((1,), (1,)), ((), ())), preferred_element_type=F32)
        s = s + tb_ref[si]
        m = jnp.max(s, -1, keepdims=True)
        p = jnp.exp2(s - m)
        l = jnp.sum(p, -1, keepdims=True)
        o4 = jnp.dot(p.astype(BF16), vb, preferred_element_type=F32) * (1.0 / l)
        out = jnp.zeros((GRID_W, 256), F32)
        for h, sel in enumerate(sels):
            out = jnp.where(sel, o4[h * GRID_W:(h + 1) * GRID_W], out)
        o_ref[pl.ds(pl.multiple_of(r * GRID_W, GRID_W), GRID_W), :] = out.astype(o_ref.dtype)
        return 0

    lax.fori_loop(0, rg, row, 0, unroll=2)


def _na_attention(proj, tb, nb, t):
    rows = t // GRID_W
    assert rows >= NA_KR
    rg = 8
    ng = rows // rg
    tq = rg * GRID_W
    kern = functools.partial(_na_attn_kernel, rows=rows, rg=rg)
    return pl.pallas_call(
        kern,
        grid=(nb, ng),
        in_specs=[pl.BlockSpec((None, tq, 256), lambda b, i: (S_DQ, b * ng + i, 0)),
                  pl.BlockSpec((None, t, 256), lambda b, i: (S_DK, b, 0)),
                  pl.BlockSpec((None, t, 256), lambda b, i: (S_DV, b, 0)),
                  pl.BlockSpec(tb.shape, lambda b, i: (0, 0, 0))],
        out_specs=pl.BlockSpec((tq, 256), lambda b, i: (b * ng + i, 0)),
        out_shape=jax.ShapeDtypeStruct((nb * t, 256), BF16),
        compiler_params=_cparams(("arbitrary", "arbitrary")),
        name="na_attn",
    )(proj, proj, proj, tb)


def _layer_norm(y, g, b):
    mu = jnp.mean(y, -1, keepdims=True)
    yc = y - mu
    var = jnp.mean(yc * yc, -1, keepdims=True)
    return yc * lax.rsqrt(var + LN_EPS) * g + b


def _outproj_kernel(oa_ref, ob_ref, oc_ref, od_ref, x_ref, g1_ref, sc2_ref, sh2_ref, w_ref,
                    lg_ref, lb_ref, wr_ref, x1_ref, h2_ref, aff_ref):
    m = jnp.dot(oa_ref[...], w_ref[0:256, :], preferred_element_type=F32)
    m += jnp.dot(ob_ref[...], w_ref[256:512, :], preferred_element_type=F32)
    m += jnp.dot(oc_ref[...], w_ref[512:768, :], preferred_element_type=F32)
    m += jnp.dot(od_ref[...], w_ref[768:1024, :], preferred_element_type=F32)
    y = DEEPNORM_ALPHA * x_ref[...] + (1.0 + g1_ref[...]) * m
    x1 = _layer_norm(y, lg_ref[...], lb_ref[...])
    x1_ref[...] = x1
    h2 = x1 * (1.0 + sc2_ref[...]) + sh2_ref[...]
    h2_ref[...] = h2.astype(BF16)
    h_hi, h_lo = _split_bf16(h2)
    w_hi, w_lo = _split_bf16(wr_ref[...])
    nt = (((1,), (1,)), ((), ()))
    lg = lax.dot_general(w_hi, h_hi, nt, preferred_element_type=F32)
    lg += lax.dot_general(w_hi, h_lo, nt, preferred_element_type=F32)
    lg += lax.dot_general(w_lo, h_hi, nt, preferred_element_type=F32)
    lg = lg - jnp.max(lg, 0, keepdims=True)
    e = jnp.exp(lg)
    aff_ref[...] = e / jnp.sum(e, 0, keepdims=True)


def _outproj(oa, ob, oc, od, x, g1, sc2, sh2, w_out, ln_g, ln_b, w_router_t, nb, t):
    tm = 512
    nt = t // tm
    ntok = nb * t
    tok = lambda w: pl.BlockSpec((tm, w), lambda i: (i, 0))
    per_b = pl.BlockSpec((None, 1, D_MODEL), lambda i: (i // nt, 0, 0))
    full = lambda shape: pl.BlockSpec(shape, lambda i: tuple(0 for _ in shape))
    return pl.pallas_call(
        _outproj_kernel,
        grid=(ntok // tm,),
        in_specs=[tok(256), tok(256), tok(256), tok(256), tok(D_MODEL), per_b, per_b, per_b,
                  full((D_MODEL, D_MODEL)), full((1, D_MODEL)), full((1, D_MODEL)),
                  full((N_EXPERTS, D_MODEL))],
        out_specs=[tok(D_MODEL), tok(D_MODEL),
                   pl.BlockSpec((N_EXPERTS, tm), lambda i: (0, i))],
        out_shape=[jax.ShapeDtypeStruct((ntok, D_MODEL), F32),
                   jax.ShapeDtypeStruct((ntok, D_MODEL), BF16),
                   jax.ShapeDtypeStruct((N_EXPERTS, ntok), F32)],
        compiler_params=_cparams(("arbitrary",)),
        name="out_proj",
    )(oa, ob, oc, od, x, g1, sc2, sh2, w_out, ln_g, ln_b, w_router_t)


def _expert_kernel(x_ref, wg_ref, wu_ref, wd_ref, o_ref, acc_ref):
    f = pl.program_id(2)

    @pl.when(f == 0)
    def _():
        acc_ref[...] = jnp.zeros_like(acc_ref)

    x = x_ref[...]
    g = jnp.dot(x, wg_ref[...], preferred_element_type=F32)
    u = jnp.dot(x, wu_ref[...], preferred_element_type=F32)
    hmid = (g * (1.0 / (1.0 + jnp.exp(-g))) * u).astype(BF16)
    acc_ref[...] += jnp.dot(hmid, wd_ref[...], preferred_element_type=F32)

    @pl.when(f == pl.num_programs(2) - 1)
    def _():
        o_ref[...] = acc_ref[...].astype(o_ref.dtype)


def _experts(xe, slots, wg, wu, wd):
    ne, _, d = xe.shape
    tm = math.gcd(slots, 1024)
    tf = 256
    nf = D_FF // tf
    return pl.pallas_call(
        _expert_kernel,
        grid=(ne, slots // tm, nf),
        in_specs=[pl.BlockSpec((None, tm, d), lambda e, m, f: (e, m, 0)),
                  pl.BlockSpec((None, d, tf), lambda e, m, f: (e, 0, f)),
                  pl.BlockSpec((None, d, tf), lambda e, m, f: (e, 0, f)),
                  pl.BlockSpec((None, tf, d), lambda e, m, f: (e, f, 0))],
        out_specs=pl.BlockSpec((None, tm, d), lambda e, m, f: (e, m, 0)),
        out_shape=jax.ShapeDtypeStruct((ne, slots, d), BF16),
        scratch_shapes=[pltpu.VMEM((tm, d), F32)],
        compiler_params=_cparams(("arbitrary", "arbitrary", "arbitrary")),
        name="expert_ffn",
    )(xe, wg, wu, wd)


MOE_TM = 512
MOE_WIN = 128
ROW_ALIGN = 16


def _select_kernel(aff_ref, o_ref, *, cap):
    a = aff_ref[...]
    keys = lax.bitcast_convert_type(a, jnp.int32)
    ne, n = a.shape
    capf = float(cap)

    def count(mask):
        return jnp.sum(jnp.where(mask, 1.0, 0.0), axis=1, keepdims=True)

    def key_bit(b, thr):
        cand = thr | lax.shift_left(jnp.int32(1), 30 - b)
        return jnp.where(count(keys >= cand) >= capf, cand, thr)

    thr = lax.fori_loop(0, 31, key_bit, jnp.zeros((ne, 1), jnp.int32))
    above = keys > thr
    need = capf - count(above)
    idx = lax.broadcasted_iota(jnp.int32, (ne, n), 1)
    tie_idx = jnp.where(keys == thr, idx, jnp.int32(2 ** 30))
    nbits = max(1, (n - 1).bit_length())

    def idx_bit(b, j):
        cand = j | lax.shift_left(jnp.int32(1), nbits - 1 - b)
        return jnp.where(count(tie_idx < cand) < need, cand, j)

    j = lax.fori_loop(0, nbits, idx_bit, jnp.zeros((ne, 1), jnp.int32))
    sel = above | (tie_idx <= j)
    o_ref[...] = jnp.where(sel, a, -1.0)


def _select(aff, cap):
    ne, n = aff.shape
    return pl.pallas_call(
        functools.partial(_select_kernel, cap=cap),
        out_shape=jax.ShapeDtypeStruct((ne, n), F32),
        compiler_params=pltpu.CompilerParams(vmem_limit_bytes=VMEM_LIMIT),
        name="ec_select",
    )(aff)


def _routing_tables(gs, group_tokens):
    ne = gs.shape[0]
    a_l, off_l, cnt_l, lim_l = [], [], [], []
    tok0, slot0 = 0, 0
    for n in group_tokens:
        cap = EC_FACTOR * n // N_EXPERTS
        nt = n // MOE_TM
        sel = lax.slice_in_dim(gs, tok0, tok0 + n, axis=1) >= 0
        counts = jnp.sum(sel.reshape(ne, nt, MOE_TM), axis=-1, dtype=jnp.int32)
        s0 = slot0 + jnp.cumsum(counts, axis=1) - counts
        a = (s0 // ROW_ALIGN) * ROW_ALIGN
        a_l.append(a)
        off_l.append(s0 - a)
        cnt_l.append(counts)
        lim_l.append(jnp.full((nt,), slot0 + cap - MOE_WIN, jnp.int32))
        tok0 += n
        slot0 += cap
    a = jnp.concatenate(a_l, axis=1).T
    off = jnp.concatenate(off_l, axis=1).T
    end = off + jnp.concatenate(cnt_l, axis=1).T
    rounds = (end + MOE_WIN - 1) // MOE_WIN
    gp = (end // ROW_ALIGN) * ROW_ALIGN
    return dict(a=a.reshape(-1).astype(jnp.int32), nr=rounds.reshape(-1).astype(jnp.int32),
                nrounds=jnp.max(rounds, axis=1).astype(jnp.int32), lim=jnp.concatenate(lim_l),
                off_col=off.astype(F32)[:, :, None], off_row=off.astype(F32)[:, None, :],
                off16=jnp.repeat(off.astype(F32), ROW_ALIGN, axis=1)[:, :, None],
                gp16=jnp.repeat(gp.astype(F32), ROW_ALIGN, axis=1)[:, :, None],
                total=slot0)


def _dispatch_kernel(a_tab, nrounds, nr_tab, gs_ref, off_ref, off16_ref, gp16_ref, x_ref, u_ref, xe_ref,
                     pos_ref, c_ref, c2_ref, stage_ref, carry_ref, sem, rc_ref):
    j = pl.program_id(0)
    ne, tm = gs_ref.shape
    d = x_ref.shape[1]
    win = MOE_WIN

    @pl.when(j == 0)
    def _():
        carry_ref[...] = jnp.zeros_like(carry_ref)
        rc_ref[0] = 0

    sel = gs_ref[...] >= 0.0
    rank = jnp.dot(jnp.where(sel, 1.0, 0.0).astype(BF16), u_ref[...], preferred_element_type=F32)
    pos_ref[...] = jnp.where(sel, rank + off_ref[...], -1.0)

    def wait_round(slot):
        def one(_, c):
            pltpu.make_async_copy(stage_ref.at[slot, pl.ds(0, win)], xe_ref.at[0, pl.ds(0, win)],
                                  sem.at[slot]).wait()
            return c
        lax.fori_loop(0, rc_ref[1 + slot], one, 0)

    def round_body(r, _):
        k = lax.broadcasted_iota(jnp.int32, (win, tm), 0).astype(F32) + (r * win).astype(F32)
        for e in range(ne):
            c_ref[e * win:(e + 1) * win, :] = jnp.where(pos_ref[e:e + 1, :] == k, 1.0, 0.0).astype(BF16)
        slot = rc_ref[0] % 2
        for nb in range(d // MXU_DIM):
            cols = slice(nb * MXU_DIM, (nb + 1) * MXU_DIM)
            stage_ref[slot, :, cols] = jnp.dot(c_ref[...], x_ref[:, cols],
                                               preferred_element_type=F32).astype(BF16)

        @pl.when(r == 0)
        def _():
            k16 = lax.broadcasted_iota(jnp.int32, (ROW_ALIGN, 1), 0).astype(F32)
            for e in range(ne):
                keep = k16 < off16_ref[e * ROW_ALIGN:(e + 1) * ROW_ALIGN, :]
                rows = pl.ds(e * win, ROW_ALIGN)
                stage_ref[slot, rows, :] = jnp.where(
                    keep, carry_ref[e * ROW_ALIGN:(e + 1) * ROW_ALIGN, :], stage_ref[slot, rows, :])

        @pl.when(rc_ref[0] > 0)
        def _():
            wait_round(1 - slot)

        rc_ref[1 + slot] = 0
        for e in range(ne):
            @pl.when(r < nr_tab[j * ne + e])
            def _():
                dst = pl.multiple_of(a_tab[j * ne + e] + r * win, ROW_ALIGN)
                pltpu.make_async_copy(stage_ref.at[slot, pl.ds(e * win, win)],
                                      xe_ref.at[e, pl.ds(dst, win)], sem.at[slot]).start()
                rc_ref[1 + slot] = rc_ref[1 + slot] + 1
        rc_ref[0] = rc_ref[0] + 1
        return 0

    lax.fori_loop(0, nrounds[j], round_body, 0)

    k16 = lax.broadcasted_iota(jnp.int32, (ROW_ALIGN, 1), 0).astype(F32)
    for e in range(ne):
        rows = slice(e * ROW_ALIGN, (e + 1) * ROW_ALIGN)
        c2_ref[rows, :] = jnp.where(pos_ref[e:e + 1, :] == gp16_ref[rows, :] + k16, 1.0, 0.0).astype(BF16)
    kk = jnp.concatenate([k16] * ne, axis=0)
    keep_old = (gp16_ref[...] == 0.0) & (kk < off16_ref[...])
    for nb in range(d // MXU_DIM):
        cols = slice(nb * MXU_DIM, (nb + 1) * MXU_DIM)
        new = jnp.dot(c2_ref[...], x_ref[:, cols], preferred_element_type=F32).astype(BF16)
        carry_ref[:, cols] = jnp.where(keep_old, carry_ref[:, cols], new)

    @pl.when((j == pl.num_programs(0) - 1) & (rc_ref[0] > 0))
    def _():
        wait_round((rc_ref[0] - 1) % 2)


def _dispatch(gs, h2, tabs, u_mat):
    ne, ntok = gs.shape
    d = h2.shape[1]
    nt = ntok // MOE_TM
    rows = tabs['total'] + MOE_WIN
    grid_spec = pltpu.PrefetchScalarGridSpec(
        num_scalar_prefetch=3,
        grid=(nt,),
        in_specs=[pl.BlockSpec((ne, MOE_TM), lambda j, *_: (0, j)),
                  pl.BlockSpec((None, ne, 1), lambda j, *_: (j, 0, 0)),
                  pl.BlockSpec((None, ne * ROW_ALIGN, 1), lambda j, *_: (j, 0, 0)),
                  pl.BlockSpec((None, ne * ROW_ALIGN, 1), lambda j, *_: (j, 0, 0)),
                  pl.BlockSpec((MOE_TM, d), lambda j, *_: (j, 0)),
                  pl.BlockSpec((MOE_TM, MOE_TM), lambda j, *_: (0, 0))],
        out_specs=pl.BlockSpec(memory_space=pl.ANY),
        scratch_shapes=[pltpu.VMEM((ne, MOE_TM), F32),
                        pltpu.VMEM((ne * MOE_WIN, MOE_TM), BF16),
                        pltpu.VMEM((ne * ROW_ALIGN, MOE_TM), BF16),
                        pltpu.VMEM((2, ne * MOE_WIN, d), BF16),
                        pltpu.VMEM((ne * ROW_ALIGN, d), BF16),
                        pltpu.SemaphoreType.DMA((2,)),
                        pltpu.SMEM((3,), jnp.int32)])
    return pl.pallas_call(
        _dispatch_kernel,
        grid_spec=grid_spec,
        out_shape=jax.ShapeDtypeStruct((ne, rows, d), BF16),
        compiler_params=_cparams(("arbitrary",)),
        name="ec_dispatch",
    )(tabs['a'], tabs['nrounds'], tabs['nr'], gs, tabs['off_col'], tabs['off16'], tabs['gp16'], h2,
      u_mat)


def _combine_kernel(a_tab, nrounds, lim_tab, gs_ref, off_ref, l_ref, ye_ref, x1_ref, g2_ref, lg_ref,
                    lb_ref, o_ref, p_ref, y_ref, acc_ref, sem):
    j = pl.program_id(0)
    tm, ne = gs_ref.shape
    d = x1_ref.shape[1]
    win = MOE_WIN
    gs = gs_ref[...]
    sel = gs >= 0.0
    rank = jnp.dot(l_ref[...], jnp.where(sel, 1.0, 0.0).astype(BF16), preferred_element_type=F32)
    pos = jnp.where(sel, rank + off_ref[...], -1.0)
    gate = jnp.where(sel, gs, 0.0)
    acc_ref[...] = jnp.zeros_like(acc_ref)
    lim = lim_tab[j]

    def round_body(r, _):
        copies = []
        shifts = []
        for e in range(ne):
            want = a_tab[j * ne + e] + r * win
            src = pl.multiple_of(jnp.minimum(want, lim), ROW_ALIGN)
            shifts.append((want - src).astype(F32))
            cp = pltpu.make_async_copy(ye_ref.at[e, pl.ds(src, win)], y_ref.at[pl.ds(e * win, win)],
                                       sem.at[0])
            cp.start()
            copies.append(cp)
        base = (r * win).astype(F32)
        k = lax.broadcasted_iota(jnp.int32, (tm, win), 1).astype(F32) + base
        for e in range(ne):
            pe = pos[:, e:e + 1]
            pe = jnp.where(pe >= base, pe + shifts[e], -1.0)
            pcol = jnp.broadcast_to(pe, (tm, win))
            gcol = jnp.broadcast_to(gate[:, e:e + 1], (tm, win))
            p_ref[:, e * win:(e + 1) * win] = jnp.where(pcol == k, gcol, 0.0).astype(BF16)
        for cp in copies:
            cp.wait()
        for nb in range(d // MXU_DIM):
            cols = slice(nb * MXU_DIM, (nb + 1) * MXU_DIM)
            acc_ref[:, cols] += jnp.dot(p_ref[...], y_ref[:, cols], preferred_element_type=F32)
        return 0

    lax.fori_loop(0, nrounds[j], round_body, 0)
    y = DEEPNORM_ALPHA * x1_ref[...] + (1.0 + g2_ref[...]) * acc_ref[...]
    o_ref[...] = _layer_norm(y, lg_ref[...], lb_ref[...])


def _combine_postnorm(gs_tok, ye, tabs, l_mat, x1, g2, ln_g, ln_b, t):
    ntok, ne = gs_tok.shape
    d = x1.shape[1]
    nt = ntok // MOE_TM
    tiles_per_seq = t // MOE_TM
    grid_spec = pltpu.PrefetchScalarGridSpec(
        num_scalar_prefetch=3,
        grid=(nt,),
        in_specs=[pl.BlockSpec((MOE_TM, ne), lambda j, *_: (j, 0)),
                  pl.BlockSpec((None, 1, ne), lambda j, *_: (j, 0, 0)),
                  pl.BlockSpec((MOE_TM, MOE_TM), lambda j, *_: (0, 0)),
                  pl.BlockSpec(memory_space=pl.ANY),
                  pl.BlockSpec((MOE_TM, d), lambda j, *_: (j, 0)),
                  pl.BlockSpec((None, 1, d), lambda j, *_: (j // tiles_per_seq, 0, 0)),
                  pl.BlockSpec((1, d), lambda j, *_: (0, 0)),
                  pl.BlockSpec((1, d), lambda j, *_: (0, 0))],
        out_specs=pl.BlockSpec((MOE_TM, d), lambda j, *_: (j, 0)),
        scratch_shapes=[pltpu.VMEM((MOE_TM, ne * MOE_WIN), BF16),
                        pltpu.VMEM((ne * MOE_WIN, d), BF16),
                        pltpu.VMEM((MOE_TM, d), F32),
                        pltpu.SemaphoreType.DMA((1,))])
    return pl.pallas_call(
        _combine_kernel,
        grid_spec=grid_spec,
        out_shape=jax.ShapeDtypeStruct((ntok, d), F32),
        compiler_params=_cparams(("arbitrary",)),
        name="ec_combine",
    )(tabs['a'], tabs['nrounds'], tabs['lim'], gs_tok, tabs['off_row'], l_mat, ye, x1, g2, ln_g, ln_b)


def _prep_w_in(w_in_l):
    sizes = (256, 256, 256, MLA_Q_RANK, MLA_KV_RANK, MLA_ROPE, 256, 256, 256, 256, 256, 256)
    offs = np.concatenate([[0], np.cumsum(sizes)])
    part = [w_in_l[:, offs[i]:offs[i + 1]] for i in range(len(sizes))]
    a_q, a_k, a_v, b_cq, b_ckv, b_kr, c_q, c_k, c_v, d_q, d_k, d_v = part
    d = w_in_l.shape[0]
    zeros = lambda n: jnp.zeros((d, n), w_in_l.dtype)
    a_v_heads = []
    for h in range(DA_HEADS):
        a_v_heads += [a_v[:, h * DA_DV:(h + 1) * DA_DV], zeros(LANES - DA_DV)]
    cols = [a_q, a_k] + a_v_heads + [c_q, c_k, c_v, d_q, d_k, d_v,
                                     b_cq, zeros(256 - MLA_Q_RANK), b_ckv, b_kr, b_kr,
                                     zeros(128 - 2 * MLA_ROPE)]
    return jnp.concatenate(cols, axis=1).astype(BF16)


def _prep_w_uq(w_uq_l):
    hd = MLA_NOPE + MLA_ROPE
    nope = [w_uq_l[:, h * hd:h * hd + MLA_NOPE] for h in range(MLA_HEADS)]
    rope = [w_uq_l[:, h * hd + MLA_NOPE:(h + 1) * hd] for h in range(MLA_HEADS)]
    z = jnp.zeros((w_uq_l.shape[0], 256 - 2 * hd), w_uq_l.dtype)
    cols = []
    for p in range(2):
        cols += [nope[2 * p], nope[2 * p + 1], rope[2 * p], rope[2 * p + 1], z]
    w = jnp.concatenate(cols, axis=1)
    w = jnp.concatenate([w, jnp.zeros((256 - MLA_Q_RANK, w.shape[1]), w.dtype)], axis=0)
    return w.astype(BF16)


def _prep_w_ukv(w_ukv_l):
    hd = MLA_NOPE + MLA_DV
    kn = [w_ukv_l[:, h * hd:h * hd + MLA_NOPE] for h in range(MLA_HEADS)]
    z = jnp.zeros((w_ukv_l.shape[0], LANES - MLA_DV), w_ukv_l.dtype)
    vv = []
    for h in range(MLA_HEADS):
        vv += [w_ukv_l[:, h * hd + MLA_NOPE:(h + 1) * hd], z]
    return jnp.concatenate(kn + vv, axis=1).astype(BF16)


def _select_groups(aff_t, group_tokens):
    parts, off = [], 0
    for n in group_tokens:
        parts.append(_select(lax.slice_in_dim(aff_t, off, off + n, axis=1),
                             EC_FACTOR * n // N_EXPERTS))
        off += n
    return jnp.concatenate(parts, axis=1)


def _trunk(x, c, group_tokens, nb, t, p):
    ntok = nb * t
    tabs = (_rope_tables(t, 256, DA_DQK, DA_DQK // ROPE_FRACTION, ROPE_THETA)
            + _rope_tables(t, 256, DIL_DH, DIL_DH // ROPE_FRACTION, ROPE_THETA)
            + _rope_tables(t, 128, MLA_ROPE, MLA_ROPE, MLA_ROPE_THETA))
    gmat = jnp.asarray(np.kron(np.eye(4), np.full((64, 64), 1.0 / 64)), BF16)
    ti = jnp.arange(MOE_TM)
    u_mat = (ti[:, None] < ti[None, :]).astype(BF16)
    l_mat = (ti[None, :] < ti[:, None]).astype(BF16)
    for l in range(DEPTH):
        mod = _modulation(c, p['w_ada'][l], p['b_ada'][l])
        sh1, sc1, g1, sh2, sc2, g2 = [m.reshape(nb, 1, D_MODEL) for m in jnp.split(mod, 6, axis=-1)]
        gq = jnp.concatenate([p['q_norm_g'][l], jnp.zeros((256 - MLA_Q_RANK,), F32)]).reshape(1, 256)
        gkv = p['kv_norm_g'][l].reshape(1, 128)
        proj, projv = _inproj(x, sc1, sh1, _prep_w_in(p['w_in'][l]), _prep_w_uq(p['w_uq'][l]),
                              _prep_w_ukv(p['w_ukv'][l]), gq, gkv, tabs, nb, t)
        lam_init = 0.8 - 0.6 * math.exp(-0.3 * l)
        lam = (jnp.exp(jnp.sum(p['da_lq1'][l] * p['da_lk1'][l]))
               - jnp.exp(jnp.sum(p['da_lq2'][l] * p['da_lk2'][l])) + lam_init).reshape(1)
        g_sub = jnp.tile(p['da_subln_g'][l], DA_HEADS).reshape(1, 256)
        oa = _diff_attention(proj, projv, lam, g_sub, gmat, nb, t, 1.0 - lam_init)
        ob = _mla_attention(proj, projv, nb, t)
        oc = _dil_attention(proj, nb, t)
        od = _na_attention(proj, _na_bias_table(p['na_rpb'][l]), nb, t)
        x1, h2, aff_t = _outproj(oa, ob, oc, od, x, g1, sc2, sh2, p['w_out'][l].astype(BF16),
                                 p['ln1_g'][l].reshape(1, -1), p['ln1_b'][l].reshape(1, -1),
                                 p['w_router'][l].T, nb, t)
        wg = p['w_e_gate'][l].astype(BF16)
        wu = p['w_e_up'][l].astype(BF16)
        wd = p['w_e_down'][l].astype(BF16)
        gs = _select_groups(aff_t, group_tokens)
        rt = _routing_tables(gs, group_tokens)
        xe = _dispatch(gs, h2, rt, u_mat)
        ye = _experts(xe, rt['total'], wg, wu, wd)
        x = _combine_postnorm(gs.T, ye, rt, l_mat, x1, g2, p['ln2_g'][l].reshape(1, -1),
                              p['ln2_b'][l].reshape(1, -1), t)
    return x


def kernel(x_prompt, x_sample, c_prompt, c_sample, w_in, w_uq, w_ukv, q_norm_g, kv_norm_g, da_lq1,
           da_lk1, da_lq2, da_lk2, da_subln_g, na_rpb, w_out, w_ada, b_ada, ln1_g, ln1_b, ln2_g,
           ln2_b, w_router, w_e_gate, w_e_up, w_e_down):
    p = dict(w_in=w_in, w_uq=w_uq, w_ukv=w_ukv, q_norm_g=q_norm_g, kv_norm_g=kv_norm_g,
             da_lq1=da_lq1, da_lk1=da_lk1, da_lq2=da_lq2, da_lk2=da_lk2, da_subln_g=da_subln_g,
             na_rpb=na_rpb, w_out=w_out, w_ada=w_ada, b_ada=b_ada, ln1_g=ln1_g, ln1_b=ln1_b,
             ln2_g=ln2_g, ln2_b=ln2_b, w_router=w_router, w_e_gate=w_e_gate, w_e_up=w_e_up,
             w_e_down=w_e_down)
    bp, t, d = x_prompt.shape
    bs = x_sample.shape[0]
    assert x_sample.shape[1] == t
    nb = bp + bs
    x = jnp.concatenate([x_prompt.reshape(bp * t, d), x_sample.reshape(bs * t, d)], axis=0)
    c = jnp.concatenate([c_prompt, c_sample], axis=0)
    y = _trunk(x, c, (bp * t, bs * t), nb, t, p)
    return y[:bp * t].reshape(bp, t, d), y[bp * t:].reshape(bs, t, d)
```

```python
import functools
import math

import jax
import jax.numpy as jnp
import numpy as np
from jax import lax
from jax.experimental import pallas as pl
from jax.experimental.pallas import tpu as pltpu

F32 = jnp.float32
BF16 = jnp.bfloat16

D_MODEL = 1024
DEPTH = 2
GRID_W = 64
GROUP_W = 256
DA_HEADS, DA_DV, DA_DQK = 4, 64, 32
MLA_HEADS, MLA_Q_RANK, MLA_KV_RANK, MLA_NOPE, MLA_ROPE, MLA_DV = 4, 192, 128, 64, 32, 64
MLA_ROPE_THETA = 10000.0
DIL_HEADS, DIL_DH = 4, 64
DIL_PATTERNS = ((128, 1), (512, 4), (2048, 16))
NA_HEADS, NA_DH, NA_KR, NA_KC = 4, 64, 8, 16
ROPE_THETA = 500000.0
ROPE_FRACTION = 4
N_EXPERTS = 16
EC_FACTOR = 2
D_FF = 2816
DEEPNORM_ALPHA = (2.0 * DEPTH) ** 0.25
NEG_INF = -1e30
LOG2E = math.log2(math.e)
LN_EPS = 1e-5
RMS_EPS = 1e-6

LANES = 128
MXU_DIM = 256
VMEM_LIMIT = 56 * 1024 * 1024

S_AQ, S_AK, S_BQ0, S_BQ1, S_BK0, S_BK1, S_CQ, S_CK, S_DQ, S_DK, S_DV = range(11)
N_SLOTS = 11
V_A, V_B, V_C = 0, 4, 8
N_VSLOTS = 12
V_ONE_LANE = 64
W_ALL_COLS = 13 * 256


def _cparams(sem):
    return pltpu.CompilerParams(dimension_semantics=sem, vmem_limit_bytes=VMEM_LIMIT)


def _split_bf16(a):
    hi = a.astype(BF16)
    lo = (a - hi.astype(F32)).astype(BF16)
    return hi, lo


def _mod_kernel(c_ref, w_ref, b_ref, o_ref):
    c = c_ref[...]
    a = c * (1.0 / (1.0 + jnp.exp(-c)))
    a_hi, a_lo = _split_bf16(a)
    w_hi, w_lo = _split_bf16(w_ref[...])
    acc = jnp.dot(a_hi, w_hi, preferred_element_type=F32)
    acc += jnp.dot(a_hi, w_lo, preferred_element_type=F32)
    acc += jnp.dot(a_lo, w_hi, preferred_element_type=F32)
    o_ref[...] = acc + b_ref[...]


def _modulation(c, w_ada, b_ada):
    nb, d = c.shape
    n_out = w_ada.shape[1]
    tn = 1536
    return pl.pallas_call(
        _mod_kernel,
        grid=(n_out // tn,),
        in_specs=[pl.BlockSpec((nb, d), lambda j: (0, 0)),
                  pl.BlockSpec((d, tn), lambda j: (0, j)),
                  pl.BlockSpec((1, tn), lambda j: (0, j))],
        out_specs=pl.BlockSpec((nb, tn), lambda j: (0, j)),
        out_shape=jax.ShapeDtypeStruct((nb, n_out), F32),
        compiler_params=_cparams(("arbitrary",)),
        name="adaln_mod",
    )(c, w_ada, b_ada.reshape(1, n_out))


def _rope_tables(t, width, group, rot, theta):
    half = rot // 2
    inv = theta ** (-jnp.arange(half, dtype=F32) / half)
    ang = jnp.arange(t, dtype=F32)[:, None] * inv[None, :]
    cos, sin = jnp.cos(ang), jnp.sin(ang)
    ones = jnp.ones((t, group - rot), F32)
    zeros = jnp.zeros((t, group - rot), F32)
    c_g = jnp.concatenate([cos, cos, ones], axis=1)
    s_g = jnp.concatenate([-sin, sin, zeros], axis=1)
    reps = width // group
    return jnp.tile(c_g, (1, reps)), jnp.tile(s_g, (1, reps))


def _apply_rope(x, c_tab, s_tab, group, rot):
    width = x.shape[-1]
    half = rot // 2
    lane = lax.broadcasted_iota(jnp.int32, (1, width), 1)
    first = (lane % group) < half
    fwd = pltpu.roll(x, width - half, 1)
    bwd = pltpu.roll(x, half, 1)
    return x * c_tab + s_tab * jnp.where(first, fwd, bwd)


def _inproj_kernel(x_ref, sc_ref, sh_ref, w_ref, wuq_ref, wukv_ref, gq_ref, gkv_ref,
                   ca_ref, sa_ref, cc_ref, scc_ref, cm_ref, sm_ref, o_ref, ov_ref):
    h = (x_ref[...] * (1.0 + sc_ref[...]) + sh_ref[...]).astype(BF16)

    def proj(col, width):
        return jnp.dot(h, w_ref[:, col:col + width], preferred_element_type=F32)

    one_lane = lax.broadcasted_iota(jnp.int32, (1, LANES), 1) == V_ONE_LANE

    def put_values(first_slot, vals):
        for hh in range(4):
            v = vals[:, LANES * hh:LANES * (hh + 1)]
            ov_ref[first_slot + hh] = jnp.where(one_lane, 1.0, v).astype(BF16)

    sa_scale = DA_DQK ** -0.5 * LOG2E
    sb_scale = (MLA_NOPE + MLA_ROPE) ** -0.5 * LOG2E
    sc_scale = DIL_DH ** -0.5 * LOG2E
    sd_scale = NA_DH ** -0.5 * LOG2E
    a_rot = DA_DQK // ROPE_FRACTION
    c_rot = DIL_DH // ROPE_FRACTION

    ca, sa = ca_ref[...], sa_ref[...]
    o_ref[S_AQ] = (_apply_rope(proj(0, 256), ca, sa, DA_DQK, a_rot) * sa_scale).astype(BF16)
    o_ref[S_AK] = _apply_rope(proj(256, 256), ca, sa, DA_DQK, a_rot).astype(BF16)
    put_values(V_A, proj(512, 512))
    cc, scc = cc_ref[...], scc_ref[...]
    o_ref[S_CQ] = (_apply_rope(proj(1024, 256), cc, scc, DIL_DH, c_rot) * sc_scale).astype(BF16)
    o_ref[S_CK] = _apply_rope(proj(1280, 256), cc, scc, DIL_DH, c_rot).astype(BF16)
    put_values(V_C, proj(1536, 512))
    o_ref[S_DQ] = (proj(2048, 256) * sd_scale).astype(BF16)
    o_ref[S_DK] = proj(2304, 256).astype(BF16)
    o_ref[S_DV] = proj(2560, 256).astype(BF16)

    cm, sm = cm_ref[...], sm_ref[...]
    cq = proj(2816, 256)
    cq = cq * lax.rsqrt(jnp.sum(cq * cq, -1, keepdims=True) * (1.0 / MLA_Q_RANK) + RMS_EPS)
    cq = (cq * gq_ref[...]).astype(BF16)
    q2 = jnp.dot(cq, wuq_ref[...], preferred_element_type=F32)
    for p in range(2):
        qp = q2[:, 256 * p:256 * (p + 1)]
        o_ref[S_BQ0 + p, :, 0:128] = (qp[:, 0:128] * sb_scale).astype(BF16)
        o_ref[S_BQ0 + p, :, 128:256] = (
            _apply_rope(qp[:, 128:256], cm, sm, MLA_ROPE, MLA_ROPE) * sb_scale).astype(BF16)
    ckv = proj(3072, 128)
    ckv = ckv * lax.rsqrt(jnp.mean(ckv * ckv, -1, keepdims=True) + RMS_EPS)
    ckv = (ckv * gkv_ref[...]).astype(BF16)
    kv = jnp.dot(ckv, wukv_ref[...], preferred_element_type=F32)
    kr = _apply_rope(proj(3200, 128), cm, sm, MLA_ROPE, MLA_ROPE).astype(BF16)
    for p in range(2):
        o_ref[S_BK0 + p, :, 0:128] = kv[:, 128 * p:128 * (p + 1)].astype(BF16)
        o_ref[S_BK0 + p, :, 128:256] = kr
    put_values(V_B, kv[:, 256:768])


def _inproj(x, sc, sh, w_all, wuq, wukv, gq, gkv, tabs, nb, t):
    tm = 512
    nt = t // tm
    ntok = nb * t
    ca, sa, cc, scc, cm, sm = tabs
    full = lambda shape: pl.BlockSpec(shape, lambda j, b: tuple(0 for _ in shape))
    tab = lambda w: pl.BlockSpec((tm, w), lambda j, b: (j, 0))
    return pl.pallas_call(
        _inproj_kernel,
        grid=(nt, nb),
        in_specs=[pl.BlockSpec((tm, D_MODEL), lambda j, b: (b * nt + j, 0)),
                  pl.BlockSpec((None, 1, D_MODEL), lambda j, b: (b, 0, 0)),
                  pl.BlockSpec((None, 1, D_MODEL), lambda j, b: (b, 0, 0)),
                  full((D_MODEL, W_ALL_COLS)), full((256, 512)), full((128, 768)),
                  full((1, 256)), full((1, 128)),
                  tab(256), tab(256), tab(256), tab(256), tab(128), tab(128)],
        out_specs=[pl.BlockSpec((N_SLOTS, tm, 256), lambda j, b: (0, b * nt + j, 0)),
                   pl.BlockSpec((N_VSLOTS, tm, LANES), lambda j, b: (0, b * nt + j, 0))],
        out_shape=[jax.ShapeDtypeStruct((N_SLOTS, ntok, 256), BF16),
                   jax.ShapeDtypeStruct((N_VSLOTS, ntok, LANES), BF16)],
        compiler_params=_cparams(("arbitrary", "arbitrary")),
        name="in_proj",
    )(x, sc, sh, w_all, wuq, wukv, gq, gkv, ca, sa, cc, scc, cm, sm)


def _lane_mask(width, ranges):
    lane = lax.broadcasted_iota(jnp.int32, (1, width), 1)
    m = None
    for lo, hi in ranges:
        r = (lane >= lo) & (lane < hi)
        m = r if m is None else (m | r)
    return m


SOFTMAX_SLAB = 32


def _chain_scratch(tq, kc):
    return [pltpu.VMEM((tq, kc), F32), pltpu.VMEM((tq, kc), F32), pltpu.VMEM((tq, kc), BF16),
            pltpu.VMEM((tq, LANES), F32), pltpu.VMEM((tq, LANES), F32), pltpu.VMEM((tq, LANES), F32)]


CHAIN_REFS = 6


def _attention_steps(n_steps, qms_for, k_for, v_for, finish, tq, t, kc, chains, slab=SOFTMAX_SLAB,
                     bias_ref=None):
    n_chunks = t // kc
    assert n_chunks % 2 == 0
    nt = (((1,), (1,)), ((), ()))

    def chunk_rows(chunk):
        if isinstance(chunk, int):
            return pl.ds(chunk * kc, kc)
        return pl.ds(pl.multiple_of(chunk * kc, kc), kc)

    def scores(step, chunk, slot):
        k = k_for(step)[chunk_rows(chunk), :]
        for qm, chain in zip(qms_for(step), chains):
            chain[slot][...] = lax.dot_general(qm, k, nt, preferred_element_type=F32)

    def softmax_pv(step, chunk, slot):
        for v_ref, chain in zip(v_for(step), chains):
            v = v_ref[chunk_rows(chunk), :]
            s_ref = chain[slot]
            (p_ref, m_ref, a_ref, acc_ref) = chain[2:]
            for r in range(tq // slab):
                rows = slice(r * slab, (r + 1) * slab)
                s = s_ref[rows, :]
                if bias_ref is not None:
                    s = s + bias_ref[rows, chunk * kc:(chunk + 1) * kc]
                m_prev = m_ref[rows, :]
                m_new = jnp.maximum(m_prev, jnp.max(s, -1, keepdims=True))
                d = s - jnp.concatenate([m_new] * (kc // LANES), axis=1)
                p_ref[rows, :] = jnp.exp2(d.astype(BF16))
                a_ref[rows, :] = jnp.exp2(m_prev - m_new)
                m_ref[rows, :] = m_new
            acc_ref[...] = a_ref[...] * acc_ref[...] + jnp.dot(p_ref[...], v,
                                                              preferred_element_type=F32)

    def step_body(st, _):
        for (_, _, _, m_ref, _, acc_ref) in chains:
            m_ref[...] = jnp.full(m_ref.shape, NEG_INF, F32)
            acc_ref[...] = jnp.zeros(acc_ref.shape, F32)
        nxt = jnp.minimum(st + 1, n_steps - 1)

        def body(j, _):
            scores(st, 2 * j + 1, 1)
            softmax_pv(st, 2 * j, 0)
            if n_chunks == 2:
                scores(nxt, 0, 0)
            else:
                last = 2 * j + 2 == n_chunks
                scores(jnp.where(last, nxt, st), jnp.where(last, 0, 2 * j + 2), 0)
            softmax_pv(st, 2 * j + 1, 1)
            return 0

        if n_chunks == 2:
            body(0, 0)
        else:
            assert bias_ref is None
            lax.fori_loop(0, n_chunks // 2, body, 0)
        outs = []
        for chain in chains:
            acc = chain[5][...]
            den = jnp.broadcast_to(acc[:, V_ONE_LANE:V_ONE_LANE + 1], acc.shape)
            outs.append(acc * (1.0 / den))
        finish(st, outs)
        return 0

    scores(0, 0, 0)
    lax.fori_loop(0, n_steps, step_body, 0)


def _place_head(pair_ref, o, j):
    lane = lax.broadcasted_iota(jnp.int32, (1, LANES), 1)
    low = lane < V_ONE_LANE
    shifted = pltpu.roll(o, V_ONE_LANE, 1)
    cur = pair_ref[...]
    pair_ref[...] = jnp.where(low, jnp.where(j == 0, o, cur), jnp.where(j == 1, shifted, cur))


def _group_mean_sq(x, gmat):
    sq = x * x
    hi, lo = _split_bf16(sq)
    return (jnp.dot(hi, gmat, preferred_element_type=F32)
            + jnp.dot(lo, gmat, preferred_element_type=F32))


def _diff_attn_kernel(lam_ref, q_ref, k_ref, v_ref, g_ref, gmat_ref, o_ref, acc_ref, *scratch, t, kc,
                      out_scale):
    chains = (scratch[:CHAIN_REFS], scratch[CHAIN_REFS:])
    lam = lam_ref[0]
    lane = lax.broadcasted_iota(jnp.int32, (1, 256), 1)

    def qms_for(h):
        q = q_ref[...]
        return [jnp.where((lane >= (2 * h + c) * DA_DQK) & (lane < (2 * h + c + 1) * DA_DQK), q,
                          jnp.zeros_like(q)) for c in range(2)]

    def finish(h, outs):
        _place_head(acc_ref.at[h // 2], outs[0] - lam * outs[1], h % 2)

    acc_ref[...] = jnp.zeros_like(acc_ref)
    _attention_steps(DA_HEADS, qms_for, lambda h: k_ref, lambda h: [v_ref.at[h], v_ref.at[h]],
                     finish, q_ref.shape[0], t, kc, chains)
    o = jnp.concatenate([acc_ref[0], acc_ref[1]], axis=1)
    ms = _group_mean_sq(o, gmat_ref[...])
    o_ref[...] = (o * lax.rsqrt(ms + RMS_EPS) * g_ref[...] * out_scale).astype(o_ref.dtype)


def _diff_attention(proj, projv, lam, g_tiled, gmat, nb, t, out_scale):
    tq, kc = 512, 512
    nq = t // tq
    kern = functools.partial(_diff_attn_kernel, t=t, kc=kc, out_scale=out_scale)
    return pl.pallas_call(
        kern,
        grid=(nb, nq),
        in_specs=[pl.BlockSpec(memory_space=pltpu.SMEM),
                  pl.BlockSpec((None, tq, 256), lambda b, i: (S_AQ, b * nq + i, 0)),
                  pl.BlockSpec((None, t, 256), lambda b, i: (S_AK, b, 0)),
                  pl.BlockSpec((DA_HEADS, t, LANES), lambda b, i: (V_A // DA_HEADS, b, 0)),
                  pl.BlockSpec((1, 256), lambda b, i: (0, 0)),
                  pl.BlockSpec((256, 256), lambda b, i: (0, 0))],
        out_specs=pl.BlockSpec((tq, 256), lambda b, i: (b * nq + i, 0)),
        out_shape=jax.ShapeDtypeStruct((nb * t, 256), BF16),
        scratch_shapes=[pltpu.VMEM((2, tq, LANES), F32)] + 2 * _chain_scratch(tq, kc),
        compiler_params=_cparams(("arbitrary", "arbitrary")),
        name="diff_attn",
    )(lam, proj, proj, projv, g_tiled, gmat)


def _mla_attn_kernel(q_ref, k_ref, v_ref, o_ref, acc_ref, *scratch, t, kc):
    chains = (scratch[:CHAIN_REFS], scratch[CHAIN_REFS:])
    lane = lax.broadcasted_iota(jnp.int32, (1, 256), 1)

    def qms_for(p):
        q = q_ref[p]
        qms = []
        for j in range(2):
            nope = (lane >= j * MLA_NOPE) & (lane < (j + 1) * MLA_NOPE)
            rope = (lane >= 128 + j * MLA_ROPE) & (lane < 128 + (j + 1) * MLA_ROPE)
            qms.append(jnp.where(nope | rope, q, jnp.zeros_like(q)))
        return qms

    def finish(p, outs):
        for j in range(2):
            _place_head(acc_ref.at[p], outs[j], j)

    acc_ref[...] = jnp.zeros_like(acc_ref)
    _attention_steps(MLA_HEADS // 2, qms_for, lambda p: k_ref.at[p],
                     lambda p: [v_ref.at[2 * p], v_ref.at[2 * p + 1]], finish, q_ref.shape[1], t, kc,
                     chains)
    o_ref[...] = jnp.concatenate([acc_ref[0], acc_ref[1]], axis=1).astype(o_ref.dtype)


def _mla_attention(proj, projv, nb, t):
    tq, kc = 512, 512
    nq = t // tq
    kern = functools.partial(_mla_attn_kernel, t=t, kc=kc)
    return pl.pallas_call(
        kern,
        grid=(nb, nq),
        in_specs=[pl.BlockSpec((2, tq, 256), lambda b, i: (S_BQ0 // 2, b * nq + i, 0)),
                  pl.BlockSpec((2, t, 256), lambda b, i: (S_BK0 // 2, b, 0)),
                  pl.BlockSpec((MLA_HEADS, t, LANES), lambda b, i: (V_B // MLA_HEADS, b, 0))],
        out_specs=pl.BlockSpec((tq, 256), lambda b, i: (b * nq + i, 0)),
        out_shape=jax.ShapeDtypeStruct((nb * t, 256), BF16),
        scratch_shapes=[pltpu.VMEM((2, tq, LANES), F32)] + 2 * _chain_scratch(tq, kc),
        compiler_params=_cparams(("arbitrary", "arbitrary")),
        name="mla_attn",
    )(proj, proj, projv)


DIL_REACH = max(w // 2 for w, _ in DIL_PATTERNS)


def _dil_attn_kernel(q_ref, k_ref, v_ref, o_ref, tables_ref, acc_ref, *scratch, t, tq, band):
    chains = (scratch[:CHAIN_REFS], scratch[CHAIN_REFS:])
    i = pl.program_id(1)

    def band_start(blk):
        return jnp.clip(blk * tq - DIL_REACH, 0, t - band)

    start = pl.multiple_of(band_start(i), tq)
    table = (i * tq - start) // tq

    @pl.when((pl.program_id(0) == 0) & (i == 0))
    def _():
        def build(n, _):
            qi = lax.broadcasted_iota(jnp.int32, (tq, band), 0)
            kj = lax.broadcasted_iota(jnp.int32, (tq, band), 1)
            delta = kj - qi - n * tq
            ad = jnp.abs(delta)
            cnt = jnp.zeros((tq, band), F32)
            for window, dil in DIL_PATTERNS:
                ok = (ad <= window // 2) & ((delta & (dil - 1)) == 0)
                cnt = cnt + jnp.where(ok, 1.0, 0.0)
            tables_ref[n] = jnp.where(cnt > 2.5, math.log2(3.0),
                                      jnp.where(cnt > 1.5, 1.0,
                                                jnp.where(cnt > 0.5, 0.0, NEG_INF)))
            return 0

        lax.fori_loop(0, tables_ref.shape[0], build, 0)

    bias_ref = tables_ref.at[table]

    lane = lax.broadcasted_iota(jnp.int32, (1, 256), 1)

    def qms_for(hp):
        q = q_ref[...]
        return [jnp.where((lane >= (2 * hp + j) * DIL_DH) & (lane < (2 * hp + j + 1) * DIL_DH), q,
                          jnp.zeros_like(q)) for j in range(2)]

    def finish(hp, outs):
        for j in range(2):
            _place_head(acc_ref.at[hp], outs[j], j)

    acc_ref[...] = jnp.zeros_like(acc_ref)
    _attention_steps(DIL_HEADS // 2, qms_for, lambda hp: k_ref.at[pl.ds(start, band)],
                     lambda hp: [v_ref.at[2 * hp + j, pl.ds(start, band)] for j in range(2)],
                     finish, tq, band, band // 2, chains, slab=DIL_SLAB, bias_ref=bias_ref)
    o_ref[...] = jnp.concatenate([acc_ref[0], acc_ref[1]], axis=1).astype(o_ref.dtype)


DIL_SLAB = 16


def _dil_attention(proj, projv, nb, t):
    tq = 256
    band = min(t, tq + 2 * DIL_REACH)
    nq = t // tq
    kern = functools.partial(_dil_attn_kernel, t=t, tq=tq, band=band)
    return pl.pallas_call(
        kern,
        grid=(nb, nq),
        in_specs=[pl.BlockSpec((None, tq, 256), lambda b, i: (S_CQ, b * nq + i, 0)),
                  pl.BlockSpec((None, t, 256), lambda b, i: (S_CK, b, 0)),
                  pl.BlockSpec((DIL_HEADS, t, LANES), lambda b, i: (V_C // DIL_HEADS, b, 0))],
        out_specs=pl.BlockSpec((tq, 256), lambda b, i: (b * nq + i, 0)),
        out_shape=jax.ShapeDtypeStruct((nb * t, 256), BF16),
        scratch_shapes=([pltpu.VMEM((band // tq, tq, band), F32), pltpu.VMEM((2, tq, LANES), F32)]
                        + 2 * _chain_scratch(tq, band // 2)),
        compiler_params=_cparams(("arbitrary", "arbitrary")),
        name="dil_attn",
    )(proj, proj, projv)


def _na_bias_table(rpb):
    c = np.arange(GRID_W)
    cs = np.clip(c - NA_KC // 2, 0, GRID_W - NA_KC)
    colmask = (c[None, :] >= cs[:, None]) & (c[None, :] < cs[:, None] + NA_KC)
    rows = jnp.stack([rpb[:, si:si + NA_KR, :] for si in range(NA_KR)], axis=1).astype(F32)
    edge = GRID_W - NA_KC
    padded = jnp.concatenate([jnp.repeat(rows[..., :1], edge, axis=-1), rows,
                              jnp.repeat(rows[..., -1:], edge, axis=-1)], axis=-1)
    b = jnp.stack([padded[..., GRID_W - 1 - qc:2 * GRID_W - 1 - qc] for qc in range(GRID_W)],
                  axis=2)
    b = jnp.where(colmask[None, None, :, None, :], b * LOG2E, NEG_INF)
    b = b.transpose(1, 0, 2, 3, 4)
    return b.reshape(NA_KR, rpb.shape[0] * GRID_W, NA_KR * GRID_W)


def _na_attn_kernel(q_ref, k_ref, v_ref, tb_ref, o_ref, *, rows, rg):
    g = pl.program_id(1)
    lane = lax.broadcasted_iota(jnp.int32, (1, 256), 1)
    nk = NA_KR * GRID_W
    sels = [(lane >= h * NA_DH) & (lane < (h + 1) * NA_DH) for h in range(NA_HEADS)]

    def row(r, _):
        grow = g * rg + r
        rs = jnp.clip(grow - NA_KR // 2, 0, rows - NA_KR)
        si = rs - grow + (NA_KR - 1)
        q = q_ref[pl.ds(pl.multiple_of(r * GRID_W, GRID_W), GRID_W), :]
        koff = pl.multiple_of(rs * GRID_W, GRID_W)
        kb = k_ref[pl.ds(koff, nk), :]
        vb = v_ref[pl.ds(koff, nk), :]
        q4 = jnp.concatenate([jnp.where(sel, q, jnp.zeros_like(q)) for sel in sels], axis=0)
        s = lax.dot_general(q4, kb, (((1,), (1,)), ((), ())), preferred_element_type=F32)
        s = s + tb_ref[si]
        m = jnp.max(s, -1, keepdims=True)
        p = jnp.exp2(s - m)
        l = jnp.sum(p, -1, keepdims=True)
        o4 = jnp.dot(p.astype(BF16), vb, preferred_element_type=F32) * (1.0 / l)
        out = jnp.zeros((GRID_W, 256), F32)
        for h, sel in enumerate(sels):
            out = jnp.where(sel, o4[h * GRID_W:(h + 1) * GRID_W], out)
        o_ref[pl.ds(pl.multiple_of(r * GRID_W, GRID_W), GRID_W), :] = out.astype(o_ref.dtype)
        return 0

    lax.fori_loop(0, rg, row, 0, unroll=2)


def _na_attention(proj, tb, nb, t):
    rows = t // GRID_W
    assert rows >= NA_KR
    rg = 8
    ng = rows // rg
    tq = rg * GRID_W
    kern = functools.partial(_na_attn_kernel, rows=rows, rg=rg)
    return pl.pallas_call(
        kern,
        grid=(nb, ng),
        in_specs=[pl.BlockSpec((None, tq, 256), lambda b, i: (S_DQ, b * ng + i, 0)),
                  pl.BlockSpec((None, t, 256), lambda b, i: (S_DK, b, 0)),
                  pl.BlockSpec((None, t, 256), lambda b, i: (S_DV, b, 0)),
                  pl.BlockSpec(tb.shape, lambda b, i: (0, 0, 0))],
        out_specs=pl.BlockSpec((tq, 256), lambda b, i: (b * ng + i, 0)),
        out_shape=jax.ShapeDtypeStruct((nb * t, 256), BF16),
        compiler_params=_cparams(("arbitrary", "arbitrary")),
        name="na_attn",
    )(proj, proj, proj, tb)


def _layer_norm(y, g, b):
    mu = jnp.mean(y, -1, keepdims=True)
    yc = y - mu
    var = jnp.mean(yc * yc, -1, keepdims=True)
    return yc * lax.rsqrt(var + LN_EPS) * g + b


def _outproj_kernel(oa_ref, ob_ref, oc_ref, od_ref, x_ref, g1_ref, sc2_ref, sh2_ref, w_ref,
                    lg_ref, lb_ref, wr_ref, x1_ref, h2_ref, aff_ref):
    m = jnp.dot(oa_ref[...], w_ref[0:256, :], preferred_element_type=F32)
    m += jnp.dot(ob_ref[...], w_ref[256:512, :], preferred_element_type=F32)
    m += jnp.dot(oc_ref[...], w_ref[512:768, :], preferred_element_type=F32)
    m += jnp.dot(od_ref[...], w_ref[768:1024, :], preferred_element_type=F32)
    y = DEEPNORM_ALPHA * x_ref[...] + (1.0 + g1_ref[...]) * m
    x1 = _layer_norm(y, lg_ref[...], lb_ref[...])
    x1_ref[...] = x1
    h2 = x1 * (1.0 + sc2_ref[...]) + sh2_ref[...]
    h2_ref[...] = h2.astype(BF16)
    h_hi, h_lo = _split_bf16(h2)
    w_hi, w_lo = _split_bf16(wr_ref[...])
    nt = (((1,), (1,)), ((), ()))
    lg = lax.dot_general(w_hi, h_hi, nt, preferred_element_type=F32)
    lg += lax.dot_general(w_hi, h_lo, nt, preferred_element_type=F32)
    lg += lax.dot_general(w_lo, h_hi, nt, preferred_element_type=F32)
    lg = lg - jnp.max(lg, 0, keepdims=True)
    e = jnp.exp(lg)
    aff_ref[...] = e / jnp.sum(e, 0, keepdims=True)


def _outproj(oa, ob, oc, od, x, g1, sc2, sh2, w_out, ln_g, ln_b, w_router_t, nb, t):
    tm = 512
    nt = t // tm
    ntok = nb * t
    tok = lambda w: pl.BlockSpec((tm, w), lambda i: (i, 0))
    per_b = pl.BlockSpec((None, 1, D_MODEL), lambda i: (i // nt, 0, 0))
    full = lambda shape: pl.BlockSpec(shape, lambda i: tuple(0 for _ in shape))
    return pl.pallas_call(
        _outproj_kernel,
        grid=(ntok // tm,),
        in_specs=[tok(256), tok(256), tok(256), tok(256), tok(D_MODEL), per_b, per_b, per_b,
                  full((D_MODEL, D_MODEL)), full((1, D_MODEL)), full((1, D_MODEL)),
                  full((N_EXPERTS, D_MODEL))],
        out_specs=[tok(D_MODEL), tok(D_MODEL),
                   pl.BlockSpec((N_EXPERTS, tm), lambda i: (0, i))],
        out_shape=[jax.ShapeDtypeStruct((ntok, D_MODEL), F32),
                   jax.ShapeDtypeStruct((ntok, D_MODEL), BF16),
                   jax.ShapeDtypeStruct((N_EXPERTS, ntok), F32)],
        compiler_params=_cparams(("arbitrary",)),
        name="out_proj",
    )(oa, ob, oc, od, x, g1, sc2, sh2, w_out, ln_g, ln_b, w_router_t)


def _expert_kernel(x_ref, wg_ref, wu_ref, wd_ref, o_ref, acc_ref):
    f = pl.program_id(2)

    @pl.when(f == 0)
    def _():
        acc_ref[...] = jnp.zeros_like(acc_ref)

    x = x_ref[...]
    g = jnp.dot(x, wg_ref[...], preferred_element_type=F32)
    u = jnp.dot(x, wu_ref[...], preferred_element_type=F32)
    hmid = (g * (1.0 / (1.0 + jnp.exp(-g))) * u).astype(BF16)
    acc_ref[...] += jnp.dot(hmid, wd_ref[...], preferred_element_type=F32)

    @pl.when(f == pl.num_programs(2) - 1)
    def _():
        o_ref[...] = acc_ref[...].astype(o_ref.dtype)


def _experts(xe, slots, wg, wu, wd):
    ne, _, d = xe.shape
    tm = math.gcd(slots, 1024)
    tf = 256
    nf = D_FF // tf
    return pl.pallas_call(
        _expert_kernel,
        grid=(ne, slots // tm, nf),
        in_specs=[pl.BlockSpec((None, tm, d), lambda e, m, f: (e, m, 0)),
                  pl.BlockSpec((None, d, tf), lambda e, m, f: (e, 0, f)),
                  pl.BlockSpec((None, d, tf), lambda e, m, f: (e, 0, f)),
                  pl.BlockSpec((None, tf, d), lambda e, m, f: (e, f, 0))],
        out_specs=pl.BlockSpec((None, tm, d), lambda e, m, f: (e, m, 0)),
        out_shape=jax.ShapeDtypeStruct((ne, slots, d), BF16),
        scratch_shapes=[pltpu.VMEM((tm, d), F32)],
        compiler_params=_cparams(("arbitrary", "arbitrary", "arbitrary")),
        name="expert_ffn",
    )(xe, wg, wu, wd)


MOE_TM = 512
MOE_WIN = 128
ROW_ALIGN = 16


def _select_kernel(aff_ref, o_ref, *, cap):
    a = aff_ref[...]
    keys = lax.bitcast_convert_type(a, jnp.int32)
    ne, n = a.shape
    capf = float(cap)

    def count(mask):
        return jnp.sum(jnp.where(mask, 1.0, 0.0), axis=1, keepdims=True)

    def key_bit(b, thr):
        cand = thr | lax.shift_left(jnp.int32(1), 30 - b)
        return jnp.where(count(keys >= cand) >= capf, cand, thr)

    thr = lax.fori_loop(0, 31, key_bit, jnp.zeros((ne, 1), jnp.int32))
    above = keys > thr
    need = capf - count(above)
    idx = lax.broadcasted_iota(jnp.int32, (ne, n), 1)
    tie_idx = jnp.where(keys == thr, idx, jnp.int32(2 ** 30))
    nbits = max(1, (n - 1).bit_length())

    def idx_bit(b, j):
        cand = j | lax.shift_left(jnp.int32(1), nbits - 1 - b)
        return jnp.where(count(tie_idx < cand) < need, cand, j)

    j = lax.fori_loop(0, nbits, idx_bit, jnp.zeros((ne, 1), jnp.int32))
    sel = above | (tie_idx <= j)
    o_ref[...] = jnp.where(sel, a, -1.0)


def _select(aff, cap):
    ne, n = aff.shape
    return pl.pallas_call(
        functools.partial(_select_kernel, cap=cap),
        out_shape=jax.ShapeDtypeStruct((ne, n), F32),
        compiler_params=pltpu.CompilerParams(vmem_limit_bytes=VMEM_LIMIT),
        name="ec_select",
    )(aff)


def _routing_tables(gs, group_tokens):
    ne = gs.shape[0]
    a_l, off_l, cnt_l, lim_l = [], [], [], []
    tok0, slot0 = 0, 0
    for n in group_tokens:
        cap = EC_FACTOR * n // N_EXPERTS
        nt = n // MOE_TM
        sel = lax.slice_in_dim(gs, tok0, tok0 + n, axis=1) >= 0
        counts = jnp.sum(sel.reshape(ne, nt, MOE_TM), axis=-1, dtype=jnp.int32)
        s0 = slot0 + jnp.cumsum(counts, axis=1) - counts
        a = (s0 // ROW_ALIGN) * ROW_ALIGN
        a_l.append(a)
        off_l.append(s0 - a)
        cnt_l.append(counts)
        lim_l.append(jnp.full((nt,), slot0 + cap - MOE_WIN, jnp.int32))
        tok0 += n
        slot0 += cap
    a = jnp.concatenate(a_l, axis=1).T
    off = jnp.concatenate(off_l, axis=1).T
    end = off + jnp.concatenate(cnt_l, axis=1).T
    rounds = (end + MOE_WIN - 1) // MOE_WIN
    gp = (end // ROW_ALIGN) * ROW_ALIGN
    return dict(a=a.reshape(-1).astype(jnp.int32), nr=rounds.reshape(-1).astype(jnp.int32),
                nrounds=jnp.max(rounds, axis=1).astype(jnp.int32), lim=jnp.concatenate(lim_l),
                off_col=off.astype(F32)[:, :, None], off_row=off.astype(F32)[:, None, :],
                off16=jnp.repeat(off.astype(F32), ROW_ALIGN, axis=1)[:, :, None],
                gp16=jnp.repeat(gp.astype(F32), ROW_ALIGN, axis=1)[:, :, None],
                total=slot0)


def _dispatch_kernel(a_tab, nrounds, nr_tab, gs_ref, off_ref, off16_ref, gp16_ref, x_ref, u_ref, xe_ref,
                     pos_ref, c_ref, c2_ref, stage_ref, carry_ref, sem, rc_ref):
    j = pl.program_id(0)
    ne, tm = gs_ref.shape
    d = x_ref.shape[1]
    win = MOE_WIN

    @pl.when(j == 0)
    def _():
        carry_ref[...] = jnp.zeros_like(carry_ref)
        rc_ref[0] = 0

    sel = gs_ref[...] >= 0.0
    rank = jnp.dot(jnp.where(sel, 1.0, 0.0).astype(BF16), u_ref[...], preferred_element_type=F32)
    pos_ref[...] = jnp.where(sel, rank + off_ref[...], -1.0)

    def wait_round(slot):
        def one(_, c):
            pltpu.make_async_copy(stage_ref.at[slot, pl.ds(0, win)], xe_ref.at[0, pl.ds(0, win)],
                                  sem.at[slot]).wait()
            return c
        lax.fori_loop(0, rc_ref[1 + slot], one, 0)

    def round_body(r, _):
        k = lax.broadcasted_iota(jnp.int32, (win, tm), 0).astype(F32) + (r * win).astype(F32)
        for e in range(ne):
            c_ref[e * win:(e + 1) * win, :] = jnp.where(pos_ref[e:e + 1, :] == k, 1.0, 0.0).astype(BF16)
        slot = rc_ref[0] % 2
        for nb in range(d // MXU_DIM):
            cols = slice(nb * MXU_DIM, (nb + 1) * MXU_DIM)
            stage_ref[slot, :, cols] = jnp.dot(c_ref[...], x_ref[:, cols],
                                               preferred_element_type=F32).astype(BF16)

        @pl.when(r == 0)
        def _():
            k16 = lax.broadcasted_iota(jnp.int32, (ROW_ALIGN, 1), 0).astype(F32)
            for e in range(ne):
                keep = k16 < off16_ref[e * ROW_ALIGN:(e + 1) * ROW_ALIGN, :]
                rows = pl.ds(e * win, ROW_ALIGN)
                stage_ref[slot, rows, :] = jnp.where(
                    keep, carry_ref[e * ROW_ALIGN:(e + 1) * ROW_ALIGN, :], stage_ref[slot, rows, :])

        @pl.when(rc_ref[0] > 0)
        def _():
            wait_round(1 - slot)

        rc_ref[1 + slot] = 0
        for e in range(ne):
            @pl.when(r < nr_tab[j * ne + e])
            def _():
                dst = pl.multiple_of(a_tab[j * ne + e] + r * win, ROW_ALIGN)
                pltpu.make_async_copy(stage_ref.at[slot, pl.ds(e * win, win)],
                                      xe_ref.at[e, pl.ds(dst, win)], sem.at[slot]).start()
                rc_ref[1 + slot] = rc_ref[1 + slot] + 1
        rc_ref[0] = rc_ref[0] + 1
        return 0

    lax.fori_loop(0, nrounds[j], round_body, 0)

    k16 = lax.broadcasted_iota(jnp.int32, (ROW_ALIGN, 1), 0).astype(F32)
    for e in range(ne):
        rows = slice(e * ROW_ALIGN, (e + 1) * ROW_ALIGN)
        c2_ref[rows, :] = jnp.where(pos_ref[e:e + 1, :] == gp16_ref[rows, :] + k16, 1.0, 0.0).astype(BF16)
    kk = jnp.concatenate([k16] * ne, axis=0)
    keep_old = (gp16_ref[...] == 0.0) & (kk < off16_ref[...])
    for nb in range(d // MXU_DIM):
        cols = slice(nb * MXU_DIM, (nb + 1) * MXU_DIM)
        new = jnp.dot(c2_ref[...], x_ref[:, cols], preferred_element_type=F32).astype(BF16)
        carry_ref[:, cols] = jnp.where(keep_old, carry_ref[:, cols], new)

    @pl.when((j == pl.num_programs(0) - 1) & (rc_ref[0] > 0))
    def _():
        wait_round((rc_ref[0] - 1) % 2)


def _dispatch(gs, h2, tabs, u_mat):
    ne, ntok = gs.shape
    d = h2.shape[1]
    nt = ntok // MOE_TM
    rows = tabs['total'] + MOE_WIN
    grid_spec = pltpu.PrefetchScalarGridSpec(
        num_scalar_prefetch=3,
        grid=(nt,),
        in_specs=[pl.BlockSpec((ne, MOE_TM), lambda j, *_: (0, j)),
                  pl.BlockSpec((None, ne, 1), lambda j, *_: (j, 0, 0)),
                  pl.BlockSpec((None, ne * ROW_ALIGN, 1), lambda j, *_: (j, 0, 0)),
                  pl.BlockSpec((None, ne * ROW_ALIGN, 1), lambda j, *_: (j, 0, 0)),
                  pl.BlockSpec((MOE_TM, d), lambda j, *_: (j, 0)),
                  pl.BlockSpec((MOE_TM, MOE_TM), lambda j, *_: (0, 0))],
        out_specs=pl.BlockSpec(memory_space=pl.ANY),
        scratch_shapes=[pltpu.VMEM((ne, MOE_TM), F32),
                        pltpu.VMEM((ne * MOE_WIN, MOE_TM), BF16),
                        pltpu.VMEM((ne * ROW_ALIGN, MOE_TM), BF16),
                        pltpu.VMEM((2, ne * MOE_WIN, d), BF16),
                        pltpu.VMEM((ne * ROW_ALIGN, d), BF16),
                        pltpu.SemaphoreType.DMA((2,)),
                        pltpu.SMEM((3,), jnp.int32)])
    return pl.pallas_call(
        _dispatch_kernel,
        grid_spec=grid_spec,
        out_shape=jax.ShapeDtypeStruct((ne, rows, d), BF16),
        compiler_params=_cparams(("arbitrary",)),
        name="ec_dispatch",
    )(tabs['a'], tabs['nrounds'], tabs['nr'], gs, tabs['off_col'], tabs['off16'], tabs['gp16'], h2,
      u_mat)


def _combine_kernel(a_tab, nrounds, lim_tab, gs_ref, off_ref, l_ref, ye_ref, x1_ref, g2_ref, lg_ref,
                    lb_ref, o_ref, p_ref, y_ref, acc_ref, sem):
    j = pl.program_id(0)
    tm, ne = gs_ref.shape
    d = x1_ref.shape[1]
    win = MOE_WIN
    gs = gs_ref[...]
    sel = gs >= 0.0
    rank = jnp.dot(l_ref[...], jnp.where(sel, 1.0, 0.0).astype(BF16), preferred_element_type=F32)
    pos = jnp.where(sel, rank + off_ref[...], -1.0)
    gate = jnp.where(sel, gs, 0.0)
    acc_ref[...] = jnp.zeros_like(acc_ref)
    lim = lim_tab[j]

    def round_body(r, _):
        copies = []
        shifts = []
        for e in range(ne):
            want = a_tab[j * ne + e] + r * win
            src = pl.multiple_of(jnp.minimum(want, lim), ROW_ALIGN)
            shifts.append((want - src).astype(F32))
            cp = pltpu.make_async_copy(ye_ref.at[e, pl.ds(src, win)], y_ref.at[pl.ds(e * win, win)],
                                       sem.at[0])
            cp.start()
            copies.append(cp)
        base = (r * win).astype(F32)
        k = lax.broadcasted_iota(jnp.int32, (tm, win), 1).astype(F32) + base
        for e in range(ne):
            pe = pos[:, e:e + 1]
            pe = jnp.where(pe >= base, pe + shifts[e], -1.0)
            pcol = jnp.broadcast_to(pe, (tm, win))
            gcol = jnp.broadcast_to(gate[:, e:e + 1], (tm, win))
            p_ref[:, e * win:(e + 1) * win] = jnp.where(pcol == k, gcol, 0.0).astype(BF16)
        for cp in copies:
            cp.wait()
        for nb in range(d // MXU_DIM):
            cols = slice(nb * MXU_DIM, (nb + 1) * MXU_DIM)
            acc_ref[:, cols] += jnp.dot(p_ref[...], y_ref[:, cols], preferred_element_type=F32)
        return 0

    lax.fori_loop(0, nrounds[j], round_body, 0)
    y = DEEPNORM_ALPHA * x1_ref[...] + (1.0 + g2_ref[...]) * acc_ref[...]
    o_ref[...] = _layer_norm(y, lg_ref[...], lb_ref[...])


def _combine_postnorm(gs_tok, ye, tabs, l_mat, x1, g2, ln_g, ln_b, t):
    ntok, ne = gs_tok.shape
    d = x1.shape[1]
    nt = ntok // MOE_TM
    tiles_per_seq = t // MOE_TM
    grid_spec = pltpu.PrefetchScalarGridSpec(
        num_scalar_prefetch=3,
        grid=(nt,),
        in_specs=[pl.BlockSpec((MOE_TM, ne), lambda j, *_: (j, 0)),
                  pl.BlockSpec((None, 1, ne), lambda j, *_: (j, 0, 0)),
                  pl.BlockSpec((MOE_TM, MOE_TM), lambda j, *_: (0, 0)),
                  pl.BlockSpec(memory_space=pl.ANY),
                  pl.BlockSpec((MOE_TM, d), lambda j, *_: (j, 0)),
                  pl.BlockSpec((None, 1, d), lambda j, *_: (j // tiles_per_seq, 0, 0)),
                  pl.BlockSpec((1, d), lambda j, *_: (0, 0)),
                  pl.BlockSpec((1, d), lambda j, *_: (0, 0))],
        out_specs=pl.BlockSpec((MOE_TM, d), lambda j, *_: (j, 0)),
        scratch_shapes=[pltpu.VMEM((MOE_TM, ne * MOE_WIN), BF16),
                        pltpu.VMEM((ne * MOE_WIN, d), BF16),
                        pltpu.VMEM((MOE_TM, d), F32),
                        pltpu.SemaphoreType.DMA((1,))])
    return pl.pallas_call(
        _combine_kernel,
        grid_spec=grid_spec,
        out_shape=jax.ShapeDtypeStruct((ntok, d), F32),
        compiler_params=_cparams(("arbitrary",)),
        name="ec_combine",
    )(tabs['a'], tabs['nrounds'], tabs['lim'], gs_tok, tabs['off_row'], l_mat, ye, x1, g2, ln_g, ln_b)


def _prep_w_in(w_in_l):
    sizes = (256, 256, 256, MLA_Q_RANK, MLA_KV_RANK, MLA_ROPE, 256, 256, 256, 256, 256, 256)
    offs = np.concatenate([[0], np.cumsum(sizes)])
    part = [w_in_l[:, offs[i]:offs[i + 1]] for i in range(len(sizes))]
    a_q, a_k, a_v, b_cq, b_ckv, b_kr, c_q, c_k, c_v, d_q, d_k, d_v = part
    d = w_in_l.shape[0]
    zeros = lambda n: jnp.zeros((d, n), w_in_l.dtype)
    def per_head(v):
        out = []
        for h in range(4):
            out += [v[:, h * 64:(h + 1) * 64], zeros(LANES - 64)]
        return out

    cols = ([a_q, a_k] + per_head(a_v) + [c_q, c_k] + per_head(c_v)
            + [d_q, d_k, d_v, b_cq, zeros(256 - MLA_Q_RANK), b_ckv, b_kr, b_kr,
               zeros(128 - 2 * MLA_ROPE)])
    return jnp.concatenate(cols, axis=1).astype(BF16)


def _prep_w_uq(w_uq_l):
    hd = MLA_NOPE + MLA_ROPE
    nope = [w_uq_l[:, h * hd:h * hd + MLA_NOPE] for h in range(MLA_HEADS)]
    rope = [w_uq_l[:, h * hd + MLA_NOPE:(h + 1) * hd] for h in range(MLA_HEADS)]
    z = jnp.zeros((w_uq_l.shape[0], 256 - 2 * hd), w_uq_l.dtype)
    cols = []
    for p in range(2):
        cols += [nope[2 * p], nope[2 * p + 1], rope[2 * p], rope[2 * p + 1], z]
    w = jnp.concatenate(cols, axis=1)
    w = jnp.concatenate([w, jnp.zeros((256 - MLA_Q_RANK, w.shape[1]), w.dtype)], axis=0)
    return w.astype(BF16)


def _prep_w_ukv(w_ukv_l):
    hd = MLA_NOPE + MLA_DV
    kn = [w_ukv_l[:, h * hd:h * hd + MLA_NOPE] for h in range(MLA_HEADS)]
    z = jnp.zeros((w_ukv_l.shape[0], LANES - MLA_DV), w_ukv_l.dtype)
    vv = []
    for h in range(MLA_HEADS):
        vv += [w_ukv_l[:, h * hd + MLA_NOPE:(h + 1) * hd], z]
    return jnp.concatenate(kn + vv, axis=1).astype(BF16)


def _select_groups(aff_t, group_tokens):
    parts, off = [], 0
    for n in group_tokens:
        parts.append(_select(lax.slice_in_dim(aff_t, off, off + n, axis=1),
                             EC_FACTOR * n // N_EXPERTS))
        off += n
    return jnp.concatenate(parts, axis=1)


def _trunk(x, c, group_tokens, nb, t, p):
    ntok = nb * t
    tabs = (_rope_tables(t, 256, DA_DQK, DA_DQK // ROPE_FRACTION, ROPE_THETA)
            + _rope_tables(t, 256, DIL_DH, DIL_DH // ROPE_FRACTION, ROPE_THETA)
            + _rope_tables(t, 128, MLA_ROPE, MLA_ROPE, MLA_ROPE_THETA))
    gmat = jnp.asarray(np.kron(np.eye(4), np.full((64, 64), 1.0 / 64)), BF16)
    ti = jnp.arange(MOE_TM)
    u_mat = (ti[:, None] < ti[None, :]).astype(BF16)
    l_mat = (ti[None, :] < ti[:, None]).astype(BF16)
    for l in range(DEPTH):
        mod = _modulation(c, p['w_ada'][l], p['b_ada'][l])
        sh1, sc1, g1, sh2, sc2, g2 = [m.reshape(nb, 1, D_MODEL) for m in jnp.split(mod, 6, axis=-1)]
        gq = jnp.concatenate([p['q_norm_g'][l], jnp.zeros((256 - MLA_Q_RANK,), F32)]).reshape(1, 256)
        gkv = p['kv_norm_g'][l].reshape(1, 128)
        proj, projv = _inproj(x, sc1, sh1, _prep_w_in(p['w_in'][l]), _prep_w_uq(p['w_uq'][l]),
                              _prep_w_ukv(p['w_ukv'][l]), gq, gkv, tabs, nb, t)
        lam_init = 0.8 - 0.6 * math.exp(-0.3 * l)
        lam = (jnp.exp(jnp.sum(p['da_lq1'][l] * p['da_lk1'][l]))
               - jnp.exp(jnp.sum(p['da_lq2'][l] * p['da_lk2'][l])) + lam_init).reshape(1)
        g_sub = jnp.tile(p['da_subln_g'][l], DA_HEADS).reshape(1, 256)
        oa = _diff_attention(proj, projv, lam, g_sub, gmat, nb, t, 1.0 - lam_init)
        ob = _mla_attention(proj, projv, nb, t)
        oc = _dil_attention(proj, projv, nb, t)
        od = _na_attention(proj, _na_bias_table(p['na_rpb'][l]), nb, t)
        x1, h2, aff_t = _outproj(oa, ob, oc, od, x, g1, sc2, sh2, p['w_out'][l].astype(BF16),
                                 p['ln1_g'][l].reshape(1, -1), p['ln1_b'][l].reshape(1, -1),
                                 p['w_router'][l].T, nb, t)
        wg = p['w_e_gate'][l].astype(BF16)
        wu = p['w_e_up'][l].astype(BF16)
        wd = p['w_e_down'][l].astype(BF16)
        gs = _select_groups(aff_t, group_tokens)
        rt = _routing_tables(gs, group_tokens)
        xe = _dispatch(gs, h2, rt, u_mat)
        ye = _experts(xe, rt['total'], wg, wu, wd)
        x = _combine_postnorm(gs.T, ye, rt, l_mat, x1, g2, p['ln2_g'][l].reshape(1, -1),
                              p['ln2_b'][l].reshape(1, -1), t)
    return x


def kernel(x_prompt, x_sample, c_prompt, c_sample, w_in, w_uq, w_ukv, q_norm_g, kv_norm_g, da_lq1,
           da_lk1, da_lq2, da_lk2, da_subln_g, na_rpb, w_out, w_ada, b_ada, ln1_g, ln1_b, ln2_g,
           ln2_b, w_router, w_e_gate, w_e_up, w_e_down):
    p = dict(w_in=w_in, w_uq=w_uq, w_ukv=w_ukv, q_norm_g=q_norm_g, kv_norm_g=kv_norm_g,
             da_lq1=da_lq1, da_lk1=da_lk1, da_lq2=da_lq2, da_lk2=da_lk2, da_subln_g=da_subln_g,
             na_rpb=na_rpb, w_out=w_out, w_ada=w_ada, b_ada=b_ada, ln1_g=ln1_g, ln1_b=ln1_b,
             ln2_g=ln2_g, ln2_b=ln2_b, w_router=w_router, w_e_gate=w_e_gate, w_e_up=w_e_up,
             w_e_down=w_e_down)
    bp, t, d = x_prompt.shape
    bs = x_sample.shape[0]
    assert x_sample.shape[1] == t
    nb = bp + bs
    x = jnp.concatenate([x_prompt.reshape(bp * t, d), x_sample.reshape(bs * t, d)], axis=0)
    c = jnp.concatenate([c_prompt, c_sample], axis=0)
    y = _trunk(x, c, (bp * t, bs * t), nb, t, p)
    return y[:bp * t].reshape(bp, t, d), y[bp * t:].reshape(bs, t, d)
```

```python
import functools
import math

import jax
import jax.numpy as jnp
import numpy as np
from jax import lax
from jax.experimental import pallas as pl
from jax.experimental.pallas import tpu as pltpu

F32 = jnp.float32
BF16 = jnp.bfloat16

D_MODEL = 1024
DEPTH = 2
GRID_W = 64
GROUP_W = 256
DA_HEADS, DA_DV, DA_DQK = 4, 64, 32
MLA_HEADS, MLA_Q_RANK, MLA_KV_RANK, MLA_NOPE, MLA_ROPE, MLA_DV = 4, 192, 128, 64, 32, 64
MLA_ROPE_THETA = 10000.0
DIL_HEADS, DIL_DH = 4, 64
DIL_PATTERNS = ((128, 1), (512, 4), (2048, 16))
NA_HEADS, NA_DH, NA_KR, NA_KC = 4, 64, 8, 16
ROPE_THETA = 500000.0
ROPE_FRACTION = 4
N_EXPERTS = 16
EC_FACTOR = 2
D_FF = 2816
DEEPNORM_ALPHA = (2.0 * DEPTH) ** 0.25
NEG_INF = -1e30
LOG2E = math.log2(math.e)
LN_EPS = 1e-5
RMS_EPS = 1e-6

LANES = 128
MXU_DIM = 256
VMEM_LIMIT = 56 * 1024 * 1024

S_AQ, S_AK, S_BQ0, S_BQ1, S_BK0, S_BK1, S_CQ, S_CK, S_DQ, S_DK, S_DV = range(11)
N_SLOTS = 11
V_A, V_B, V_C = 0, 4, 8
N_VSLOTS = 12
V_ONE_LANE = 64
W_ALL_COLS = 13 * 256


def _cparams(sem):
    return pltpu.CompilerParams(dimension_semantics=sem, vmem_limit_bytes=VMEM_LIMIT)


def _split_bf16(a):
    hi = a.astype(BF16)
    lo = (a - hi.astype(F32)).astype(BF16)
    return hi, lo


def _mod_kernel(c_ref, w_ref, b_ref, o_ref):
    c = c_ref[...]
    a = c * (1.0 / (1.0 + jnp.exp(-c)))
    a_hi, a_lo = _split_bf16(a)
    w_hi, w_lo = _split_bf16(w_ref[...])
    acc = jnp.dot(a_hi, w_hi, preferred_element_type=F32)
    acc += jnp.dot(a_hi, w_lo, preferred_element_type=F32)
    acc += jnp.dot(a_lo, w_hi, preferred_element_type=F32)
    o_ref[...] = acc + b_ref[...]


def _modulation(c, w_ada, b_ada):
    nb, d = c.shape
    n_out = w_ada.shape[1]
    tn = 1536
    return pl.pallas_call(
        _mod_kernel,
        grid=(n_out // tn,),
        in_specs=[pl.BlockSpec((nb, d), lambda j: (0, 0)),
                  pl.BlockSpec((d, tn), lambda j: (0, j)),
                  pl.BlockSpec((1, tn), lambda j: (0, j))],
        out_specs=pl.BlockSpec((nb, tn), lambda j: (0, j)),
        out_shape=jax.ShapeDtypeStruct((nb, n_out), F32),
        compiler_params=_cparams(("arbitrary",)),
        name="adaln_mod",
    )(c, w_ada, b_ada.reshape(1, n_out))


def _rope_tables(t, width, group, rot, theta):
    half = rot // 2
    inv = theta ** (-jnp.arange(half, dtype=F32) / half)
    ang = jnp.arange(t, dtype=F32)[:, None] * inv[None, :]
    cos, sin = jnp.cos(ang), jnp.sin(ang)
    ones = jnp.ones((t, group - rot), F32)
    zeros = jnp.zeros((t, group - rot), F32)
    c_g = jnp.concatenate([cos, cos, ones], axis=1)
    s_g = jnp.concatenate([-sin, sin, zeros], axis=1)
    reps = width // group
    return jnp.tile(c_g, (1, reps)), jnp.tile(s_g, (1, reps))


def _apply_rope(x, c_tab, s_tab, group, rot):
    width = x.shape[-1]
    half = rot // 2
    lane = lax.broadcasted_iota(jnp.int32, (1, width), 1)
    first = (lane % group) < half
    fwd = pltpu.roll(x, width - half, 1)
    bwd = pltpu.roll(x, half, 1)
    return x * c_tab + s_tab * jnp.where(first, fwd, bwd)


def _inproj_kernel(x_ref, sc_ref, sh_ref, w_ref, wuq_ref, wukv_ref, gq_ref, gkv_ref,
                   ca_ref, sa_ref, cc_ref, scc_ref, cm_ref, sm_ref, o_ref, ov_ref):
    h = (x_ref[...] * (1.0 + sc_ref[...]) + sh_ref[...]).astype(BF16)

    def proj(col, width):
        return jnp.dot(h, w_ref[:, col:col + width], preferred_element_type=F32)

    one_lane = lax.broadcasted_iota(jnp.int32, (1, LANES), 1) == V_ONE_LANE

    def put_values(first_slot, vals):
        for hh in range(4):
            v = vals[:, LANES * hh:LANES * (hh + 1)]
            ov_ref[first_slot + hh] = jnp.where(one_lane, 1.0, v).astype(BF16)

    sa_scale = DA_DQK ** -0.5 * LOG2E
    sb_scale = (MLA_NOPE + MLA_ROPE) ** -0.5 * LOG2E
    sc_scale = DIL_DH ** -0.5 * LOG2E
    sd_scale = NA_DH ** -0.5 * LOG2E
    a_rot = DA_DQK // ROPE_FRACTION
    c_rot = DIL_DH // ROPE_FRACTION

    ca, sa = ca_ref[...], sa_ref[...]
    o_ref[S_AQ] = (_apply_rope(proj(0, 256), ca, sa, DA_DQK, a_rot) * sa_scale).astype(BF16)
    o_ref[S_AK] = _apply_rope(proj(256, 256), ca, sa, DA_DQK, a_rot).astype(BF16)
    put_values(V_A, proj(512, 512))
    cc, scc = cc_ref[...], scc_ref[...]
    o_ref[S_CQ] = (_apply_rope(proj(1024, 256), cc, scc, DIL_DH, c_rot) * sc_scale).astype(BF16)
    o_ref[S_CK] = _apply_rope(proj(1280, 256), cc, scc, DIL_DH, c_rot).astype(BF16)
    put_values(V_C, proj(1536, 512))
    o_ref[S_DQ] = (proj(2048, 256) * sd_scale).astype(BF16)
    o_ref[S_DK] = proj(2304, 256).astype(BF16)
    o_ref[S_DV] = proj(2560, 256).astype(BF16)

    cm, sm = cm_ref[...], sm_ref[...]
    cq = proj(2816, 256)
    cq = cq * lax.rsqrt(jnp.sum(cq * cq, -1, keepdims=True) * (1.0 / MLA_Q_RANK) + RMS_EPS)
    cq = (cq * gq_ref[...]).astype(BF16)
    q2 = jnp.dot(cq, wuq_ref[...], preferred_element_type=F32)
    for p in range(2):
        qp = q2[:, 256 * p:256 * (p + 1)]
        o_ref[S_BQ0 + p, :, 0:128] = (qp[:, 0:128] * sb_scale).astype(BF16)
        o_ref[S_BQ0 + p, :, 128:256] = (
            _apply_rope(qp[:, 128:256], cm, sm, MLA_ROPE, MLA_ROPE) * sb_scale).astype(BF16)
    ckv = proj(3072, 128)
    ckv = ckv * lax.rsqrt(jnp.mean(ckv * ckv, -1, keepdims=True) + RMS_EPS)
    ckv = (ckv * gkv_ref[...]).astype(BF16)
    kv = jnp.dot(ckv, wukv_ref[...], preferred_element_type=F32)
    kr = _apply_rope(proj(3200, 128), cm, sm, MLA_ROPE, MLA_ROPE).astype(BF16)
    for p in range(2):
        o_ref[S_BK0 + p, :, 0:128] = kv[:, 128 * p:128 * (p + 1)].astype(BF16)
        o_ref[S_BK0 + p, :, 128:256] = kr
    put_values(V_B, kv[:, 256:768])


def _inproj(x, sc, sh, w_all, wuq, wukv, gq, gkv, tabs, nb, t):
    tm = 512
    nt = t // tm
    ntok = nb * t
    ca, sa, cc, scc, cm, sm = tabs
    full = lambda shape: pl.BlockSpec(shape, lambda j, b: tuple(0 for _ in shape))
    tab = lambda w: pl.BlockSpec((tm, w), lambda j, b: (j, 0))
    return pl.pallas_call(
        _inproj_kernel,
        grid=(nt, nb),
        in_specs=[pl.BlockSpec((tm, D_MODEL), lambda j, b: (b * nt + j, 0)),
                  pl.BlockSpec((None, 1, D_MODEL), lambda j, b: (b, 0, 0)),
                  pl.BlockSpec((None, 1, D_MODEL), lambda j, b: (b, 0, 0)),
                  full((D_MODEL, W_ALL_COLS)), full((256, 512)), full((128, 768)),
                  full((1, 256)), full((1, 128)),
                  tab(256), tab(256), tab(256), tab(256), tab(128), tab(128)],
        out_specs=[pl.BlockSpec((N_SLOTS, tm, 256), lambda j, b: (0, b * nt + j, 0)),
                   pl.BlockSpec((N_VSLOTS, tm, LANES), lambda j, b: (0, b * nt + j, 0))],
        out_shape=[jax.ShapeDtypeStruct((N_SLOTS, ntok, 256), BF16),
                   jax.ShapeDtypeStruct((N_VSLOTS, ntok, LANES), BF16)],
        compiler_params=_cparams(("arbitrary", "arbitrary")),
        name="in_proj",
    )(x, sc, sh, w_all, wuq, wukv, gq, gkv, ca, sa, cc, scc, cm, sm)


def _lane_mask(width, ranges):
    lane = lax.broadcasted_iota(jnp.int32, (1, width), 1)
    m = None
    for lo, hi in ranges:
        r = (lane >= lo) & (lane < hi)
        m = r if m is None else (m | r)
    return m


SOFTMAX_SLAB = 32


def _chain_scratch(tq, kc):
    return [pltpu.VMEM((tq, kc), F32), pltpu.VMEM((tq, kc), F32), pltpu.VMEM((tq, kc), BF16),
            pltpu.VMEM((tq, LANES), F32), pltpu.VMEM((tq, LANES), F32), pltpu.VMEM((tq, LANES), F32)]


CHAIN_REFS = 6


def _attention_steps(n_steps, qms_for, k_for, v_for, finish, tq, t, kc, chains, slab=SOFTMAX_SLAB,
                     bias_ref=None):
    n_chunks = t // kc
    assert n_chunks % 2 == 0
    nt = (((1,), (1,)), ((), ()))

    def chunk_rows(chunk):
        if isinstance(chunk, int):
            return pl.ds(chunk * kc, kc)
        return pl.ds(pl.multiple_of(chunk * kc, kc), kc)

    def scores(step, chunk, slot):
        k = k_for(step)[chunk_rows(chunk), :]
        for qm, chain in zip(qms_for(step), chains):
            chain[slot][...] = lax.dot_general(qm, k, nt, preferred_element_type=F32)

    def softmax_pv(step, chunk, slot):
        for v_ref, chain in zip(v_for(step), chains):
            v = v_ref[chunk_rows(chunk), :]
            s_ref = chain[slot]
            (p_ref, m_ref, a_ref, acc_ref) = chain[2:]
            for r in range(tq // slab):
                rows = slice(r * slab, (r + 1) * slab)
                s = s_ref[rows, :]
                if bias_ref is not None:
                    s = s + bias_ref[rows, chunk * kc:(chunk + 1) * kc]
                m_prev = m_ref[rows, :]
                m_new = jnp.maximum(m_prev, jnp.max(s, -1, keepdims=True))
                d = s - jnp.concatenate([m_new] * (kc // LANES), axis=1)
                p_ref[rows, :] = jnp.exp2(d.astype(BF16))
                a_ref[rows, :] = jnp.exp2(m_prev - m_new)
                m_ref[rows, :] = m_new
            acc_ref[...] = a_ref[...] * acc_ref[...] + jnp.dot(p_ref[...], v,
                                                              preferred_element_type=F32)

    def step_body(st, _):
        for (_, _, _, m_ref, _, acc_ref) in chains:
            m_ref[...] = jnp.full(m_ref.shape, NEG_INF, F32)
            acc_ref[...] = jnp.zeros(acc_ref.shape, F32)
        nxt = jnp.minimum(st + 1, n_steps - 1)

        def body(j, _):
            scores(st, 2 * j + 1, 1)
            softmax_pv(st, 2 * j, 0)
            if n_chunks == 2:
                scores(nxt, 0, 0)
            else:
                last = 2 * j + 2 == n_chunks
                scores(jnp.where(last, nxt, st), jnp.where(last, 0, 2 * j + 2), 0)
            softmax_pv(st, 2 * j + 1, 1)
            return 0

        if n_chunks == 2:
            body(0, 0)
        else:
            assert bias_ref is None
            lax.fori_loop(0, n_chunks // 2, body, 0)
        outs = []
        for chain in chains:
            acc = chain[5][...]
            den = jnp.broadcast_to(acc[:, V_ONE_LANE:V_ONE_LANE + 1], acc.shape)
            outs.append(acc * (1.0 / den))
        finish(st, outs)
        return 0

    scores(0, 0, 0)
    lax.fori_loop(0, n_steps, step_body, 0)


def _place_head(pair_ref, o, j):
    lane = lax.broadcasted_iota(jnp.int32, (1, LANES), 1)
    low = lane < V_ONE_LANE
    shifted = pltpu.roll(o, V_ONE_LANE, 1)
    cur = pair_ref[...]
    pair_ref[...] = jnp.where(low, jnp.where(j == 0, o, cur), jnp.where(j == 1, shifted, cur))


def _group_mean_sq(x, gmat):
    sq = x * x
    hi, lo = _split_bf16(sq)
    return (jnp.dot(hi, gmat, preferred_element_type=F32)
            + jnp.dot(lo, gmat, preferred_element_type=F32))


def _diff_attn_kernel(lam_ref, q_ref, k_ref, v_ref, g_ref, gmat_ref, o_ref, acc_ref, *scratch, t, kc,
                      out_scale):
    chains = (scratch[:CHAIN_REFS], scratch[CHAIN_REFS:])
    lam = lam_ref[0]
    lane = lax.broadcasted_iota(jnp.int32, (1, 256), 1)

    def qms_for(h):
        q = q_ref[...]
        return [jnp.where((lane >= (2 * h + c) * DA_DQK) & (lane < (2 * h + c + 1) * DA_DQK), q,
                          jnp.zeros_like(q)) for c in range(2)]

    def finish(h, outs):
        _place_head(acc_ref.at[h // 2], outs[0] - lam * outs[1], h % 2)

    acc_ref[...] = jnp.zeros_like(acc_ref)
    _attention_steps(DA_HEADS, qms_for, lambda h: k_ref, lambda h: [v_ref.at[h], v_ref.at[h]],
                     finish, q_ref.shape[0], t, kc, chains)
    o = jnp.concatenate([acc_ref[0], acc_ref[1]], axis=1)
    ms = _group_mean_sq(o, gmat_ref[...])
    o_ref[...] = (o * lax.rsqrt(ms + RMS_EPS) * g_ref[...] * out_scale).astype(o_ref.dtype)


def _diff_attention(proj, projv, lam, g_tiled, gmat, nb, t, out_scale):
    tq, kc = 512, 512
    nq = t // tq
    kern = functools.partial(_diff_attn_kernel, t=t, kc=kc, out_scale=out_scale)
    return pl.pallas_call(
        kern,
        grid=(nb, nq),
        in_specs=[pl.BlockSpec(memory_space=pltpu.SMEM),
                  pl.BlockSpec((None, tq, 256), lambda b, i: (S_AQ, b * nq + i, 0)),
                  pl.BlockSpec((None, t, 256), lambda b, i: (S_AK, b, 0)),
                  pl.BlockSpec((DA_HEADS, t, LANES), lambda b, i: (V_A // DA_HEADS, b, 0)),
                  pl.BlockSpec((1, 256), lambda b, i: (0, 0)),
                  pl.BlockSpec((256, 256), lambda b, i: (0, 0))],
        out_specs=pl.BlockSpec((tq, 256), lambda b, i: (b * nq + i, 0)),
        out_shape=jax.ShapeDtypeStruct((nb * t, 256), BF16),
        scratch_shapes=[pltpu.VMEM((2, tq, LANES), F32)] + 2 * _chain_scratch(tq, kc),
        compiler_params=_cparams(("arbitrary", "arbitrary")),
        name="diff_attn",
    )(lam, proj, proj, projv, g_tiled, gmat)


def _mla_attn_kernel(q_ref, k_ref, v_ref, o_ref, acc_ref, *scratch, t, kc):
    chains = (scratch[:CHAIN_REFS], scratch[CHAIN_REFS:])
    lane = lax.broadcasted_iota(jnp.int32, (1, 256), 1)

    def qms_for(p):
        q = q_ref[p]
        qms = []
        for j in range(2):
            nope = (lane >= j * MLA_NOPE) & (lane < (j + 1) * MLA_NOPE)
            rope = (lane >= 128 + j * MLA_ROPE) & (lane < 128 + (j + 1) * MLA_ROPE)
            qms.append(jnp.where(nope | rope, q, jnp.zeros_like(q)))
        return qms

    def finish(p, outs):
        for j in range(2):
            _place_head(acc_ref.at[p], outs[j], j)

    acc_ref[...] = jnp.zeros_like(acc_ref)
    _attention_steps(MLA_HEADS // 2, qms_for, lambda p: k_ref.at[p],
                     lambda p: [v_ref.at[2 * p], v_ref.at[2 * p + 1]], finish, q_ref.shape[1], t, kc,
                     chains)
    o_ref[...] = jnp.concatenate([acc_ref[0], acc_ref[1]], axis=1).astype(o_ref.dtype)


def _mla_attention(proj, projv, nb, t):
    tq, kc = 512, 512
    nq = t // tq
    kern = functools.partial(_mla_attn_kernel, t=t, kc=kc)
    return pl.pallas_call(
        kern,
        grid=(nb, nq),
        in_specs=[pl.BlockSpec((2, tq, 256), lambda b, i: (S_BQ0 // 2, b * nq + i, 0)),
                  pl.BlockSpec((2, t, 256), lambda b, i: (S_BK0 // 2, b, 0)),
                  pl.BlockSpec((MLA_HEADS, t, LANES), lambda b, i: (V_B // MLA_HEADS, b, 0))],
        out_specs=pl.BlockSpec((tq, 256), lambda b, i: (b * nq + i, 0)),
        out_shape=jax.ShapeDtypeStruct((nb * t, 256), BF16),
        scratch_shapes=[pltpu.VMEM((2, tq, LANES), F32)] + 2 * _chain_scratch(tq, kc),
        compiler_params=_cparams(("arbitrary", "arbitrary")),
        name="mla_attn",
    )(proj, proj, projv)


DIL_REACH = max(w // 2 for w, _ in DIL_PATTERNS)


def _dil_attn_kernel(q_ref, k_ref, v_ref, o_ref, tables_ref, acc_ref, *scratch, t, tq, band):
    chains = (scratch[:CHAIN_REFS], scratch[CHAIN_REFS:])
    i = pl.program_id(1)

    def band_start(blk):
        return jnp.clip(blk * tq - DIL_REACH, 0, t - band)

    start = pl.multiple_of(band_start(i), tq)
    table = (i * tq - start) // tq

    @pl.when((pl.program_id(0) == 0) & (i == 0))
    def _():
        def build(n, _):
            qi = lax.broadcasted_iota(jnp.int32, (tq, band), 0)
            kj = lax.broadcasted_iota(jnp.int32, (tq, band), 1)
            delta = kj - qi - n * tq
            ad = jnp.abs(delta)
            cnt = jnp.zeros((tq, band), F32)
            for window, dil in DIL_PATTERNS:
                ok = (ad <= window // 2) & ((delta & (dil - 1)) == 0)
                cnt = cnt + jnp.where(ok, 1.0, 0.0)
            tables_ref[n] = jnp.where(cnt > 2.5, math.log2(3.0),
                                      jnp.where(cnt > 1.5, 1.0,
                                                jnp.where(cnt > 0.5, 0.0, NEG_INF)))
            return 0

        lax.fori_loop(0, tables_ref.shape[0], build, 0)

    bias_ref = tables_ref.at[table]

    lane = lax.broadcasted_iota(jnp.int32, (1, 256), 1)

    def qms_for(hp):
        q = q_ref[...]
        return [jnp.where((lane >= (2 * hp + j) * DIL_DH) & (lane < (2 * hp + j + 1) * DIL_DH), q,
                          jnp.zeros_like(q)) for j in range(2)]

    def finish(hp, outs):
        for j in range(2):
            _place_head(acc_ref.at[hp], outs[j], j)

    acc_ref[...] = jnp.zeros_like(acc_ref)
    _attention_steps(DIL_HEADS // 2, qms_for, lambda hp: k_ref.at[pl.ds(start, band)],
                     lambda hp: [v_ref.at[2 * hp + j, pl.ds(start, band)] for j in range(2)],
                     finish, tq, band, band // 2, chains, slab=DIL_SLAB, bias_ref=bias_ref)
    o_ref[...] = jnp.concatenate([acc_ref[0], acc_ref[1]], axis=1).astype(o_ref.dtype)


DIL_SLAB = 16


def _dil_attention(proj, projv, nb, t):
    tq = 256
    band = min(t, tq + 2 * DIL_REACH)
    nq = t // tq
    kern = functools.partial(_dil_attn_kernel, t=t, tq=tq, band=band)
    return pl.pallas_call(
        kern,
        grid=(nb, nq),
        in_specs=[pl.BlockSpec((None, tq, 256), lambda b, i: (S_CQ, b * nq + i, 0)),
                  pl.BlockSpec((None, t, 256), lambda b, i: (S_CK, b, 0)),
                  pl.BlockSpec((DIL_HEADS, t, LANES), lambda b, i: (V_C // DIL_HEADS, b, 0))],
        out_specs=pl.BlockSpec((tq, 256), lambda b, i: (b * nq + i, 0)),
        out_shape=jax.ShapeDtypeStruct((nb * t, 256), BF16),
        scratch_shapes=([pltpu.VMEM((band // tq, tq, band), F32), pltpu.VMEM((2, tq, LANES), F32)]
                        + 2 * _chain_scratch(tq, band // 2)),
        compiler_params=_cparams(("arbitrary", "arbitrary")),
        name="dil_attn",
    )(proj, proj, projv)


def _na_bias_table(rpb):
    c = np.arange(GRID_W)
    cs = np.clip(c - NA_KC // 2, 0, GRID_W - NA_KC)
    colmask = (c[None, :] >= cs[:, None]) & (c[None, :] < cs[:, None] + NA_KC)
    rows = jnp.stack([rpb[:, si:si + NA_KR, :] for si in range(NA_KR)], axis=1).astype(F32)
    edge = GRID_W - NA_KC
    padded = jnp.concatenate([jnp.repeat(rows[..., :1], edge, axis=-1), rows,
                              jnp.repeat(rows[..., -1:], edge, axis=-1)], axis=-1)
    b = jnp.stack([padded[..., GRID_W - 1 - qc:2 * GRID_W - 1 - qc] for qc in range(GRID_W)],
                  axis=2)
    b = jnp.where(colmask[None, None, :, None, :], b * LOG2E, NEG_INF)
    b = b.transpose(1, 0, 2, 3, 4)
    return b.reshape(NA_KR, rpb.shape[0] * GRID_W, NA_KR * GRID_W)


def _na_attn_kernel(q_ref, k_ref, v_ref, tb_ref, o_ref, *, rows, rg):
    g = pl.program_id(1)
    lane = lax.broadcasted_iota(jnp.int32, (1, 256), 1)
    nk = NA_KR * GRID_W
    sels = [(lane >= h * NA_DH) & (lane < (h + 1) * NA_DH) for h in range(NA_HEADS)]

    def row(r, _):
        grow = g * rg + r
        rs = jnp.clip(grow - NA_KR // 2, 0, rows - NA_KR)
        si = rs - grow + (NA_KR - 1)
        q = q_ref[pl.ds(pl.multiple_of(r * GRID_W, GRID_W), GRID_W), :]
        koff = pl.multiple_of(rs * GRID_W, GRID_W)
        kb = k_ref[pl.ds(koff, nk), :]
        vb = v_ref[pl.ds(koff, nk), :]
        q4 = jnp.concatenate([jnp.where(sel, q, jnp.zeros_like(q)) for sel in sels], axis=0)
        s = lax.dot_general(q4, kb, (((1,), (1,)), ((), ())), preferred_element_type=F32)
        s = s + tb_ref[si]
        m = jnp.max(s, -1, keepdims=True)
        p = jnp.exp2(s - m)
        l = jnp.sum(p, -1, keepdims=True)
        o4 = jnp.dot(p.astype(BF16), vb, preferred_element_type=F32) * (1.0 / l)
        out = jnp.zeros((GRID_W, 256), F32)
        for h, sel in enumerate(sels):
            out = jnp.where(sel, o4[h * GRID_W:(h + 1) * GRID_W], out)
        o_ref[pl.ds(pl.multiple_of(r * GRID_W, GRID_W), GRID_W), :] = out.astype(o_ref.dtype)
        return 0

    lax.fori_loop(0, rg, row, 0, unroll=2)


def _na_attention(proj, tb, nb, t):
    rows = t // GRID_W
    assert rows >= NA_KR
    rg = 8
    ng = rows // rg
    tq = rg * GRID_W
    kern = functools.partial(_na_attn_kernel, rows=rows, rg=rg)
    return pl.pallas_call(
        kern,
        grid=(nb, ng),
        in_specs=[pl.BlockSpec((None, tq, 256), lambda b, i: (S_DQ, b * ng + i, 0)),
                  pl.BlockSpec((None, t, 256), lambda b, i: (S_DK, b, 0)),
                  pl.BlockSpec((None, t, 256), lambda b, i: (S_DV, b, 0)),
                  pl.BlockSpec(tb.shape, lambda b, i: (0, 0, 0))],
        out_specs=pl.BlockSpec((tq, 256), lambda b, i: (b * ng + i, 0)),
        out_shape=jax.ShapeDtypeStruct((nb * t, 256), BF16),
        compiler_params=_cparams(("arbitrary", "arbitrary")),
        name="na_attn",
    )(proj, proj, proj, tb)


def _layer_norm(y, g, b):
    mu = jnp.mean(y, -1, keepdims=True)
    yc = y - mu
    var = jnp.mean(yc * yc, -1, keepdims=True)
    return yc * lax.rsqrt(var + LN_EPS) * g + b


def _outproj_kernel(oa_ref, ob_ref, oc_ref, od_ref, x_ref, g1_ref, sc2_ref, sh2_ref, w_ref,
                    lg_ref, lb_ref, wr_ref, x1_ref, h2_ref, aff_ref):
    m = jnp.dot(oa_ref[...], w_ref[0:256, :], preferred_element_type=F32)
    m += jnp.dot(ob_ref[...], w_ref[256:512, :], preferred_element_type=F32)
    m += jnp.dot(oc_ref[...], w_ref[512:768, :], preferred_element_type=F32)
    m += jnp.dot(od_ref[...], w_ref[768:1024, :], preferred_element_type=F32)
    y = DEEPNORM_ALPHA * x_ref[...] + (1.0 + g1_ref[...]) * m
    x1 = _layer_norm(y, lg_ref[...], lb_ref[...])
    x1_ref[...] = x1
    h2 = x1 * (1.0 + sc2_ref[...]) + sh2_ref[...]
    h2_ref[...] = h2.astype(BF16)
    h_hi, h_lo = _split_bf16(h2)
    w_hi, w_lo = _split_bf16(wr_ref[...])
    nt = (((1,), (1,)), ((), ()))
    lg = lax.dot_general(w_hi, h_hi, nt, preferred_element_type=F32)
    lg += lax.dot_general(w_hi, h_lo, nt, preferred_element_type=F32)
    lg += lax.dot_general(w_lo, h_hi, nt, preferred_element_type=F32)
    lg = lg - jnp.max(lg, 0, keepdims=True)
    e = jnp.exp(lg)
    aff_ref[...] = e / jnp.sum(e, 0, keepdims=True)


def _outproj(oa, ob, oc, od, x, g1, sc2, sh2, w_out, ln_g, ln_b, w_router_t, nb, t):
    tm = 512
    nt = t // tm
    ntok = nb * t
    tok = lambda w: pl.BlockSpec((tm, w), lambda i: (i, 0))
    per_b = pl.BlockSpec((None, 1, D_MODEL), lambda i: (i // nt, 0, 0))
    full = lambda shape: pl.BlockSpec(shape, lambda i: tuple(0 for _ in shape))
    return pl.pallas_call(
        _outproj_kernel,
        grid=(ntok // tm,),
        in_specs=[tok(256), tok(256), tok(256), tok(256), tok(D_MODEL), per_b, per_b, per_b,
                  full((D_MODEL, D_MODEL)), full((1, D_MODEL)), full((1, D_MODEL)),
                  full((N_EXPERTS, D_MODEL))],
        out_specs=[tok(D_MODEL), tok(D_MODEL),
                   pl.BlockSpec((N_EXPERTS, tm), lambda i: (0, i))],
        out_shape=[jax.ShapeDtypeStruct((ntok, D_MODEL), F32),
                   jax.ShapeDtypeStruct((ntok, D_MODEL), BF16),
                   jax.ShapeDtypeStruct((N_EXPERTS, ntok), F32)],
        compiler_params=_cparams(("arbitrary",)),
        name="out_proj",
    )(oa, ob, oc, od, x, g1, sc2, sh2, w_out, ln_g, ln_b, w_router_t)


def _expert_kernel(x_ref, wg_ref, wu_ref, wd_ref, o_ref, acc_ref):
    f = pl.program_id(2)

    @pl.when(f == 0)
    def _():
        acc_ref[...] = jnp.zeros_like(acc_ref)

    x = x_ref[...]
    g = jnp.dot(x, wg_ref[...], preferred_element_type=F32)
    u = jnp.dot(x, wu_ref[...], preferred_element_type=F32)
    hmid = (g * (1.0 / (1.0 + jnp.exp(-g))) * u).astype(BF16)
    acc_ref[...] += jnp.dot(hmid, wd_ref[...], preferred_element_type=F32)

    @pl.when(f == pl.num_programs(2) - 1)
    def _():
        o_ref[...] = acc_ref[...].astype(o_ref.dtype)


def _experts(xe, slots, wg, wu, wd):
    ne, _, d = xe.shape
    tm = math.gcd(slots, 1024)
    tf = 256
    nf = D_FF // tf
    return pl.pallas_call(
        _expert_kernel,
        grid=(ne, slots // tm, nf),
        in_specs=[pl.BlockSpec((None, tm, d), lambda e, m, f: (e, m, 0)),
                  pl.BlockSpec((None, d, tf), lambda e, m, f: (e, 0, f)),
                  pl.BlockSpec((None, d, tf), lambda e, m, f: (e, 0, f)),
                  pl.BlockSpec((None, tf, d), lambda e, m, f: (e, f, 0))],
        out_specs=pl.BlockSpec((None, tm, d), lambda e, m, f: (e, m, 0)),
        out_shape=jax.ShapeDtypeStruct((ne, slots, d), BF16),
        scratch_shapes=[pltpu.VMEM((tm, d), F32)],
        compiler_params=_cparams(("arbitrary", "arbitrary", "arbitrary")),
        name="expert_ffn",
    )(xe, wg, wu, wd)


MOE_TM = 512
MOE_WIN = 128
ROW_ALIGN = 16


def _select_kernel(aff_ref, o_ref, *, cap):
    a = aff_ref[...]
    keys = lax.bitcast_convert_type(a, jnp.int32)
    ne, n = a.shape
    capf = float(cap)

    def count(mask):
        return jnp.sum(jnp.where(mask, 1.0, 0.0), axis=1, keepdims=True)

    def key_bit(b, thr):
        cand = thr | lax.shift_left(jnp.int32(1), 30 - b)
        return jnp.where(count(keys >= cand) >= capf, cand, thr)

    thr = lax.fori_loop(0, 31, key_bit, jnp.zeros((ne, 1), jnp.int32))
    above = keys > thr
    need = capf - count(above)
    idx = lax.broadcasted_iota(jnp.int32, (ne, n), 1)
    tie_idx = jnp.where(keys == thr, idx, jnp.int32(2 ** 30))
    nbits = max(1, (n - 1).bit_length())

    def idx_bit(b, j):
        cand = j | lax.shift_left(jnp.int32(1), nbits - 1 - b)
        return jnp.where(count(tie_idx < cand) < need, cand, j)

    j = lax.fori_loop(0, nbits, idx_bit, jnp.zeros((ne, 1), jnp.int32))
    sel = above | (tie_idx <= j)
    o_ref[...] = jnp.where(sel, a, -1.0)


def _select(aff, cap):
    ne, n = aff.shape
    return pl.pallas_call(
        functools.partial(_select_kernel, cap=cap),
        out_shape=jax.ShapeDtypeStruct((ne, n), F32),
        compiler_params=pltpu.CompilerParams(vmem_limit_bytes=VMEM_LIMIT),
        name="ec_select",
    )(aff)


def _routing_tables(gs, group_tokens):
    ne = gs.shape[0]
    a_l, off_l, cnt_l, lim_l = [], [], [], []
    tok0, slot0 = 0, 0
    for n in group_tokens:
        cap = EC_FACTOR * n // N_EXPERTS
        nt = n // MOE_TM
        sel = lax.slice_in_dim(gs, tok0, tok0 + n, axis=1) >= 0
        counts = jnp.sum(sel.reshape(ne, nt, MOE_TM), axis=-1, dtype=jnp.int32)
        s0 = slot0 + jnp.cumsum(counts, axis=1) - counts
        a = (s0 // ROW_ALIGN) * ROW_ALIGN
        a_l.append(a)
        off_l.append(s0 - a)
        cnt_l.append(counts)
        lim_l.append(jnp.full((nt,), slot0 + cap - MOE_WIN, jnp.int32))
        tok0 += n
        slot0 += cap
    a = jnp.concatenate(a_l, axis=1).T
    off = jnp.concatenate(off_l, axis=1).T
    end = off + jnp.concatenate(cnt_l, axis=1).T
    rounds = (end + MOE_WIN - 1) // MOE_WIN
    gp = (end // ROW_ALIGN) * ROW_ALIGN
    return dict(a=a.reshape(-1).astype(jnp.int32), nr=rounds.reshape(-1).astype(jnp.int32),
                nrounds=jnp.max(rounds, axis=1).astype(jnp.int32), lim=jnp.concatenate(lim_l),
                off_col=off.astype(F32)[:, :, None], off_row=off.astype(F32)[:, None, :],
                off16=jnp.repeat(off.astype(F32), ROW_ALIGN, axis=1)[:, :, None],
                gp16=jnp.repeat(gp.astype(F32), ROW_ALIGN, axis=1)[:, :, None],
                total=slot0)


def _dispatch_kernel(a_tab, nrounds, nr_tab, gs_ref, off_ref, off16_ref, gp16_ref, x_ref, u_ref, xe_ref,
                     pos_ref, c_ref, c2_ref, stage_ref, carry_ref, sem, rc_ref):
    j = pl.program_id(0)
    ne, tm = gs_ref.shape
    d = x_ref.shape[1]
    win = MOE_WIN

    @pl.when(j == 0)
    def _():
        carry_ref[...] = jnp.zeros_like(carry_ref)
        rc_ref[0] = 0

    sel = gs_ref[...] >= 0.0
    rank = jnp.dot(jnp.where(sel, 1.0, 0.0).astype(BF16), u_ref[...], preferred_element_type=F32)
    pos_ref[...] = jnp.where(sel, rank + off_ref[...], -1.0)

    def wait_round(slot):
        def one(_, c):
            pltpu.make_async_copy(stage_ref.at[slot, pl.ds(0, win)], xe_ref.at[0, pl.ds(0, win)],
                                  sem.at[slot]).wait()
            return c
        lax.fori_loop(0, rc_ref[1 + slot], one, 0)

    def round_body(r, _):
        k = lax.broadcasted_iota(jnp.int32, (win, tm), 0).astype(F32) + (r * win).astype(F32)
        for e in range(ne):
            c_ref[e * win:(e + 1) * win, :] = jnp.where(pos_ref[e:e + 1, :] == k, 1.0, 0.0).astype(BF16)
        slot = rc_ref[0] % 2
        for nb in range(d // MXU_DIM):
            cols = slice(nb * MXU_DIM, (nb + 1) * MXU_DIM)
            stage_ref[slot, :, cols] = jnp.dot(c_ref[...], x_ref[:, cols],
                                               preferred_element_type=F32).astype(BF16)

        @pl.when(r == 0)
        def _():
            k16 = lax.broadcasted_iota(jnp.int32, (ROW_ALIGN, 1), 0).astype(F32)
            for e in range(ne):
                keep = k16 < off16_ref[e * ROW_ALIGN:(e + 1) * ROW_ALIGN, :]
                rows = pl.ds(e * win, ROW_ALIGN)
                stage_ref[slot, rows, :] = jnp.where(
                    keep, carry_ref[e * ROW_ALIGN:(e + 1) * ROW_ALIGN, :], stage_ref[slot, rows, :])

        @pl.when(rc_ref[0] > 0)
        def _():
            wait_round(1 - slot)

        rc_ref[1 + slot] = 0
        for e in range(ne):
            @pl.when(r < nr_tab[j * ne + e])
            def _():
                dst = pl.multiple_of(a_tab[j * ne + e] + r * win, ROW_ALIGN)
                pltpu.make_async_copy(stage_ref.at[slot, pl.ds(e * win, win)],
                                      xe_ref.at[e, pl.ds(dst, win)], sem.at[slot]).start()
                rc_ref[1 + slot] = rc_ref[1 + slot] + 1
        rc_ref[0] = rc_ref[0] + 1
        return 0

    lax.fori_loop(0, nrounds[j], round_body, 0)

    k16 = lax.broadcasted_iota(jnp.int32, (ROW_ALIGN, 1), 0).astype(F32)
    for e in range(ne):
        rows = slice(e * ROW_ALIGN, (e + 1) * ROW_ALIGN)
        c2_ref[rows, :] = jnp.where(pos_ref[e:e + 1, :] == gp16_ref[rows, :] + k16, 1.0, 0.0).astype(BF16)
    kk = jnp.concatenate([k16] * ne, axis=0)
    keep_old = (gp16_ref[...] == 0.0) & (kk < off16_ref[...])
    for nb in range(d // MXU_DIM):
        cols = slice(nb * MXU_DIM, (nb + 1) * MXU_DIM)
        new = jnp.dot(c2_ref[...], x_ref[:, cols], preferred_element_type=F32).astype(BF16)
        carry_ref[:, cols] = jnp.where(keep_old, carry_ref[:, cols], new)

    @pl.when((j == pl.num_programs(0) - 1) & (rc_ref[0] > 0))
    def _():
        wait_round((rc_ref[0] - 1) % 2)


def _dispatch(gs, h2, tabs, u_mat):
    ne, ntok = gs.shape
    d = h2.shape[1]
    nt = ntok // MOE_TM
    rows = tabs['total'] + MOE_WIN
    grid_spec = pltpu.PrefetchScalarGridSpec(
        num_scalar_prefetch=3,
        grid=(nt,),
        in_specs=[pl.BlockSpec((ne, MOE_TM), lambda j, *_: (0, j)),
                  pl.BlockSpec((None, ne, 1), lambda j, *_: (j, 0, 0)),
                  pl.BlockSpec((None, ne * ROW_ALIGN, 1), lambda j, *_: (j, 0, 0)),
                  pl.BlockSpec((None, ne * ROW_ALIGN, 1), lambda j, *_: (j, 0, 0)),
                  pl.BlockSpec((MOE_TM, d), lambda j, *_: (j, 0)),
                  pl.BlockSpec((MOE_TM, MOE_TM), lambda j, *_: (0, 0))],
        out_specs=pl.BlockSpec(memory_space=pl.ANY),
        scratch_shapes=[pltpu.VMEM((ne, MOE_TM), F32),
                        pltpu.VMEM((ne * MOE_WIN, MOE_TM), BF16),
                        pltpu.VMEM((ne * ROW_ALIGN, MOE_TM), BF16),
                        pltpu.VMEM((2, ne * MOE_WIN, d), BF16),
                        pltpu.VMEM((ne * ROW_ALIGN, d), BF16),
                        pltpu.SemaphoreType.DMA((2,)),
                        pltpu.SMEM((3,), jnp.int32)])
    return pl.pallas_call(
        _dispatch_kernel,
        grid_spec=grid_spec,
        out_shape=jax.ShapeDtypeStruct((ne, rows, d), BF16),
        compiler_params=_cparams(("arbitrary",)),
        name="ec_dispatch",
    )(tabs['a'], tabs['nrounds'], tabs['nr'], gs, tabs['off_col'], tabs['off16'], tabs['gp16'], h2,
      u_mat)


def _combine_kernel(a_tab, nrounds, lim_tab, gs_ref, off_ref, l_ref, ye_ref, x1_ref, g2_ref, lg_ref,
                    lb_ref, *refs, split_tiles):
    outs, (p_ref, y_ref, acc_ref, sem) = refs[:-4], refs[-4:]
    j = pl.program_id(0)
    tm, ne = gs_ref.shape
    d = x1_ref.shape[1]
    win = MOE_WIN
    gs = gs_ref[...]
    sel = gs >= 0.0
    rank = jnp.dot(l_ref[...], jnp.where(sel, 1.0, 0.0).astype(BF16), preferred_element_type=F32)
    pos = jnp.where(sel, rank + off_ref[...], -1.0)
    gate = jnp.where(sel, gs, 0.0)
    acc_ref[...] = jnp.zeros_like(acc_ref)
    n_tiles = pl.num_programs(0)
    slot = j % 2

    def window(tile, r, e):
        want = a_tab[tile * ne + e] + r * win
        src = pl.multiple_of(jnp.minimum(want, lim_tab[tile]), ROW_ALIGN)
        return src, want - src

    def start_round(tile, r, sl):
        for e in range(ne):
            src, _ = window(tile, r, e)
            pltpu.make_async_copy(ye_ref.at[e, pl.ds(src, win)], y_ref.at[sl, pl.ds(e * win, win)],
                                  sem.at[sl]).start()

    def wait_round(sl):
        for e in range(ne):
            pltpu.make_async_copy(ye_ref.at[e, pl.ds(0, win)], y_ref.at[sl, pl.ds(e * win, win)],
                                  sem.at[sl]).wait()

    @pl.when((j == 0) & (nrounds[0] > 0))
    def _():
        start_round(0, 0, 0)

    nxt = jnp.minimum(j + 1, n_tiles - 1)

    @pl.when((j + 1 < n_tiles) & (nrounds[nxt] > 0))
    def _():
        start_round(nxt, 0, 1 - slot)

    half = ne // 2

    def accumulate(r):
        base = lax.convert_element_type(r * win, F32)
        k = lax.broadcasted_iota(jnp.int32, (tm, win), 1).astype(F32) + base
        for grp in range(2):
            for e in range(grp * half, (grp + 1) * half):
                pe = pos[:, e:e + 1]
                pe = jnp.where(pe >= base, pe + window(j, r, e)[1].astype(F32), -1.0)
                pcol = jnp.broadcast_to(pe, (tm, win))
                gcol = jnp.broadcast_to(gate[:, e:e + 1], (tm, win))
                p_ref[:, e * win:(e + 1) * win] = jnp.where(pcol == k, gcol, 0.0).astype(BF16)
            rows = slice(grp * half * win, (grp + 1) * half * win)
            for nb in range(d // MXU_DIM):
                cols = slice(nb * MXU_DIM, (nb + 1) * MXU_DIM)
                acc_ref[:, cols] += jnp.dot(p_ref[:, rows], y_ref[slot, rows, cols],
                                            preferred_element_type=F32)

    @pl.when(nrounds[j] > 0)
    def _():
        wait_round(slot)
        accumulate(0)

    def extra_round(r, _):
        start_round(j, r, slot)
        wait_round(slot)
        accumulate(r)
        return 0

    lax.fori_loop(1, nrounds[j], extra_round, 0)
    y = DEEPNORM_ALPHA * x1_ref[...] + (1.0 + g2_ref[...]) * acc_ref[...]
    res = _layer_norm(y, lg_ref[...], lb_ref[...])
    if split_tiles is None:
        outs[0][...] = res
    else:
        @pl.when(j < split_tiles)
        def _():
            outs[0][...] = res

        @pl.when(j >= split_tiles)
        def _():
            outs[1][...] = res


def _combine_postnorm(gs_tok, ye, tabs, l_mat, x1, g2, ln_g, ln_b, t, split_tiles=None):
    ntok, ne = gs_tok.shape
    d = x1.shape[1]
    nt = ntok // MOE_TM
    tiles_per_seq = t // MOE_TM
    if split_tiles is None:
        out_specs = pl.BlockSpec((MOE_TM, d), lambda j, *_: (j, 0))
        out_shape = jax.ShapeDtypeStruct((ntok, d), F32)
    else:
        out_specs = [pl.BlockSpec((MOE_TM, d), lambda j, *_: (jnp.minimum(j, split_tiles - 1), 0)),
                     pl.BlockSpec((MOE_TM, d), lambda j, *_: (jnp.maximum(j - split_tiles, 0), 0))]
        out_shape = [jax.ShapeDtypeStruct((split_tiles * MOE_TM, d), F32),
                     jax.ShapeDtypeStruct((ntok - split_tiles * MOE_TM, d), F32)]
    grid_spec = pltpu.PrefetchScalarGridSpec(
        num_scalar_prefetch=3,
        grid=(nt,),
        in_specs=[pl.BlockSpec((MOE_TM, ne), lambda j, *_: (j, 0)),
                  pl.BlockSpec((None, 1, ne), lambda j, *_: (j, 0, 0)),
                  pl.BlockSpec((MOE_TM, MOE_TM), lambda j, *_: (0, 0)),
                  pl.BlockSpec(memory_space=pl.ANY),
                  pl.BlockSpec((MOE_TM, d), lambda j, *_: (j, 0)),
                  pl.BlockSpec((None, 1, d), lambda j, *_: (j // tiles_per_seq, 0, 0)),
                  pl.BlockSpec((1, d), lambda j, *_: (0, 0)),
                  pl.BlockSpec((1, d), lambda j, *_: (0, 0))],
        out_specs=out_specs,
        scratch_shapes=[pltpu.VMEM((MOE_TM, ne * MOE_WIN), BF16),
                        pltpu.VMEM((2, ne * MOE_WIN, d), BF16),
                        pltpu.VMEM((MOE_TM, d), F32),
                        pltpu.SemaphoreType.DMA((2,))])
    return pl.pallas_call(
        functools.partial(_combine_kernel, split_tiles=split_tiles),
        grid_spec=grid_spec,
        out_shape=out_shape,
        compiler_params=_cparams(("arbitrary",)),
        name="ec_combine",
    )(tabs['a'], tabs['nrounds'], tabs['lim'], gs_tok, tabs['off_row'], l_mat, ye, x1, g2, ln_g, ln_b)


def _prep_w_in(w_in_l):
    sizes = (256, 256, 256, MLA_Q_RANK, MLA_KV_RANK, MLA_ROPE, 256, 256, 256, 256, 256, 256)
    offs = np.concatenate([[0], np.cumsum(sizes)])
    part = [w_in_l[:, offs[i]:offs[i + 1]] for i in range(len(sizes))]
    a_q, a_k, a_v, b_cq, b_ckv, b_kr, c_q, c_k, c_v, d_q, d_k, d_v = part
    d = w_in_l.shape[0]
    zeros = lambda n: jnp.zeros((d, n), w_in_l.dtype)
    def per_head(v):
        out = []
        for h in range(4):
            out += [v[:, h * 64:(h + 1) * 64], zeros(LANES - 64)]
        return out

    cols = ([a_q, a_k] + per_head(a_v) + [c_q, c_k] + per_head(c_v)
            + [d_q, d_k, d_v, b_cq, zeros(256 - MLA_Q_RANK), b_ckv, b_kr, b_kr,
               zeros(128 - 2 * MLA_ROPE)])
    return jnp.concatenate(cols, axis=1).astype(BF16)


def _prep_w_uq(w_uq_l):
    hd = MLA_NOPE + MLA_ROPE
    nope = [w_uq_l[:, h * hd:h * hd + MLA_NOPE] for h in range(MLA_HEADS)]
    rope = [w_uq_l[:, h * hd + MLA_NOPE:(h + 1) * hd] for h in range(MLA_HEADS)]
    z = jnp.zeros((w_uq_l.shape[0], 256 - 2 * hd), w_uq_l.dtype)
    cols = []
    for p in range(2):
        cols += [nope[2 * p], nope[2 * p + 1], rope[2 * p], rope[2 * p + 1], z]
    w = jnp.concatenate(cols, axis=1)
    w = jnp.concatenate([w, jnp.zeros((256 - MLA_Q_RANK, w.shape[1]), w.dtype)], axis=0)
    return w.astype(BF16)


def _prep_w_ukv(w_ukv_l):
    hd = MLA_NOPE + MLA_DV
    kn = [w_ukv_l[:, h * hd:h * hd + MLA_NOPE] for h in range(MLA_HEADS)]
    z = jnp.zeros((w_ukv_l.shape[0], LANES - MLA_DV), w_ukv_l.dtype)
    vv = []
    for h in range(MLA_HEADS):
        vv += [w_ukv_l[:, h * hd + MLA_NOPE:(h + 1) * hd], z]
    return jnp.concatenate(kn + vv, axis=1).astype(BF16)


def _select_groups(aff_t, group_tokens):
    parts, off = [], 0
    for n in group_tokens:
        parts.append(_select(lax.slice_in_dim(aff_t, off, off + n, axis=1),
                             EC_FACTOR * n // N_EXPERTS))
        off += n
    return jnp.concatenate(parts, axis=1)


def _trunk(x, c, group_tokens, nb, t, p):
    ntok = nb * t
    tabs = (_rope_tables(t, 256, DA_DQK, DA_DQK // ROPE_FRACTION, ROPE_THETA)
            + _rope_tables(t, 256, DIL_DH, DIL_DH // ROPE_FRACTION, ROPE_THETA)
            + _rope_tables(t, 128, MLA_ROPE, MLA_ROPE, MLA_ROPE_THETA))
    gmat = jnp.asarray(np.kron(np.eye(4), np.full((64, 64), 1.0 / 64)), BF16)
    ti = jnp.arange(MOE_TM)
    u_mat = (ti[:, None] < ti[None, :]).astype(BF16)
    l_mat = (ti[None, :] < ti[:, None]).astype(BF16)
    for l in range(DEPTH):
        mod = _modulation(c, p['w_ada'][l], p['b_ada'][l])
        sh1, sc1, g1, sh2, sc2, g2 = [m.reshape(nb, 1, D_MODEL) for m in jnp.split(mod, 6, axis=-1)]
        gq = jnp.concatenate([p['q_norm_g'][l], jnp.zeros((256 - MLA_Q_RANK,), F32)]).reshape(1, 256)
        gkv = p['kv_norm_g'][l].reshape(1, 128)
        proj, projv = _inproj(x, sc1, sh1, _prep_w_in(p['w_in'][l]), _prep_w_uq(p['w_uq'][l]),
                              _prep_w_ukv(p['w_ukv'][l]), gq, gkv, tabs, nb, t)
        lam_init = 0.8 - 0.6 * math.exp(-0.3 * l)
        lam = (jnp.exp(jnp.sum(p['da_lq1'][l] * p['da_lk1'][l]))
               - jnp.exp(jnp.sum(p['da_lq2'][l] * p['da_lk2'][l])) + lam_init).reshape(1)
        g_sub = jnp.tile(p['da_subln_g'][l], DA_HEADS).reshape(1, 256)
        oa = _diff_attention(proj, projv, lam, g_sub, gmat, nb, t, 1.0 - lam_init)
        ob = _mla_attention(proj, projv, nb, t)
        oc = _dil_attention(proj, projv, nb, t)
        od = _na_attention(proj, _na_bias_table(p['na_rpb'][l]), nb, t)
        x1, h2, aff_t = _outproj(oa, ob, oc, od, x, g1, sc2, sh2, p['w_out'][l].astype(BF16),
                                 p['ln1_g'][l].reshape(1, -1), p['ln1_b'][l].reshape(1, -1),
                                 p['w_router'][l].T, nb, t)
        wg = p['w_e_gate'][l].astype(BF16)
        wu = p['w_e_up'][l].astype(BF16)
        wd = p['w_e_down'][l].astype(BF16)
        gs = _select_groups(aff_t, group_tokens)
        rt = _routing_tables(gs, group_tokens)
        xe = _dispatch(gs, h2, rt, u_mat)
        ye = _experts(xe, rt['total'], wg, wu, wd)
        split = group_tokens[0] // MOE_TM if l == DEPTH - 1 else None
        x = _combine_postnorm(gs.T, ye, rt, l_mat, x1, g2, p['ln2_g'][l].reshape(1, -1),
                              p['ln2_b'][l].reshape(1, -1), t, split_tiles=split)
    return x


def kernel(x_prompt, x_sample, c_prompt, c_sample, w_in, w_uq, w_ukv, q_norm_g, kv_norm_g, da_lq1,
           da_lk1, da_lq2, da_lk2, da_subln_g, na_rpb, w_out, w_ada, b_ada, ln1_g, ln1_b, ln2_g,
           ln2_b, w_router, w_e_gate, w_e_up, w_e_down):
    p = dict(w_in=w_in, w_uq=w_uq, w_ukv=w_ukv, q_norm_g=q_norm_g, kv_norm_g=kv_norm_g,
             da_lq1=da_lq1, da_lk1=da_lk1, da_lq2=da_lq2, da_lk2=da_lk2, da_subln_g=da_subln_g,
             na_rpb=na_rpb, w_out=w_out, w_ada=w_ada, b_ada=b_ada, ln1_g=ln1_g, ln1_b=ln1_b,
             ln2_g=ln2_g, ln2_b=ln2_b, w_router=w_router, w_e_gate=w_e_gate, w_e_up=w_e_up,
             w_e_down=w_e_down)
    bp, t, d = x_prompt.shape
    bs = x_sample.shape[0]
    assert x_sample.shape[1] == t
    nb = bp + bs
    x = jnp.concatenate([x_prompt.reshape(bp * t, d), x_sample.reshape(bs * t, d)], axis=0)
    c = jnp.concatenate([c_prompt, c_sample], axis=0)
    y_prompt, y_sample = _trunk(x, c, (bp * t, bs * t), nb, t, p)
    return y_prompt.reshape(bp, t, d), y_sample.reshape(bs, t, d)
```

```python
import functools
import math

import jax
import jax.numpy as jnp
import numpy as np
from jax import lax
from jax.experimental import pallas as pl
from jax.experimental.pallas import tpu as pltpu

F32 = jnp.float32
BF16 = jnp.bfloat16

D_MODEL = 1024
DEPTH = 2
GRID_W = 64
GROUP_W = 256
DA_HEADS, DA_DV, DA_DQK = 4, 64, 32
MLA_HEADS, MLA_Q_RANK, MLA_KV_RANK, MLA_NOPE, MLA_ROPE, MLA_DV = 4, 192, 128, 64, 32, 64
MLA_ROPE_THETA = 10000.0
DIL_HEADS, DIL_DH = 4, 64
DIL_PATTERNS = ((128, 1), (512, 4), (2048, 16))
NA_HEADS, NA_DH, NA_KR, NA_KC = 4, 64, 8, 16
ROPE_THETA = 500000.0
ROPE_FRACTION = 4
N_EXPERTS = 16
EC_FACTOR = 2
D_FF = 2816
DEEPNORM_ALPHA = (2.0 * DEPTH) ** 0.25
NEG_INF = -1e30
LOG2E = math.log2(math.e)
LN_EPS = 1e-5
RMS_EPS = 1e-6

LANES = 128
MXU_DIM = 256
VMEM_LIMIT = 56 * 1024 * 1024

S_AQ, S_AK, S_BQ0, S_BQ1, S_BK0, S_BK1, S_CQ, S_CK, S_DQ, S_DK, S_DV = range(11)
N_SLOTS = 11
V_A, V_B, V_C = 0, 4, 8
N_VSLOTS = 12
V_ONE_LANE = 64
W_ALL_COLS = 13 * 256


def _cparams(sem):
    return pltpu.CompilerParams(dimension_semantics=sem, vmem_limit_bytes=VMEM_LIMIT)


def _split_bf16(a):
    hi = a.astype(BF16)
    lo = (a - hi.astype(F32)).astype(BF16)
    return hi, lo


def _mod_kernel(c_ref, w_ref, b_ref, o_ref):
    c = c_ref[...]
    a = c * (1.0 / (1.0 + jnp.exp(-c)))
    a_hi, a_lo = _split_bf16(a)
    w_hi, w_lo = _split_bf16(w_ref[...])
    acc = jnp.dot(a_hi, w_hi, preferred_element_type=F32)
    acc += jnp.dot(a_hi, w_lo, preferred_element_type=F32)
    acc += jnp.dot(a_lo, w_hi, preferred_element_type=F32)
    o_ref[...] = acc + b_ref[...]


def _modulation(c, w_ada, b_ada):
    nb, d = c.shape
    n_out = w_ada.shape[1]
    tn = 1536
    return pl.pallas_call(
        _mod_kernel,
        grid=(n_out // tn,),
        in_specs=[pl.BlockSpec((nb, d), lambda j: (0, 0)),
                  pl.BlockSpec((d, tn), lambda j: (0, j)),
                  pl.BlockSpec((1, tn), lambda j: (0, j))],
        out_specs=pl.BlockSpec((nb, tn), lambda j: (0, j)),
        out_shape=jax.ShapeDtypeStruct((nb, n_out), F32),
        compiler_params=_cparams(("arbitrary",)),
        name="adaln_mod",
    )(c, w_ada, b_ada.reshape(1, n_out))


def _rope_tables(t, width, group, rot, theta):
    half = rot // 2
    inv = theta ** (-jnp.arange(half, dtype=F32) / half)
    ang = jnp.arange(t, dtype=F32)[:, None] * inv[None, :]
    cos, sin = jnp.cos(ang), jnp.sin(ang)
    ones = jnp.ones((t, group - rot), F32)
    zeros = jnp.zeros((t, group - rot), F32)
    c_g = jnp.concatenate([cos, cos, ones], axis=1)
    s_g = jnp.concatenate([-sin, sin, zeros], axis=1)
    reps = width // group
    return jnp.tile(c_g, (1, reps)), jnp.tile(s_g, (1, reps))


def _apply_rope(x, c_tab, s_tab, group, rot):
    width = x.shape[-1]
    half = rot // 2
    lane = lax.broadcasted_iota(jnp.int32, (1, width), 1)
    first = (lane % group) < half
    fwd = pltpu.roll(x, width - half, 1)
    bwd = pltpu.roll(x, half, 1)
    return x * c_tab + s_tab * jnp.where(first, fwd, bwd)


def _inproj_kernel(x_ref, sc_ref, sh_ref, w_ref, wuq_ref, wukv_ref, gq_ref, gkv_ref,
                   ca_ref, sa_ref, cc_ref, scc_ref, cm_ref, sm_ref, o_ref, ov_ref):
    h = (x_ref[...] * (1.0 + sc_ref[...]) + sh_ref[...]).astype(BF16)

    def proj(col, width):
        return jnp.dot(h, w_ref[:, col:col + width], preferred_element_type=F32)

    one_lane = lax.broadcasted_iota(jnp.int32, (1, LANES), 1) == V_ONE_LANE

    def put_values(first_slot, vals):
        for hh in range(4):
            v = vals[:, LANES * hh:LANES * (hh + 1)]
            ov_ref[first_slot + hh] = jnp.where(one_lane, 1.0, v).astype(BF16)

    sa_scale = DA_DQK ** -0.5 * LOG2E
    sb_scale = (MLA_NOPE + MLA_ROPE) ** -0.5 * LOG2E
    sc_scale = DIL_DH ** -0.5 * LOG2E
    sd_scale = NA_DH ** -0.5 * LOG2E
    a_rot = DA_DQK // ROPE_FRACTION
    c_rot = DIL_DH // ROPE_FRACTION

    ca, sa = ca_ref[...], sa_ref[...]
    o_ref[S_AQ] = (_apply_rope(proj(0, 256), ca, sa, DA_DQK, a_rot) * sa_scale).astype(BF16)
    o_ref[S_AK] = _apply_rope(proj(256, 256), ca, sa, DA_DQK, a_rot).astype(BF16)
    put_values(V_A, proj(512, 512))
    cc, scc = cc_ref[...], scc_ref[...]
    o_ref[S_CQ] = (_apply_rope(proj(1024, 256), cc, scc, DIL_DH, c_rot) * sc_scale).astype(BF16)
    o_ref[S_CK] = _apply_rope(proj(1280, 256), cc, scc, DIL_DH, c_rot).astype(BF16)
    put_values(V_C, proj(1536, 512))
    o_ref[S_DQ] = (proj(2048, 256) * sd_scale).astype(BF16)
    o_ref[S_DK] = proj(2304, 256).astype(BF16)
    o_ref[S_DV] = proj(2560, 256).astype(BF16)

    cm, sm = cm_ref[...], sm_ref[...]
    cq = proj(2816, 256)
    cq = cq * lax.rsqrt(jnp.sum(cq * cq, -1, keepdims=True) * (1.0 / MLA_Q_RANK) + RMS_EPS)
    cq = (cq * gq_ref[...]).astype(BF16)
    q2 = jnp.dot(cq, wuq_ref[...], preferred_element_type=F32)
    for p in range(2):
        qp = q2[:, 256 * p:256 * (p + 1)]
        o_ref[S_BQ0 + p, :, 0:128] = (qp[:, 0:128] * sb_scale).astype(BF16)
        o_ref[S_BQ0 + p, :, 128:256] = (
            _apply_rope(qp[:, 128:256], cm, sm, MLA_ROPE, MLA_ROPE) * sb_scale).astype(BF16)
    ckv = proj(3072, 128)
    ckv = ckv * lax.rsqrt(jnp.mean(ckv * ckv, -1, keepdims=True) + RMS_EPS)
    ckv = (ckv * gkv_ref[...]).astype(BF16)
    kv = jnp.dot(ckv, wukv_ref[...], preferred_element_type=F32)
    kr = _apply_rope(proj(3200, 128), cm, sm, MLA_ROPE, MLA_ROPE).astype(BF16)
    for p in range(2):
        o_ref[S_BK0 + p, :, 0:128] = kv[:, 128 * p:128 * (p + 1)].astype(BF16)
        o_ref[S_BK0 + p, :, 128:256] = kr
    put_values(V_B, kv[:, 256:768])


def _inproj(x, sc, sh, w_all, wuq, wukv, gq, gkv, tabs, nb, t):
    tm = 512
    nt = t // tm
    ntok = nb * t
    ca, sa, cc, scc, cm, sm = tabs
    full = lambda shape: pl.BlockSpec(shape, lambda j, b: tuple(0 for _ in shape))
    tab = lambda w: pl.BlockSpec((tm, w), lambda j, b: (j, 0))
    return pl.pallas_call(
        _inproj_kernel,
        grid=(nt, nb),
        in_specs=[pl.BlockSpec((tm, D_MODEL), lambda j, b: (b * nt + j, 0)),
                  pl.BlockSpec((None, 1, D_MODEL), lambda j, b: (b, 0, 0)),
                  pl.BlockSpec((None, 1, D_MODEL), lambda j, b: (b, 0, 0)),
                  full((D_MODEL, W_ALL_COLS)), full((256, 512)), full((128, 768)),
                  full((1, 256)), full((1, 128)),
                  tab(256), tab(256), tab(256), tab(256), tab(128), tab(128)],
        out_specs=[pl.BlockSpec((N_SLOTS, tm, 256), lambda j, b: (0, b * nt + j, 0)),
                   pl.BlockSpec((N_VSLOTS, tm, LANES), lambda j, b: (0, b * nt + j, 0))],
        out_shape=[jax.ShapeDtypeStruct((N_SLOTS, ntok, 256), BF16),
                   jax.ShapeDtypeStruct((N_VSLOTS, ntok, LANES), BF16)],
        compiler_params=_cparams(("arbitrary", "arbitrary")),
        name="in_proj",
    )(x, sc, sh, w_all, wuq, wukv, gq, gkv, ca, sa, cc, scc, cm, sm)


def _lane_mask(width, ranges):
    lane = lax.broadcasted_iota(jnp.int32, (1, width), 1)
    m = None
    for lo, hi in ranges:
        r = (lane >= lo) & (lane < hi)
        m = r if m is None else (m | r)
    return m


SOFTMAX_SLAB = 32


def _chain_scratch(tq, kc):
    return [pltpu.VMEM((tq, kc), F32), pltpu.VMEM((tq, kc), F32), pltpu.VMEM((tq, kc), BF16),
            pltpu.VMEM((tq, LANES), F32), pltpu.VMEM((tq, LANES), F32), pltpu.VMEM((tq, LANES), F32)]


CHAIN_REFS = 6


def _attention_steps(n_steps, qms_for, k_for, v_for, finish, tq, t, kc, chains, slab=SOFTMAX_SLAB,
                     bias_ref=None):
    n_chunks = t // kc
    assert n_chunks % 2 == 0
    nt = (((1,), (1,)), ((), ()))

    def chunk_rows(chunk):
        if isinstance(chunk, int):
            return pl.ds(chunk * kc, kc)
        return pl.ds(pl.multiple_of(chunk * kc, kc), kc)

    def scores(step, chunk, slot):
        k = k_for(step)[chunk_rows(chunk), :]
        for qm, chain in zip(qms_for(step), chains):
            chain[slot][...] = lax.dot_general(qm, k, nt, preferred_element_type=F32)

    def softmax_pv(step, chunk, slot):
        for v_ref, chain in zip(v_for(step), chains):
            v = v_ref[chunk_rows(chunk), :]
            s_ref = chain[slot]
            (p_ref, m_ref, a_ref, acc_ref) = chain[2:]
            for r in range(tq // slab):
                rows = slice(r * slab, (r + 1) * slab)
                s = s_ref[rows, :]
                if bias_ref is not None:
                    s = s + bias_ref[rows, chunk * kc:(chunk + 1) * kc]
                m_prev = m_ref[rows, :]
                m_new = jnp.maximum(m_prev, jnp.max(s, -1, keepdims=True))
                d = s - jnp.concatenate([m_new] * (kc // LANES), axis=1)
                p_ref[rows, :] = jnp.exp2(d.astype(BF16))
                a_ref[rows, :] = jnp.exp2(m_prev - m_new)
                m_ref[rows, :] = m_new
            acc_ref[...] = a_ref[...] * acc_ref[...] + jnp.dot(p_ref[...], v,
                                                              preferred_element_type=F32)

    def step_body(st, _):
        for (_, _, _, m_ref, _, acc_ref) in chains:
            m_ref[...] = jnp.full(m_ref.shape, NEG_INF, F32)
            acc_ref[...] = jnp.zeros(acc_ref.shape, F32)
        nxt = jnp.minimum(st + 1, n_steps - 1)

        def body(j, _):
            scores(st, 2 * j + 1, 1)
            softmax_pv(st, 2 * j, 0)
            if n_chunks == 2:
                scores(nxt, 0, 0)
            else:
                last = 2 * j + 2 == n_chunks
                scores(jnp.where(last, nxt, st), jnp.where(last, 0, 2 * j + 2), 0)
            softmax_pv(st, 2 * j + 1, 1)
            return 0

        if n_chunks == 2:
            body(0, 0)
        else:
            assert bias_ref is None
            lax.fori_loop(0, n_chunks // 2, body, 0, unroll=4)
        outs = []
        for chain in chains:
            acc = chain[5][...]
            den = jnp.broadcast_to(acc[:, V_ONE_LANE:V_ONE_LANE + 1], acc.shape)
            outs.append(acc * (1.0 / den))
        finish(st, outs)
        return 0

    scores(0, 0, 0)
    lax.fori_loop(0, n_steps, step_body, 0)


def _place_head(pair_ref, o, j):
    lane = lax.broadcasted_iota(jnp.int32, (1, LANES), 1)
    low = lane < V_ONE_LANE
    shifted = pltpu.roll(o, V_ONE_LANE, 1)
    cur = pair_ref[...]
    pair_ref[...] = jnp.where(low, jnp.where(j == 0, o, cur), jnp.where(j == 1, shifted, cur))


def _group_mean_sq(x, gmat):
    sq = x * x
    hi, lo = _split_bf16(sq)
    return (jnp.dot(hi, gmat, preferred_element_type=F32)
            + jnp.dot(lo, gmat, preferred_element_type=F32))


def _diff_attn_kernel(lam_ref, q_ref, k_ref, v_ref, g_ref, gmat_ref, o_ref, acc_ref, *scratch, t, kc,
                      out_scale):
    chains = (scratch[:CHAIN_REFS], scratch[CHAIN_REFS:])
    lam = lam_ref[0]
    lane = lax.broadcasted_iota(jnp.int32, (1, 256), 1)

    def qms_for(h):
        q = q_ref[...]
        return [jnp.where((lane >= (2 * h + c) * DA_DQK) & (lane < (2 * h + c + 1) * DA_DQK), q,
                          jnp.zeros_like(q)) for c in range(2)]

    def finish(h, outs):
        _place_head(acc_ref.at[h // 2], outs[0] - lam * outs[1], h % 2)

    acc_ref[...] = jnp.zeros_like(acc_ref)
    _attention_steps(DA_HEADS, qms_for, lambda h: k_ref, lambda h: [v_ref.at[h], v_ref.at[h]],
                     finish, q_ref.shape[0], t, kc, chains)
    o = jnp.concatenate([acc_ref[0], acc_ref[1]], axis=1)
    ms = _group_mean_sq(o, gmat_ref[...])
    o_ref[...] = (o * lax.rsqrt(ms + RMS_EPS) * g_ref[...] * out_scale).astype(o_ref.dtype)


def _diff_attention(proj, projv, lam, g_tiled, gmat, nb, t, out_scale):
    tq, kc = 512, 512
    nq = t // tq
    kern = functools.partial(_diff_attn_kernel, t=t, kc=kc, out_scale=out_scale)
    return pl.pallas_call(
        kern,
        grid=(nb, nq),
        in_specs=[pl.BlockSpec(memory_space=pltpu.SMEM),
                  pl.BlockSpec((None, tq, 256), lambda b, i: (S_AQ, b * nq + i, 0)),
                  pl.BlockSpec((None, t, 256), lambda b, i: (S_AK, b, 0)),
                  pl.BlockSpec((DA_HEADS, t, LANES), lambda b, i: (V_A // DA_HEADS, b, 0)),
                  pl.BlockSpec((1, 256), lambda b, i: (0, 0)),
                  pl.BlockSpec((256, 256), lambda b, i: (0, 0))],
        out_specs=pl.BlockSpec((tq, 256), lambda b, i: (b * nq + i, 0)),
        out_shape=jax.ShapeDtypeStruct((nb * t, 256), BF16),
        scratch_shapes=[pltpu.VMEM((2, tq, LANES), F32)] + 2 * _chain_scratch(tq, kc),
        compiler_params=_cparams(("arbitrary", "arbitrary")),
        name="diff_attn",
    )(lam, proj, proj, projv, g_tiled, gmat)


def _mla_attn_kernel(q_ref, k_ref, v_ref, o_ref, acc_ref, *scratch, t, kc):
    chains = (scratch[:CHAIN_REFS], scratch[CHAIN_REFS:])
    lane = lax.broadcasted_iota(jnp.int32, (1, 256), 1)

    def qms_for(p):
        q = q_ref[p]
        qms = []
        for j in range(2):
            nope = (lane >= j * MLA_NOPE) & (lane < (j + 1) * MLA_NOPE)
            rope = (lane >= 128 + j * MLA_ROPE) & (lane < 128 + (j + 1) * MLA_ROPE)
            qms.append(jnp.where(nope | rope, q, jnp.zeros_like(q)))
        return qms

    def finish(p, outs):
        for j in range(2):
            _place_head(acc_ref.at[p], outs[j], j)

    acc_ref[...] = jnp.zeros_like(acc_ref)
    _attention_steps(MLA_HEADS // 2, qms_for, lambda p: k_ref.at[p],
                     lambda p: [v_ref.at[2 * p], v_ref.at[2 * p + 1]], finish, q_ref.shape[1], t, kc,
                     chains)
    o_ref[...] = jnp.concatenate([acc_ref[0], acc_ref[1]], axis=1).astype(o_ref.dtype)


def _mla_attention(proj, projv, nb, t):
    tq, kc = 512, 512
    nq = t // tq
    kern = functools.partial(_mla_attn_kernel, t=t, kc=kc)
    return pl.pallas_call(
        kern,
        grid=(nb, nq),
        in_specs=[pl.BlockSpec((2, tq, 256), lambda b, i: (S_BQ0 // 2, b * nq + i, 0)),
                  pl.BlockSpec((2, t, 256), lambda b, i: (S_BK0 // 2, b, 0)),
                  pl.BlockSpec((MLA_HEADS, t, LANES), lambda b, i: (V_B // MLA_HEADS, b, 0))],
        out_specs=pl.BlockSpec((tq, 256), lambda b, i: (b * nq + i, 0)),
        out_shape=jax.ShapeDtypeStruct((nb * t, 256), BF16),
        scratch_shapes=[pltpu.VMEM((2, tq, LANES), F32)] + 2 * _chain_scratch(tq, kc),
        compiler_params=_cparams(("arbitrary", "arbitrary")),
        name="mla_attn",
    )(proj, proj, projv)


DIL_REACH = max(w // 2 for w, _ in DIL_PATTERNS)


def _dil_attn_kernel(q_ref, k_ref, v_ref, o_ref, tables_ref, acc_ref, *scratch, t, tq, band):
    chains = (scratch[:CHAIN_REFS], scratch[CHAIN_REFS:])
    i = pl.program_id(1)

    def band_start(blk):
        return jnp.clip(blk * tq - DIL_REACH, 0, t - band)

    start = pl.multiple_of(band_start(i), tq)
    table = (i * tq - start) // tq

    @pl.when((pl.program_id(0) == 0) & (i == 0))
    def _():
        def build(n, _):
            qi = lax.broadcasted_iota(jnp.int32, (tq, band), 0)
            kj = lax.broadcasted_iota(jnp.int32, (tq, band), 1)
            delta = kj - qi - n * tq
            ad = jnp.abs(delta)
            cnt = jnp.zeros((tq, band), F32)
            for window, dil in DIL_PATTERNS:
                ok = (ad <= window // 2) & ((delta & (dil - 1)) == 0)
                cnt = cnt + jnp.where(ok, 1.0, 0.0)
            tables_ref[n] = jnp.where(cnt > 2.5, math.log2(3.0),
                                      jnp.where(cnt > 1.5, 1.0,
                                                jnp.where(cnt > 0.5, 0.0, NEG_INF)))
            return 0

        lax.fori_loop(0, tables_ref.shape[0], build, 0)

    bias_ref = tables_ref.at[table]

    lane = lax.broadcasted_iota(jnp.int32, (1, 256), 1)

    def qms_for(hp):
        q = q_ref[...]
        return [jnp.where((lane >= (2 * hp + j) * DIL_DH) & (lane < (2 * hp + j + 1) * DIL_DH), q,
                          jnp.zeros_like(q)) for j in range(2)]

    def finish(hp, outs):
        for j in range(2):
            _place_head(acc_ref.at[hp], outs[j], j)

    acc_ref[...] = jnp.zeros_like(acc_ref)
    _attention_steps(DIL_HEADS // 2, qms_for, lambda hp: k_ref.at[pl.ds(start, band)],
                     lambda hp: [v_ref.at[2 * hp + j, pl.ds(start, band)] for j in range(2)],
                     finish, tq, band, band // 2, chains, slab=DIL_SLAB, bias_ref=bias_ref)
    o_ref[...] = jnp.concatenate([acc_ref[0], acc_ref[1]], axis=1).astype(o_ref.dtype)


DIL_SLAB = 16


def _dil_attention(proj, projv, nb, t):
    tq = 256
    band = min(t, tq + 2 * DIL_REACH)
    nq = t // tq
    kern = functools.partial(_dil_attn_kernel, t=t, tq=tq, band=band)
    return pl.pallas_call(
        kern,
        grid=(nb, nq),
        in_specs=[pl.BlockSpec((None, tq, 256), lambda b, i: (S_CQ, b * nq + i, 0)),
                  pl.BlockSpec((None, t, 256), lambda b, i: (S_CK, b, 0)),
                  pl.BlockSpec((DIL_HEADS, t, LANES), lambda b, i: (V_C // DIL_HEADS, b, 0))],
        out_specs=pl.BlockSpec((tq, 256), lambda b, i: (b * nq + i, 0)),
        out_shape=jax.ShapeDtypeStruct((nb * t, 256), BF16),
        scratch_shapes=([pltpu.VMEM((band // tq, tq, band), F32), pltpu.VMEM((2, tq, LANES), F32)]
                        + 2 * _chain_scratch(tq, band // 2)),
        compiler_params=_cparams(("arbitrary", "arbitrary")),
        name="dil_attn",
    )(proj, proj, projv)


def _na_bias_table(rpb):
    c = np.arange(GRID_W)
    cs = np.clip(c - NA_KC // 2, 0, GRID_W - NA_KC)
    colmask = (c[None, :] >= cs[:, None]) & (c[None, :] < cs[:, None] + NA_KC)
    rows = jnp.stack([rpb[:, si:si + NA_KR, :] for si in range(NA_KR)], axis=1).astype(F32)
    edge = GRID_W - NA_KC
    padded = jnp.concatenate([jnp.repeat(rows[..., :1], edge, axis=-1), rows,
                              jnp.repeat(rows[..., -1:], edge, axis=-1)], axis=-1)
    b = jnp.stack([padded[..., GRID_W - 1 - qc:2 * GRID_W - 1 - qc] for qc in range(GRID_W)],
                  axis=2)
    b = jnp.where(colmask[None, None, :, None, :], b * LOG2E, NEG_INF)
    b = b.transpose(1, 0, 2, 3, 4)
    return b.reshape(NA_KR, rpb.shape[0] * GRID_W, NA_KR * GRID_W)


def _na_attn_kernel(q_ref, k_ref, v_ref, tb_ref, o_ref, *, rows, rg):
    g = pl.program_id(1)
    lane = lax.broadcasted_iota(jnp.int32, (1, 256), 1)
    nk = NA_KR * GRID_W
    sels = [(lane >= h * NA_DH) & (lane < (h + 1) * NA_DH) for h in range(NA_HEADS)]

    def row(r, _):
        grow = g * rg + r
        rs = jnp.clip(grow - NA_KR // 2, 0, rows - NA_KR)
        si = rs - grow + (NA_KR - 1)
        q = q_ref[pl.ds(pl.multiple_of(r * GRID_W, GRID_W), GRID_W), :]
        koff = pl.multiple_of(rs * GRID_W, GRID_W)
        kb = k_ref[pl.ds(koff, nk), :]
        vb = v_ref[pl.ds(koff, nk), :]
        q4 = jnp.concatenate([jnp.where(sel, q, jnp.zeros_like(q)) for sel in sels], axis=0)
        s = lax.dot_general(q4, kb, (((1,), (1,)), ((), ())), preferred_element_type=F32)
        s = s + tb_ref[si]
        m = jnp.max(s, -1, keepdims=True)
        p = jnp.exp2(s - m)
        l = jnp.sum(p, -1, keepdims=True)
        o4 = jnp.dot(p.astype(BF16), vb, preferred_element_type=F32) * (1.0 / l)
        out = jnp.zeros((GRID_W, 256), F32)
        for h, sel in enumerate(sels):
            out = jnp.where(sel, o4[h * GRID_W:(h + 1) * GRID_W], out)
        o_ref[pl.ds(pl.multiple_of(r * GRID_W, GRID_W), GRID_W), :] = out.astype(o_ref.dtype)
        return 0

    lax.fori_loop(0, rg, row, 0, unroll=2)


def _na_attention(proj, tb, nb, t):
    rows = t // GRID_W
    assert rows >= NA_KR
    rg = 8
    ng = rows // rg
    tq = rg * GRID_W
    kern = functools.partial(_na_attn_kernel, rows=rows, rg=rg)
    return pl.pallas_call(
        kern,
        grid=(nb, ng),
        in_specs=[pl.BlockSpec((None, tq, 256), lambda b, i: (S_DQ, b * ng + i, 0)),
                  pl.BlockSpec((None, t, 256), lambda b, i: (S_DK, b, 0)),
                  pl.BlockSpec((None, t, 256), lambda b, i: (S_DV, b, 0)),
                  pl.BlockSpec(tb.shape, lambda b, i: (0, 0, 0))],
        out_specs=pl.BlockSpec((tq, 256), lambda b, i: (b * ng + i, 0)),
        out_shape=jax.ShapeDtypeStruct((nb * t, 256), BF16),
        compiler_params=_cparams(("arbitrary", "arbitrary")),
        name="na_attn",
    )(proj, proj, proj, tb)


def _layer_norm(y, g, b):
    mu = jnp.mean(y, -1, keepdims=True)
    yc = y - mu
    var = jnp.mean(yc * yc, -1, keepdims=True)
    return yc * lax.rsqrt(var + LN_EPS) * g + b


def _outproj_kernel(oa_ref, ob_ref, oc_ref, od_ref, x_ref, g1_ref, sc2_ref, sh2_ref, w_ref,
                    lg_ref, lb_ref, wr_ref, x1_ref, h2_ref, aff_ref):
    m = jnp.dot(oa_ref[...], w_ref[0:256, :], preferred_element_type=F32)
    m += jnp.dot(ob_ref[...], w_ref[256:512, :], preferred_element_type=F32)
    m += jnp.dot(oc_ref[...], w_ref[512:768, :], preferred_element_type=F32)
    m += jnp.dot(od_ref[...], w_ref[768:1024, :], preferred_element_type=F32)
    y = DEEPNORM_ALPHA * x_ref[...] + (1.0 + g1_ref[...]) * m
    x1 = _layer_norm(y, lg_ref[...], lb_ref[...])
    x1_ref[...] = x1
    h2 = x1 * (1.0 + sc2_ref[...]) + sh2_ref[...]
    h2_ref[...] = h2.astype(BF16)
    h_hi, h_lo = _split_bf16(h2)
    w_hi, w_lo = _split_bf16(wr_ref[...])
    nt = (((1,), (1,)), ((), ()))
    lg = lax.dot_general(w_hi, h_hi, nt, preferred_element_type=F32)
    lg += lax.dot_general(w_hi, h_lo, nt, preferred_element_type=F32)
    lg += lax.dot_general(w_lo, h_hi, nt, preferred_element_type=F32)
    lg = lg - jnp.max(lg, 0, keepdims=True)
    e = jnp.exp(lg)
    aff_ref[...] = e / jnp.sum(e, 0, keepdims=True)


def _outproj(oa, ob, oc, od, x, g1, sc2, sh2, w_out, ln_g, ln_b, w_router_t, nb, t):
    tm = 512
    nt = t // tm
    ntok = nb * t
    tok = lambda w: pl.BlockSpec((tm, w), lambda i: (i, 0))
    per_b = pl.BlockSpec((None, 1, D_MODEL), lambda i: (i // nt, 0, 0))
    full = lambda shape: pl.BlockSpec(shape, lambda i: tuple(0 for _ in shape))
    return pl.pallas_call(
        _outproj_kernel,
        grid=(ntok // tm,),
        in_specs=[tok(256), tok(256), tok(256), tok(256), tok(D_MODEL), per_b, per_b, per_b,
                  full((D_MODEL, D_MODEL)), full((1, D_MODEL)), full((1, D_MODEL)),
                  full((N_EXPERTS, D_MODEL))],
        out_specs=[tok(D_MODEL), tok(D_MODEL),
                   pl.BlockSpec((N_EXPERTS, tm), lambda i: (0, i))],
        out_shape=[jax.ShapeDtypeStruct((ntok, D_MODEL), F32),
                   jax.ShapeDtypeStruct((ntok, D_MODEL), BF16),
                   jax.ShapeDtypeStruct((N_EXPERTS, ntok), F32)],
        compiler_params=_cparams(("arbitrary",)),
        name="out_proj",
    )(oa, ob, oc, od, x, g1, sc2, sh2, w_out, ln_g, ln_b, w_router_t)


def _expert_kernel(x_ref, wg_ref, wu_ref, wd_ref, o_ref, acc_ref):
    f = pl.program_id(2)

    @pl.when(f == 0)
    def _():
        acc_ref[...] = jnp.zeros_like(acc_ref)

    x = x_ref[...]
    g = jnp.dot(x, wg_ref[...], preferred_element_type=F32)
    u = jnp.dot(x, wu_ref[...], preferred_element_type=F32)
    hmid = (g * (1.0 / (1.0 + jnp.exp(-g))) * u).astype(BF16)
    acc_ref[...] += jnp.dot(hmid, wd_ref[...], preferred_element_type=F32)

    @pl.when(f == pl.num_programs(2) - 1)
    def _():
        o_ref[...] = acc_ref[...].astype(o_ref.dtype)


def _experts(xe, slots, wg, wu, wd):
    ne, _, d = xe.shape
    tm = math.gcd(slots, 1024)
    tf = 256
    nf = D_FF // tf
    return pl.pallas_call(
        _expert_kernel,
        grid=(ne, slots // tm, nf),
        in_specs=[pl.BlockSpec((None, tm, d), lambda e, m, f: (e, m, 0)),
                  pl.BlockSpec((None, d, tf), lambda e, m, f: (e, 0, f)),
                  pl.BlockSpec((None, d, tf), lambda e, m, f: (e, 0, f)),
                  pl.BlockSpec((None, tf, d), lambda e, m, f: (e, f, 0))],
        out_specs=pl.BlockSpec((None, tm, d), lambda e, m, f: (e, m, 0)),
        out_shape=jax.ShapeDtypeStruct((ne, slots, d), BF16),
        scratch_shapes=[pltpu.VMEM((tm, d), F32)],
        compiler_params=_cparams(("arbitrary", "arbitrary", "arbitrary")),
        name="expert_ffn",
    )(xe, wg, wu, wd)


MOE_TM = 512
MOE_WIN = 128
ROW_ALIGN = 16


def _select_kernel(aff_ref, o_ref, *, cap):
    a = aff_ref[...]
    keys = lax.bitcast_convert_type(a, jnp.int32)
    ne, n = a.shape
    capf = float(cap)

    def count(mask):
        return jnp.sum(jnp.where(mask, 1.0, 0.0), axis=1, keepdims=True)

    def key_bit(b, thr):
        cand = thr | lax.shift_left(jnp.int32(1), 30 - b)
        return jnp.where(count(keys >= cand) >= capf, cand, thr)

    thr = lax.fori_loop(0, 31, key_bit, jnp.zeros((ne, 1), jnp.int32))
    above = keys > thr
    need = capf - count(above)
    idx = lax.broadcasted_iota(jnp.int32, (ne, n), 1)
    tie_idx = jnp.where(keys == thr, idx, jnp.int32(2 ** 30))
    nbits = max(1, (n - 1).bit_length())

    def idx_bit(b, j):
        cand = j | lax.shift_left(jnp.int32(1), nbits - 1 - b)
        return jnp.where(count(tie_idx < cand) < need, cand, j)

    j = lax.fori_loop(0, nbits, idx_bit, jnp.zeros((ne, 1), jnp.int32))
    sel = above | (tie_idx <= j)
    o_ref[...] = jnp.where(sel, a, -1.0)


def _select(aff, cap):
    ne, n = aff.shape
    return pl.pallas_call(
        functools.partial(_select_kernel, cap=cap),
        out_shape=jax.ShapeDtypeStruct((ne, n), F32),
        compiler_params=pltpu.CompilerParams(vmem_limit_bytes=VMEM_LIMIT),
        name="ec_select",
    )(aff)


def _routing_tables(gs, group_tokens):
    ne = gs.shape[0]
    a_l, off_l, cnt_l, lim_l = [], [], [], []
    tok0, slot0 = 0, 0
    for n in group_tokens:
        cap = EC_FACTOR * n // N_EXPERTS
        nt = n // MOE_TM
        sel = lax.slice_in_dim(gs, tok0, tok0 + n, axis=1) >= 0
        counts = jnp.sum(sel.reshape(ne, nt, MOE_TM), axis=-1, dtype=jnp.int32)
        s0 = slot0 + jnp.cumsum(counts, axis=1) - counts
        a = (s0 // ROW_ALIGN) * ROW_ALIGN
        a_l.append(a)
        off_l.append(s0 - a)
        cnt_l.append(counts)
        lim_l.append(jnp.full((nt,), slot0 + cap - MOE_WIN, jnp.int32))
        tok0 += n
        slot0 += cap
    a = jnp.concatenate(a_l, axis=1).T
    off = jnp.concatenate(off_l, axis=1).T
    end = off + jnp.concatenate(cnt_l, axis=1).T
    rounds = (end + MOE_WIN - 1) // MOE_WIN
    gp = (end // ROW_ALIGN) * ROW_ALIGN
    return dict(a=a.reshape(-1).astype(jnp.int32), nr=rounds.reshape(-1).astype(jnp.int32),
                nrounds=jnp.max(rounds, axis=1).astype(jnp.int32), lim=jnp.concatenate(lim_l),
                off_col=off.astype(F32)[:, :, None], off_row=off.astype(F32)[:, None, :],
                off16=jnp.repeat(off.astype(F32), ROW_ALIGN, axis=1)[:, :, None],
                gp16=jnp.repeat(gp.astype(F32), ROW_ALIGN, axis=1)[:, :, None],
                total=slot0)


def _dispatch_kernel(a_tab, nrounds, nr_tab, gs_ref, off_ref, off16_ref, gp16_ref, x_ref, u_ref, xe_ref,
                     pos_ref, c_ref, c2_ref, stage_ref, carry_ref, sem, rc_ref):
    j = pl.program_id(0)
    ne, tm = gs_ref.shape
    d = x_ref.shape[1]
    win = MOE_WIN

    @pl.when(j == 0)
    def _():
        carry_ref[...] = jnp.zeros_like(carry_ref)
        rc_ref[0] = 0

    sel = gs_ref[...] >= 0.0
    rank = jnp.dot(jnp.where(sel, 1.0, 0.0).astype(BF16), u_ref[...], preferred_element_type=F32)
    pos_ref[...] = jnp.where(sel, rank + off_ref[...], -1.0)

    def wait_round(slot):
        def one(_, c):
            pltpu.make_async_copy(stage_ref.at[slot, pl.ds(0, win)], xe_ref.at[0, pl.ds(0, win)],
                                  sem.at[slot]).wait()
            return c
        lax.fori_loop(0, rc_ref[1 + slot], one, 0)

    def round_body(r, _):
        k = lax.broadcasted_iota(jnp.int32, (win, tm), 0).astype(F32) + (r * win).astype(F32)
        for e in range(ne):
            c_ref[e * win:(e + 1) * win, :] = jnp.where(pos_ref[e:e + 1, :] == k, 1.0, 0.0).astype(BF16)
        slot = rc_ref[0] % 2
        for nb in range(d // MXU_DIM):
            cols = slice(nb * MXU_DIM, (nb + 1) * MXU_DIM)
            stage_ref[slot, :, cols] = jnp.dot(c_ref[...], x_ref[:, cols],
                                               preferred_element_type=F32).astype(BF16)

        @pl.when(r == 0)
        def _():
            k16 = lax.broadcasted_iota(jnp.int32, (ROW_ALIGN, 1), 0).astype(F32)
            for e in range(ne):
                keep = k16 < off16_ref[e * ROW_ALIGN:(e + 1) * ROW_ALIGN, :]
                rows = pl.ds(e * win, ROW_ALIGN)
                stage_ref[slot, rows, :] = jnp.where(
                    keep, carry_ref[e * ROW_ALIGN:(e + 1) * ROW_ALIGN, :], stage_ref[slot, rows, :])

        @pl.when(rc_ref[0] > 0)
        def _():
            wait_round(1 - slot)

        rc_ref[1 + slot] = 0
        for e in range(ne):
            @pl.when(r < nr_tab[j * ne + e])
            def _():
                dst = pl.multiple_of(a_tab[j * ne + e] + r * win, ROW_ALIGN)
                pltpu.make_async_copy(stage_ref.at[slot, pl.ds(e * win, win)],
                                      xe_ref.at[e, pl.ds(dst, win)], sem.at[slot]).start()
                rc_ref[1 + slot] = rc_ref[1 + slot] + 1
        rc_ref[0] = rc_ref[0] + 1
        return 0

    lax.fori_loop(0, nrounds[j], round_body, 0)

    k16 = lax.broadcasted_iota(jnp.int32, (ROW_ALIGN, 1), 0).astype(F32)
    for e in range(ne):
        rows = slice(e * ROW_ALIGN, (e + 1) * ROW_ALIGN)
        c2_ref[rows, :] = jnp.where(pos_ref[e:e + 1, :] == gp16_ref[rows, :] + k16, 1.0, 0.0).astype(BF16)
    kk = jnp.concatenate([k16] * ne, axis=0)
    keep_old = (gp16_ref[...] == 0.0) & (kk < off16_ref[...])
    for nb in range(d // MXU_DIM):
        cols = slice(nb * MXU_DIM, (nb + 1) * MXU_DIM)
        new = jnp.dot(c2_ref[...], x_ref[:, cols], preferred_element_type=F32).astype(BF16)
        carry_ref[:, cols] = jnp.where(keep_old, carry_ref[:, cols], new)

    @pl.when((j == pl.num_programs(0) - 1) & (rc_ref[0] > 0))
    def _():
        wait_round((rc_ref[0] - 1) % 2)


def _dispatch(gs, h2, tabs, u_mat):
    ne, ntok = gs.shape
    d = h2.shape[1]
    nt = ntok // MOE_TM
    rows = tabs['total'] + MOE_WIN
    grid_spec = pltpu.PrefetchScalarGridSpec(
        num_scalar_prefetch=3,
        grid=(nt,),
        in_specs=[pl.BlockSpec((ne, MOE_TM), lambda j, *_: (0, j)),
                  pl.BlockSpec((None, ne, 1), lambda j, *_: (j, 0, 0)),
                  pl.BlockSpec((None, ne * ROW_ALIGN, 1), lambda j, *_: (j, 0, 0)),
                  pl.BlockSpec((None, ne * ROW_ALIGN, 1), lambda j, *_: (j, 0, 0)),
                  pl.BlockSpec((MOE_TM, d), lambda j, *_: (j, 0)),
                  pl.BlockSpec((MOE_TM, MOE_TM), lambda j, *_: (0, 0))],
        out_specs=pl.BlockSpec(memory_space=pl.ANY),
        scratch_shapes=[pltpu.VMEM((ne, MOE_TM), F32),
                        pltpu.VMEM((ne * MOE_WIN, MOE_TM), BF16),
                        pltpu.VMEM((ne * ROW_ALIGN, MOE_TM), BF16),
                        pltpu.VMEM((2, ne * MOE_WIN, d), BF16),
                        pltpu.VMEM((ne * ROW_ALIGN, d), BF16),
                        pltpu.SemaphoreType.DMA((2,)),
                        pltpu.SMEM((3,), jnp.int32)])
    return pl.pallas_call(
        _dispatch_kernel,
        grid_spec=grid_spec,
        out_shape=jax.ShapeDtypeStruct((ne, rows, d), BF16),
        compiler_params=_cparams(("arbitrary",)),
        name="ec_dispatch",
    )(tabs['a'], tabs['nrounds'], tabs['nr'], gs, tabs['off_col'], tabs['off16'], tabs['gp16'], h2,
      u_mat)


def _combine_kernel(a_tab, nrounds, lim_tab, gs_ref, off_ref, l_ref, ye_ref, x1_ref, g2_ref, lg_ref,
                    lb_ref, *refs, split_tiles):
    outs, (p_ref, y_ref, acc_ref, sem) = refs[:-4], refs[-4:]
    j = pl.program_id(0)
    tm, ne = gs_ref.shape
    d = x1_ref.shape[1]
    win = MOE_WIN
    gs = gs_ref[...]
    sel = gs >= 0.0
    rank = jnp.dot(l_ref[...], jnp.where(sel, 1.0, 0.0).astype(BF16), preferred_element_type=F32)
    pos = jnp.where(sel, rank + off_ref[...], -1.0)
    gate = jnp.where(sel, gs, 0.0)
    acc_ref[...] = jnp.zeros_like(acc_ref)
    n_tiles = pl.num_programs(0)
    slot = j % 2

    def window(tile, r, e):
        want = a_tab[tile * ne + e] + r * win
        src = pl.multiple_of(jnp.minimum(want, lim_tab[tile]), ROW_ALIGN)
        return src, want - src

    def start_round(tile, r, sl):
        for e in range(ne):
            src, _ = window(tile, r, e)
            pltpu.make_async_copy(ye_ref.at[e, pl.ds(src, win)], y_ref.at[sl, pl.ds(e * win, win)],
                                  sem.at[sl]).start()

    def wait_round(sl):
        for e in range(ne):
            pltpu.make_async_copy(ye_ref.at[e, pl.ds(0, win)], y_ref.at[sl, pl.ds(e * win, win)],
                                  sem.at[sl]).wait()

    @pl.when((j == 0) & (nrounds[0] > 0))
    def _():
        start_round(0, 0, 0)

    nxt = jnp.minimum(j + 1, n_tiles - 1)

    @pl.when((j + 1 < n_tiles) & (nrounds[nxt] > 0))
    def _():
        start_round(nxt, 0, 1 - slot)

    half = ne // 2

    def accumulate(r):
        base = lax.convert_element_type(r * win, F32)
        k = lax.broadcasted_iota(jnp.int32, (tm, win), 1).astype(F32) + base
        for grp in range(2):
            for e in range(grp * half, (grp + 1) * half):
                pe = pos[:, e:e + 1]
                pe = jnp.where(pe >= base, pe + window(j, r, e)[1].astype(F32), -1.0)
                pcol = jnp.broadcast_to(pe, (tm, win))
                gcol = jnp.broadcast_to(gate[:, e:e + 1], (tm, win))
                p_ref[:, e * win:(e + 1) * win] = jnp.where(pcol == k, gcol, 0.0).astype(BF16)
            rows = slice(grp * half * win, (grp + 1) * half * win)
            for nb in range(d // MXU_DIM):
                cols = slice(nb * MXU_DIM, (nb + 1) * MXU_DIM)
                acc_ref[:, cols] += jnp.dot(p_ref[:, rows], y_ref[slot, rows, cols],
                                            preferred_element_type=F32)

    @pl.when(nrounds[j] > 0)
    def _():
        wait_round(slot)
        accumulate(0)

    def extra_round(r, _):
        start_round(j, r, slot)
        wait_round(slot)
        accumulate(r)
        return 0

    lax.fori_loop(1, nrounds[j], extra_round, 0)
    y = DEEPNORM_ALPHA * x1_ref[...] + (1.0 + g2_ref[...]) * acc_ref[...]
    res = _layer_norm(y, lg_ref[...], lb_ref[...])
    if split_tiles is None:
        outs[0][...] = res
    else:
        @pl.when(j < split_tiles)
        def _():
            outs[0][...] = res

        @pl.when(j >= split_tiles)
        def _():
            outs[1][...] = res


def _combine_postnorm(gs_tok, ye, tabs, l_mat, x1, g2, ln_g, ln_b, t, split_tiles=None):
    ntok, ne = gs_tok.shape
    d = x1.shape[1]
    nt = ntok // MOE_TM
    tiles_per_seq = t // MOE_TM
    if split_tiles is None:
        out_specs = pl.BlockSpec((MOE_TM, d), lambda j, *_: (j, 0))
        out_shape = jax.ShapeDtypeStruct((ntok, d), F32)
    else:
        out_specs = [pl.BlockSpec((MOE_TM, d), lambda j, *_: (jnp.minimum(j, split_tiles - 1), 0)),
                     pl.BlockSpec((MOE_TM, d), lambda j, *_: (jnp.maximum(j - split_tiles, 0), 0))]
        out_shape = [jax.ShapeDtypeStruct((split_tiles * MOE_TM, d), F32),
                     jax.ShapeDtypeStruct((ntok - split_tiles * MOE_TM, d), F32)]
    grid_spec = pltpu.PrefetchScalarGridSpec(
        num_scalar_prefetch=3,
        grid=(nt,),
        in_specs=[pl.BlockSpec((MOE_TM, ne), lambda j, *_: (j, 0)),
                  pl.BlockSpec((None, 1, ne), lambda j, *_: (j, 0, 0)),
                  pl.BlockSpec((MOE_TM, MOE_TM), lambda j, *_: (0, 0)),
                  pl.BlockSpec(memory_space=pl.ANY),
                  pl.BlockSpec((MOE_TM, d), lambda j, *_: (j, 0)),
                  pl.BlockSpec((None, 1, d), lambda j, *_: (j // tiles_per_seq, 0, 0)),
                  pl.BlockSpec((1, d), lambda j, *_: (0, 0)),
                  pl.BlockSpec((1, d), lambda j, *_: (0, 0))],
        out_specs=out_specs,
        scratch_shapes=[pltpu.VMEM((MOE_TM, ne * MOE_WIN), BF16),
                        pltpu.VMEM((2, ne * MOE_WIN, d), BF16),
                        pltpu.VMEM((MOE_TM, d), F32),
                        pltpu.SemaphoreType.DMA((2,))])
    return pl.pallas_call(
        functools.partial(_combine_kernel, split_tiles=split_tiles),
        grid_spec=grid_spec,
        out_shape=out_shape,
        compiler_params=_cparams(("arbitrary",)),
        name="ec_combine",
    )(tabs['a'], tabs['nrounds'], tabs['lim'], gs_tok, tabs['off_row'], l_mat, ye, x1, g2, ln_g, ln_b)


def _prep_w_in(w_in_l):
    sizes = (256, 256, 256, MLA_Q_RANK, MLA_KV_RANK, MLA_ROPE, 256, 256, 256, 256, 256, 256)
    offs = np.concatenate([[0], np.cumsum(sizes)])
    part = [w_in_l[:, offs[i]:offs[i + 1]] for i in range(len(sizes))]
    a_q, a_k, a_v, b_cq, b_ckv, b_kr, c_q, c_k, c_v, d_q, d_k, d_v = part
    d = w_in_l.shape[0]
    zeros = lambda n: jnp.zeros((d, n), w_in_l.dtype)
    def per_head(v):
        out = []
        for h in range(4):
            out += [v[:, h * 64:(h + 1) * 64], zeros(LANES - 64)]
        return out

    cols = ([a_q, a_k] + per_head(a_v) + [c_q, c_k] + per_head(c_v)
            + [d_q, d_k, d_v, b_cq, zeros(256 - MLA_Q_RANK), b_ckv, b_kr, b_kr,
               zeros(128 - 2 * MLA_ROPE)])
    return jnp.concatenate(cols, axis=1).astype(BF16)


def _prep_w_uq(w_uq_l):
    hd = MLA_NOPE + MLA_ROPE
    nope = [w_uq_l[:, h * hd:h * hd + MLA_NOPE] for h in range(MLA_HEADS)]
    rope = [w_uq_l[:, h * hd + MLA_NOPE:(h + 1) * hd] for h in range(MLA_HEADS)]
    z = jnp.zeros((w_uq_l.shape[0], 256 - 2 * hd), w_uq_l.dtype)
    cols = []
    for p in range(2):
        cols += [nope[2 * p], nope[2 * p + 1], rope[2 * p], rope[2 * p + 1], z]
    w = jnp.concatenate(cols, axis=1)
    w = jnp.concatenate([w, jnp.zeros((256 - MLA_Q_RANK, w.shape[1]), w.dtype)], axis=0)
    return w.astype(BF16)


def _prep_w_ukv(w_ukv_l):
    hd = MLA_NOPE + MLA_DV
    kn = [w_ukv_l[:, h * hd:h * hd + MLA_NOPE] for h in range(MLA_HEADS)]
    z = jnp.zeros((w_ukv_l.shape[0], LANES - MLA_DV), w_ukv_l.dtype)
    vv = []
    for h in range(MLA_HEADS):
        vv += [w_ukv_l[:, h * hd + MLA_NOPE:(h + 1) * hd], z]
    return jnp.concatenate(kn + vv, axis=1).astype(BF16)


def _select_groups(aff_t, group_tokens):
    parts, off = [], 0
    for n in group_tokens:
        parts.append(_select(lax.slice_in_dim(aff_t, off, off + n, axis=1),
                             EC_FACTOR * n // N_EXPERTS))
        off += n
    return jnp.concatenate(parts, axis=1)


def _trunk(x, c, group_tokens, nb, t, p):
    ntok = nb * t
    tabs = (_rope_tables(t, 256, DA_DQK, DA_DQK // ROPE_FRACTION, ROPE_THETA)
            + _rope_tables(t, 256, DIL_DH, DIL_DH // ROPE_FRACTION, ROPE_THETA)
            + _rope_tables(t, 128, MLA_ROPE, MLA_ROPE, MLA_ROPE_THETA))
    gmat = jnp.asarray(np.kron(np.eye(4), np.full((64, 64), 1.0 / 64)), BF16)
    ti = jnp.arange(MOE_TM)
    u_mat = (ti[:, None] < ti[None, :]).astype(BF16)
    l_mat = (ti[None, :] < ti[:, None]).astype(BF16)
    for l in range(DEPTH):
        mod = _modulation(c, p['w_ada'][l], p['b_ada'][l])
        sh1, sc1, g1, sh2, sc2, g2 = [m.reshape(nb, 1, D_MODEL) for m in jnp.split(mod, 6, axis=-1)]
        gq = jnp.concatenate([p['q_norm_g'][l], jnp.zeros((256 - MLA_Q_RANK,), F32)]).reshape(1, 256)
        gkv = p['kv_norm_g'][l].reshape(1, 128)
        proj, projv = _inproj(x, sc1, sh1, _prep_w_in(p['w_in'][l]), _prep_w_uq(p['w_uq'][l]),
                              _prep_w_ukv(p['w_ukv'][l]), gq, gkv, tabs, nb, t)
        lam_init = 0.8 - 0.6 * math.exp(-0.3 * l)
        lam = (jnp.exp(jnp.sum(p['da_lq1'][l] * p['da_lk1'][l]))
               - jnp.exp(jnp.sum(p['da_lq2'][l] * p['da_lk2'][l])) + lam_init).reshape(1)
        g_sub = jnp.tile(p['da_subln_g'][l], DA_HEADS).reshape(1, 256)
        oa = _diff_attention(proj, projv, lam, g_sub, gmat, nb, t, 1.0 - lam_init)
        ob = _mla_attention(proj, projv, nb, t)
        oc = _dil_attention(proj, projv, nb, t)
        od = _na_attention(proj, _na_bias_table(p['na_rpb'][l]), nb, t)
        x1, h2, aff_t = _outproj(oa, ob, oc, od, x, g1, sc2, sh2, p['w_out'][l].astype(BF16),
                                 p['ln1_g'][l].reshape(1, -1), p['ln1_b'][l].reshape(1, -1),
                                 p['w_router'][l].T, nb, t)
        wg = p['w_e_gate'][l].astype(BF16)
        wu = p['w_e_up'][l].astype(BF16)
        wd = p['w_e_down'][l].astype(BF16)
        gs = _select_groups(aff_t, group_tokens)
        rt = _routing_tables(gs, group_tokens)
        xe = _dispatch(gs, h2, rt, u_mat)
        ye = _experts(xe, rt['total'], wg, wu, wd)
        split = group_tokens[0] // MOE_TM if l == DEPTH - 1 else None
        x = _combine_postnorm(gs.T, ye, rt, l_mat, x1, g2, p['ln2_g'][l].reshape(1, -1),
                              p['ln2_b'][l].reshape(1, -1), t, split_tiles=split)
    return x


def kernel(x_prompt, x_sample, c_prompt, c_sample, w_in, w_uq, w_ukv, q_norm_g, kv_norm_g, da_lq1,
           da_lk1, da_lq2, da_lk2, da_subln_g, na_rpb, w_out, w_ada, b_ada, ln1_g, ln1_b, ln2_g,
           ln2_b, w_router, w_e_gate, w_e_up, w_e_down):
    p = dict(w_in=w_in, w_uq=w_uq, w_ukv=w_ukv, q_norm_g=q_norm_g, kv_norm_g=kv_norm_g,
             da_lq1=da_lq1, da_lk1=da_lk1, da_lq2=da_lq2, da_lk2=da_lk2, da_subln_g=da_subln_g,
             na_rpb=na_rpb, w_out=w_out, w_ada=w_ada, b_ada=b_ada, ln1_g=ln1_g, ln1_b=ln1_b,
             ln2_g=ln2_g, ln2_b=ln2_b, w_router=w_router, w_e_gate=w_e_gate, w_e_up=w_e_up,
             w_e_down=w_e_down)
    bp, t, d = x_prompt.shape
    bs = x_sample.shape[0]
    assert x_sample.shape[1] == t
    nb = bp + bs
    x = jnp.concatenate([x_prompt.reshape(bp * t, d), x_sample.reshape(bs * t, d)], axis=0)
    c = jnp.concatenate([c_prompt, c_sample], axis=0)
    y_prompt, y_sample = _trunk(x, c, (bp * t, bs * t), nb, t, p)
    return y_prompt.reshape(bp, t, d), y_sample.reshape(bs, t, d)
```

```python
import functools
import math

import jax
import jax.numpy as jnp
import numpy as np
from jax import lax
from jax.experimental import pallas as pl
from jax.experimental.pallas import tpu as pltpu

F32 = jnp.float32
BF16 = jnp.bfloat16

D_MODEL = 1024
DEPTH = 2
GRID_W = 64
GROUP_W = 256
DA_HEADS, DA_DV, DA_DQK = 4, 64, 32
MLA_HEADS, MLA_Q_RANK, MLA_KV_RANK, MLA_NOPE, MLA_ROPE, MLA_DV = 4, 192, 128, 64, 32, 64
MLA_ROPE_THETA = 10000.0
DIL_HEADS, DIL_DH = 4, 64
DIL_PATTERNS = ((128, 1), (512, 4), (2048, 16))
NA_HEADS, NA_DH, NA_KR, NA_KC = 4, 64, 8, 16
ROPE_THETA = 500000.0
ROPE_FRACTION = 4
N_EXPERTS = 16
EC_FACTOR = 2
D_FF = 2816
DEEPNORM_ALPHA = (2.0 * DEPTH) ** 0.25
NEG_INF = -1e30
LOG2E = math.log2(math.e)
LN_EPS = 1e-5
RMS_EPS = 1e-6

LANES = 128
MXU_DIM = 256
VMEM_LIMIT = 56 * 1024 * 1024

S_AQ, S_AK, S_BQ0, S_BQ1, S_BK0, S_BK1, S_CQ, S_CK, S_DQ, S_DK, S_DV = range(11)
N_SLOTS = 11
V_A, V_B, V_C = 0, 4, 8
N_VSLOTS = 12
V_ONE_LANE = 64
W_ALL_COLS = 13 * 256


def _cparams(sem):
    return pltpu.CompilerParams(dimension_semantics=sem, vmem_limit_bytes=VMEM_LIMIT)


def _split_bf16(a):
    hi = a.astype(BF16)
    lo = (a - hi.astype(F32)).astype(BF16)
    return hi, lo


def _mod_kernel(c_ref, w_ref, b_ref, o_ref):
    c = c_ref[...]
    a = c * (1.0 / (1.0 + jnp.exp(-c)))
    a_hi, a_lo = _split_bf16(a)
    w_hi, w_lo = _split_bf16(w_ref[...])
    acc = jnp.dot(a_hi, w_hi, preferred_element_type=F32)
    acc += jnp.dot(a_hi, w_lo, preferred_element_type=F32)
    acc += jnp.dot(a_lo, w_hi, preferred_element_type=F32)
    o_ref[...] = acc + b_ref[...]


def _modulation(c, w_ada, b_ada):
    nb, d = c.shape
    n_out = w_ada.shape[1]
    tn = 1536
    return pl.pallas_call(
        _mod_kernel,
        grid=(n_out // tn,),
        in_specs=[pl.BlockSpec((nb, d), lambda j: (0, 0)),
                  pl.BlockSpec((d, tn), lambda j: (0, j)),
                  pl.BlockSpec((1, tn), lambda j: (0, j))],
        out_specs=pl.BlockSpec((nb, tn), lambda j: (0, j)),
        out_shape=jax.ShapeDtypeStruct((nb, n_out), F32),
        compiler_params=_cparams(("arbitrary",)),
        name="adaln_mod",
    )(c, w_ada, b_ada.reshape(1, n_out))


def _rope_tables(t, width, group, rot, theta):
    half = rot // 2
    inv = theta ** (-jnp.arange(half, dtype=F32) / half)
    ang = jnp.arange(t, dtype=F32)[:, None] * inv[None, :]
    cos, sin = jnp.cos(ang), jnp.sin(ang)
    ones = jnp.ones((t, group - rot), F32)
    zeros = jnp.zeros((t, group - rot), F32)
    c_g = jnp.concatenate([cos, cos, ones], axis=1)
    s_g = jnp.concatenate([-sin, sin, zeros], axis=1)
    reps = width // group
    return jnp.tile(c_g, (1, reps)), jnp.tile(s_g, (1, reps))


def _apply_rope(x, c_tab, s_tab, group, rot):
    width = x.shape[-1]
    half = rot // 2
    lane = lax.broadcasted_iota(jnp.int32, (1, width), 1)
    first = (lane % group) < half
    fwd = pltpu.roll(x, width - half, 1)
    bwd = pltpu.roll(x, half, 1)
    return x * c_tab + s_tab * jnp.where(first, fwd, bwd)


def _inproj_kernel(x_ref, sc_ref, sh_ref, w_ref, wuq_ref, wukv_ref, gq_ref, gkv_ref,
                   ca_ref, sa_ref, cc_ref, scc_ref, cm_ref, sm_ref, o_ref, ov_ref):
    h = (x_ref[...] * (1.0 + sc_ref[...]) + sh_ref[...]).astype(BF16)

    def proj(col, width):
        return jnp.dot(h, w_ref[:, col:col + width], preferred_element_type=F32)

    one_lane = lax.broadcasted_iota(jnp.int32, (1, LANES), 1) == V_ONE_LANE

    def put_values(first_slot, vals):
        for hh in range(4):
            v = vals[:, LANES * hh:LANES * (hh + 1)]
            ov_ref[first_slot + hh] = jnp.where(one_lane, 1.0, v).astype(BF16)

    sa_scale = DA_DQK ** -0.5 * LOG2E
    sb_scale = (MLA_NOPE + MLA_ROPE) ** -0.5 * LOG2E
    sc_scale = DIL_DH ** -0.5 * LOG2E
    sd_scale = NA_DH ** -0.5 * LOG2E
    a_rot = DA_DQK // ROPE_FRACTION
    c_rot = DIL_DH // ROPE_FRACTION

    ca, sa = ca_ref[...], sa_ref[...]
    o_ref[S_AQ] = (_apply_rope(proj(0, 256), ca, sa, DA_DQK, a_rot) * sa_scale).astype(BF16)
    o_ref[S_AK] = _apply_rope(proj(256, 256), ca, sa, DA_DQK, a_rot).astype(BF16)
    put_values(V_A, proj(512, 512))
    cc, scc = cc_ref[...], scc_ref[...]
    o_ref[S_CQ] = (_apply_rope(proj(1024, 256), cc, scc, DIL_DH, c_rot) * sc_scale).astype(BF16)
    o_ref[S_CK] = _apply_rope(proj(1280, 256), cc, scc, DIL_DH, c_rot).astype(BF16)
    put_values(V_C, proj(1536, 512))
    o_ref[S_DQ] = (proj(2048, 256) * sd_scale).astype(BF16)
    o_ref[S_DK] = proj(2304, 256).astype(BF16)
    o_ref[S_DV] = proj(2560, 256).astype(BF16)

    cm, sm = cm_ref[...], sm_ref[...]
    cq = proj(2816, 256)
    cq = cq * lax.rsqrt(jnp.sum(cq * cq, -1, keepdims=True) * (1.0 / MLA_Q_RANK) + RMS_EPS)
    cq = (cq * gq_ref[...]).astype(BF16)
    q2 = jnp.dot(cq, wuq_ref[...], preferred_element_type=F32)
    for p in range(2):
        qp = q2[:, 256 * p:256 * (p + 1)]
        o_ref[S_BQ0 + p, :, 0:128] = (qp[:, 0:128] * sb_scale).astype(BF16)
        o_ref[S_BQ0 + p, :, 128:256] = (
            _apply_rope(qp[:, 128:256], cm, sm, MLA_ROPE, MLA_ROPE) * sb_scale).astype(BF16)
    ckv = proj(3072, 128)
    ckv = ckv * lax.rsqrt(jnp.mean(ckv * ckv, -1, keepdims=True) + RMS_EPS)
    ckv = (ckv * gkv_ref[...]).astype(BF16)
    kv = jnp.dot(ckv, wukv_ref[...], preferred_element_type=F32)
    kr = _apply_rope(proj(3200, 128), cm, sm, MLA_ROPE, MLA_ROPE).astype(BF16)
    for p in range(2):
        o_ref[S_BK0 + p, :, 0:128] = kv[:, 128 * p:128 * (p + 1)].astype(BF16)
        o_ref[S_BK0 + p, :, 128:256] = kr
    put_values(V_B, kv[:, 256:768])


def _inproj(x, sc, sh, w_all, wuq, wukv, gq, gkv, tabs, nb, t):
    tm = 512
    nt = t // tm
    ntok = nb * t
    ca, sa, cc, scc, cm, sm = tabs
    full = lambda shape: pl.BlockSpec(shape, lambda j, b: tuple(0 for _ in shape))
    tab = lambda w: pl.BlockSpec((tm, w), lambda j, b: (j, 0))
    return pl.pallas_call(
        _inproj_kernel,
        grid=(nt, nb),
        in_specs=[pl.BlockSpec((tm, D_MODEL), lambda j, b: (b * nt + j, 0)),
                  pl.BlockSpec((None, 1, D_MODEL), lambda j, b: (b, 0, 0)),
                  pl.BlockSpec((None, 1, D_MODEL), lambda j, b: (b, 0, 0)),
                  full((D_MODEL, W_ALL_COLS)), full((256, 512)), full((128, 768)),
                  full((1, 256)), full((1, 128)),
                  tab(256), tab(256), tab(256), tab(256), tab(128), tab(128)],
        out_specs=[pl.BlockSpec((N_SLOTS, tm, 256), lambda j, b: (0, b * nt + j, 0)),
                   pl.BlockSpec((N_VSLOTS, tm, LANES), lambda j, b: (0, b * nt + j, 0))],
        out_shape=[jax.ShapeDtypeStruct((N_SLOTS, ntok, 256), BF16),
                   jax.ShapeDtypeStruct((N_VSLOTS, ntok, LANES), BF16)],
        compiler_params=_cparams(("arbitrary", "arbitrary")),
        name="in_proj",
    )(x, sc, sh, w_all, wuq, wukv, gq, gkv, ca, sa, cc, scc, cm, sm)


def _lane_mask(width, ranges):
    lane = lax.broadcasted_iota(jnp.int32, (1, width), 1)
    m = None
    for lo, hi in ranges:
        r = (lane >= lo) & (lane < hi)
        m = r if m is None else (m | r)
    return m


SOFTMAX_SLAB = 32


def _chain_scratch(tq, kc):
    return [pltpu.VMEM((tq, kc), F32), pltpu.VMEM((tq, kc), F32), pltpu.VMEM((tq, kc), BF16),
            pltpu.VMEM((tq, LANES), F32), pltpu.VMEM((tq, LANES), F32), pltpu.VMEM((tq, LANES), F32)]


CHAIN_REFS = 6


def _attention_steps(n_steps, qms_for, k_for, v_for, finish, tq, t, kc, chains, slab=SOFTMAX_SLAB,
                     bias_ref=None, step_unroll=1):
    n_chunks = t // kc
    assert n_chunks % 2 == 0
    nt = (((1,), (1,)), ((), ()))

    def chunk_rows(chunk):
        if isinstance(chunk, int):
            return pl.ds(chunk * kc, kc)
        return pl.ds(pl.multiple_of(chunk * kc, kc), kc)

    def scores(step, chunk, slot):
        k = k_for(step)[chunk_rows(chunk), :]
        for qm, chain in zip(qms_for(step), chains):
            chain[slot][...] = lax.dot_general(qm, k, nt, preferred_element_type=F32)

    def softmax_pv(step, chunk, slot):
        for v_ref, chain in zip(v_for(step), chains):
            v = v_ref[chunk_rows(chunk), :]
            s_ref = chain[slot]
            (p_ref, m_ref, a_ref, acc_ref) = chain[2:]
            for r in range(tq // slab):
                rows = slice(r * slab, (r + 1) * slab)
                s = s_ref[rows, :]
                if bias_ref is not None:
                    s = s + bias_ref[rows, chunk * kc:(chunk + 1) * kc]
                m_prev = m_ref[rows, :]
                m_new = jnp.maximum(m_prev, jnp.max(s, -1, keepdims=True))
                d = s - jnp.concatenate([m_new] * (kc // LANES), axis=1)
                p_ref[rows, :] = jnp.exp2(d.astype(BF16))
                a_ref[rows, :] = jnp.exp2(m_prev - m_new)
                m_ref[rows, :] = m_new
            acc_ref[...] = a_ref[...] * acc_ref[...] + jnp.dot(p_ref[...], v,
                                                              preferred_element_type=F32)

    def step_body(st, _):
        for (_, _, _, m_ref, _, acc_ref) in chains:
            m_ref[...] = jnp.full(m_ref.shape, NEG_INF, F32)
            acc_ref[...] = jnp.zeros(acc_ref.shape, F32)
        nxt = min(st + 1, n_steps - 1) if isinstance(st, int) else jnp.minimum(st + 1, n_steps - 1)

        def body(j, _):
            scores(st, 2 * j + 1, 1)
            softmax_pv(st, 2 * j, 0)
            if n_chunks == 2:
                scores(nxt, 0, 0)
            else:
                last = 2 * j + 2 == n_chunks
                scores(jnp.where(last, nxt, st), jnp.where(last, 0, 2 * j + 2), 0)
            softmax_pv(st, 2 * j + 1, 1)
            return 0

        if n_chunks == 2:
            body(0, 0)
        else:
            assert bias_ref is None
            lax.fori_loop(0, n_chunks // 2, body, 0, unroll=4)
        outs = []
        for chain in chains:
            acc = chain[5][...]
            den = jnp.broadcast_to(acc[:, V_ONE_LANE:V_ONE_LANE + 1], acc.shape)
            outs.append(acc * (1.0 / den))
        finish(st, outs)
        return 0

    scores(0, 0, 0)
    if step_unroll == n_steps:
        for st in range(n_steps):
            step_body(st, 0)
    else:
        lax.fori_loop(0, n_steps, step_body, 0, unroll=step_unroll)


def _place_head(pair_ref, o, j):
    lane = lax.broadcasted_iota(jnp.int32, (1, LANES), 1)
    low = lane < V_ONE_LANE
    shifted = pltpu.roll(o, V_ONE_LANE, 1)
    cur = pair_ref[...]
    pair_ref[...] = jnp.where(low, jnp.where(j == 0, o, cur), jnp.where(j == 1, shifted, cur))


def _group_mean_sq(x, gmat):
    sq = x * x
    hi, lo = _split_bf16(sq)
    return (jnp.dot(hi, gmat, preferred_element_type=F32)
            + jnp.dot(lo, gmat, preferred_element_type=F32))


def _diff_attn_kernel(lam_ref, q_ref, k_ref, v_ref, g_ref, gmat_ref, o_ref, acc_ref, *scratch, t, kc,
                      out_scale):
    chains = (scratch[:CHAIN_REFS], scratch[CHAIN_REFS:])
    lam = lam_ref[0]
    lane = lax.broadcasted_iota(jnp.int32, (1, 256), 1)

    def qms_for(h):
        q = q_ref[...]
        return [jnp.where((lane >= (2 * h + c) * DA_DQK) & (lane < (2 * h + c + 1) * DA_DQK), q,
                          jnp.zeros_like(q)) for c in range(2)]

    def finish(h, outs):
        _place_head(acc_ref.at[h // 2], outs[0] - lam * outs[1], h % 2)

    acc_ref[...] = jnp.zeros_like(acc_ref)
    _attention_steps(DA_HEADS, qms_for, lambda h: k_ref, lambda h: [v_ref.at[h], v_ref.at[h]],
                     finish, q_ref.shape[0], t, kc, chains)
    o = jnp.concatenate([acc_ref[0], acc_ref[1]], axis=1)
    ms = _group_mean_sq(o, gmat_ref[...])
    o_ref[...] = (o * lax.rsqrt(ms + RMS_EPS) * g_ref[...] * out_scale).astype(o_ref.dtype)


def _diff_attention(proj, projv, lam, g_tiled, gmat, nb, t, out_scale):
    tq, kc = 512, 512
    nq = t // tq
    kern = functools.partial(_diff_attn_kernel, t=t, kc=kc, out_scale=out_scale)
    return pl.pallas_call(
        kern,
        grid=(nb, nq),
        in_specs=[pl.BlockSpec(memory_space=pltpu.SMEM),
                  pl.BlockSpec((None, tq, 256), lambda b, i: (S_AQ, b * nq + i, 0)),
                  pl.BlockSpec((None, t, 256), lambda b, i: (S_AK, b, 0)),
                  pl.BlockSpec((DA_HEADS, t, LANES), lambda b, i: (V_A // DA_HEADS, b, 0)),
                  pl.BlockSpec((1, 256), lambda b, i: (0, 0)),
                  pl.BlockSpec((256, 256), lambda b, i: (0, 0))],
        out_specs=pl.BlockSpec((tq, 256), lambda b, i: (b * nq + i, 0)),
        out_shape=jax.ShapeDtypeStruct((nb * t, 256), BF16),
        scratch_shapes=[pltpu.VMEM((2, tq, LANES), F32)] + 2 * _chain_scratch(tq, kc),
        compiler_params=_cparams(("arbitrary", "arbitrary")),
        name="diff_attn",
    )(lam, proj, proj, projv, g_tiled, gmat)


def _mla_attn_kernel(q_ref, k_ref, v_ref, o_ref, acc_ref, *scratch, t, kc):
    chains = (scratch[:CHAIN_REFS], scratch[CHAIN_REFS:])
    lane = lax.broadcasted_iota(jnp.int32, (1, 256), 1)

    def qms_for(p):
        q = q_ref[p]
        qms = []
        for j in range(2):
            nope = (lane >= j * MLA_NOPE) & (lane < (j + 1) * MLA_NOPE)
            rope = (lane >= 128 + j * MLA_ROPE) & (lane < 128 + (j + 1) * MLA_ROPE)
            qms.append(jnp.where(nope | rope, q, jnp.zeros_like(q)))
        return qms

    def finish(p, outs):
        for j in range(2):
            _place_head(acc_ref.at[p], outs[j], j)

    acc_ref[...] = jnp.zeros_like(acc_ref)
    _attention_steps(MLA_HEADS // 2, qms_for, lambda p: k_ref.at[p],
                     lambda p: [v_ref.at[2 * p], v_ref.at[2 * p + 1]], finish, q_ref.shape[1], t, kc,
                     chains)
    o_ref[...] = jnp.concatenate([acc_ref[0], acc_ref[1]], axis=1).astype(o_ref.dtype)


def _mla_attention(proj, projv, nb, t):
    tq, kc = 512, 512
    nq = t // tq
    kern = functools.partial(_mla_attn_kernel, t=t, kc=kc)
    return pl.pallas_call(
        kern,
        grid=(nb, nq),
        in_specs=[pl.BlockSpec((2, tq, 256), lambda b, i: (S_BQ0 // 2, b * nq + i, 0)),
                  pl.BlockSpec((2, t, 256), lambda b, i: (S_BK0 // 2, b, 0)),
                  pl.BlockSpec((MLA_HEADS, t, LANES), lambda b, i: (V_B // MLA_HEADS, b, 0))],
        out_specs=pl.BlockSpec((tq, 256), lambda b, i: (b * nq + i, 0)),
        out_shape=jax.ShapeDtypeStruct((nb * t, 256), BF16),
        scratch_shapes=[pltpu.VMEM((2, tq, LANES), F32)] + 2 * _chain_scratch(tq, kc),
        compiler_params=_cparams(("arbitrary", "arbitrary")),
        name="mla_attn",
    )(proj, proj, projv)


DIL_REACH = max(w // 2 for w, _ in DIL_PATTERNS)


def _dil_attn_kernel(q_ref, k_ref, v_ref, o_ref, tables_ref, acc_ref, *scratch, t, tq, band):
    chains = (scratch[:CHAIN_REFS], scratch[CHAIN_REFS:])
    i = pl.program_id(1)

    def band_start(blk):
        return jnp.clip(blk * tq - DIL_REACH, 0, t - band)

    start = pl.multiple_of(band_start(i), tq)
    table = (i * tq - start) // tq

    @pl.when((pl.program_id(0) == 0) & (i == 0))
    def _():
        def build(n, _):
            qi = lax.broadcasted_iota(jnp.int32, (tq, band), 0)
            kj = lax.broadcasted_iota(jnp.int32, (tq, band), 1)
            delta = kj - qi - n * tq
            ad = jnp.abs(delta)
            cnt = jnp.zeros((tq, band), F32)
            for window, dil in DIL_PATTERNS:
                ok = (ad <= window // 2) & ((delta & (dil - 1)) == 0)
                cnt = cnt + jnp.where(ok, 1.0, 0.0)
            tables_ref[n] = jnp.where(cnt > 2.5, math.log2(3.0),
                                      jnp.where(cnt > 1.5, 1.0,
                                                jnp.where(cnt > 0.5, 0.0, NEG_INF)))
            return 0

        lax.fori_loop(0, tables_ref.shape[0], build, 0)

    bias_ref = tables_ref.at[table]

    lane = lax.broadcasted_iota(jnp.int32, (1, 256), 1)

    def qms_for(hp):
        q = q_ref[...]
        return [jnp.where((lane >= (2 * hp + j) * DIL_DH) & (lane < (2 * hp + j + 1) * DIL_DH), q,
                          jnp.zeros_like(q)) for j in range(2)]

    def finish(hp, outs):
        for j in range(2):
            _place_head(acc_ref.at[hp], outs[j], j)

    acc_ref[...] = jnp.zeros_like(acc_ref)
    _attention_steps(DIL_HEADS // 2, qms_for, lambda hp: k_ref.at[pl.ds(start, band)],
                     lambda hp: [v_ref.at[2 * hp + j, pl.ds(start, band)] for j in range(2)],
                     finish, tq, band, band // 2, chains, slab=DIL_SLAB, bias_ref=bias_ref)
    o_ref[...] = jnp.concatenate([acc_ref[0], acc_ref[1]], axis=1).astype(o_ref.dtype)


DIL_SLAB = 16


def _dil_attention(proj, projv, nb, t):
    tq = 256
    band = min(t, tq + 2 * DIL_REACH)
    nq = t // tq
    kern = functools.partial(_dil_attn_kernel, t=t, tq=tq, band=band)
    return pl.pallas_call(
        kern,
        grid=(nb, nq),
        in_specs=[pl.BlockSpec((None, tq, 256), lambda b, i: (S_CQ, b * nq + i, 0)),
                  pl.BlockSpec((None, t, 256), lambda b, i: (S_CK, b, 0)),
                  pl.BlockSpec((DIL_HEADS, t, LANES), lambda b, i: (V_C // DIL_HEADS, b, 0))],
        out_specs=pl.BlockSpec((tq, 256), lambda b, i: (b * nq + i, 0)),
        out_shape=jax.ShapeDtypeStruct((nb * t, 256), BF16),
        scratch_shapes=([pltpu.VMEM((band // tq, tq, band), F32), pltpu.VMEM((2, tq, LANES), F32)]
                        + 2 * _chain_scratch(tq, band // 2)),
        compiler_params=_cparams(("arbitrary", "arbitrary")),
        name="dil_attn",
    )(proj, proj, projv)


def _na_bias_table(rpb):
    c = np.arange(GRID_W)
    cs = np.clip(c - NA_KC // 2, 0, GRID_W - NA_KC)
    colmask = (c[None, :] >= cs[:, None]) & (c[None, :] < cs[:, None] + NA_KC)
    rows = jnp.stack([rpb[:, si:si + NA_KR, :] for si in range(NA_KR)], axis=1).astype(F32)
    edge = GRID_W - NA_KC
    padded = jnp.concatenate([jnp.repeat(rows[..., :1], edge, axis=-1), rows,
                              jnp.repeat(rows[..., -1:], edge, axis=-1)], axis=-1)
    b = jnp.stack([padded[..., GRID_W - 1 - qc:2 * GRID_W - 1 - qc] for qc in range(GRID_W)],
                  axis=2)
    b = jnp.where(colmask[None, None, :, None, :], b * LOG2E, NEG_INF)
    b = b.transpose(1, 0, 2, 3, 4)
    return b.reshape(NA_KR, rpb.shape[0] * GRID_W, NA_KR * GRID_W)


def _na_attn_kernel(q_ref, k_ref, v_ref, tb_ref, o_ref, *, rows, rg):
    g = pl.program_id(1)
    lane = lax.broadcasted_iota(jnp.int32, (1, 256), 1)
    nk = NA_KR * GRID_W
    sels = [(lane >= h * NA_DH) & (lane < (h + 1) * NA_DH) for h in range(NA_HEADS)]

    def row(r, _):
        grow = g * rg + r
        rs = jnp.clip(grow - NA_KR // 2, 0, rows - NA_KR)
        si = rs - grow + (NA_KR - 1)
        q = q_ref[r * GRID_W:(r + 1) * GRID_W, :]
        koff = pl.multiple_of(rs * GRID_W, GRID_W)
        kb = k_ref[pl.ds(koff, nk), :]
        vb = v_ref[pl.ds(koff, nk), :]
        q4 = jnp.concatenate([jnp.where(sel, q, jnp.zeros_like(q)) for sel in sels], axis=0)
        s = lax.dot_general(q4, kb, (((1,), (1,)), ((), ())), preferred_element_type=F32)
        s = s + tb_ref[si]
        m = jnp.max(s, -1, keepdims=True)
        p = jnp.exp2(s - m)
        l = jnp.sum(p, -1, keepdims=True)
        o4 = jnp.dot(p.astype(BF16), vb, preferred_element_type=F32) * (1.0 / l)
        out = jnp.zeros((GRID_W, 256), F32)
        for h, sel in enumerate(sels):
            out = jnp.where(sel, o4[h * GRID_W:(h + 1) * GRID_W], out)
        o_ref[r * GRID_W:(r + 1) * GRID_W, :] = out.astype(o_ref.dtype)
        return 0

    for r in range(rg):
        row(r, 0)


def _na_attention(proj, tb, nb, t):
    rows = t // GRID_W
    assert rows >= NA_KR
    rg = 8
    ng = rows // rg
    tq = rg * GRID_W
    kern = functools.partial(_na_attn_kernel, rows=rows, rg=rg)
    return pl.pallas_call(
        kern,
        grid=(nb, ng),
        in_specs=[pl.BlockSpec((None, tq, 256), lambda b, i: (S_DQ, b * ng + i, 0)),
                  pl.BlockSpec((None, t, 256), lambda b, i: (S_DK, b, 0)),
                  pl.BlockSpec((None, t, 256), lambda b, i: (S_DV, b, 0)),
                  pl.BlockSpec(tb.shape, lambda b, i: (0, 0, 0))],
        out_specs=pl.BlockSpec((tq, 256), lambda b, i: (b * ng + i, 0)),
        out_shape=jax.ShapeDtypeStruct((nb * t, 256), BF16),
        compiler_params=_cparams(("arbitrary", "arbitrary")),
        name="na_attn",
    )(proj, proj, proj, tb)


def _layer_norm(y, g, b):
    mu = jnp.mean(y, -1, keepdims=True)
    yc = y - mu
    var = jnp.mean(yc * yc, -1, keepdims=True)
    return yc * lax.rsqrt(var + LN_EPS) * g + b


def _outproj_kernel(oa_ref, ob_ref, oc_ref, od_ref, x_ref, g1_ref, sc2_ref, sh2_ref, w_ref,
                    lg_ref, lb_ref, wr_ref, x1_ref, h2_ref, aff_ref):
    m = jnp.dot(oa_ref[...], w_ref[0:256, :], preferred_element_type=F32)
    m += jnp.dot(ob_ref[...], w_ref[256:512, :], preferred_element_type=F32)
    m += jnp.dot(oc_ref[...], w_ref[512:768, :], preferred_element_type=F32)
    m += jnp.dot(od_ref[...], w_ref[768:1024, :], preferred_element_type=F32)
    y = DEEPNORM_ALPHA * x_ref[...] + (1.0 + g1_ref[...]) * m
    x1 = _layer_norm(y, lg_ref[...], lb_ref[...])
    x1_ref[...] = x1
    h2 = x1 * (1.0 + sc2_ref[...]) + sh2_ref[...]
    h2_ref[...] = h2.astype(BF16)
    h_hi, h_lo = _split_bf16(h2)
    w_hi, w_lo = _split_bf16(wr_ref[...])
    nt = (((1,), (1,)), ((), ()))
    lg = lax.dot_general(w_hi, h_hi, nt, preferred_element_type=F32)
    lg += lax.dot_general(w_hi, h_lo, nt, preferred_element_type=F32)
    lg += lax.dot_general(w_lo, h_hi, nt, preferred_element_type=F32)
    lg = lg - jnp.max(lg, 0, keepdims=True)
    e = jnp.exp(lg)
    aff_ref[...] = e / jnp.sum(e, 0, keepdims=True)


def _outproj(oa, ob, oc, od, x, g1, sc2, sh2, w_out, ln_g, ln_b, w_router_t, nb, t):
    tm = 512
    nt = t // tm
    ntok = nb * t
    tok = lambda w: pl.BlockSpec((tm, w), lambda i: (i, 0))
    per_b = pl.BlockSpec((None, 1, D_MODEL), lambda i: (i // nt, 0, 0))
    full = lambda shape: pl.BlockSpec(shape, lambda i: tuple(0 for _ in shape))
    return pl.pallas_call(
        _outproj_kernel,
        grid=(ntok // tm,),
        in_specs=[tok(256), tok(256), tok(256), tok(256), tok(D_MODEL), per_b, per_b, per_b,
                  full((D_MODEL, D_MODEL)), full((1, D_MODEL)), full((1, D_MODEL)),
                  full((N_EXPERTS, D_MODEL))],
        out_specs=[tok(D_MODEL), tok(D_MODEL),
                   pl.BlockSpec((N_EXPERTS, tm), lambda i: (0, i))],
        out_shape=[jax.ShapeDtypeStruct((ntok, D_MODEL), F32),
                   jax.ShapeDtypeStruct((ntok, D_MODEL), BF16),
                   jax.ShapeDtypeStruct((N_EXPERTS, ntok), F32)],
        compiler_params=_cparams(("arbitrary",)),
        name="out_proj",
    )(oa, ob, oc, od, x, g1, sc2, sh2, w_out, ln_g, ln_b, w_router_t)


def _expert_kernel(x_ref, wg_ref, wu_ref, wd_ref, o_ref, acc_ref):
    f = pl.program_id(2)

    @pl.when(f == 0)
    def _():
        acc_ref[...] = jnp.zeros_like(acc_ref)

    x = x_ref[...]
    g = jnp.dot(x, wg_ref[...], preferred_element_type=F32)
    u = jnp.dot(x, wu_ref[...], preferred_element_type=F32)
    hmid = (g * (1.0 / (1.0 + jnp.exp(-g))) * u).astype(BF16)
    acc_ref[...] += jnp.dot(hmid, wd_ref[...], preferred_element_type=F32)

    @pl.when(f == pl.num_programs(2) - 1)
    def _():
        o_ref[...] = acc_ref[...].astype(o_ref.dtype)


def _experts(xe, slots, wg, wu, wd):
    ne, _, d = xe.shape
    tm = math.gcd(slots, 1024)
    tf = 256
    nf = D_FF // tf
    return pl.pallas_call(
        _expert_kernel,
        grid=(ne, slots // tm, nf),
        in_specs=[pl.BlockSpec((None, tm, d), lambda e, m, f: (e, m, 0)),
                  pl.BlockSpec((None, d, tf), lambda e, m, f: (e, 0, f)),
                  pl.BlockSpec((None, d, tf), lambda e, m, f: (e, 0, f)),
                  pl.BlockSpec((None, tf, d), lambda e, m, f: (e, f, 0))],
        out_specs=pl.BlockSpec((None, tm, d), lambda e, m, f: (e, m, 0)),
        out_shape=jax.ShapeDtypeStruct((ne, slots, d), BF16),
        scratch_shapes=[pltpu.VMEM((tm, d), F32)],
        compiler_params=_cparams(("arbitrary", "arbitrary", "arbitrary")),
        name="expert_ffn",
    )(xe, wg, wu, wd)


MOE_TM = 512
MOE_WIN = 128
ROW_ALIGN = 16


def _select_kernel(aff_ref, o_ref, *, cap):
    a = aff_ref[...]
    keys = lax.bitcast_convert_type(a, jnp.int32)
    ne, n = a.shape
    capf = float(cap)

    def count(mask):
        return jnp.sum(jnp.where(mask, 1.0, 0.0), axis=1, keepdims=True)

    def key_bit(b, thr):
        cand = thr | lax.shift_left(jnp.int32(1), 30 - b)
        return jnp.where(count(keys >= cand) >= capf, cand, thr)

    thr = lax.fori_loop(0, 31, key_bit, jnp.zeros((ne, 1), jnp.int32))
    above = keys > thr
    need = capf - count(above)
    idx = lax.broadcasted_iota(jnp.int32, (ne, n), 1)
    tie_idx = jnp.where(keys == thr, idx, jnp.int32(2 ** 30))
    nbits = max(1, (n - 1).bit_length())

    def idx_bit(b, j):
        cand = j | lax.shift_left(jnp.int32(1), nbits - 1 - b)
        return jnp.where(count(tie_idx < cand) < need, cand, j)

    j = lax.fori_loop(0, nbits, idx_bit, jnp.zeros((ne, 1), jnp.int32))
    sel = above | (tie_idx <= j)
    o_ref[...] = jnp.where(sel, a, -1.0)


def _select(aff, cap):
    ne, n = aff.shape
    return pl.pallas_call(
        functools.partial(_select_kernel, cap=cap),
        out_shape=jax.ShapeDtypeStruct((ne, n), F32),
        compiler_params=pltpu.CompilerParams(vmem_limit_bytes=VMEM_LIMIT),
        name="ec_select",
    )(aff)


def _routing_tables(gs, group_tokens):
    ne = gs.shape[0]
    a_l, off_l, cnt_l, lim_l = [], [], [], []
    tok0, slot0 = 0, 0
    for n in group_tokens:
        cap = EC_FACTOR * n // N_EXPERTS
        nt = n // MOE_TM
        sel = lax.slice_in_dim(gs, tok0, tok0 + n, axis=1) >= 0
        counts = jnp.sum(sel.reshape(ne, nt, MOE_TM), axis=-1, dtype=jnp.int32)
        s0 = slot0 + jnp.cumsum(counts, axis=1) - counts
        a = (s0 // ROW_ALIGN) * ROW_ALIGN
        a_l.append(a)
        off_l.append(s0 - a)
        cnt_l.append(counts)
        lim_l.append(jnp.full((nt,), slot0 + cap - MOE_WIN, jnp.int32))
        tok0 += n
        slot0 += cap
    a = jnp.concatenate(a_l, axis=1).T
    off = jnp.concatenate(off_l, axis=1).T
    end = off + jnp.concatenate(cnt_l, axis=1).T
    rounds = (end + MOE_WIN - 1) // MOE_WIN
    gp = (end // ROW_ALIGN) * ROW_ALIGN
    return dict(a=a.reshape(-1).astype(jnp.int32), nr=rounds.reshape(-1).astype(jnp.int32),
                nrounds=jnp.max(rounds, axis=1).astype(jnp.int32), lim=jnp.concatenate(lim_l),
                off_col=off.astype(F32)[:, :, None], off_row=off.astype(F32)[:, None, :],
                off16=jnp.repeat(off.astype(F32), ROW_ALIGN, axis=1)[:, :, None],
                gp16=jnp.repeat(gp.astype(F32), ROW_ALIGN, axis=1)[:, :, None],
                total=slot0)


def _dispatch_kernel(a_tab, nrounds, nr_tab, gs_ref, off_ref, off16_ref, gp16_ref, x_ref, u_ref, xe_ref,
                     pos_ref, c_ref, c2_ref, stage_ref, carry_ref, sem, rc_ref):
    j = pl.program_id(0)
    ne, tm = gs_ref.shape
    d = x_ref.shape[1]
    win = MOE_WIN

    @pl.when(j == 0)
    def _():
        carry_ref[...] = jnp.zeros_like(carry_ref)
        rc_ref[0] = 0

    sel = gs_ref[...] >= 0.0
    rank = jnp.dot(jnp.where(sel, 1.0, 0.0).astype(BF16), u_ref[...], preferred_element_type=F32)
    pos_ref[...] = jnp.where(sel, rank + off_ref[...], -1.0)

    def wait_round(slot):
        def one(_, c):
            pltpu.make_async_copy(stage_ref.at[slot, pl.ds(0, win)], xe_ref.at[0, pl.ds(0, win)],
                                  sem.at[slot]).wait()
            return c
        lax.fori_loop(0, rc_ref[1 + slot], one, 0)

    def round_body(r, _):
        k = lax.broadcasted_iota(jnp.int32, (win, tm), 0).astype(F32) + (r * win).astype(F32)
        for e in range(ne):
            c_ref[e * win:(e + 1) * win, :] = jnp.where(pos_ref[e:e + 1, :] == k, 1.0, 0.0).astype(BF16)
        slot = rc_ref[0] % 2
        for nb in range(d // MXU_DIM):
            cols = slice(nb * MXU_DIM, (nb + 1) * MXU_DIM)
            stage_ref[slot, :, cols] = jnp.dot(c_ref[...], x_ref[:, cols],
                                               preferred_element_type=F32).astype(BF16)

        @pl.when(r == 0)
        def _():
            k16 = lax.broadcasted_iota(jnp.int32, (ROW_ALIGN, 1), 0).astype(F32)
            for e in range(ne):
                keep = k16 < off16_ref[e * ROW_ALIGN:(e + 1) * ROW_ALIGN, :]
                rows = pl.ds(e * win, ROW_ALIGN)
                stage_ref[slot, rows, :] = jnp.where(
                    keep, carry_ref[e * ROW_ALIGN:(e + 1) * ROW_ALIGN, :], stage_ref[slot, rows, :])

        @pl.when(rc_ref[0] > 0)
        def _():
            wait_round(1 - slot)

        rc_ref[1 + slot] = 0
        for e in range(ne):
            @pl.when(r < nr_tab[j * ne + e])
            def _():
                dst = pl.multiple_of(a_tab[j * ne + e] + r * win, ROW_ALIGN)
                pltpu.make_async_copy(stage_ref.at[slot, pl.ds(e * win, win)],
                                      xe_ref.at[e, pl.ds(dst, win)], sem.at[slot]).start()
                rc_ref[1 + slot] = rc_ref[1 + slot] + 1
        rc_ref[0] = rc_ref[0] + 1
        return 0

    lax.fori_loop(0, nrounds[j], round_body, 0)

    k16 = lax.broadcasted_iota(jnp.int32, (ROW_ALIGN, 1), 0).astype(F32)
    for e in range(ne):
        rows = slice(e * ROW_ALIGN, (e + 1) * ROW_ALIGN)
        c2_ref[rows, :] = jnp.where(pos_ref[e:e + 1, :] == gp16_ref[rows, :] + k16, 1.0, 0.0).astype(BF16)
    kk = jnp.concatenate([k16] * ne, axis=0)
    keep_old = (gp16_ref[...] == 0.0) & (kk < off16_ref[...])
    for nb in range(d // MXU_DIM):
        cols = slice(nb * MXU_DIM, (nb + 1) * MXU_DIM)
        new = jnp.dot(c2_ref[...], x_ref[:, cols], preferred_element_type=F32).astype(BF16)
        carry_ref[:, cols] = jnp.where(keep_old, carry_ref[:, cols], new)

    @pl.when((j == pl.num_programs(0) - 1) & (rc_ref[0] > 0))
    def _():
        wait_round((rc_ref[0] - 1) % 2)


def _dispatch(gs, h2, tabs, u_mat):
    ne, ntok = gs.shape
    d = h2.shape[1]
    nt = ntok // MOE_TM
    rows = tabs['total'] + MOE_WIN
    grid_spec = pltpu.PrefetchScalarGridSpec(
        num_scalar_prefetch=3,
        grid=(nt,),
        in_specs=[pl.BlockSpec((ne, MOE_TM), lambda j, *_: (0, j)),
                  pl.BlockSpec((None, ne, 1), lambda j, *_: (j, 0, 0)),
                  pl.BlockSpec((None, ne * ROW_ALIGN, 1), lambda j, *_: (j, 0, 0)),
                  pl.BlockSpec((None, ne * ROW_ALIGN, 1), lambda j, *_: (j, 0, 0)),
                  pl.BlockSpec((MOE_TM, d), lambda j, *_: (j, 0)),
                  pl.BlockSpec((MOE_TM, MOE_TM), lambda j, *_: (0, 0))],
        out_specs=pl.BlockSpec(memory_space=pl.ANY),
        scratch_shapes=[pltpu.VMEM((ne, MOE_TM), F32),
                        pltpu.VMEM((ne * MOE_WIN, MOE_TM), BF16),
                        pltpu.VMEM((ne * ROW_ALIGN, MOE_TM), BF16),
                        pltpu.VMEM((2, ne * MOE_WIN, d), BF16),
                        pltpu.VMEM((ne * ROW_ALIGN, d), BF16),
                        pltpu.SemaphoreType.DMA((2,)),
                        pltpu.SMEM((3,), jnp.int32)])
    return pl.pallas_call(
        _dispatch_kernel,
        grid_spec=grid_spec,
        out_shape=jax.ShapeDtypeStruct((ne, rows, d), BF16),
        compiler_params=_cparams(("arbitrary",)),
        name="ec_dispatch",
    )(tabs['a'], tabs['nrounds'], tabs['nr'], gs, tabs['off_col'], tabs['off16'], tabs['gp16'], h2,
      u_mat)


def _combine_kernel(a_tab, nrounds, lim_tab, gs_ref, off_ref, l_ref, ye_ref, x1_ref, g2_ref, lg_ref,
                    lb_ref, *refs, split_tiles):
    outs, (p_ref, y_ref, acc_ref, sem) = refs[:-4], refs[-4:]
    j = pl.program_id(0)
    tm, ne = gs_ref.shape
    d = x1_ref.shape[1]
    win = MOE_WIN
    gs = gs_ref[...]
    sel = gs >= 0.0
    rank = jnp.dot(l_ref[...], jnp.where(sel, 1.0, 0.0).astype(BF16), preferred_element_type=F32)
    pos = jnp.where(sel, rank + off_ref[...], -1.0)
    gate = jnp.where(sel, gs, 0.0)
    acc_ref[...] = jnp.zeros_like(acc_ref)
    n_tiles = pl.num_programs(0)
    slot = j % 2

    def window(tile, r, e):
        want = a_tab[tile * ne + e] + r * win
        src = pl.multiple_of(jnp.minimum(want, lim_tab[tile]), ROW_ALIGN)
        return src, want - src

    def start_round(tile, r, sl):
        for e in range(ne):
            src, _ = window(tile, r, e)
            pltpu.make_async_copy(ye_ref.at[e, pl.ds(src, win)], y_ref.at[sl, pl.ds(e * win, win)],
                                  sem.at[sl]).start()

    def wait_round(sl):
        for e in range(ne):
            pltpu.make_async_copy(ye_ref.at[e, pl.ds(0, win)], y_ref.at[sl, pl.ds(e * win, win)],
                                  sem.at[sl]).wait()

    @pl.when((j == 0) & (nrounds[0] > 0))
    def _():
        start_round(0, 0, 0)

    nxt = jnp.minimum(j + 1, n_tiles - 1)

    @pl.when((j + 1 < n_tiles) & (nrounds[nxt] > 0))
    def _():
        start_round(nxt, 0, 1 - slot)

    half = ne // 2

    def accumulate(r):
        base = lax.convert_element_type(r * win, F32)
        k = lax.broadcasted_iota(jnp.int32, (tm, win), 1).astype(F32) + base
        for grp in range(2):
            for e in range(grp * half, (grp + 1) * half):
                pe = pos[:, e:e + 1]
                pe = jnp.where(pe >= base, pe + window(j, r, e)[1].astype(F32), -1.0)
                pcol = jnp.broadcast_to(pe, (tm, win))
                gcol = jnp.broadcast_to(gate[:, e:e + 1], (tm, win))
                p_ref[:, e * win:(e + 1) * win] = jnp.where(pcol == k, gcol, 0.0).astype(BF16)
            rows = slice(grp * half * win, (grp + 1) * half * win)
            for nb in range(d // MXU_DIM):
                cols = slice(nb * MXU_DIM, (nb + 1) * MXU_DIM)
                acc_ref[:, cols] += jnp.dot(p_ref[:, rows], y_ref[slot, rows, cols],
                                            preferred_element_type=F32)

    @pl.when(nrounds[j] > 0)
    def _():
        wait_round(slot)
        accumulate(0)

    def extra_round(r, _):
        start_round(j, r, slot)
        wait_round(slot)
        accumulate(r)
        return 0

    lax.fori_loop(1, nrounds[j], extra_round, 0)
    y = DEEPNORM_ALPHA * x1_ref[...] + (1.0 + g2_ref[...]) * acc_ref[...]
    res = _layer_norm(y, lg_ref[...], lb_ref[...])
    if split_tiles is None:
        outs[0][...] = res
    else:
        @pl.when(j < split_tiles)
        def _():
            outs[0][...] = res

        @pl.when(j >= split_tiles)
        def _():
            outs[1][...] = res


def _combine_postnorm(gs_tok, ye, tabs, l_mat, x1, g2, ln_g, ln_b, t, split_tiles=None):
    ntok, ne = gs_tok.shape
    d = x1.shape[1]
    nt = ntok // MOE_TM
    tiles_per_seq = t // MOE_TM
    if split_tiles is None:
        out_specs = pl.BlockSpec((MOE_TM, d), lambda j, *_: (j, 0))
        out_shape = jax.ShapeDtypeStruct((ntok, d), F32)
    else:
        out_specs = [pl.BlockSpec((MOE_TM, d), lambda j, *_: (jnp.minimum(j, split_tiles - 1), 0)),
                     pl.BlockSpec((MOE_TM, d), lambda j, *_: (jnp.maximum(j - split_tiles, 0), 0))]
        out_shape = [jax.ShapeDtypeStruct((split_tiles * MOE_TM, d), F32),
                     jax.ShapeDtypeStruct((ntok - split_tiles * MOE_TM, d), F32)]
    grid_spec = pltpu.PrefetchScalarGridSpec(
        num_scalar_prefetch=3,
        grid=(nt,),
        in_specs=[pl.BlockSpec((MOE_TM, ne), lambda j, *_: (j, 0)),
                  pl.BlockSpec((None, 1, ne), lambda j, *_: (j, 0, 0)),
                  pl.BlockSpec((MOE_TM, MOE_TM), lambda j, *_: (0, 0)),
                  pl.BlockSpec(memory_space=pl.ANY),
                  pl.BlockSpec((MOE_TM, d), lambda j, *_: (j, 0)),
                  pl.BlockSpec((None, 1, d), lambda j, *_: (j // tiles_per_seq, 0, 0)),
                  pl.BlockSpec((1, d), lambda j, *_: (0, 0)),
                  pl.BlockSpec((1, d), lambda j, *_: (0, 0))],
        out_specs=out_specs,
        scratch_shapes=[pltpu.VMEM((MOE_TM, ne * MOE_WIN), BF16),
                        pltpu.VMEM((2, ne * MOE_WIN, d), BF16),
                        pltpu.VMEM((MOE_TM, d), F32),
                        pltpu.SemaphoreType.DMA((2,))])
    return pl.pallas_call(
        functools.partial(_combine_kernel, split_tiles=split_tiles),
        grid_spec=grid_spec,
        out_shape=out_shape,
        compiler_params=_cparams(("arbitrary",)),
        name="ec_combine",
    )(tabs['a'], tabs['nrounds'], tabs['lim'], gs_tok, tabs['off_row'], l_mat, ye, x1, g2, ln_g, ln_b)


def _prep_w_in(w_in_l):
    sizes = (256, 256, 256, MLA_Q_RANK, MLA_KV_RANK, MLA_ROPE, 256, 256, 256, 256, 256, 256)
    offs = np.concatenate([[0], np.cumsum(sizes)])
    part = [w_in_l[:, offs[i]:offs[i + 1]] for i in range(len(sizes))]
    a_q, a_k, a_v, b_cq, b_ckv, b_kr, c_q, c_k, c_v, d_q, d_k, d_v = part
    d = w_in_l.shape[0]
    zeros = lambda n: jnp.zeros((d, n), w_in_l.dtype)
    def per_head(v):
        out = []
        for h in range(4):
            out += [v[:, h * 64:(h + 1) * 64], zeros(LANES - 64)]
        return out

    cols = ([a_q, a_k] + per_head(a_v) + [c_q, c_k] + per_head(c_v)
            + [d_q, d_k, d_v, b_cq, zeros(256 - MLA_Q_RANK), b_ckv, b_kr, b_kr,
               zeros(128 - 2 * MLA_ROPE)])
    return jnp.concatenate(cols, axis=1).astype(BF16)


def _prep_w_uq(w_uq_l):
    hd = MLA_NOPE + MLA_ROPE
    nope = [w_uq_l[:, h * hd:h * hd + MLA_NOPE] for h in range(MLA_HEADS)]
    rope = [w_uq_l[:, h * hd + MLA_NOPE:(h + 1) * hd] for h in range(MLA_HEADS)]
    z = jnp.zeros((w_uq_l.shape[0], 256 - 2 * hd), w_uq_l.dtype)
    cols = []
    for p in range(2):
        cols += [nope[2 * p], nope[2 * p + 1], rope[2 * p], rope[2 * p + 1], z]
    w = jnp.concatenate(cols, axis=1)
    w = jnp.concatenate([w, jnp.zeros((256 - MLA_Q_RANK, w.shape[1]), w.dtype)], axis=0)
    return w.astype(BF16)


def _prep_w_ukv(w_ukv_l):
    hd = MLA_NOPE + MLA_DV
    kn = [w_ukv_l[:, h * hd:h * hd + MLA_NOPE] for h in range(MLA_HEADS)]
    z = jnp.zeros((w_ukv_l.shape[0], LANES - MLA_DV), w_ukv_l.dtype)
    vv = []
    for h in range(MLA_HEADS):
        vv += [w_ukv_l[:, h * hd + MLA_NOPE:(h + 1) * hd], z]
    return jnp.concatenate(kn + vv, axis=1).astype(BF16)


def _select_groups(aff_t, group_tokens):
    parts, off = [], 0
    for n in group_tokens:
        parts.append(_select(lax.slice_in_dim(aff_t, off, off + n, axis=1),
                             EC_FACTOR * n // N_EXPERTS))
        off += n
    return jnp.concatenate(parts, axis=1)


def _trunk(x, c, group_tokens, nb, t, p):
    ntok = nb * t
    tabs = (_rope_tables(t, 256, DA_DQK, DA_DQK // ROPE_FRACTION, ROPE_THETA)
            + _rope_tables(t, 256, DIL_DH, DIL_DH // ROPE_FRACTION, ROPE_THETA)
            + _rope_tables(t, 128, MLA_ROPE, MLA_ROPE, MLA_ROPE_THETA))
    gmat = jnp.asarray(np.kron(np.eye(4), np.full((64, 64), 1.0 / 64)), BF16)
    ti = jnp.arange(MOE_TM)
    u_mat = (ti[:, None] < ti[None, :]).astype(BF16)
    l_mat = (ti[None, :] < ti[:, None]).astype(BF16)
    for l in range(DEPTH):
        mod = _modulation(c, p['w_ada'][l], p['b_ada'][l])
        sh1, sc1, g1, sh2, sc2, g2 = [m.reshape(nb, 1, D_MODEL) for m in jnp.split(mod, 6, axis=-1)]
        gq = jnp.concatenate([p['q_norm_g'][l], jnp.zeros((256 - MLA_Q_RANK,), F32)]).reshape(1, 256)
        gkv = p['kv_norm_g'][l].reshape(1, 128)
        proj, projv = _inproj(x, sc1, sh1, _prep_w_in(p['w_in'][l]), _prep_w_uq(p['w_uq'][l]),
                              _prep_w_ukv(p['w_ukv'][l]), gq, gkv, tabs, nb, t)
        lam_init = 0.8 - 0.6 * math.exp(-0.3 * l)
        lam = (jnp.exp(jnp.sum(p['da_lq1'][l] * p['da_lk1'][l]))
               - jnp.exp(jnp.sum(p['da_lq2'][l] * p['da_lk2'][l])) + lam_init).reshape(1)
        g_sub = jnp.tile(p['da_subln_g'][l], DA_HEADS).reshape(1, 256)
        oa = _diff_attention(proj, projv, lam, g_sub, gmat, nb, t, 1.0 - lam_init)
        ob = _mla_attention(proj, projv, nb, t)
        oc = _dil_attention(proj, projv, nb, t)
        od = _na_attention(proj, _na_bias_table(p['na_rpb'][l]), nb, t)
        x1, h2, aff_t = _outproj(oa, ob, oc, od, x, g1, sc2, sh2, p['w_out'][l].astype(BF16),
                                 p['ln1_g'][l].reshape(1, -1), p['ln1_b'][l].reshape(1, -1),
                                 p['w_router'][l].T, nb, t)
        wg = p['w_e_gate'][l].astype(BF16)
        wu = p['w_e_up'][l].astype(BF16)
        wd = p['w_e_down'][l].astype(BF16)
        gs = _select_groups(aff_t, group_tokens)
        rt = _routing_tables(gs, group_tokens)
        xe = _dispatch(gs, h2, rt, u_mat)
        ye = _experts(xe, rt['total'], wg, wu, wd)
        split = group_tokens[0] // MOE_TM if l == DEPTH - 1 else None
        x = _combine_postnorm(gs.T, ye, rt, l_mat, x1, g2, p['ln2_g'][l].reshape(1, -1),
                              p['ln2_b'][l].reshape(1, -1), t, split_tiles=split)
    return x


def kernel(x_prompt, x_sample, c_prompt, c_sample, w_in, w_uq, w_ukv, q_norm_g, kv_norm_g, da_lq1,
           da_lk1, da_lq2, da_lk2, da_subln_g, na_rpb, w_out, w_ada, b_ada, ln1_g, ln1_b, ln2_g,
           ln2_b, w_router, w_e_gate, w_e_up, w_e_down):
    p = dict(w_in=w_in, w_uq=w_uq, w_ukv=w_ukv, q_norm_g=q_norm_g, kv_norm_g=kv_norm_g,
             da_lq1=da_lq1, da_lk1=da_lk1, da_lq2=da_lq2, da_lk2=da_lk2, da_subln_g=da_subln_g,
             na_rpb=na_rpb, w_out=w_out, w_ada=w_ada, b_ada=b_ada, ln1_g=ln1_g, ln1_b=ln1_b,
             ln2_g=ln2_g, ln2_b=ln2_b, w_router=w_router, w_e_gate=w_e_gate, w_e_up=w_e_up,
             w_e_down=w_e_down)
    bp, t, d = x_prompt.shape
    bs = x_sample.shape[0]
    assert x_sample.shape[1] == t
    nb = bp + bs
    x = jnp.concatenate([x_prompt.reshape(bp * t, d), x_sample.reshape(bs * t, d)], axis=0)
    c = jnp.concatenate([c_prompt, c_sample], axis=0)
    y_prompt, y_sample = _trunk(x, c, (bp * t, bs * t), nb, t, p)
    return y_prompt.reshape(bp, t, d), y_sample.reshape(bs, t, d)
```

```python
import functools
import math

import jax
import jax.numpy as jnp
import numpy as np
from jax import lax
from jax.experimental import pallas as pl
from jax.experimental.pallas import tpu as pltpu

F32 = jnp.float32
BF16 = jnp.bfloat16

D_MODEL = 1024
DEPTH = 2
GRID_W = 64
GROUP_W = 256
DA_HEADS, DA_DV, DA_DQK = 4, 64, 32
MLA_HEADS, MLA_Q_RANK, MLA_KV_RANK, MLA_NOPE, MLA_ROPE, MLA_DV = 4, 192, 128, 64, 32, 64
MLA_ROPE_THETA = 10000.0
DIL_HEADS, DIL_DH = 4, 64
DIL_PATTERNS = ((128, 1), (512, 4), (2048, 16))
NA_HEADS, NA_DH, NA_KR, NA_KC = 4, 64, 8, 16
ROPE_THETA = 500000.0
ROPE_FRACTION = 4
N_EXPERTS = 16
EC_FACTOR = 2
D_FF = 2816
DEEPNORM_ALPHA = (2.0 * DEPTH) ** 0.25
NEG_INF = -1e30
LOG2E = math.log2(math.e)
LN_EPS = 1e-5
RMS_EPS = 1e-6

LANES = 128
MXU_DIM = 256
VMEM_LIMIT = 56 * 1024 * 1024

S_AQ, S_AK, S_BQ0, S_BQ1, S_BK0, S_BK1, S_CQ, S_CK, S_DQ, S_DK, S_DV = range(11)
N_SLOTS = 11
V_A, V_B, V_C = 0, 4, 8
N_VSLOTS = 12
V_ONE_LANE = 64
W_ALL_COLS = 13 * 256


def _cparams(sem):
    return pltpu.CompilerParams(dimension_semantics=sem, vmem_limit_bytes=VMEM_LIMIT)


def _split_bf16(a):
    hi = a.astype(BF16)
    lo = (a - hi.astype(F32)).astype(BF16)
    return hi, lo


def _mod_kernel(c_ref, w_ref, b_ref, o_ref):
    c = c_ref[...]
    a = c * (1.0 / (1.0 + jnp.exp(-c)))
    a_hi, a_lo = _split_bf16(a)
    w_hi, w_lo = _split_bf16(w_ref[...])
    acc = jnp.dot(a_hi, w_hi, preferred_element_type=F32)
    acc += jnp.dot(a_hi, w_lo, preferred_element_type=F32)
    acc += jnp.dot(a_lo, w_hi, preferred_element_type=F32)
    o_ref[...] = acc + b_ref[...]


def _modulation(c, w_ada, b_ada):
    nb, d = c.shape
    n_out = w_ada.shape[1]
    tn = 1536
    return pl.pallas_call(
        _mod_kernel,
        grid=(n_out // tn,),
        in_specs=[pl.BlockSpec((nb, d), lambda j: (0, 0)),
                  pl.BlockSpec((d, tn), lambda j: (0, j)),
                  pl.BlockSpec((1, tn), lambda j: (0, j))],
        out_specs=pl.BlockSpec((nb, tn), lambda j: (0, j)),
        out_shape=jax.ShapeDtypeStruct((nb, n_out), F32),
        compiler_params=_cparams(("arbitrary",)),
        name="adaln_mod",
    )(c, w_ada, b_ada.reshape(1, n_out))


def _rope_tables(t, width, group, rot, theta):
    half = rot // 2
    inv = theta ** (-jnp.arange(half, dtype=F32) / half)
    ang = jnp.arange(t, dtype=F32)[:, None] * inv[None, :]
    cos, sin = jnp.cos(ang), jnp.sin(ang)
    ones = jnp.ones((t, group - rot), F32)
    zeros = jnp.zeros((t, group - rot), F32)
    c_g = jnp.concatenate([cos, cos, ones], axis=1)
    s_g = jnp.concatenate([-sin, sin, zeros], axis=1)
    reps = width // group
    return jnp.tile(c_g, (1, reps)), jnp.tile(s_g, (1, reps))


def _apply_rope(x, c_tab, s_tab, group, rot):
    width = x.shape[-1]
    half = rot // 2
    lane = lax.broadcasted_iota(jnp.int32, (1, width), 1)
    first = (lane % group) < half
    fwd = pltpu.roll(x, width - half, 1)
    bwd = pltpu.roll(x, half, 1)
    return x * c_tab + s_tab * jnp.where(first, fwd, bwd)


def _inproj_kernel(x_ref, sc_ref, sh_ref, w_ref, wuq_ref, wukv_ref, gq_ref, gkv_ref,
                   ca_ref, sa_ref, cc_ref, scc_ref, cm_ref, sm_ref, o_ref, ov_ref):
    h = (x_ref[...] * (1.0 + sc_ref[...]) + sh_ref[...]).astype(BF16)

    def proj(col, width):
        return jnp.dot(h, w_ref[:, col:col + width], preferred_element_type=F32)

    one_lane = lax.broadcasted_iota(jnp.int32, (1, LANES), 1) == V_ONE_LANE

    def put_values(first_slot, vals):
        for hh in range(4):
            v = vals[:, LANES * hh:LANES * (hh + 1)]
            ov_ref[first_slot + hh] = jnp.where(one_lane, 1.0, v).astype(BF16)

    sa_scale = DA_DQK ** -0.5 * LOG2E
    sb_scale = (MLA_NOPE + MLA_ROPE) ** -0.5 * LOG2E
    sc_scale = DIL_DH ** -0.5 * LOG2E
    sd_scale = NA_DH ** -0.5 * LOG2E
    a_rot = DA_DQK // ROPE_FRACTION
    c_rot = DIL_DH // ROPE_FRACTION

    ca, sa = ca_ref[...], sa_ref[...]
    o_ref[S_AQ] = (_apply_rope(proj(0, 256), ca, sa, DA_DQK, a_rot) * sa_scale).astype(BF16)
    o_ref[S_AK] = _apply_rope(proj(256, 256), ca, sa, DA_DQK, a_rot).astype(BF16)
    put_values(V_A, proj(512, 512))
    cc, scc = cc_ref[...], scc_ref[...]
    o_ref[S_CQ] = (_apply_rope(proj(1024, 256), cc, scc, DIL_DH, c_rot) * sc_scale).astype(BF16)
    o_ref[S_CK] = _apply_rope(proj(1280, 256), cc, scc, DIL_DH, c_rot).astype(BF16)
    put_values(V_C, proj(1536, 512))
    o_ref[S_DQ] = (proj(2048, 256) * sd_scale).astype(BF16)
    o_ref[S_DK] = proj(2304, 256).astype(BF16)
    o_ref[S_DV] = proj(2560, 256).astype(BF16)

    cm, sm = cm_ref[...], sm_ref[...]
    cq = proj(2816, 256)
    cq = cq * lax.rsqrt(jnp.sum(cq * cq, -1, keepdims=True) * (1.0 / MLA_Q_RANK) + RMS_EPS)
    cq = (cq * gq_ref[...]).astype(BF16)
    q2 = jnp.dot(cq, wuq_ref[...], preferred_element_type=F32)
    for p in range(2):
        qp = q2[:, 256 * p:256 * (p + 1)]
        o_ref[S_BQ0 + p, :, 0:128] = (qp[:, 0:128] * sb_scale).astype(BF16)
        o_ref[S_BQ0 + p, :, 128:256] = (
            _apply_rope(qp[:, 128:256], cm, sm, MLA_ROPE, MLA_ROPE) * sb_scale).astype(BF16)
    ckv = proj(3072, 128)
    ckv = ckv * lax.rsqrt(jnp.mean(ckv * ckv, -1, keepdims=True) + RMS_EPS)
    ckv = (ckv * gkv_ref[...]).astype(BF16)
    kv = jnp.dot(ckv, wukv_ref[...], preferred_element_type=F32)
    kr = _apply_rope(proj(3200, 128), cm, sm, MLA_ROPE, MLA_ROPE).astype(BF16)
    for p in range(2):
        o_ref[S_BK0 + p, :, 0:128] = kv[:, 128 * p:128 * (p + 1)].astype(BF16)
        o_ref[S_BK0 + p, :, 128:256] = kr
    put_values(V_B, kv[:, 256:768])


def _inproj(x, sc, sh, w_all, wuq, wukv, gq, gkv, tabs, nb, t):
    tm = 512
    nt = t // tm
    ntok = nb * t
    ca, sa, cc, scc, cm, sm = tabs
    full = lambda shape: pl.BlockSpec(shape, lambda j, b: tuple(0 for _ in shape))
    tab = lambda w: pl.BlockSpec((tm, w), lambda j, b: (j, 0))
    return pl.pallas_call(
        _inproj_kernel,
        grid=(nt, nb),
        in_specs=[pl.BlockSpec((tm, D_MODEL), lambda j, b: (b * nt + j, 0)),
                  pl.BlockSpec((None, 1, D_MODEL), lambda j, b: (b, 0, 0)),
                  pl.BlockSpec((None, 1, D_MODEL), lambda j, b: (b, 0, 0)),
                  full((D_MODEL, W_ALL_COLS)), full((256, 512)), full((128, 768)),
                  full((1, 256)), full((1, 128)),
                  tab(256), tab(256), tab(256), tab(256), tab(128), tab(128)],
        out_specs=[pl.BlockSpec((N_SLOTS, tm, 256), lambda j, b: (0, b * nt + j, 0)),
                   pl.BlockSpec((N_VSLOTS, tm, LANES), lambda j, b: (0, b * nt + j, 0))],
        out_shape=[jax.ShapeDtypeStruct((N_SLOTS, ntok, 256), BF16),
                   jax.ShapeDtypeStruct((N_VSLOTS, ntok, LANES), BF16)],
        compiler_params=_cparams(("arbitrary", "arbitrary")),
        name="in_proj",
    )(x, sc, sh, w_all, wuq, wukv, gq, gkv, ca, sa, cc, scc, cm, sm)


def _lane_mask(width, ranges):
    lane = lax.broadcasted_iota(jnp.int32, (1, width), 1)
    m = None
    for lo, hi in ranges:
        r = (lane >= lo) & (lane < hi)
        m = r if m is None else (m | r)
    return m


SOFTMAX_SLAB = 32


def _chain_scratch(tq, kc):
    return [pltpu.VMEM((tq, kc), F32), pltpu.VMEM((tq, kc), F32), pltpu.VMEM((tq, kc), BF16),
            pltpu.VMEM((tq, LANES), F32), pltpu.VMEM((tq, LANES), F32), pltpu.VMEM((tq, LANES), F32)]


CHAIN_REFS = 6


def _attention_steps(n_steps, qms_for, k_for, v_for, finish, tq, t, kc, chains, slab=SOFTMAX_SLAB,
                     bias_ref=None, step_unroll=1):
    n_chunks = t // kc
    assert n_chunks % 2 == 0
    nt = (((1,), (1,)), ((), ()))

    def chunk_rows(chunk):
        if isinstance(chunk, int):
            return pl.ds(chunk * kc, kc)
        return pl.ds(pl.multiple_of(chunk * kc, kc), kc)

    def scores(step, chunk, slot):
        k = k_for(step)[chunk_rows(chunk), :]
        for qm, chain in zip(qms_for(step), chains):
            chain[slot][...] = lax.dot_general(qm, k, nt, preferred_element_type=F32)

    def softmax_pv(step, chunk, slot):
        for v_ref, chain in zip(v_for(step), chains):
            v = v_ref[chunk_rows(chunk), :]
            s_ref = chain[slot]
            (p_ref, m_ref, a_ref, acc_ref) = chain[2:]
            for r in range(tq // slab):
                rows = slice(r * slab, (r + 1) * slab)
                s = s_ref[rows, :]
                if bias_ref is not None:
                    s = s + bias_ref[rows, chunk * kc:(chunk + 1) * kc]
                m_prev = m_ref[rows, :]
                m_new = jnp.maximum(m_prev, jnp.max(s, -1, keepdims=True))
                d = s - jnp.concatenate([m_new] * (kc // LANES), axis=1)
                p_ref[rows, :] = jnp.exp2(d.astype(BF16))
                a_ref[rows, :] = jnp.exp2(m_prev - m_new)
                m_ref[rows, :] = m_new
            acc_ref[...] = a_ref[...] * acc_ref[...] + jnp.dot(p_ref[...], v,
                                                              preferred_element_type=F32)

    def step_body(st, _):
        for (_, _, _, m_ref, _, acc_ref) in chains:
            m_ref[...] = jnp.full(m_ref.shape, NEG_INF, F32)
            acc_ref[...] = jnp.zeros(acc_ref.shape, F32)
        nxt = min(st + 1, n_steps - 1) if isinstance(st, int) else jnp.minimum(st + 1, n_steps - 1)

        def body(j, _):
            scores(st, 2 * j + 1, 1)
            softmax_pv(st, 2 * j, 0)
            if n_chunks == 2:
                if n_steps > 1:
                    scores(nxt, 0, 0)
            else:
                last = 2 * j + 2 == n_chunks
                scores(jnp.where(last, nxt, st), jnp.where(last, 0, 2 * j + 2), 0)
            softmax_pv(st, 2 * j + 1, 1)
            return 0

        if n_chunks == 2:
            body(0, 0)
        else:
            assert bias_ref is None
            lax.fori_loop(0, n_chunks // 2, body, 0, unroll=4)
        outs = []
        for chain in chains:
            acc = chain[5][...]
            den = jnp.broadcast_to(acc[:, V_ONE_LANE:V_ONE_LANE + 1], acc.shape)
            outs.append(acc * (1.0 / den))
        finish(st, outs)
        return 0

    scores(0, 0, 0)
    if step_unroll == n_steps:
        for st in range(n_steps):
            step_body(st, 0)
    else:
        lax.fori_loop(0, n_steps, step_body, 0, unroll=step_unroll)


def _place_head(pair_ref, o, j):
    lane = lax.broadcasted_iota(jnp.int32, (1, LANES), 1)
    low = lane < V_ONE_LANE
    shifted = pltpu.roll(o, V_ONE_LANE, 1)
    cur = pair_ref[...]
    pair_ref[...] = jnp.where(low, jnp.where(j == 0, o, cur), jnp.where(j == 1, shifted, cur))


def _group_mean_sq(x, gmat):
    sq = x * x
    hi, lo = _split_bf16(sq)
    return (jnp.dot(hi, gmat, preferred_element_type=F32)
            + jnp.dot(lo, gmat, preferred_element_type=F32))


def _diff_attn_kernel(lam_ref, q_ref, k_ref, v_ref, g_ref, gmat_ref, o_ref, acc_ref, *scratch, t, kc,
                      out_scale):
    chains = (scratch[:CHAIN_REFS], scratch[CHAIN_REFS:])
    lam = lam_ref[0]
    lane = lax.broadcasted_iota(jnp.int32, (1, 256), 1)

    def qms_for(h):
        q = q_ref[...]
        return [jnp.where((lane >= (2 * h + c) * DA_DQK) & (lane < (2 * h + c + 1) * DA_DQK), q,
                          jnp.zeros_like(q)) for c in range(2)]

    def finish(h, outs):
        _place_head(acc_ref.at[h // 2], outs[0] - lam * outs[1], h % 2)

    acc_ref[...] = jnp.zeros_like(acc_ref)
    _attention_steps(DA_HEADS, qms_for, lambda h: k_ref, lambda h: [v_ref.at[h], v_ref.at[h]],
                     finish, q_ref.shape[0], t, kc, chains)
    o = jnp.concatenate([acc_ref[0], acc_ref[1]], axis=1)
    ms = _group_mean_sq(o, gmat_ref[...])
    o_ref[...] = (o * lax.rsqrt(ms + RMS_EPS) * g_ref[...] * out_scale).astype(o_ref.dtype)


def _diff_attention(proj, projv, lam, g_tiled, gmat, nb, t, out_scale):
    tq, kc = 512, 512
    nq = t // tq
    kern = functools.partial(_diff_attn_kernel, t=t, kc=kc, out_scale=out_scale)
    return pl.pallas_call(
        kern,
        grid=(nb, nq),
        in_specs=[pl.BlockSpec(memory_space=pltpu.SMEM),
                  pl.BlockSpec((None, tq, 256), lambda b, i: (S_AQ, b * nq + i, 0)),
                  pl.BlockSpec((None, t, 256), lambda b, i: (S_AK, b, 0)),
                  pl.BlockSpec((DA_HEADS, t, LANES), lambda b, i: (V_A // DA_HEADS, b, 0)),
                  pl.BlockSpec((1, 256), lambda b, i: (0, 0)),
                  pl.BlockSpec((256, 256), lambda b, i: (0, 0))],
        out_specs=pl.BlockSpec((tq, 256), lambda b, i: (b * nq + i, 0)),
        out_shape=jax.ShapeDtypeStruct((nb * t, 256), BF16),
        scratch_shapes=[pltpu.VMEM((2, tq, LANES), F32)] + 2 * _chain_scratch(tq, kc),
        compiler_params=_cparams(("arbitrary", "arbitrary")),
        name="diff_attn",
    )(lam, proj, proj, projv, g_tiled, gmat)


def _mla_attn_kernel(q_ref, k_ref, v_ref, o_ref, acc_ref, *scratch, t, kc):
    chains = (scratch[:CHAIN_REFS], scratch[CHAIN_REFS:])
    lane = lax.broadcasted_iota(jnp.int32, (1, 256), 1)

    def qms_for(p):
        q = q_ref[p]
        qms = []
        for j in range(2):
            nope = (lane >= j * MLA_NOPE) & (lane < (j + 1) * MLA_NOPE)
            rope = (lane >= 128 + j * MLA_ROPE) & (lane < 128 + (j + 1) * MLA_ROPE)
            qms.append(jnp.where(nope | rope, q, jnp.zeros_like(q)))
        return qms

    def finish(p, outs):
        for j in range(2):
            _place_head(acc_ref.at[p], outs[j], j)

    acc_ref[...] = jnp.zeros_like(acc_ref)
    _attention_steps(MLA_HEADS // 2, qms_for, lambda p: k_ref.at[p],
                     lambda p: [v_ref.at[2 * p], v_ref.at[2 * p + 1]], finish, q_ref.shape[1], t, kc,
                     chains)
    o_ref[...] = jnp.concatenate([acc_ref[0], acc_ref[1]], axis=1).astype(o_ref.dtype)


def _mla_attention(proj, projv, nb, t):
    tq, kc = 512, 512
    nq = t // tq
    kern = functools.partial(_mla_attn_kernel, t=t, kc=kc)
    return pl.pallas_call(
        kern,
        grid=(nb, nq),
        in_specs=[pl.BlockSpec((2, tq, 256), lambda b, i: (S_BQ0 // 2, b * nq + i, 0)),
                  pl.BlockSpec((2, t, 256), lambda b, i: (S_BK0 // 2, b, 0)),
                  pl.BlockSpec((MLA_HEADS, t, LANES), lambda b, i: (V_B // MLA_HEADS, b, 0))],
        out_specs=pl.BlockSpec((tq, 256), lambda b, i: (b * nq + i, 0)),
        out_shape=jax.ShapeDtypeStruct((nb * t, 256), BF16),
        scratch_shapes=[pltpu.VMEM((2, tq, LANES), F32)] + 2 * _chain_scratch(tq, kc),
        compiler_params=_cparams(("arbitrary", "arbitrary")),
        name="mla_attn",
    )(proj, proj, projv)


DIL_REACH = max(w // 2 for w, _ in DIL_PATTERNS)


def _dil_attn_kernel(q_ref, k_ref, v_ref, o_ref, tables_ref, acc_ref, *scratch, t, tq, band):
    chains = tuple(scratch[c * CHAIN_REFS:(c + 1) * CHAIN_REFS] for c in range(DIL_HEADS))
    i = pl.program_id(1)

    def band_start(blk):
        return jnp.clip(blk * tq - DIL_REACH, 0, t - band)

    start = pl.multiple_of(band_start(i), tq)
    table = (i * tq - start) // tq

    @pl.when((pl.program_id(0) == 0) & (i == 0))
    def _():
        def build(n, _):
            qi = lax.broadcasted_iota(jnp.int32, (tq, band), 0)
            kj = lax.broadcasted_iota(jnp.int32, (tq, band), 1)
            delta = kj - qi - n * tq
            ad = jnp.abs(delta)
            cnt = jnp.zeros((tq, band), F32)
            for window, dil in DIL_PATTERNS:
                ok = (ad <= window // 2) & ((delta & (dil - 1)) == 0)
                cnt = cnt + jnp.where(ok, 1.0, 0.0)
            tables_ref[n] = jnp.where(cnt > 2.5, math.log2(3.0),
                                      jnp.where(cnt > 1.5, 1.0,
                                                jnp.where(cnt > 0.5, 0.0, NEG_INF)))
            return 0

        lax.fori_loop(0, tables_ref.shape[0], build, 0)

    bias_ref = tables_ref.at[table]

    lane = lax.broadcasted_iota(jnp.int32, (1, 256), 1)

    def qms_for(_):
        q = q_ref[...]
        return [jnp.where((lane >= h * DIL_DH) & (lane < (h + 1) * DIL_DH), q, jnp.zeros_like(q))
                for h in range(DIL_HEADS)]

    def finish(_, outs):
        for h in range(DIL_HEADS):
            _place_head(acc_ref.at[h // 2], outs[h], h % 2)

    acc_ref[...] = jnp.zeros_like(acc_ref)
    _attention_steps(1, qms_for, lambda _: k_ref.at[pl.ds(start, band)],
                     lambda _: [v_ref.at[h, pl.ds(start, band)] for h in range(DIL_HEADS)],
                     finish, tq, band, band // 2, chains, slab=DIL_SLAB, bias_ref=bias_ref,
                     step_unroll=1)
    o_ref[...] = jnp.concatenate([acc_ref[0], acc_ref[1]], axis=1).astype(o_ref.dtype)


DIL_SLAB = 16


def _dil_attention(proj, projv, nb, t):
    tq = 256
    band = min(t, tq + 2 * DIL_REACH)
    nq = t // tq
    kern = functools.partial(_dil_attn_kernel, t=t, tq=tq, band=band)
    return pl.pallas_call(
        kern,
        grid=(nb, nq),
        in_specs=[pl.BlockSpec((None, tq, 256), lambda b, i: (S_CQ, b * nq + i, 0)),
                  pl.BlockSpec((None, t, 256), lambda b, i: (S_CK, b, 0)),
                  pl.BlockSpec((DIL_HEADS, t, LANES), lambda b, i: (V_C // DIL_HEADS, b, 0))],
        out_specs=pl.BlockSpec((tq, 256), lambda b, i: (b * nq + i, 0)),
        out_shape=jax.ShapeDtypeStruct((nb * t, 256), BF16),
        scratch_shapes=([pltpu.VMEM((band // tq, tq, band), F32), pltpu.VMEM((2, tq, LANES), F32)]
                        + DIL_HEADS * _chain_scratch(tq, band // 2)),
        compiler_params=_cparams(("arbitrary", "arbitrary")),
        name="dil_attn",
    )(proj, proj, projv)


def _na_bias_table(rpb):
    c = np.arange(GRID_W)
    cs = np.clip(c - NA_KC // 2, 0, GRID_W - NA_KC)
    colmask = (c[None, :] >= cs[:, None]) & (c[None, :] < cs[:, None] + NA_KC)
    rows = jnp.stack([rpb[:, si:si + NA_KR, :] for si in range(NA_KR)], axis=1).astype(F32)
    edge = GRID_W - NA_KC
    padded = jnp.concatenate([jnp.repeat(rows[..., :1], edge, axis=-1), rows,
                              jnp.repeat(rows[..., -1:], edge, axis=-1)], axis=-1)
    b = jnp.stack([padded[..., GRID_W - 1 - qc:2 * GRID_W - 1 - qc] for qc in range(GRID_W)],
                  axis=2)
    b = jnp.where(colmask[None, None, :, None, :], b * LOG2E, NEG_INF)
    b = b.transpose(1, 0, 2, 3, 4)
    return b.reshape(NA_KR, rpb.shape[0] * GRID_W, NA_KR * GRID_W)


def _na_attn_kernel(q_ref, k_ref, v_ref, tb_ref, o_ref, *, rows, rg):
    g = pl.program_id(1)
    lane = lax.broadcasted_iota(jnp.int32, (1, 256), 1)
    nk = NA_KR * GRID_W
    sels = [(lane >= h * NA_DH) & (lane < (h + 1) * NA_DH) for h in range(NA_HEADS)]

    def row(r, _):
        grow = g * rg + r
        rs = jnp.clip(grow - NA_KR // 2, 0, rows - NA_KR)
        si = rs - grow + (NA_KR - 1)
        q = q_ref[r * GRID_W:(r + 1) * GRID_W, :]
        koff = pl.multiple_of(rs * GRID_W, GRID_W)
        kb = k_ref[pl.ds(koff, nk), :]
        vb = v_ref[pl.ds(koff, nk), :]
        q4 = jnp.concatenate([jnp.where(sel, q, jnp.zeros_like(q)) for sel in sels], axis=0)
        s = lax.dot_general(q4, kb, (((1,), (1,)), ((), ())), preferred_element_type=F32)
        s = s + tb_ref[si]
        m = jnp.max(s, -1, keepdims=True)
        p = jnp.exp2(s - m)
        l = jnp.sum(p, -1, keepdims=True)
        o4 = jnp.dot(p.astype(BF16), vb, preferred_element_type=F32) * (1.0 / l)
        out = jnp.zeros((GRID_W, 256), F32)
        for h, sel in enumerate(sels):
            out = jnp.where(sel, o4[h * GRID_W:(h + 1) * GRID_W], out)
        o_ref[r * GRID_W:(r + 1) * GRID_W, :] = out.astype(o_ref.dtype)
        return 0

    for r in range(rg):
        row(r, 0)


def _na_attention(proj, tb, nb, t):
    rows = t // GRID_W
    assert rows >= NA_KR
    rg = 8
    ng = rows // rg
    tq = rg * GRID_W
    kern = functools.partial(_na_attn_kernel, rows=rows, rg=rg)
    return pl.pallas_call(
        kern,
        grid=(nb, ng),
        in_specs=[pl.BlockSpec((None, tq, 256), lambda b, i: (S_DQ, b * ng + i, 0)),
                  pl.BlockSpec((None, t, 256), lambda b, i: (S_DK, b, 0)),
                  pl.BlockSpec((None, t, 256), lambda b, i: (S_DV, b, 0)),
                  pl.BlockSpec(tb.shape, lambda b, i: (0, 0, 0))],
        out_specs=pl.BlockSpec((tq, 256), lambda b, i: (b * ng + i, 0)),
        out_shape=jax.ShapeDtypeStruct((nb * t, 256), BF16),
        compiler_params=_cparams(("arbitrary", "arbitrary")),
        name="na_attn",
    )(proj, proj, proj, tb)


def _layer_norm(y, g, b):
    mu = jnp.mean(y, -1, keepdims=True)
    yc = y - mu
    var = jnp.mean(yc * yc, -1, keepdims=True)
    return yc * lax.rsqrt(var + LN_EPS) * g + b


def _outproj_kernel(oa_ref, ob_ref, oc_ref, od_ref, x_ref, g1_ref, sc2_ref, sh2_ref, w_ref,
                    lg_ref, lb_ref, wr_ref, x1_ref, h2_ref, aff_ref):
    m = jnp.dot(oa_ref[...], w_ref[0:256, :], preferred_element_type=F32)
    m += jnp.dot(ob_ref[...], w_ref[256:512, :], preferred_element_type=F32)
    m += jnp.dot(oc_ref[...], w_ref[512:768, :], preferred_element_type=F32)
    m += jnp.dot(od_ref[...], w_ref[768:1024, :], preferred_element_type=F32)
    y = DEEPNORM_ALPHA * x_ref[...] + (1.0 + g1_ref[...]) * m
    x1 = _layer_norm(y, lg_ref[...], lb_ref[...])
    x1_ref[...] = x1
    h2 = x1 * (1.0 + sc2_ref[...]) + sh2_ref[...]
    h2_ref[...] = h2.astype(BF16)
    h_hi, h_lo = _split_bf16(h2)
    w_hi, w_lo = _split_bf16(wr_ref[...])
    nt = (((1,), (1,)), ((), ()))
    lg = lax.dot_general(w_hi, h_hi, nt, preferred_element_type=F32)
    lg += lax.dot_general(w_hi, h_lo, nt, preferred_element_type=F32)
    lg += lax.dot_general(w_lo, h_hi, nt, preferred_element_type=F32)
    lg = lg - jnp.max(lg, 0, keepdims=True)
    e = jnp.exp(lg)
    aff_ref[...] = e / jnp.sum(e, 0, keepdims=True)


def _outproj(oa, ob, oc, od, x, g1, sc2, sh2, w_out, ln_g, ln_b, w_router_t, nb, t):
    tm = 512
    nt = t // tm
    ntok = nb * t
    tok = lambda w: pl.BlockSpec((tm, w), lambda i: (i, 0))
    per_b = pl.BlockSpec((None, 1, D_MODEL), lambda i: (i // nt, 0, 0))
    full = lambda shape: pl.BlockSpec(shape, lambda i: tuple(0 for _ in shape))
    return pl.pallas_call(
        _outproj_kernel,
        grid=(ntok // tm,),
        in_specs=[tok(256), tok(256), tok(256), tok(256), tok(D_MODEL), per_b, per_b, per_b,
                  full((D_MODEL, D_MODEL)), full((1, D_MODEL)), full((1, D_MODEL)),
                  full((N_EXPERTS, D_MODEL))],
        out_specs=[tok(D_MODEL), tok(D_MODEL),
                   pl.BlockSpec((N_EXPERTS, tm), lambda i: (0, i))],
        out_shape=[jax.ShapeDtypeStruct((ntok, D_MODEL), F32),
                   jax.ShapeDtypeStruct((ntok, D_MODEL), BF16),
                   jax.ShapeDtypeStruct((N_EXPERTS, ntok), F32)],
        compiler_params=_cparams(("arbitrary",)),
        name="out_proj",
    )(oa, ob, oc, od, x, g1, sc2, sh2, w_out, ln_g, ln_b, w_router_t)


def _expert_kernel(x_ref, wg_ref, wu_ref, wd_ref, o_ref, acc_ref):
    f = pl.program_id(2)

    @pl.when(f == 0)
    def _():
        acc_ref[...] = jnp.zeros_like(acc_ref)

    x = x_ref[...]
    g = jnp.dot(x, wg_ref[...], preferred_element_type=F32)
    u = jnp.dot(x, wu_ref[...], preferred_element_type=F32)
    hmid = (g * (1.0 / (1.0 + jnp.exp(-g))) * u).astype(BF16)
    acc_ref[...] += jnp.dot(hmid, wd_ref[...], preferred_element_type=F32)

    @pl.when(f == pl.num_programs(2) - 1)
    def _():
        o_ref[...] = acc_ref[...].astype(o_ref.dtype)


def _experts(xe, slots, wg, wu, wd):
    ne, _, d = xe.shape
    tm = math.gcd(slots, 1024)
    tf = 256
    nf = D_FF // tf
    return pl.pallas_call(
        _expert_kernel,
        grid=(ne, slots // tm, nf),
        in_specs=[pl.BlockSpec((None, tm, d), lambda e, m, f: (e, m, 0)),
                  pl.BlockSpec((None, d, tf), lambda e, m, f: (e, 0, f)),
                  pl.BlockSpec((None, d, tf), lambda e, m, f: (e, 0, f)),
                  pl.BlockSpec((None, tf, d), lambda e, m, f: (e, f, 0))],
        out_specs=pl.BlockSpec((None, tm, d), lambda e, m, f: (e, m, 0)),
        out_shape=jax.ShapeDtypeStruct((ne, slots, d), BF16),
        scratch_shapes=[pltpu.VMEM((tm, d), F32)],
        compiler_params=_cparams(("arbitrary", "arbitrary", "arbitrary")),
        name="expert_ffn",
    )(xe, wg, wu, wd)


MOE_TM = 512
MOE_WIN = 128
ROW_ALIGN = 16


def _select_kernel(aff_ref, o_ref, *, cap):
    a = aff_ref[...]
    keys = lax.bitcast_convert_type(a, jnp.int32)
    ne, n = a.shape
    capf = float(cap)

    def count(mask):
        return jnp.sum(jnp.where(mask, 1.0, 0.0), axis=1, keepdims=True)

    def key_bit(b, thr):
        cand = thr | lax.shift_left(jnp.int32(1), 30 - b)
        return jnp.where(count(keys >= cand) >= capf, cand, thr)

    thr = lax.fori_loop(0, 31, key_bit, jnp.zeros((ne, 1), jnp.int32))
    above = keys > thr
    need = capf - count(above)
    idx = lax.broadcasted_iota(jnp.int32, (ne, n), 1)
    tie_idx = jnp.where(keys == thr, idx, jnp.int32(2 ** 30))
    nbits = max(1, (n - 1).bit_length())

    def idx_bit(b, j):
        cand = j | lax.shift_left(jnp.int32(1), nbits - 1 - b)
        return jnp.where(count(tie_idx < cand) < need, cand, j)

    j = lax.fori_loop(0, nbits, idx_bit, jnp.zeros((ne, 1), jnp.int32))
    sel = above | (tie_idx <= j)
    o_ref[...] = jnp.where(sel, a, -1.0)


def _select(aff, cap):
    ne, n = aff.shape
    return pl.pallas_call(
        functools.partial(_select_kernel, cap=cap),
        out_shape=jax.ShapeDtypeStruct((ne, n), F32),
        compiler_params=pltpu.CompilerParams(vmem_limit_bytes=VMEM_LIMIT),
        name="ec_select",
    )(aff)


def _routing_tables(gs, group_tokens):
    ne = gs.shape[0]
    a_l, off_l, cnt_l, lim_l = [], [], [], []
    tok0, slot0 = 0, 0
    for n in group_tokens:
        cap = EC_FACTOR * n // N_EXPERTS
        nt = n // MOE_TM
        sel = lax.slice_in_dim(gs, tok0, tok0 + n, axis=1) >= 0
        counts = jnp.sum(sel.reshape(ne, nt, MOE_TM), axis=-1, dtype=jnp.int32)
        s0 = slot0 + jnp.cumsum(counts, axis=1) - counts
        a = (s0 // ROW_ALIGN) * ROW_ALIGN
        a_l.append(a)
        off_l.append(s0 - a)
        cnt_l.append(counts)
        lim_l.append(jnp.full((nt,), slot0 + cap - MOE_WIN, jnp.int32))
        tok0 += n
        slot0 += cap
    a = jnp.concatenate(a_l, axis=1).T
    off = jnp.concatenate(off_l, axis=1).T
    end = off + jnp.concatenate(cnt_l, axis=1).T
    rounds = (end + MOE_WIN - 1) // MOE_WIN
    gp = (end // ROW_ALIGN) * ROW_ALIGN
    return dict(a=a.reshape(-1).astype(jnp.int32), nr=rounds.reshape(-1).astype(jnp.int32),
                nrounds=jnp.max(rounds, axis=1).astype(jnp.int32), lim=jnp.concatenate(lim_l),
                off_col=off.astype(F32)[:, :, None], off_row=off.astype(F32)[:, None, :],
                off16=jnp.repeat(off.astype(F32), ROW_ALIGN, axis=1)[:, :, None],
                gp16=jnp.repeat(gp.astype(F32), ROW_ALIGN, axis=1)[:, :, None],
                total=slot0)


def _dispatch_kernel(a_tab, nrounds, nr_tab, gs_ref, off_ref, off16_ref, gp16_ref, x_ref, u_ref, xe_ref,
                     pos_ref, c_ref, c2_ref, stage_ref, carry_ref, sem, rc_ref):
    j = pl.program_id(0)
    ne, tm = gs_ref.shape
    d = x_ref.shape[1]
    win = MOE_WIN

    @pl.when(j == 0)
    def _():
        carry_ref[...] = jnp.zeros_like(carry_ref)
        rc_ref[0] = 0

    sel = gs_ref[...] >= 0.0
    rank = jnp.dot(jnp.where(sel, 1.0, 0.0).astype(BF16), u_ref[...], preferred_element_type=F32)
    pos_ref[...] = jnp.where(sel, rank + off_ref[...], -1.0)

    def wait_round(slot):
        def one(_, c):
            pltpu.make_async_copy(stage_ref.at[slot, pl.ds(0, win)], xe_ref.at[0, pl.ds(0, win)],
                                  sem.at[slot]).wait()
            return c
        lax.fori_loop(0, rc_ref[1 + slot], one, 0)

    def round_body(r, _):
        k = lax.broadcasted_iota(jnp.int32, (win, tm), 0).astype(F32) + (r * win).astype(F32)
        for e in range(ne):
            c_ref[e * win:(e + 1) * win, :] = jnp.where(pos_ref[e:e + 1, :] == k, 1.0, 0.0).astype(BF16)
        slot = rc_ref[0] % 2
        for nb in range(d // MXU_DIM):
            cols = slice(nb * MXU_DIM, (nb + 1) * MXU_DIM)
            stage_ref[slot, :, cols] = jnp.dot(c_ref[...], x_ref[:, cols],
                                               preferred_element_type=F32).astype(BF16)

        @pl.when(r == 0)
        def _():
            k16 = lax.broadcasted_iota(jnp.int32, (ROW_ALIGN, 1), 0).astype(F32)
            for e in range(ne):
                keep = k16 < off16_ref[e * ROW_ALIGN:(e + 1) * ROW_ALIGN, :]
                rows = pl.ds(e * win, ROW_ALIGN)
                stage_ref[slot, rows, :] = jnp.where(
                    keep, carry_ref[e * ROW_ALIGN:(e + 1) * ROW_ALIGN, :], stage_ref[slot, rows, :])

        @pl.when(rc_ref[0] > 0)
        def _():
            wait_round(1 - slot)

        rc_ref[1 + slot] = 0
        for e in range(ne):
            @pl.when(r < nr_tab[j * ne + e])
            def _():
                dst = pl.multiple_of(a_tab[j * ne + e] + r * win, ROW_ALIGN)
                pltpu.make_async_copy(stage_ref.at[slot, pl.ds(e * win, win)],
                                      xe_ref.at[e, pl.ds(dst, win)], sem.at[slot]).start()
                rc_ref[1 + slot] = rc_ref[1 + slot] + 1
        rc_ref[0] = rc_ref[0] + 1
        return 0

    lax.fori_loop(0, nrounds[j], round_body, 0)

    k16 = lax.broadcasted_iota(jnp.int32, (ROW_ALIGN, 1), 0).astype(F32)
    for e in range(ne):
        rows = slice(e * ROW_ALIGN, (e + 1) * ROW_ALIGN)
        c2_ref[rows, :] = jnp.where(pos_ref[e:e + 1, :] == gp16_ref[rows, :] + k16, 1.0, 0.0).astype(BF16)
    kk = jnp.concatenate([k16] * ne, axis=0)
    keep_old = (gp16_ref[...] == 0.0) & (kk < off16_ref[...])
    for nb in range(d // MXU_DIM):
        cols = slice(nb * MXU_DIM, (nb + 1) * MXU_DIM)
        new = jnp.dot(c2_ref[...], x_ref[:, cols], preferred_element_type=F32).astype(BF16)
        carry_ref[:, cols] = jnp.where(keep_old, carry_ref[:, cols], new)

    @pl.when((j == pl.num_programs(0) - 1) & (rc_ref[0] > 0))
    def _():
        wait_round((rc_ref[0] - 1) % 2)


def _dispatch(gs, h2, tabs, u_mat):
    ne, ntok = gs.shape
    d = h2.shape[1]
    nt = ntok // MOE_TM
    rows = tabs['total'] + MOE_WIN
    grid_spec = pltpu.PrefetchScalarGridSpec(
        num_scalar_prefetch=3,
        grid=(nt,),
        in_specs=[pl.BlockSpec((ne, MOE_TM), lambda j, *_: (0, j)),
                  pl.BlockSpec((None, ne, 1), lambda j, *_: (j, 0, 0)),
                  pl.BlockSpec((None, ne * ROW_ALIGN, 1), lambda j, *_: (j, 0, 0)),
                  pl.BlockSpec((None, ne * ROW_ALIGN, 1), lambda j, *_: (j, 0, 0)),
                  pl.BlockSpec((MOE_TM, d), lambda j, *_: (j, 0)),
                  pl.BlockSpec((MOE_TM, MOE_TM), lambda j, *_: (0, 0))],
        out_specs=pl.BlockSpec(memory_space=pl.ANY),
        scratch_shapes=[pltpu.VMEM((ne, MOE_TM), F32),
                        pltpu.VMEM((ne * MOE_WIN, MOE_TM), BF16),
                        pltpu.VMEM((ne * ROW_ALIGN, MOE_TM), BF16),
                        pltpu.VMEM((2, ne * MOE_WIN, d), BF16),
                        pltpu.VMEM((ne * ROW_ALIGN, d), BF16),
                        pltpu.SemaphoreType.DMA((2,)),
                        pltpu.SMEM((3,), jnp.int32)])
    return pl.pallas_call(
        _dispatch_kernel,
        grid_spec=grid_spec,
        out_shape=jax.ShapeDtypeStruct((ne, rows, d), BF16),
        compiler_params=_cparams(("arbitrary",)),
        name="ec_dispatch",
    )(tabs['a'], tabs['nrounds'], tabs['nr'], gs, tabs['off_col'], tabs['off16'], tabs['gp16'], h2,
      u_mat)


def _combine_kernel(a_tab, nrounds, lim_tab, gs_ref, off_ref, l_ref, ye_ref, x1_ref, g2_ref, lg_ref,
                    lb_ref, *refs, split_tiles):
    outs, (p_ref, y_ref, acc_ref, sem) = refs[:-4], refs[-4:]
    j = pl.program_id(0)
    tm, ne = gs_ref.shape
    d = x1_ref.shape[1]
    win = MOE_WIN
    gs = gs_ref[...]
    sel = gs >= 0.0
    rank = jnp.dot(l_ref[...], jnp.where(sel, 1.0, 0.0).astype(BF16), preferred_element_type=F32)
    pos = jnp.where(sel, rank + off_ref[...], -1.0)
    gate = jnp.where(sel, gs, 0.0)
    acc_ref[...] = jnp.zeros_like(acc_ref)
    n_tiles = pl.num_programs(0)
    slot = j % 2

    def window(tile, r, e):
        want = a_tab[tile * ne + e] + r * win
        src = pl.multiple_of(jnp.minimum(want, lim_tab[tile]), ROW_ALIGN)
        return src, want - src

    def start_round(tile, r, sl):
        for e in range(ne):
            src, _ = window(tile, r, e)
            pltpu.make_async_copy(ye_ref.at[e, pl.ds(src, win)], y_ref.at[sl, pl.ds(e * win, win)],
                                  sem.at[sl]).start()

    def wait_round(sl):
        for e in range(ne):
            pltpu.make_async_copy(ye_ref.at[e, pl.ds(0, win)], y_ref.at[sl, pl.ds(e * win, win)],
                                  sem.at[sl]).wait()

    @pl.when((j == 0) & (nrounds[0] > 0))
    def _():
        start_round(0, 0, 0)

    nxt = jnp.minimum(j + 1, n_tiles - 1)

    @pl.when((j + 1 < n_tiles) & (nrounds[nxt] > 0))
    def _():
        start_round(nxt, 0, 1 - slot)

    half = ne // 2

    def accumulate(r):
        base = lax.convert_element_type(r * win, F32)
        k = lax.broadcasted_iota(jnp.int32, (tm, win), 1).astype(F32) + base
        for grp in range(2):
            for e in range(grp * half, (grp + 1) * half):
                pe = pos[:, e:e + 1]
                pe = jnp.where(pe >= base, pe + window(j, r, e)[1].astype(F32), -1.0)
                pcol = jnp.broadcast_to(pe, (tm, win))
                gcol = jnp.broadcast_to(gate[:, e:e + 1], (tm, win))
                p_ref[:, e * win:(e + 1) * win] = jnp.where(pcol == k, gcol, 0.0).astype(BF16)
            rows = slice(grp * half * win, (grp + 1) * half * win)
            for nb in range(d // MXU_DIM):
                cols = slice(nb * MXU_DIM, (nb + 1) * MXU_DIM)
                acc_ref[:, cols] += jnp.dot(p_ref[:, rows], y_ref[slot, rows, cols],
                                            preferred_element_type=F32)

    @pl.when(nrounds[j] > 0)
    def _():
        wait_round(slot)
        accumulate(0)

    def extra_round(r, _):
        start_round(j, r, slot)
        wait_round(slot)
        accumulate(r)
        return 0

    lax.fori_loop(1, nrounds[j], extra_round, 0)
    y = DEEPNORM_ALPHA * x1_ref[...] + (1.0 + g2_ref[...]) * acc_ref[...]
    res = _layer_norm(y, lg_ref[...], lb_ref[...])
    if split_tiles is None:
        outs[0][...] = res
    else:
        @pl.when(j < split_tiles)
        def _():
            outs[0][...] = res

        @pl.when(j >= split_tiles)
        def _():
            outs[1][...] = res


def _combine_postnorm(gs_tok, ye, tabs, l_mat, x1, g2, ln_g, ln_b, t, split_tiles=None):
    ntok, ne = gs_tok.shape
    d = x1.shape[1]
    nt = ntok // MOE_TM
    tiles_per_seq = t // MOE_TM
    if split_tiles is None:
        out_specs = pl.BlockSpec((MOE_TM, d), lambda j, *_: (j, 0))
        out_shape = jax.ShapeDtypeStruct((ntok, d), F32)
    else:
        out_specs = [pl.BlockSpec((MOE_TM, d), lambda j, *_: (jnp.minimum(j, split_tiles - 1), 0)),
                     pl.BlockSpec((MOE_TM, d), lambda j, *_: (jnp.maximum(j - split_tiles, 0), 0))]
        out_shape = [jax.ShapeDtypeStruct((split_tiles * MOE_TM, d), F32),
                     jax.ShapeDtypeStruct((ntok - split_tiles * MOE_TM, d), F32)]
    grid_spec = pltpu.PrefetchScalarGridSpec(
        num_scalar_prefetch=3,
        grid=(nt,),
        in_specs=[pl.BlockSpec((MOE_TM, ne), lambda j, *_: (j, 0)),
                  pl.BlockSpec((None, 1, ne), lambda j, *_: (j, 0, 0)),
                  pl.BlockSpec((MOE_TM, MOE_TM), lambda j, *_: (0, 0)),
                  pl.BlockSpec(memory_space=pl.ANY),
                  pl.BlockSpec((MOE_TM, d), lambda j, *_: (j, 0)),
                  pl.BlockSpec((None, 1, d), lambda j, *_: (j // tiles_per_seq, 0, 0)),
                  pl.BlockSpec((1, d), lambda j, *_: (0, 0)),
                  pl.BlockSpec((1, d), lambda j, *_: (0, 0))],
        out_specs=out_specs,
        scratch_shapes=[pltpu.VMEM((MOE_TM, ne * MOE_WIN), BF16),
                        pltpu.VMEM((2, ne * MOE_WIN, d), BF16),
                        pltpu.VMEM((MOE_TM, d), F32),
                        pltpu.SemaphoreType.DMA((2,))])
    return pl.pallas_call(
        functools.partial(_combine_kernel, split_tiles=split_tiles),
        grid_spec=grid_spec,
        out_shape=out_shape,
        compiler_params=_cparams(("arbitrary",)),
        name="ec_combine",
    )(tabs['a'], tabs['nrounds'], tabs['lim'], gs_tok, tabs['off_row'], l_mat, ye, x1, g2, ln_g, ln_b)


def _prep_w_in(w_in_l):
    sizes = (256, 256, 256, MLA_Q_RANK, MLA_KV_RANK, MLA_ROPE, 256, 256, 256, 256, 256, 256)
    offs = np.concatenate([[0], np.cumsum(sizes)])
    part = [w_in_l[:, offs[i]:offs[i + 1]] for i in range(len(sizes))]
    a_q, a_k, a_v, b_cq, b_ckv, b_kr, c_q, c_k, c_v, d_q, d_k, d_v = part
    d = w_in_l.shape[0]
    zeros = lambda n: jnp.zeros((d, n), w_in_l.dtype)
    def per_head(v):
        out = []
        for h in range(4):
            out += [v[:, h * 64:(h + 1) * 64], zeros(LANES - 64)]
        return out

    cols = ([a_q, a_k] + per_head(a_v) + [c_q, c_k] + per_head(c_v)
            + [d_q, d_k, d_v, b_cq, zeros(256 - MLA_Q_RANK), b_ckv, b_kr, b_kr,
               zeros(128 - 2 * MLA_ROPE)])
    return jnp.concatenate(cols, axis=1).astype(BF16)


def _prep_w_uq(w_uq_l):
    hd = MLA_NOPE + MLA_ROPE
    nope = [w_uq_l[:, h * hd:h * hd + MLA_NOPE] for h in range(MLA_HEADS)]
    rope = [w_uq_l[:, h * hd + MLA_NOPE:(h + 1) * hd] for h in range(MLA_HEADS)]
    z = jnp.zeros((w_uq_l.shape[0], 256 - 2 * hd), w_uq_l.dtype)
    cols = []
    for p in range(2):
        cols += [nope[2 * p], nope[2 * p + 1], rope[2 * p], rope[2 * p + 1], z]
    w = jnp.concatenate(cols, axis=1)
    w = jnp.concatenate([w, jnp.zeros((256 - MLA_Q_RANK, w.shape[1]), w.dtype)], axis=0)
    return w.astype(BF16)


def _prep_w_ukv(w_ukv_l):
    hd = MLA_NOPE + MLA_DV
    kn = [w_ukv_l[:, h * hd:h * hd + MLA_NOPE] for h in range(MLA_HEADS)]
    z = jnp.zeros((w_ukv_l.shape[0], LANES - MLA_DV), w_ukv_l.dtype)
    vv = []
    for h in range(MLA_HEADS):
        vv += [w_ukv_l[:, h * hd + MLA_NOPE:(h + 1) * hd], z]
    return jnp.concatenate(kn + vv, axis=1).astype(BF16)


def _select_groups(aff_t, group_tokens):
    parts, off = [], 0
    for n in group_tokens:
        parts.append(_select(lax.slice_in_dim(aff_t, off, off + n, axis=1),
                             EC_FACTOR * n // N_EXPERTS))
        off += n
    return jnp.concatenate(parts, axis=1)


def _trunk(x, c, group_tokens, nb, t, p):
    ntok = nb * t
    tabs = (_rope_tables(t, 256, DA_DQK, DA_DQK // ROPE_FRACTION, ROPE_THETA)
            + _rope_tables(t, 256, DIL_DH, DIL_DH // ROPE_FRACTION, ROPE_THETA)
            + _rope_tables(t, 128, MLA_ROPE, MLA_ROPE, MLA_ROPE_THETA))
    gmat = jnp.asarray(np.kron(np.eye(4), np.full((64, 64), 1.0 / 64)), BF16)
    ti = jnp.arange(MOE_TM)
    u_mat = (ti[:, None] < ti[None, :]).astype(BF16)
    l_mat = (ti[None, :] < ti[:, None]).astype(BF16)
    for l in range(DEPTH):
        mod = _modulation(c, p['w_ada'][l], p['b_ada'][l])
        sh1, sc1, g1, sh2, sc2, g2 = [m.reshape(nb, 1, D_MODEL) for m in jnp.split(mod, 6, axis=-1)]
        gq = jnp.concatenate([p['q_norm_g'][l], jnp.zeros((256 - MLA_Q_RANK,), F32)]).reshape(1, 256)
        gkv = p['kv_norm_g'][l].reshape(1, 128)
        proj, projv = _inproj(x, sc1, sh1, _prep_w_in(p['w_in'][l]), _prep_w_uq(p['w_uq'][l]),
                              _prep_w_ukv(p['w_ukv'][l]), gq, gkv, tabs, nb, t)
        lam_init = 0.8 - 0.6 * math.exp(-0.3 * l)
        lam = (jnp.exp(jnp.sum(p['da_lq1'][l] * p['da_lk1'][l]))
               - jnp.exp(jnp.sum(p['da_lq2'][l] * p['da_lk2'][l])) + lam_init).reshape(1)
        g_sub = jnp.tile(p['da_subln_g'][l], DA_HEADS).reshape(1, 256)
        oa = _diff_attention(proj, projv, lam, g_sub, gmat, nb, t, 1.0 - lam_init)
        ob = _mla_attention(proj, projv, nb, t)
        oc = _dil_attention(proj, projv, nb, t)
        od = _na_attention(proj, _na_bias_table(p['na_rpb'][l]), nb, t)
        x1, h2, aff_t = _outproj(oa, ob, oc, od, x, g1, sc2, sh2, p['w_out'][l].astype(BF16),
                                 p['ln1_g'][l].reshape(1, -1), p['ln1_b'][l].reshape(1, -1),
                                 p['w_router'][l].T, nb, t)
        wg = p['w_e_gate'][l].astype(BF16)
        wu = p['w_e_up'][l].astype(BF16)
        wd = p['w_e_down'][l].astype(BF16)
        gs = _select_groups(aff_t, group_tokens)
        rt = _routing_tables(gs, group_tokens)
        xe = _dispatch(gs, h2, rt, u_mat)
        ye = _experts(xe, rt['total'], wg, wu, wd)
        split = group_tokens[0] // MOE_TM if l == DEPTH - 1 else None
        x = _combine_postnorm(gs.T, ye, rt, l_mat, x1, g2, p['ln2_g'][l].reshape(1, -1),
                              p['ln2_b'][l].reshape(1, -1), t, split_tiles=split)
    return x


def kernel(x_prompt, x_sample, c_prompt, c_sample, w_in, w_uq, w_ukv, q_norm_g, kv_norm_g, da_lq1,
           da_lk1, da_lq2, da_lk2, da_subln_g, na_rpb, w_out, w_ada, b_ada, ln1_g, ln1_b, ln2_g,
           ln2_b, w_router, w_e_gate, w_e_up, w_e_down):
    p = dict(w_in=w_in, w_uq=w_uq, w_ukv=w_ukv, q_norm_g=q_norm_g, kv_norm_g=kv_norm_g,
             da_lq1=da_lq1, da_lk1=da_lk1, da_lq2=da_lq2, da_lk2=da_lk2, da_subln_g=da_subln_g,
             na_rpb=na_rpb, w_out=w_out, w_ada=w_ada, b_ada=b_ada, ln1_g=ln1_g, ln1_b=ln1_b,
             ln2_g=ln2_g, ln2_b=ln2_b, w_router=w_router, w_e_gate=w_e_gate, w_e_up=w_e_up,
             w_e_down=w_e_down)
    bp, t, d = x_prompt.shape
    bs = x_sample.shape[0]
    assert x_sample.shape[1] == t
    nb = bp + bs
    x = jnp.concatenate([x_prompt.reshape(bp * t, d), x_sample.reshape(bs * t, d)], axis=0)
    c = jnp.concatenate([c_prompt, c_sample], axis=0)
    y_prompt, y_sample = _trunk(x, c, (bp * t, bs * t), nb, t, p)
    return y_prompt.reshape(bp, t, d), y_sample.reshape(bs, t, d)
```

```python
import functools
import math

import jax
import jax.numpy as jnp
import numpy as np
from jax import lax
from jax.experimental import pallas as pl
from jax.experimental.pallas import tpu as pltpu

F32 = jnp.float32
BF16 = jnp.bfloat16

D_MODEL = 1024
DEPTH = 2
GRID_W = 64
GROUP_W = 256
DA_HEADS, DA_DV, DA_DQK = 4, 64, 32
MLA_HEADS, MLA_Q_RANK, MLA_KV_RANK, MLA_NOPE, MLA_ROPE, MLA_DV = 4, 192, 128, 64, 32, 64
MLA_ROPE_THETA = 10000.0
DIL_HEADS, DIL_DH = 4, 64
DIL_PATTERNS = ((128, 1), (512, 4), (2048, 16))
NA_HEADS, NA_DH, NA_KR, NA_KC = 4, 64, 8, 16
ROPE_THETA = 500000.0
ROPE_FRACTION = 4
N_EXPERTS = 16
EC_FACTOR = 2
D_FF = 2816
DEEPNORM_ALPHA = (2.0 * DEPTH) ** 0.25
NEG_INF = -1e30
LOG2E = math.log2(math.e)
LN_EPS = 1e-5
RMS_EPS = 1e-6

LANES = 128
MXU_DIM = 256
VMEM_LIMIT = 56 * 1024 * 1024

S_AQ, S_AK, S_BQ0, S_BQ1, S_BK0, S_BK1, S_CQ, S_CK, S_DQ, S_DK, S_DV = range(11)
N_SLOTS = 11
V_A, V_B, V_C = 0, 4, 8
N_VSLOTS = 12
V_ONE_LANE = 64
W_ALL_COLS = 13 * 256


def _cparams(sem):
    return pltpu.CompilerParams(dimension_semantics=sem, vmem_limit_bytes=VMEM_LIMIT)


def _split_bf16(a):
    hi = a.astype(BF16)
    lo = (a - hi.astype(F32)).astype(BF16)
    return hi, lo


def _mod_kernel(c_ref, w_ref, b_ref, o_ref):
    c = c_ref[...]
    a = c * (1.0 / (1.0 + jnp.exp(-c)))
    a_hi, a_lo = _split_bf16(a)
    w_hi, w_lo = _split_bf16(w_ref[...])
    acc = jnp.dot(a_hi, w_hi, preferred_element_type=F32)
    acc += jnp.dot(a_hi, w_lo, preferred_element_type=F32)
    acc += jnp.dot(a_lo, w_hi, preferred_element_type=F32)
    o_ref[...] = acc + b_ref[...]


def _modulation(c, w_ada, b_ada):
    nb, d = c.shape
    n_out = w_ada.shape[1]
    tn = 1536
    return pl.pallas_call(
        _mod_kernel,
        grid=(n_out // tn,),
        in_specs=[pl.BlockSpec((nb, d), lambda j: (0, 0)),
                  pl.BlockSpec((d, tn), lambda j: (0, j)),
                  pl.BlockSpec((1, tn), lambda j: (0, j))],
        out_specs=pl.BlockSpec((nb, tn), lambda j: (0, j)),
        out_shape=jax.ShapeDtypeStruct((nb, n_out), F32),
        compiler_params=_cparams(("arbitrary",)),
        name="adaln_mod",
    )(c, w_ada, b_ada.reshape(1, n_out))


def _rope_tables(t, width, group, rot, theta):
    half = rot // 2
    inv = theta ** (-jnp.arange(half, dtype=F32) / half)
    ang = jnp.arange(t, dtype=F32)[:, None] * inv[None, :]
    cos, sin = jnp.cos(ang), jnp.sin(ang)
    ones = jnp.ones((t, group - rot), F32)
    zeros = jnp.zeros((t, group - rot), F32)
    c_g = jnp.concatenate([cos, cos, ones], axis=1)
    s_g = jnp.concatenate([-sin, sin, zeros], axis=1)
    reps = width // group
    return jnp.tile(c_g, (1, reps)), jnp.tile(s_g, (1, reps))


def _apply_rope(x, c_tab, s_tab, group, rot):
    width = x.shape[-1]
    half = rot // 2
    lane = lax.broadcasted_iota(jnp.int32, (1, width), 1)
    first = (lane % group) < half
    fwd = pltpu.roll(x, width - half, 1)
    bwd = pltpu.roll(x, half, 1)
    return x * c_tab + s_tab * jnp.where(first, fwd, bwd)


def _inproj_kernel(x_ref, sc_ref, sh_ref, w_ref, wuq_ref, wukv_ref, gq_ref, gkv_ref,
                   ca_ref, sa_ref, cc_ref, scc_ref, cm_ref, sm_ref, o_ref, ov_ref):
    h = (x_ref[...] * (1.0 + sc_ref[...]) + sh_ref[...]).astype(BF16)

    def proj(col, width):
        return jnp.dot(h, w_ref[:, col:col + width], preferred_element_type=F32)

    one_lane = lax.broadcasted_iota(jnp.int32, (1, LANES), 1) == V_ONE_LANE

    def put_values(first_slot, vals):
        for hh in range(4):
            v = vals[:, LANES * hh:LANES * (hh + 1)]
            ov_ref[first_slot + hh] = jnp.where(one_lane, 1.0, v).astype(BF16)

    sa_scale = DA_DQK ** -0.5 * LOG2E
    sb_scale = (MLA_NOPE + MLA_ROPE) ** -0.5 * LOG2E
    sc_scale = DIL_DH ** -0.5 * LOG2E
    sd_scale = NA_DH ** -0.5 * LOG2E
    a_rot = DA_DQK // ROPE_FRACTION
    c_rot = DIL_DH // ROPE_FRACTION

    ca, sa = ca_ref[...], sa_ref[...]
    o_ref[S_AQ] = (_apply_rope(proj(0, 256), ca, sa, DA_DQK, a_rot) * sa_scale).astype(BF16)
    o_ref[S_AK] = _apply_rope(proj(256, 256), ca, sa, DA_DQK, a_rot).astype(BF16)
    put_values(V_A, proj(512, 512))
    cc, scc = cc_ref[...], scc_ref[...]
    o_ref[S_CQ] = (_apply_rope(proj(1024, 256), cc, scc, DIL_DH, c_rot) * sc_scale).astype(BF16)
    o_ref[S_CK] = _apply_rope(proj(1280, 256), cc, scc, DIL_DH, c_rot).astype(BF16)
    put_values(V_C, proj(1536, 512))
    o_ref[S_DQ] = (proj(2048, 256) * sd_scale).astype(BF16)
    o_ref[S_DK] = proj(2304, 256).astype(BF16)
    o_ref[S_DV] = proj(2560, 256).astype(BF16)

    cm, sm = cm_ref[...], sm_ref[...]
    cq = proj(2816, 256)
    cq = cq * lax.rsqrt(jnp.sum(cq * cq, -1, keepdims=True) * (1.0 / MLA_Q_RANK) + RMS_EPS)
    cq = (cq * gq_ref[...]).astype(BF16)
    q2 = jnp.dot(cq, wuq_ref[...], preferred_element_type=F32)
    for p in range(2):
        qp = q2[:, 256 * p:256 * (p + 1)]
        o_ref[S_BQ0 + p, :, 0:128] = (qp[:, 0:128] * sb_scale).astype(BF16)
        o_ref[S_BQ0 + p, :, 128:256] = (
            _apply_rope(qp[:, 128:256], cm, sm, MLA_ROPE, MLA_ROPE) * sb_scale).astype(BF16)
    ckv = proj(3072, 128)
    ckv = ckv * lax.rsqrt(jnp.mean(ckv * ckv, -1, keepdims=True) + RMS_EPS)
    ckv = (ckv * gkv_ref[...]).astype(BF16)
    kv = jnp.dot(ckv, wukv_ref[...], preferred_element_type=F32)
    kr = _apply_rope(proj(3200, 128), cm, sm, MLA_ROPE, MLA_ROPE).astype(BF16)
    for p in range(2):
        o_ref[S_BK0 + p, :, 0:128] = kv[:, 128 * p:128 * (p + 1)].astype(BF16)
        o_ref[S_BK0 + p, :, 128:256] = kr
    put_values(V_B, kv[:, 256:768])


def _inproj(x, sc, sh, w_all, wuq, wukv, gq, gkv, tabs, nb, t):
    tm = 512
    nt = t // tm
    ntok = nb * t
    ca, sa, cc, scc, cm, sm = tabs
    full = lambda shape: pl.BlockSpec(shape, lambda j, b: tuple(0 for _ in shape))
    tab = lambda w: pl.BlockSpec((tm, w), lambda j, b: (j, 0))
    return pl.pallas_call(
        _inproj_kernel,
        grid=(nt, nb),
        in_specs=[pl.BlockSpec((tm, D_MODEL), lambda j, b: (b * nt + j, 0)),
                  pl.BlockSpec((None, 1, D_MODEL), lambda j, b: (b, 0, 0)),
                  pl.BlockSpec((None, 1, D_MODEL), lambda j, b: (b, 0, 0)),
                  full((D_MODEL, W_ALL_COLS)), full((256, 512)), full((128, 768)),
                  full((1, 256)), full((1, 128)),
                  tab(256), tab(256), tab(256), tab(256), tab(128), tab(128)],
        out_specs=[pl.BlockSpec((N_SLOTS, tm, 256), lambda j, b: (0, b * nt + j, 0)),
                   pl.BlockSpec((N_VSLOTS, tm, LANES), lambda j, b: (0, b * nt + j, 0))],
        out_shape=[jax.ShapeDtypeStruct((N_SLOTS, ntok, 256), BF16),
                   jax.ShapeDtypeStruct((N_VSLOTS, ntok, LANES), BF16)],
        compiler_params=_cparams(("arbitrary", "arbitrary")),
        name="in_proj",
    )(x, sc, sh, w_all, wuq, wukv, gq, gkv, ca, sa, cc, scc, cm, sm)


def _lane_mask(width, ranges):
    lane = lax.broadcasted_iota(jnp.int32, (1, width), 1)
    m = None
    for lo, hi in ranges:
        r = (lane >= lo) & (lane < hi)
        m = r if m is None else (m | r)
    return m


SOFTMAX_SLAB = 32


def _chain_scratch(tq, kc):
    return [pltpu.VMEM((tq, kc), F32), pltpu.VMEM((tq, kc), F32), pltpu.VMEM((tq, kc), BF16),
            pltpu.VMEM((tq, LANES), F32), pltpu.VMEM((tq, LANES), F32), pltpu.VMEM((tq, LANES), F32)]


CHAIN_REFS = 6


def _attention_steps(n_steps, qms_for, k_for, v_for, finish, tq, t, kc, chains, slab=SOFTMAX_SLAB,
                     bias_ref=None, step_unroll=1):
    n_chunks = t // kc
    assert n_chunks % 2 == 0
    nt = (((1,), (1,)), ((), ()))

    def chunk_rows(chunk):
        if isinstance(chunk, int):
            return pl.ds(chunk * kc, kc)
        return pl.ds(pl.multiple_of(chunk * kc, kc), kc)

    def scores(step, chunk, slot):
        k = k_for(step)[chunk_rows(chunk), :]
        for qm, chain in zip(qms_for(step), chains):
            chain[slot][...] = lax.dot_general(qm, k, nt, preferred_element_type=F32)

    def softmax_pv(step, chunk, slot):
        for v_ref, chain in zip(v_for(step), chains):
            v = v_ref[chunk_rows(chunk), :]
            s_ref = chain[slot]
            (p_ref, m_ref, a_ref, acc_ref) = chain[2:]
            for r in range(tq // slab):
                rows = slice(r * slab, (r + 1) * slab)
                s = s_ref[rows, :]
                if bias_ref is not None:
                    s = s + bias_ref[rows, chunk * kc:(chunk + 1) * kc]
                m_prev = m_ref[rows, :]
                m_new = jnp.maximum(m_prev, jnp.max(s, -1, keepdims=True))
                d = s - jnp.concatenate([m_new] * (kc // LANES), axis=1)
                p_ref[rows, :] = jnp.exp2(d.astype(BF16))
                a_ref[rows, :] = jnp.exp2(m_prev - m_new)
                m_ref[rows, :] = m_new
            acc_ref[...] = a_ref[...] * acc_ref[...] + jnp.dot(p_ref[...], v,
                                                              preferred_element_type=F32)

    def step_body(st, _):
        for (_, _, _, m_ref, _, acc_ref) in chains:
            m_ref[...] = jnp.full(m_ref.shape, NEG_INF, F32)
            acc_ref[...] = jnp.zeros(acc_ref.shape, F32)
        nxt = min(st + 1, n_steps - 1) if isinstance(st, int) else jnp.minimum(st + 1, n_steps - 1)

        def body(j, _):
            scores(st, 2 * j + 1, 1)
            softmax_pv(st, 2 * j, 0)
            if n_chunks == 2:
                if n_steps > 1:
                    scores(nxt, 0, 0)
            else:
                last = 2 * j + 2 == n_chunks
                scores(jnp.where(last, nxt, st), jnp.where(last, 0, 2 * j + 2), 0)
            softmax_pv(st, 2 * j + 1, 1)
            return 0

        if n_chunks == 2:
            body(0, 0)
        else:
            assert bias_ref is None
            lax.fori_loop(0, n_chunks // 2, body, 0, unroll=4)
        outs = []
        for chain in chains:
            acc = chain[5][...]
            den = jnp.broadcast_to(acc[:, V_ONE_LANE:V_ONE_LANE + 1], acc.shape)
            outs.append(acc * (1.0 / den))
        finish(st, outs)
        return 0

    scores(0, 0, 0)
    if step_unroll == n_steps:
        for st in range(n_steps):
            step_body(st, 0)
    else:
        lax.fori_loop(0, n_steps, step_body, 0, unroll=step_unroll)


def _place_head(pair_ref, o, j):
    lane = lax.broadcasted_iota(jnp.int32, (1, LANES), 1)
    low = lane < V_ONE_LANE
    shifted = pltpu.roll(o, V_ONE_LANE, 1)
    cur = pair_ref[...]
    pair_ref[...] = jnp.where(low, jnp.where(j == 0, o, cur), jnp.where(j == 1, shifted, cur))


def _group_mean_sq(x, gmat):
    sq = x * x
    hi, lo = _split_bf16(sq)
    return (jnp.dot(hi, gmat, preferred_element_type=F32)
            + jnp.dot(lo, gmat, preferred_element_type=F32))


def _diff_attn_kernel(lam_ref, q_ref, k_ref, v_ref, g_ref, gmat_ref, o_ref, acc_ref, *scratch, t, kc,
                      out_scale):
    chains = (scratch[:CHAIN_REFS], scratch[CHAIN_REFS:])
    lam = lam_ref[0]
    lane = lax.broadcasted_iota(jnp.int32, (1, 256), 1)

    def qms_for(h):
        q = q_ref[...]
        return [jnp.where((lane >= (2 * h + c) * DA_DQK) & (lane < (2 * h + c + 1) * DA_DQK), q,
                          jnp.zeros_like(q)) for c in range(2)]

    def finish(h, outs):
        _place_head(acc_ref.at[h // 2], outs[0] - lam * outs[1], h % 2)

    acc_ref[...] = jnp.zeros_like(acc_ref)
    _attention_steps(DA_HEADS, qms_for, lambda h: k_ref, lambda h: [v_ref.at[h], v_ref.at[h]],
                     finish, q_ref.shape[0], t, kc, chains)
    o = jnp.concatenate([acc_ref[0], acc_ref[1]], axis=1)
    ms = _group_mean_sq(o, gmat_ref[...])
    o_ref[...] = (o * lax.rsqrt(ms + RMS_EPS) * g_ref[...] * out_scale).astype(o_ref.dtype)


def _diff_attention(proj, projv, lam, g_tiled, gmat, nb, t, out_scale):
    tq, kc = 512, 512
    nq = t // tq
    kern = functools.partial(_diff_attn_kernel, t=t, kc=kc, out_scale=out_scale)
    return pl.pallas_call(
        kern,
        grid=(nb, nq),
        in_specs=[pl.BlockSpec(memory_space=pltpu.SMEM),
                  pl.BlockSpec((None, tq, 256), lambda b, i: (S_AQ, b * nq + i, 0)),
                  pl.BlockSpec((None, t, 256), lambda b, i: (S_AK, b, 0)),
                  pl.BlockSpec((DA_HEADS, t, LANES), lambda b, i: (V_A // DA_HEADS, b, 0)),
                  pl.BlockSpec((1, 256), lambda b, i: (0, 0)),
                  pl.BlockSpec((256, 256), lambda b, i: (0, 0))],
        out_specs=pl.BlockSpec((tq, 256), lambda b, i: (b * nq + i, 0)),
        out_shape=jax.ShapeDtypeStruct((nb * t, 256), BF16),
        scratch_shapes=[pltpu.VMEM((2, tq, LANES), F32)] + 2 * _chain_scratch(tq, kc),
        compiler_params=_cparams(("arbitrary", "arbitrary")),
        name="diff_attn",
    )(lam, proj, proj, projv, g_tiled, gmat)


def _mla_attn_kernel(q_ref, k_ref, v_ref, o_ref, acc_ref, *scratch, t, kc):
    chains = (scratch[:CHAIN_REFS], scratch[CHAIN_REFS:])
    lane = lax.broadcasted_iota(jnp.int32, (1, 256), 1)

    def qms_for(p):
        q = q_ref[p]
        qms = []
        for j in range(2):
            nope = (lane >= j * MLA_NOPE) & (lane < (j + 1) * MLA_NOPE)
            rope = (lane >= 128 + j * MLA_ROPE) & (lane < 128 + (j + 1) * MLA_ROPE)
            qms.append(jnp.where(nope | rope, q, jnp.zeros_like(q)))
        return qms

    def finish(p, outs):
        for j in range(2):
            _place_head(acc_ref.at[p], outs[j], j)

    acc_ref[...] = jnp.zeros_like(acc_ref)
    _attention_steps(MLA_HEADS // 2, qms_for, lambda p: k_ref.at[p],
                     lambda p: [v_ref.at[2 * p], v_ref.at[2 * p + 1]], finish, q_ref.shape[1], t, kc,
                     chains)
    o_ref[...] = jnp.concatenate([acc_ref[0], acc_ref[1]], axis=1).astype(o_ref.dtype)


def _mla_attention(proj, projv, nb, t):
    tq, kc = 512, 512
    nq = t // tq
    kern = functools.partial(_mla_attn_kernel, t=t, kc=kc)
    return pl.pallas_call(
        kern,
        grid=(nb, nq),
        in_specs=[pl.BlockSpec((2, tq, 256), lambda b, i: (S_BQ0 // 2, b * nq + i, 0)),
                  pl.BlockSpec((2, t, 256), lambda b, i: (S_BK0 // 2, b, 0)),
                  pl.BlockSpec((MLA_HEADS, t, LANES), lambda b, i: (V_B // MLA_HEADS, b, 0))],
        out_specs=pl.BlockSpec((tq, 256), lambda b, i: (b * nq + i, 0)),
        out_shape=jax.ShapeDtypeStruct((nb * t, 256), BF16),
        scratch_shapes=[pltpu.VMEM((2, tq, LANES), F32)] + 2 * _chain_scratch(tq, kc),
        compiler_params=_cparams(("arbitrary", "arbitrary")),
        name="mla_attn",
    )(proj, proj, projv)


DIL_REACH = max(w // 2 for w, _ in DIL_PATTERNS)


def _dil_attn_kernel(q_ref, k_ref, v_ref, o_ref, tables_ref, acc_ref, *scratch, t, tq, band):
    chains = tuple(scratch[c * CHAIN_REFS:(c + 1) * CHAIN_REFS] for c in range(DIL_HEADS))
    i = pl.program_id(1)

    def band_start(blk):
        return jnp.clip(blk * tq - DIL_REACH, 0, t - band)

    start = pl.multiple_of(band_start(i), tq)
    table = (i * tq - start) // tq

    @pl.when((pl.program_id(0) == 0) & (i == 0))
    def _():
        def build(n, _):
            qi = lax.broadcasted_iota(jnp.int32, (tq, band), 0)
            kj = lax.broadcasted_iota(jnp.int32, (tq, band), 1)
            delta = kj - qi - n * tq
            ad = jnp.abs(delta)
            cnt = jnp.zeros((tq, band), F32)
            for window, dil in DIL_PATTERNS:
                ok = (ad <= window // 2) & ((delta & (dil - 1)) == 0)
                cnt = cnt + jnp.where(ok, 1.0, 0.0)
            tables_ref[n] = jnp.where(cnt > 2.5, math.log2(3.0),
                                      jnp.where(cnt > 1.5, 1.0,
                                                jnp.where(cnt > 0.5, 0.0, NEG_INF)))
            return 0

        lax.fori_loop(0, tables_ref.shape[0], build, 0)

    bias_ref = tables_ref.at[table]

    lane = lax.broadcasted_iota(jnp.int32, (1, 256), 1)

    def qms_for(_):
        q = q_ref[...]
        return [jnp.where((lane >= h * DIL_DH) & (lane < (h + 1) * DIL_DH), q, jnp.zeros_like(q))
                for h in range(DIL_HEADS)]

    def finish(_, outs):
        for h in range(DIL_HEADS):
            _place_head(acc_ref.at[h // 2], outs[h], h % 2)

    acc_ref[...] = jnp.zeros_like(acc_ref)
    _attention_steps(1, qms_for, lambda _: k_ref.at[pl.ds(start, band)],
                     lambda _: [v_ref.at[h, pl.ds(start, band)] for h in range(DIL_HEADS)],
                     finish, tq, band, band // 2, chains, slab=DIL_SLAB, bias_ref=bias_ref,
                     step_unroll=1)
    o_ref[...] = jnp.concatenate([acc_ref[0], acc_ref[1]], axis=1).astype(o_ref.dtype)


DIL_SLAB = 16


def _dil_attention(proj, projv, nb, t):
    tq = 256
    band = min(t, tq + 2 * DIL_REACH)
    nq = t // tq
    kern = functools.partial(_dil_attn_kernel, t=t, tq=tq, band=band)
    return pl.pallas_call(
        kern,
        grid=(nb, nq),
        in_specs=[pl.BlockSpec((None, tq, 256), lambda b, i: (S_CQ, b * nq + i, 0)),
                  pl.BlockSpec((None, t, 256), lambda b, i: (S_CK, b, 0)),
                  pl.BlockSpec((DIL_HEADS, t, LANES), lambda b, i: (V_C // DIL_HEADS, b, 0))],
        out_specs=pl.BlockSpec((tq, 256), lambda b, i: (b * nq + i, 0)),
        out_shape=jax.ShapeDtypeStruct((nb * t, 256), BF16),
        scratch_shapes=([pltpu.VMEM((band // tq, tq, band), F32), pltpu.VMEM((2, tq, LANES), F32)]
                        + DIL_HEADS * _chain_scratch(tq, band // 2)),
        compiler_params=_cparams(("arbitrary", "arbitrary")),
        name="dil_attn",
    )(proj, proj, projv)


def _na_bias_table(rpb):
    c = np.arange(GRID_W)
    cs = np.clip(c - NA_KC // 2, 0, GRID_W - NA_KC)
    colmask = (c[None, :] >= cs[:, None]) & (c[None, :] < cs[:, None] + NA_KC)
    rows = jnp.stack([rpb[:, si:si + NA_KR, :] for si in range(NA_KR)], axis=1).astype(F32)
    edge = GRID_W - NA_KC
    padded = jnp.concatenate([jnp.repeat(rows[..., :1], edge, axis=-1), rows,
                              jnp.repeat(rows[..., -1:], edge, axis=-1)], axis=-1)
    b = jnp.stack([padded[..., GRID_W - 1 - qc:2 * GRID_W - 1 - qc] for qc in range(GRID_W)],
                  axis=2)
    b = jnp.where(colmask[None, None, :, None, :], b * LOG2E, NEG_INF)
    b = b.transpose(1, 0, 2, 3, 4)
    return b.reshape(NA_KR, rpb.shape[0] * GRID_W, NA_KR * GRID_W)


def _na_attn_kernel(q_ref, k_ref, v_ref, tb_ref, o_ref, *, rows, rg):
    g = pl.program_id(1)
    lane = lax.broadcasted_iota(jnp.int32, (1, 256), 1)
    nk = NA_KR * GRID_W
    sels = [(lane >= h * NA_DH) & (lane < (h + 1) * NA_DH) for h in range(NA_HEADS)]

    def row(r, _):
        grow = g * rg + r
        rs = jnp.clip(grow - NA_KR // 2, 0, rows - NA_KR)
        si = rs - grow + (NA_KR - 1)
        q = q_ref[r * GRID_W:(r + 1) * GRID_W, :]
        koff = pl.multiple_of(rs * GRID_W, GRID_W)
        kb = k_ref[pl.ds(koff, nk), :]
        vb = v_ref[pl.ds(koff, nk), :]
        q4 = jnp.concatenate([jnp.where(sel, q, jnp.zeros_like(q)) for sel in sels], axis=0)
        s = lax.dot_general(q4, kb, (((1,), (1,)), ((), ())), preferred_element_type=F32)
        s = s + tb_ref[si]
        m = jnp.max(s, -1, keepdims=True)
        p = jnp.exp2(s - m)
        l = jnp.sum(p, -1, keepdims=True)
        o4 = jnp.dot(p.astype(BF16), vb, preferred_element_type=F32) * (1.0 / l)
        out = jnp.zeros((GRID_W, 256), F32)
        for h, sel in enumerate(sels):
            out = jnp.where(sel, o4[h * GRID_W:(h + 1) * GRID_W], out)
        o_ref[r * GRID_W:(r + 1) * GRID_W, :] = out.astype(o_ref.dtype)
        return 0

    for r in range(rg):
        row(r, 0)


def _na_attention(proj, tb, nb, t):
    rows = t // GRID_W
    assert rows >= NA_KR
    rg = 8
    ng = rows // rg
    tq = rg * GRID_W
    kern = functools.partial(_na_attn_kernel, rows=rows, rg=rg)
    return pl.pallas_call(
        kern,
        grid=(nb, ng),
        in_specs=[pl.BlockSpec((None, tq, 256), lambda b, i: (S_DQ, b * ng + i, 0)),
                  pl.BlockSpec((None, t, 256), lambda b, i: (S_DK, b, 0)),
                  pl.BlockSpec((None, t, 256), lambda b, i: (S_DV, b, 0)),
                  pl.BlockSpec(tb.shape, lambda b, i: (0, 0, 0))],
        out_specs=pl.BlockSpec((tq, 256), lambda b, i: (b * ng + i, 0)),
        out_shape=jax.ShapeDtypeStruct((nb * t, 256), BF16),
        compiler_params=_cparams(("arbitrary", "arbitrary")),
        name="na_attn",
    )(proj, proj, proj, tb)


def _layer_norm(y, g, b):
    mu = jnp.mean(y, -1, keepdims=True)
    yc = y - mu
    var = jnp.mean(yc * yc, -1, keepdims=True)
    return yc * lax.rsqrt(var + LN_EPS) * g + b


def _outproj_kernel(oa_ref, ob_ref, oc_ref, od_ref, x_ref, g1_ref, sc2_ref, sh2_ref, w_ref,
                    lg_ref, lb_ref, wr_ref, x1_ref, h2_ref, aff_ref):
    m = jnp.dot(oa_ref[...], w_ref[0:256, :], preferred_element_type=F32)
    m += jnp.dot(ob_ref[...], w_ref[256:512, :], preferred_element_type=F32)
    m += jnp.dot(oc_ref[...], w_ref[512:768, :], preferred_element_type=F32)
    m += jnp.dot(od_ref[...], w_ref[768:1024, :], preferred_element_type=F32)
    y = DEEPNORM_ALPHA * x_ref[...] + (1.0 + g1_ref[...]) * m
    x1 = _layer_norm(y, lg_ref[...], lb_ref[...])
    x1_ref[...] = x1
    h2 = x1 * (1.0 + sc2_ref[...]) + sh2_ref[...]
    h2_ref[...] = h2.astype(BF16)
    h_hi, h_lo = _split_bf16(h2)
    w_hi, w_lo = _split_bf16(wr_ref[...])
    nt = (((1,), (1,)), ((), ()))
    lg = lax.dot_general(w_hi, h_hi, nt, preferred_element_type=F32)
    lg += lax.dot_general(w_hi, h_lo, nt, preferred_element_type=F32)
    lg += lax.dot_general(w_lo, h_hi, nt, preferred_element_type=F32)
    lg = lg - jnp.max(lg, 0, keepdims=True)
    e = jnp.exp(lg)
    aff_ref[...] = e / jnp.sum(e, 0, keepdims=True)


def _outproj(oa, ob, oc, od, x, g1, sc2, sh2, w_out, ln_g, ln_b, w_router_t, nb, t):
    tm = 512
    nt = t // tm
    ntok = nb * t
    tok = lambda w: pl.BlockSpec((tm, w), lambda i: (i, 0))
    per_b = pl.BlockSpec((None, 1, D_MODEL), lambda i: (i // nt, 0, 0))
    full = lambda shape: pl.BlockSpec(shape, lambda i: tuple(0 for _ in shape))
    return pl.pallas_call(
        _outproj_kernel,
        grid=(ntok // tm,),
        in_specs=[tok(256), tok(256), tok(256), tok(256), tok(D_MODEL), per_b, per_b, per_b,
                  full((D_MODEL, D_MODEL)), full((1, D_MODEL)), full((1, D_MODEL)),
                  full((N_EXPERTS, D_MODEL))],
        out_specs=[tok(D_MODEL), tok(D_MODEL),
                   pl.BlockSpec((N_EXPERTS, tm), lambda i: (0, i))],
        out_shape=[jax.ShapeDtypeStruct((ntok, D_MODEL), F32),
                   jax.ShapeDtypeStruct((ntok, D_MODEL), BF16),
                   jax.ShapeDtypeStruct((N_EXPERTS, ntok), F32)],
        compiler_params=_cparams(("arbitrary",)),
        name="out_proj",
    )(oa, ob, oc, od, x, g1, sc2, sh2, w_out, ln_g, ln_b, w_router_t)


EXPERT_TF = 256


def _expert_kernel(x_ref, wg_ref, wu_ref, wd_ref, o_ref, acc_ref):
    x = x_ref[...]
    for c in range(D_FF // EXPERT_TF):
        cols = slice(c * EXPERT_TF, (c + 1) * EXPERT_TF)
        g = jnp.dot(x, wg_ref[:, cols], preferred_element_type=F32)
        u = jnp.dot(x, wu_ref[:, cols], preferred_element_type=F32)
        hmid = (g * (1.0 / (1.0 + jnp.exp(-g))) * u).astype(BF16)
        part = jnp.dot(hmid, wd_ref[cols, :], preferred_element_type=F32)
        if c == 0:
            acc_ref[...] = part
        else:
            acc_ref[...] += part
    o_ref[...] = acc_ref[...].astype(o_ref.dtype)


def _experts(xe, slots, wg, wu, wd):
    ne, _, d = xe.shape
    tm = math.gcd(slots, 1024)
    return pl.pallas_call(
        _expert_kernel,
        grid=(ne, slots // tm),
        in_specs=[pl.BlockSpec((None, tm, d), lambda e, m: (e, m, 0)),
                  pl.BlockSpec((None, d, D_FF), lambda e, m: (e, 0, 0)),
                  pl.BlockSpec((None, d, D_FF), lambda e, m: (e, 0, 0)),
                  pl.BlockSpec((None, D_FF, d), lambda e, m: (e, 0, 0))],
        out_specs=pl.BlockSpec((None, tm, d), lambda e, m: (e, m, 0)),
        out_shape=jax.ShapeDtypeStruct((ne, slots, d), BF16),
        scratch_shapes=[pltpu.VMEM((tm, d), F32)],
        compiler_params=_cparams(("arbitrary", "arbitrary")),
        name="expert_ffn",
    )(xe, wg, wu, wd)


MOE_TM = 512
MOE_WIN = 128
ROW_ALIGN = 16


def _select_kernel(aff_ref, o_ref, *, cap):
    a = aff_ref[...]
    keys = lax.bitcast_convert_type(a, jnp.int32)
    ne, n = a.shape
    capf = float(cap)

    def count(mask):
        return jnp.sum(jnp.where(mask, 1.0, 0.0), axis=1, keepdims=True)

    def key_bit(b, thr):
        cand = thr | lax.shift_left(jnp.int32(1), 30 - b)
        return jnp.where(count(keys >= cand) >= capf, cand, thr)

    thr = lax.fori_loop(0, 31, key_bit, jnp.zeros((ne, 1), jnp.int32))
    above = keys > thr
    need = capf - count(above)
    idx = lax.broadcasted_iota(jnp.int32, (ne, n), 1)
    tie_idx = jnp.where(keys == thr, idx, jnp.int32(2 ** 30))
    nbits = max(1, (n - 1).bit_length())

    def idx_bit(b, j):
        cand = j | lax.shift_left(jnp.int32(1), nbits - 1 - b)
        return jnp.where(count(tie_idx < cand) < need, cand, j)

    j = lax.fori_loop(0, nbits, idx_bit, jnp.zeros((ne, 1), jnp.int32))
    sel = above | (tie_idx <= j)
    o_ref[...] = jnp.where(sel, a, -1.0)


def _select(aff, cap):
    ne, n = aff.shape
    return pl.pallas_call(
        functools.partial(_select_kernel, cap=cap),
        out_shape=jax.ShapeDtypeStruct((ne, n), F32),
        compiler_params=pltpu.CompilerParams(vmem_limit_bytes=VMEM_LIMIT),
        name="ec_select",
    )(aff)


def _routing_tables(gs, group_tokens):
    ne = gs.shape[0]
    a_l, off_l, cnt_l, lim_l = [], [], [], []
    tok0, slot0 = 0, 0
    for n in group_tokens:
        cap = EC_FACTOR * n // N_EXPERTS
        nt = n // MOE_TM
        sel = lax.slice_in_dim(gs, tok0, tok0 + n, axis=1) >= 0
        counts = jnp.sum(sel.reshape(ne, nt, MOE_TM), axis=-1, dtype=jnp.int32)
        s0 = slot0 + jnp.cumsum(counts, axis=1) - counts
        a = (s0 // ROW_ALIGN) * ROW_ALIGN
        a_l.append(a)
        off_l.append(s0 - a)
        cnt_l.append(counts)
        lim_l.append(jnp.full((nt,), slot0 + cap - MOE_WIN, jnp.int32))
        tok0 += n
        slot0 += cap
    a = jnp.concatenate(a_l, axis=1).T
    off = jnp.concatenate(off_l, axis=1).T
    end = off + jnp.concatenate(cnt_l, axis=1).T
    rounds = (end + MOE_WIN - 1) // MOE_WIN
    gp = (end // ROW_ALIGN) * ROW_ALIGN
    return dict(a=a.reshape(-1).astype(jnp.int32), nr=rounds.reshape(-1).astype(jnp.int32),
                nrounds=jnp.max(rounds, axis=1).astype(jnp.int32), lim=jnp.concatenate(lim_l),
                off_col=off.astype(F32)[:, :, None], off_row=off.astype(F32)[:, None, :],
                off16=jnp.repeat(off.astype(F32), ROW_ALIGN, axis=1)[:, :, None],
                gp16=jnp.repeat(gp.astype(F32), ROW_ALIGN, axis=1)[:, :, None],
                total=slot0)


def _dispatch_kernel(a_tab, nrounds, nr_tab, gs_ref, off_ref, off16_ref, gp16_ref, x_ref, u_ref, xe_ref,
                     pos_ref, c_ref, c2_ref, stage_ref, carry_ref, sem, rc_ref):
    j = pl.program_id(0)
    ne, tm = gs_ref.shape
    d = x_ref.shape[1]
    win = MOE_WIN

    @pl.when(j == 0)
    def _():
        carry_ref[...] = jnp.zeros_like(carry_ref)
        rc_ref[0] = 0

    sel = gs_ref[...] >= 0.0
    rank = jnp.dot(jnp.where(sel, 1.0, 0.0).astype(BF16), u_ref[...], preferred_element_type=F32)
    pos_ref[...] = jnp.where(sel, rank + off_ref[...], -1.0)

    def wait_round(slot):
        def one(_, c):
            pltpu.make_async_copy(stage_ref.at[slot, pl.ds(0, win)], xe_ref.at[0, pl.ds(0, win)],
                                  sem.at[slot]).wait()
            return c
        lax.fori_loop(0, rc_ref[1 + slot], one, 0)

    def round_body(r, _):
        k = lax.broadcasted_iota(jnp.int32, (win, tm), 0).astype(F32) + (r * win).astype(F32)
        for e in range(ne):
            c_ref[e * win:(e + 1) * win, :] = jnp.where(pos_ref[e:e + 1, :] == k, 1.0, 0.0).astype(BF16)
        slot = rc_ref[0] % 2
        for nb in range(d // MXU_DIM):
            cols = slice(nb * MXU_DIM, (nb + 1) * MXU_DIM)
            stage_ref[slot, :, cols] = jnp.dot(c_ref[...], x_ref[:, cols],
                                               preferred_element_type=F32).astype(BF16)

        @pl.when(r == 0)
        def _():
            k16 = lax.broadcasted_iota(jnp.int32, (ROW_ALIGN, 1), 0).astype(F32)
            for e in range(ne):
                keep = k16 < off16_ref[e * ROW_ALIGN:(e + 1) * ROW_ALIGN, :]
                rows = pl.ds(e * win, ROW_ALIGN)
                stage_ref[slot, rows, :] = jnp.where(
                    keep, carry_ref[e * ROW_ALIGN:(e + 1) * ROW_ALIGN, :], stage_ref[slot, rows, :])

        @pl.when(rc_ref[0] > 0)
        def _():
            wait_round(1 - slot)

        rc_ref[1 + slot] = 0
        for e in range(ne):
            @pl.when(r < nr_tab[j * ne + e])
            def _():
                dst = pl.multiple_of(a_tab[j * ne + e] + r * win, ROW_ALIGN)
                pltpu.make_async_copy(stage_ref.at[slot, pl.ds(e * win, win)],
                                      xe_ref.at[e, pl.ds(dst, win)], sem.at[slot]).start()
                rc_ref[1 + slot] = rc_ref[1 + slot] + 1
        rc_ref[0] = rc_ref[0] + 1
        return 0

    lax.fori_loop(0, nrounds[j], round_body, 0)

    k16 = lax.broadcasted_iota(jnp.int32, (ROW_ALIGN, 1), 0).astype(F32)
    for e in range(ne):
        rows = slice(e * ROW_ALIGN, (e + 1) * ROW_ALIGN)
        c2_ref[rows, :] = jnp.where(pos_ref[e:e + 1, :] == gp16_ref[rows, :] + k16, 1.0, 0.0).astype(BF16)
    kk = jnp.concatenate([k16] * ne, axis=0)
    keep_old = (gp16_ref[...] == 0.0) & (kk < off16_ref[...])
    for nb in range(d // MXU_DIM):
        cols = slice(nb * MXU_DIM, (nb + 1) * MXU_DIM)
        new = jnp.dot(c2_ref[...], x_ref[:, cols], preferred_element_type=F32).astype(BF16)
        carry_ref[:, cols] = jnp.where(keep_old, carry_ref[:, cols], new)

    @pl.when((j == pl.num_programs(0) - 1) & (rc_ref[0] > 0))
    def _():
        wait_round((rc_ref[0] - 1) % 2)


def _dispatch(gs, h2, tabs, u_mat):
    ne, ntok = gs.shape
    d = h2.shape[1]
    nt = ntok // MOE_TM
    rows = tabs['total'] + MOE_WIN
    grid_spec = pltpu.PrefetchScalarGridSpec(
        num_scalar_prefetch=3,
        grid=(nt,),
        in_specs=[pl.BlockSpec((ne, MOE_TM), lambda j, *_: (0, j)),
                  pl.BlockSpec((None, ne, 1), lambda j, *_: (j, 0, 0)),
                  pl.BlockSpec((None, ne * ROW_ALIGN, 1), lambda j, *_: (j, 0, 0)),
                  pl.BlockSpec((None, ne * ROW_ALIGN, 1), lambda j, *_: (j, 0, 0)),
                  pl.BlockSpec((MOE_TM, d), lambda j, *_: (j, 0)),
                  pl.BlockSpec((MOE_TM, MOE_TM), lambda j, *_: (0, 0))],
        out_specs=pl.BlockSpec(memory_space=pl.ANY),
        scratch_shapes=[pltpu.VMEM((ne, MOE_TM), F32),
                        pltpu.VMEM((ne * MOE_WIN, MOE_TM), BF16),
                        pltpu.VMEM((ne * ROW_ALIGN, MOE_TM), BF16),
                        pltpu.VMEM((2, ne * MOE_WIN, d), BF16),
                        pltpu.VMEM((ne * ROW_ALIGN, d), BF16),
                        pltpu.SemaphoreType.DMA((2,)),
                        pltpu.SMEM((3,), jnp.int32)])
    return pl.pallas_call(
        _dispatch_kernel,
        grid_spec=grid_spec,
        out_shape=jax.ShapeDtypeStruct((ne, rows, d), BF16),
        compiler_params=_cparams(("arbitrary",)),
        name="ec_dispatch",
    )(tabs['a'], tabs['nrounds'], tabs['nr'], gs, tabs['off_col'], tabs['off16'], tabs['gp16'], h2,
      u_mat)


def _combine_kernel(a_tab, nrounds, lim_tab, gs_ref, off_ref, l_ref, ye_ref, x1_ref, g2_ref, lg_ref,
                    lb_ref, *refs, split_tiles):
    outs, (p_ref, y_ref, acc_ref, sem) = refs[:-4], refs[-4:]
    j = pl.program_id(0)
    tm, ne = gs_ref.shape
    d = x1_ref.shape[1]
    win = MOE_WIN
    gs = gs_ref[...]
    sel = gs >= 0.0
    rank = jnp.dot(l_ref[...], jnp.where(sel, 1.0, 0.0).astype(BF16), preferred_element_type=F32)
    pos = jnp.where(sel, rank + off_ref[...], -1.0)
    gate = jnp.where(sel, gs, 0.0)
    acc_ref[...] = jnp.zeros_like(acc_ref)
    n_tiles = pl.num_programs(0)
    slot = j % 2

    def window(tile, r, e):
        want = a_tab[tile * ne + e] + r * win
        src = pl.multiple_of(jnp.minimum(want, lim_tab[tile]), ROW_ALIGN)
        return src, want - src

    def start_round(tile, r, sl):
        for e in range(ne):
            src, _ = window(tile, r, e)
            pltpu.make_async_copy(ye_ref.at[e, pl.ds(src, win)], y_ref.at[sl, pl.ds(e * win, win)],
                                  sem.at[sl]).start()

    def wait_round(sl):
        for e in range(ne):
            pltpu.make_async_copy(ye_ref.at[e, pl.ds(0, win)], y_ref.at[sl, pl.ds(e * win, win)],
                                  sem.at[sl]).wait()

    @pl.when((j == 0) & (nrounds[0] > 0))
    def _():
        start_round(0, 0, 0)

    nxt = jnp.minimum(j + 1, n_tiles - 1)

    @pl.when((j + 1 < n_tiles) & (nrounds[nxt] > 0))
    def _():
        start_round(nxt, 0, 1 - slot)

    half = ne // 2

    def accumulate(r):
        base = lax.convert_element_type(r * win, F32)
        k = lax.broadcasted_iota(jnp.int32, (tm, win), 1).astype(F32) + base
        for grp in range(2):
            for e in range(grp * half, (grp + 1) * half):
                pe = pos[:, e:e + 1]
                pe = jnp.where(pe >= base, pe + window(j, r, e)[1].astype(F32), -1.0)
                pcol = jnp.broadcast_to(pe, (tm, win))
                gcol = jnp.broadcast_to(gate[:, e:e + 1], (tm, win))
                p_ref[:, e * win:(e + 1) * win] = jnp.where(pcol == k, gcol, 0.0).astype(BF16)
            rows = slice(grp * half * win, (grp + 1) * half * win)
            for nb in range(d // MXU_DIM):
                cols = slice(nb * MXU_DIM, (nb + 1) * MXU_DIM)
                acc_ref[:, cols] += jnp.dot(p_ref[:, rows], y_ref[slot, rows, cols],
                                            preferred_element_type=F32)

    @pl.when(nrounds[j] > 0)
    def _():
        wait_round(slot)
        accumulate(0)

    def extra_round(r, _):
        start_round(j, r, slot)
        wait_round(slot)
        accumulate(r)
        return 0

    lax.fori_loop(1, nrounds[j], extra_round, 0)
    y = DEEPNORM_ALPHA * x1_ref[...] + (1.0 + g2_ref[...]) * acc_ref[...]
    res = _layer_norm(y, lg_ref[...], lb_ref[...])
    if split_tiles is None:
        outs[0][...] = res
    else:
        @pl.when(j < split_tiles)
        def _():
            outs[0][...] = res

        @pl.when(j >= split_tiles)
        def _():
            outs[1][...] = res


def _combine_postnorm(gs_tok, ye, tabs, l_mat, x1, g2, ln_g, ln_b, t, split_tiles=None):
    ntok, ne = gs_tok.shape
    d = x1.shape[1]
    nt = ntok // MOE_TM
    tiles_per_seq = t // MOE_TM
    if split_tiles is None:
        out_specs = pl.BlockSpec((MOE_TM, d), lambda j, *_: (j, 0))
        out_shape = jax.ShapeDtypeStruct((ntok, d), F32)
    else:
        out_specs = [pl.BlockSpec((MOE_TM, d), lambda j, *_: (jnp.minimum(j, split_tiles - 1), 0)),
                     pl.BlockSpec((MOE_TM, d), lambda j, *_: (jnp.maximum(j - split_tiles, 0), 0))]
        out_shape = [jax.ShapeDtypeStruct((split_tiles * MOE_TM, d), F32),
                     jax.ShapeDtypeStruct((ntok - split_tiles * MOE_TM, d), F32)]
    grid_spec = pltpu.PrefetchScalarGridSpec(
        num_scalar_prefetch=3,
        grid=(nt,),
        in_specs=[pl.BlockSpec((MOE_TM, ne), lambda j, *_: (j, 0)),
                  pl.BlockSpec((None, 1, ne), lambda j, *_: (j, 0, 0)),
                  pl.BlockSpec((MOE_TM, MOE_TM), lambda j, *_: (0, 0)),
                  pl.BlockSpec(memory_space=pl.ANY),
                  pl.BlockSpec((MOE_TM, d), lambda j, *_: (j, 0)),
                  pl.BlockSpec((None, 1, d), lambda j, *_: (j // tiles_per_seq, 0, 0)),
                  pl.BlockSpec((1, d), lambda j, *_: (0, 0)),
                  pl.BlockSpec((1, d), lambda j, *_: (0, 0))],
        out_specs=out_specs,
        scratch_shapes=[pltpu.VMEM((MOE_TM, ne * MOE_WIN), BF16),
                        pltpu.VMEM((2, ne * MOE_WIN, d), BF16),
                        pltpu.VMEM((MOE_TM, d), F32),
                        pltpu.SemaphoreType.DMA((2,))])
    return pl.pallas_call(
        functools.partial(_combine_kernel, split_tiles=split_tiles),
        grid_spec=grid_spec,
        out_shape=out_shape,
        compiler_params=_cparams(("arbitrary",)),
        name="ec_combine",
    )(tabs['a'], tabs['nrounds'], tabs['lim'], gs_tok, tabs['off_row'], l_mat, ye, x1, g2, ln_g, ln_b)


def _prep_w_in(w_in_l):
    sizes = (256, 256, 256, MLA_Q_RANK, MLA_KV_RANK, MLA_ROPE, 256, 256, 256, 256, 256, 256)
    offs = np.concatenate([[0], np.cumsum(sizes)])
    part = [w_in_l[:, offs[i]:offs[i + 1]] for i in range(len(sizes))]
    a_q, a_k, a_v, b_cq, b_ckv, b_kr, c_q, c_k, c_v, d_q, d_k, d_v = part
    d = w_in_l.shape[0]
    zeros = lambda n: jnp.zeros((d, n), w_in_l.dtype)
    def per_head(v):
        out = []
        for h in range(4):
            out += [v[:, h * 64:(h + 1) * 64], zeros(LANES - 64)]
        return out

    cols = ([a_q, a_k] + per_head(a_v) + [c_q, c_k] + per_head(c_v)
            + [d_q, d_k, d_v, b_cq, zeros(256 - MLA_Q_RANK), b_ckv, b_kr, b_kr,
               zeros(128 - 2 * MLA_ROPE)])
    return jnp.concatenate(cols, axis=1).astype(BF16)


def _prep_w_uq(w_uq_l):
    hd = MLA_NOPE + MLA_ROPE
    nope = [w_uq_l[:, h * hd:h * hd + MLA_NOPE] for h in range(MLA_HEADS)]
    rope = [w_uq_l[:, h * hd + MLA_NOPE:(h + 1) * hd] for h in range(MLA_HEADS)]
    z = jnp.zeros((w_uq_l.shape[0], 256 - 2 * hd), w_uq_l.dtype)
    cols = []
    for p in range(2):
        cols += [nope[2 * p], nope[2 * p + 1], rope[2 * p], rope[2 * p + 1], z]
    w = jnp.concatenate(cols, axis=1)
    w = jnp.concatenate([w, jnp.zeros((256 - MLA_Q_RANK, w.shape[1]), w.dtype)], axis=0)
    return w.astype(BF16)


def _prep_w_ukv(w_ukv_l):
    hd = MLA_NOPE + MLA_DV
    kn = [w_ukv_l[:, h * hd:h * hd + MLA_NOPE] for h in range(MLA_HEADS)]
    z = jnp.zeros((w_ukv_l.shape[0], LANES - MLA_DV), w_ukv_l.dtype)
    vv = []
    for h in range(MLA_HEADS):
        vv += [w_ukv_l[:, h * hd + MLA_NOPE:(h + 1) * hd], z]
    return jnp.concatenate(kn + vv, axis=1).astype(BF16)


def _select_groups(aff_t, group_tokens):
    parts, off = [], 0
    for n in group_tokens:
        parts.append(_select(lax.slice_in_dim(aff_t, off, off + n, axis=1),
                             EC_FACTOR * n // N_EXPERTS))
        off += n
    return jnp.concatenate(parts, axis=1)


def _trunk(x, c, group_tokens, nb, t, p):
    ntok = nb * t
    tabs = (_rope_tables(t, 256, DA_DQK, DA_DQK // ROPE_FRACTION, ROPE_THETA)
            + _rope_tables(t, 256, DIL_DH, DIL_DH // ROPE_FRACTION, ROPE_THETA)
            + _rope_tables(t, 128, MLA_ROPE, MLA_ROPE, MLA_ROPE_THETA))
    gmat = jnp.asarray(np.kron(np.eye(4), np.full((64, 64), 1.0 / 64)), BF16)
    ti = jnp.arange(MOE_TM)
    u_mat = (ti[:, None] < ti[None, :]).astype(BF16)
    l_mat = (ti[None, :] < ti[:, None]).astype(BF16)
    for l in range(DEPTH):
        mod = _modulation(c, p['w_ada'][l], p['b_ada'][l])
        sh1, sc1, g1, sh2, sc2, g2 = [m.reshape(nb, 1, D_MODEL) for m in jnp.split(mod, 6, axis=-1)]
        gq = jnp.concatenate([p['q_norm_g'][l], jnp.zeros((256 - MLA_Q_RANK,), F32)]).reshape(1, 256)
        gkv = p['kv_norm_g'][l].reshape(1, 128)
        proj, projv = _inproj(x, sc1, sh1, _prep_w_in(p['w_in'][l]), _prep_w_uq(p['w_uq'][l]),
                              _prep_w_ukv(p['w_ukv'][l]), gq, gkv, tabs, nb, t)
        lam_init = 0.8 - 0.6 * math.exp(-0.3 * l)
        lam = (jnp.exp(jnp.sum(p['da_lq1'][l] * p['da_lk1'][l]))
               - jnp.exp(jnp.sum(p['da_lq2'][l] * p['da_lk2'][l])) + lam_init).reshape(1)
        g_sub = jnp.tile(p['da_subln_g'][l], DA_HEADS).reshape(1, 256)
        oa = _diff_attention(proj, projv, lam, g_sub, gmat, nb, t, 1.0 - lam_init)
        ob = _mla_attention(proj, projv, nb, t)
        oc = _dil_attention(proj, projv, nb, t)
        od = _na_attention(proj, _na_bias_table(p['na_rpb'][l]), nb, t)
        x1, h2, aff_t = _outproj(oa, ob, oc, od, x, g1, sc2, sh2, p['w_out'][l].astype(BF16),
                                 p['ln1_g'][l].reshape(1, -1), p['ln1_b'][l].reshape(1, -1),
                                 p['w_router'][l].T, nb, t)
        wg = p['w_e_gate'][l].astype(BF16)
        wu = p['w_e_up'][l].astype(BF16)
        wd = p['w_e_down'][l].astype(BF16)
        gs = _select_groups(aff_t, group_tokens)
        rt = _routing_tables(gs, group_tokens)
        xe = _dispatch(gs, h2, rt, u_mat)
        ye = _experts(xe, rt['total'], wg, wu, wd)
        split = group_tokens[0] // MOE_TM if l == DEPTH - 1 else None
        x = _combine_postnorm(gs.T, ye, rt, l_mat, x1, g2, p['ln2_g'][l].reshape(1, -1),
                              p['ln2_b'][l].reshape(1, -1), t, split_tiles=split)
    return x


def kernel(x_prompt, x_sample, c_prompt, c_sample, w_in, w_uq, w_ukv, q_norm_g, kv_norm_g, da_lq1,
           da_lk1, da_lq2, da_lk2, da_subln_g, na_rpb, w_out, w_ada, b_ada, ln1_g, ln1_b, ln2_g,
           ln2_b, w_router, w_e_gate, w_e_up, w_e_down):
    p = dict(w_in=w_in, w_uq=w_uq, w_ukv=w_ukv, q_norm_g=q_norm_g, kv_norm_g=kv_norm_g,
             da_lq1=da_lq1, da_lk1=da_lk1, da_lq2=da_lq2, da_lk2=da_lk2, da_subln_g=da_subln_g,
             na_rpb=na_rpb, w_out=w_out, w_ada=w_ada, b_ada=b_ada, ln1_g=ln1_g, ln1_b=ln1_b,
             ln2_g=ln2_g, ln2_b=ln2_b, w_router=w_router, w_e_gate=w_e_gate, w_e_up=w_e_up,
             w_e_down=w_e_down)
    bp, t, d = x_prompt.shape
    bs = x_sample.shape[0]
    assert x_sample.shape[1] == t
    nb = bp + bs
    x = jnp.concatenate([x_prompt.reshape(bp * t, d), x_sample.reshape(bs * t, d)], axis=0)
    c = jnp.concatenate([c_prompt, c_sample], axis=0)
    y_prompt, y_sample = _trunk(x, c, (bp * t, bs * t), nb, t, p)
    return y_prompt.reshape(bp, t, d), y_sample.reshape(bs, t, d)
```

```python
import functools
import math

import jax
import jax.numpy as jnp
import numpy as np
from jax import lax
from jax.experimental import pallas as pl
from jax.experimental.pallas import tpu as pltpu

F32 = jnp.float32
BF16 = jnp.bfloat16

D_MODEL = 1024
DEPTH = 2
GRID_W = 64
GROUP_W = 256
DA_HEADS, DA_DV, DA_DQK = 4, 64, 32
MLA_HEADS, MLA_Q_RANK, MLA_KV_RANK, MLA_NOPE, MLA_ROPE, MLA_DV = 4, 192, 128, 64, 32, 64
MLA_ROPE_THETA = 10000.0
DIL_HEADS, DIL_DH = 4, 64
DIL_PATTERNS = ((128, 1), (512, 4), (2048, 16))
NA_HEADS, NA_DH, NA_KR, NA_KC = 4, 64, 8, 16
ROPE_THETA = 500000.0
ROPE_FRACTION = 4
N_EXPERTS = 16
EC_FACTOR = 2
D_FF = 2816
DEEPNORM_ALPHA = (2.0 * DEPTH) ** 0.25
NEG_INF = -1e30
LOG2E = math.log2(math.e)
LN_EPS = 1e-5
RMS_EPS = 1e-6

LANES = 128
MXU_DIM = 256
VMEM_LIMIT = 56 * 1024 * 1024

S_AQ, S_AK, S_BQ0, S_BQ1, S_BK0, S_BK1, S_CQ, S_CK, S_DQ, S_DK, S_DV = range(11)
N_SLOTS = 11
V_A, V_B, V_C = 0, 4, 8
N_VSLOTS = 12
V_ONE_LANE = 64
W_ALL_COLS = 13 * 256


def _cparams(sem):
    return pltpu.CompilerParams(dimension_semantics=sem, vmem_limit_bytes=VMEM_LIMIT)


def _split_bf16(a):
    hi = a.astype(BF16)
    lo = (a - hi.astype(F32)).astype(BF16)
    return hi, lo


def _mod_kernel(c_ref, w_ref, b_ref, o_ref):
    c = c_ref[...]
    a = c * (1.0 / (1.0 + jnp.exp(-c)))
    a_hi, a_lo = _split_bf16(a)
    w_hi, w_lo = _split_bf16(w_ref[...])
    acc = jnp.dot(a_hi, w_hi, preferred_element_type=F32)
    acc += jnp.dot(a_hi, w_lo, preferred_element_type=F32)
    acc += jnp.dot(a_lo, w_hi, preferred_element_type=F32)
    o_ref[...] = acc + b_ref[...]


def _modulation(c, w_ada, b_ada):
    nb, d = c.shape
    n_out = w_ada.shape[1]
    tn = 1536
    return pl.pallas_call(
        _mod_kernel,
        grid=(n_out // tn,),
        in_specs=[pl.BlockSpec((nb, d), lambda j: (0, 0)),
                  pl.BlockSpec((d, tn), lambda j: (0, j)),
                  pl.BlockSpec((1, tn), lambda j: (0, j))],
        out_specs=pl.BlockSpec((nb, tn), lambda j: (0, j)),
        out_shape=jax.ShapeDtypeStruct((nb, n_out), F32),
        compiler_params=_cparams(("arbitrary",)),
        name="adaln_mod",
    )(c, w_ada, b_ada.reshape(1, n_out))


def _rope_tables(t, width, group, rot, theta):
    half = rot // 2
    inv = theta ** (-jnp.arange(half, dtype=F32) / half)
    ang = jnp.arange(t, dtype=F32)[:, None] * inv[None, :]
    cos, sin = jnp.cos(ang), jnp.sin(ang)
    ones = jnp.ones((t, group - rot), F32)
    zeros = jnp.zeros((t, group - rot), F32)
    c_g = jnp.concatenate([cos, cos, ones], axis=1)
    s_g = jnp.concatenate([-sin, sin, zeros], axis=1)
    reps = width // group
    return jnp.tile(c_g, (1, reps)), jnp.tile(s_g, (1, reps))


def _apply_rope(x, c_tab, s_tab, group, rot):
    width = x.shape[-1]
    half = rot // 2
    lane = lax.broadcasted_iota(jnp.int32, (1, width), 1)
    first = (lane % group) < half
    fwd = pltpu.roll(x, width - half, 1)
    bwd = pltpu.roll(x, half, 1)
    return x * c_tab + s_tab * jnp.where(first, fwd, bwd)


def _inproj_kernel(x_ref, sc_ref, sh_ref, w_ref, wuq_ref, wukv_ref, gq_ref, gkv_ref,
                   ca_ref, sa_ref, cc_ref, scc_ref, cm_ref, sm_ref, o_ref, ov_ref):
    h = (x_ref[...] * (1.0 + sc_ref[...]) + sh_ref[...]).astype(BF16)

    def proj(col, width):
        return jnp.dot(h, w_ref[:, col:col + width], preferred_element_type=F32)

    one_lane = lax.broadcasted_iota(jnp.int32, (1, LANES), 1) == V_ONE_LANE

    def put_values(first_slot, vals):
        for hh in range(4):
            v = vals[:, LANES * hh:LANES * (hh + 1)]
            ov_ref[first_slot + hh] = jnp.where(one_lane, 1.0, v).astype(BF16)

    sa_scale = DA_DQK ** -0.5 * LOG2E
    sb_scale = (MLA_NOPE + MLA_ROPE) ** -0.5 * LOG2E
    sc_scale = DIL_DH ** -0.5 * LOG2E
    sd_scale = NA_DH ** -0.5 * LOG2E
    a_rot = DA_DQK // ROPE_FRACTION
    c_rot = DIL_DH // ROPE_FRACTION

    cm, sm = cm_ref[...], sm_ref[...]
    cq = proj(2816, 256)
    cq = cq * lax.rsqrt(jnp.sum(cq * cq, -1, keepdims=True) * (1.0 / MLA_Q_RANK) + RMS_EPS)
    cq = (cq * gq_ref[...]).astype(BF16)
    q2 = jnp.dot(cq, wuq_ref[...], preferred_element_type=F32)
    for p in range(2):
        qp = q2[:, 256 * p:256 * (p + 1)]
        o_ref[S_BQ0 + p, :, 0:128] = (qp[:, 0:128] * sb_scale).astype(BF16)
        o_ref[S_BQ0 + p, :, 128:256] = (
            _apply_rope(qp[:, 128:256], cm, sm, MLA_ROPE, MLA_ROPE) * sb_scale).astype(BF16)
    ckv = proj(3072, 128)
    ckv = ckv * lax.rsqrt(jnp.mean(ckv * ckv, -1, keepdims=True) + RMS_EPS)
    ckv = (ckv * gkv_ref[...]).astype(BF16)
    kv = jnp.dot(ckv, wukv_ref[...], preferred_element_type=F32)
    kr = _apply_rope(proj(3200, 128), cm, sm, MLA_ROPE, MLA_ROPE).astype(BF16)
    for p in range(2):
        o_ref[S_BK0 + p, :, 0:128] = kv[:, 128 * p:128 * (p + 1)].astype(BF16)
        o_ref[S_BK0 + p, :, 128:256] = kr
    put_values(V_B, kv[:, 256:768])

    ca, sa = ca_ref[...], sa_ref[...]
    o_ref[S_AQ] = (_apply_rope(proj(0, 256), ca, sa, DA_DQK, a_rot) * sa_scale).astype(BF16)
    o_ref[S_AK] = _apply_rope(proj(256, 256), ca, sa, DA_DQK, a_rot).astype(BF16)
    put_values(V_A, proj(512, 512))
    cc, scc = cc_ref[...], scc_ref[...]
    o_ref[S_CQ] = (_apply_rope(proj(1024, 256), cc, scc, DIL_DH, c_rot) * sc_scale).astype(BF16)
    o_ref[S_CK] = _apply_rope(proj(1280, 256), cc, scc, DIL_DH, c_rot).astype(BF16)
    put_values(V_C, proj(1536, 512))
    o_ref[S_DQ] = (proj(2048, 256) * sd_scale).astype(BF16)
    o_ref[S_DK] = proj(2304, 256).astype(BF16)
    o_ref[S_DV] = proj(2560, 256).astype(BF16)


def _inproj(x, sc, sh, w_all, wuq, wukv, gq, gkv, tabs, nb, t):
    tm = 512
    nt = t // tm
    ntok = nb * t
    ca, sa, cc, scc, cm, sm = tabs
    full = lambda shape: pl.BlockSpec(shape, lambda j, b: tuple(0 for _ in shape))
    tab = lambda w: pl.BlockSpec((tm, w), lambda j, b: (j, 0))
    return pl.pallas_call(
        _inproj_kernel,
        grid=(nt, nb),
        in_specs=[pl.BlockSpec((tm, D_MODEL), lambda j, b: (b * nt + j, 0)),
                  pl.BlockSpec((None, 1, D_MODEL), lambda j, b: (b, 0, 0)),
                  pl.BlockSpec((None, 1, D_MODEL), lambda j, b: (b, 0, 0)),
                  full((D_MODEL, W_ALL_COLS)), full((256, 512)), full((128, 768)),
                  full((1, 256)), full((1, 128)),
                  tab(256), tab(256), tab(256), tab(256), tab(128), tab(128)],
        out_specs=[pl.BlockSpec((N_SLOTS, tm, 256), lambda j, b: (0, b * nt + j, 0)),
                   pl.BlockSpec((N_VSLOTS, tm, LANES), lambda j, b: (0, b * nt + j, 0))],
        out_shape=[jax.ShapeDtypeStruct((N_SLOTS, ntok, 256), BF16),
                   jax.ShapeDtypeStruct((N_VSLOTS, ntok, LANES), BF16)],
        compiler_params=_cparams(("arbitrary", "arbitrary")),
        name="in_proj",
    )(x, sc, sh, w_all, wuq, wukv, gq, gkv, ca, sa, cc, scc, cm, sm)


def _lane_mask(width, ranges):
    lane = lax.broadcasted_iota(jnp.int32, (1, width), 1)
    m = None
    for lo, hi in ranges:
        r = (lane >= lo) & (lane < hi)
        m = r if m is None else (m | r)
    return m


SOFTMAX_SLAB = 32


def _chain_scratch(tq, kc):
    return [pltpu.VMEM((tq, kc), F32), pltpu.VMEM((tq, kc), F32), pltpu.VMEM((tq, kc), BF16),
            pltpu.VMEM((tq, LANES), F32), pltpu.VMEM((tq, LANES), F32), pltpu.VMEM((tq, LANES), F32)]


CHAIN_REFS = 6


def _attention_steps(n_steps, qms_for, k_for, v_for, finish, tq, t, kc, chains, slab=SOFTMAX_SLAB,
                     bias_ref=None, step_unroll=1):
    n_chunks = t // kc
    assert n_chunks % 2 == 0
    nt = (((1,), (1,)), ((), ()))

    def chunk_rows(chunk):
        if isinstance(chunk, int):
            return pl.ds(chunk * kc, kc)
        return pl.ds(pl.multiple_of(chunk * kc, kc), kc)

    def scores(step, chunk, slot):
        k_refs = k_for(step)
        if not isinstance(k_refs, (list, tuple)):
            k_refs = [k_refs] * len(chains)
        loaded = {}
        for qm, k_ref, chain in zip(qms_for(step), k_refs, chains):
            if id(k_ref) not in loaded:
                loaded[id(k_ref)] = k_ref[chunk_rows(chunk), :]
            chain[slot][...] = lax.dot_general(qm, loaded[id(k_ref)], nt, preferred_element_type=F32)

    def softmax_pv(step, chunk, slot):
        for v_ref, chain in zip(v_for(step), chains):
            v = v_ref[chunk_rows(chunk), :]
            s_ref = chain[slot]
            (p_ref, m_ref, a_ref, acc_ref) = chain[2:]
            for r in range(tq // slab):
                rows = slice(r * slab, (r + 1) * slab)
                s = s_ref[rows, :]
                if bias_ref is not None:
                    s = s + bias_ref[rows, chunk * kc:(chunk + 1) * kc]
                m_prev = m_ref[rows, :]
                m_new = jnp.maximum(m_prev, jnp.max(s, -1, keepdims=True))
                d = s - jnp.concatenate([m_new] * (kc // LANES), axis=1)
                p_ref[rows, :] = jnp.exp2(d.astype(BF16))
                a_ref[rows, :] = jnp.exp2(m_prev - m_new)
                m_ref[rows, :] = m_new
            acc_ref[...] = a_ref[...] * acc_ref[...] + jnp.dot(p_ref[...], v,
                                                              preferred_element_type=F32)

    def step_body(st, _):
        for (_, _, _, m_ref, _, acc_ref) in chains:
            m_ref[...] = jnp.full(m_ref.shape, NEG_INF, F32)
            acc_ref[...] = jnp.zeros(acc_ref.shape, F32)
        nxt = min(st + 1, n_steps - 1) if isinstance(st, int) else jnp.minimum(st + 1, n_steps - 1)

        def body(j, _):
            scores(st, 2 * j + 1, 1)
            softmax_pv(st, 2 * j, 0)
            if 2 * j + 2 < n_chunks:
                scores(st, 2 * j + 2, 0)
            elif n_steps > 1:
                scores(nxt, 0, 0)
            softmax_pv(st, 2 * j + 1, 1)
            return 0

        for j in range(n_chunks // 2):
            body(j, 0)
        outs = []
        for chain in chains:
            acc = chain[5][...]
            den = jnp.broadcast_to(acc[:, V_ONE_LANE:V_ONE_LANE + 1], acc.shape)
            outs.append(acc * (1.0 / den))
        finish(st, outs)
        return 0

    scores(0, 0, 0)
    if step_unroll == n_steps:
        for st in range(n_steps):
            step_body(st, 0)
    else:
        lax.fori_loop(0, n_steps, step_body, 0, unroll=step_unroll)


def _place_head(pair_ref, o, j):
    lane = lax.broadcasted_iota(jnp.int32, (1, LANES), 1)
    low = lane < V_ONE_LANE
    shifted = pltpu.roll(o, V_ONE_LANE, 1)
    cur = pair_ref[...]
    pair_ref[...] = jnp.where(low, jnp.where(j == 0, o, cur), jnp.where(j == 1, shifted, cur))


def _group_mean_sq(x, gmat):
    sq = x * x
    hi, lo = _split_bf16(sq)
    return (jnp.dot(hi, gmat, preferred_element_type=F32)
            + jnp.dot(lo, gmat, preferred_element_type=F32))


def _diff_attn_kernel(lam_ref, q_ref, k_ref, v_ref, g_ref, gmat_ref, o_ref, acc_ref, *scratch, t, kc,
                      out_scale):
    chains = (scratch[:CHAIN_REFS], scratch[CHAIN_REFS:])
    lam = lam_ref[0]
    lane = lax.broadcasted_iota(jnp.int32, (1, 256), 1)

    def qms_for(h):
        q = q_ref[...]
        return [jnp.where((lane >= (2 * h + c) * DA_DQK) & (lane < (2 * h + c + 1) * DA_DQK), q,
                          jnp.zeros_like(q)) for c in range(2)]

    def finish(h, outs):
        _place_head(acc_ref.at[h // 2], outs[0] - lam * outs[1], h % 2)

    acc_ref[...] = jnp.zeros_like(acc_ref)
    _attention_steps(DA_HEADS, qms_for, lambda h: k_ref, lambda h: [v_ref.at[h], v_ref.at[h]],
                     finish, q_ref.shape[0], t, kc, chains)
    o = jnp.concatenate([acc_ref[0], acc_ref[1]], axis=1)
    ms = _group_mean_sq(o, gmat_ref[...])
    o_ref[...] = (o * lax.rsqrt(ms + RMS_EPS) * g_ref[...] * out_scale).astype(o_ref.dtype)


def _diff_attention(proj, projv, lam, g_tiled, gmat, nb, t, out_scale):
    tq, kc = 512, 512
    nq = t // tq
    kern = functools.partial(_diff_attn_kernel, t=t, kc=kc, out_scale=out_scale)
    return pl.pallas_call(
        kern,
        grid=(nb, nq),
        in_specs=[pl.BlockSpec(memory_space=pltpu.SMEM),
                  pl.BlockSpec((None, tq, 256), lambda b, i: (S_AQ, b * nq + i, 0)),
                  pl.BlockSpec((None, t, 256), lambda b, i: (S_AK, b, 0)),
                  pl.BlockSpec((DA_HEADS, t, LANES), lambda b, i: (V_A // DA_HEADS, b, 0)),
                  pl.BlockSpec((1, 256), lambda b, i: (0, 0)),
                  pl.BlockSpec((256, 256), lambda b, i: (0, 0))],
        out_specs=pl.BlockSpec((tq, 256), lambda b, i: (b * nq + i, 0)),
        out_shape=jax.ShapeDtypeStruct((nb * t, 256), BF16),
        scratch_shapes=[pltpu.VMEM((2, tq, LANES), F32)] + 2 * _chain_scratch(tq, kc),
        compiler_params=_cparams(("arbitrary", "arbitrary")),
        name="diff_attn",
    )(lam, proj, proj, projv, g_tiled, gmat)


def _mla_attn_kernel(q_ref, k_ref, v_ref, o_ref, acc_ref, *scratch, t, kc):
    chains = tuple(scratch[c * CHAIN_REFS:(c + 1) * CHAIN_REFS] for c in range(MLA_HEADS))
    lane = lax.broadcasted_iota(jnp.int32, (1, 256), 1)
    k_pairs = [k_ref.at[0], k_ref.at[1]]

    def qms_for(_):
        qms = []
        for h in range(MLA_HEADS):
            q, j = q_ref[h // 2], h % 2
            nope = (lane >= j * MLA_NOPE) & (lane < (j + 1) * MLA_NOPE)
            rope = (lane >= 128 + j * MLA_ROPE) & (lane < 128 + (j + 1) * MLA_ROPE)
            qms.append(jnp.where(nope | rope, q, jnp.zeros_like(q)))
        return qms

    def finish(_, outs):
        for h in range(MLA_HEADS):
            _place_head(acc_ref.at[h // 2], outs[h], h % 2)

    acc_ref[...] = jnp.zeros_like(acc_ref)
    _attention_steps(1, qms_for, lambda _: [k_pairs[h // 2] for h in range(MLA_HEADS)],
                     lambda _: [v_ref.at[h] for h in range(MLA_HEADS)], finish, q_ref.shape[1], t, kc,
                     chains, step_unroll=1)
    o_ref[...] = jnp.concatenate([acc_ref[0], acc_ref[1]], axis=1).astype(o_ref.dtype)


def _mla_attention(proj, projv, nb, t):
    tq, kc = 512, 512
    nq = t // tq
    kern = functools.partial(_mla_attn_kernel, t=t, kc=kc)
    return pl.pallas_call(
        kern,
        grid=(nb, nq),
        in_specs=[pl.BlockSpec((2, tq, 256), lambda b, i: (S_BQ0 // 2, b * nq + i, 0)),
                  pl.BlockSpec((2, t, 256), lambda b, i: (S_BK0 // 2, b, 0)),
                  pl.BlockSpec((MLA_HEADS, t, LANES), lambda b, i: (V_B // MLA_HEADS, b, 0))],
        out_specs=pl.BlockSpec((tq, 256), lambda b, i: (b * nq + i, 0)),
        out_shape=jax.ShapeDtypeStruct((nb * t, 256), BF16),
        scratch_shapes=[pltpu.VMEM((2, tq, LANES), F32)] + MLA_HEADS * _chain_scratch(tq, kc),
        compiler_params=_cparams(("arbitrary", "arbitrary")),
        name="mla_attn",
    )(proj, proj, projv)


DIL_REACH = max(w // 2 for w, _ in DIL_PATTERNS)


def _dil_attn_kernel(q_ref, k_ref, v_ref, o_ref, tables_ref, acc_ref, *scratch, t, tq, band):
    chains = tuple(scratch[c * CHAIN_REFS:(c + 1) * CHAIN_REFS] for c in range(DIL_HEADS))
    i = pl.program_id(1)

    def band_start(blk):
        return jnp.clip(blk * tq - DIL_REACH, 0, t - band)

    start = pl.multiple_of(band_start(i), tq)
    table = (i * tq - start) // tq

    @pl.when((pl.program_id(0) == 0) & (i == 0))
    def _():
        def build(n, _):
            qi = lax.broadcasted_iota(jnp.int32, (tq, band), 0)
            kj = lax.broadcasted_iota(jnp.int32, (tq, band), 1)
            delta = kj - qi - n * tq
            ad = jnp.abs(delta)
            cnt = jnp.zeros((tq, band), F32)
            for window, dil in DIL_PATTERNS:
                ok = (ad <= window // 2) & ((delta & (dil - 1)) == 0)
                cnt = cnt + jnp.where(ok, 1.0, 0.0)
            tables_ref[n] = jnp.where(cnt > 2.5, math.log2(3.0),
                                      jnp.where(cnt > 1.5, 1.0,
                                                jnp.where(cnt > 0.5, 0.0, NEG_INF)))
            return 0

        lax.fori_loop(0, tables_ref.shape[0], build, 0)

    bias_ref = tables_ref.at[table]

    lane = lax.broadcasted_iota(jnp.int32, (1, 256), 1)

    def qms_for(_):
        q = q_ref[...]
        return [jnp.where((lane >= h * DIL_DH) & (lane < (h + 1) * DIL_DH), q, jnp.zeros_like(q))
                for h in range(DIL_HEADS)]

    def finish(_, outs):
        for h in range(DIL_HEADS):
            _place_head(acc_ref.at[h // 2], outs[h], h % 2)

    acc_ref[...] = jnp.zeros_like(acc_ref)
    _attention_steps(1, qms_for, lambda _: k_ref.at[pl.ds(start, band)],
                     lambda _: [v_ref.at[h, pl.ds(start, band)] for h in range(DIL_HEADS)],
                     finish, tq, band, band // 2, chains, slab=DIL_SLAB, bias_ref=bias_ref,
                     step_unroll=1)
    o_ref[...] = jnp.concatenate([acc_ref[0], acc_ref[1]], axis=1).astype(o_ref.dtype)


DIL_SLAB = 16


def _dil_attention(proj, projv, nb, t):
    tq = 256
    band = min(t, tq + 2 * DIL_REACH)
    nq = t // tq
    kern = functools.partial(_dil_attn_kernel, t=t, tq=tq, band=band)
    return pl.pallas_call(
        kern,
        grid=(nb, nq),
        in_specs=[pl.BlockSpec((None, tq, 256), lambda b, i: (S_CQ, b * nq + i, 0)),
                  pl.BlockSpec((None, t, 256), lambda b, i: (S_CK, b, 0)),
                  pl.BlockSpec((DIL_HEADS, t, LANES), lambda b, i: (V_C // DIL_HEADS, b, 0))],
        out_specs=pl.BlockSpec((tq, 256), lambda b, i: (b * nq + i, 0)),
        out_shape=jax.ShapeDtypeStruct((nb * t, 256), BF16),
        scratch_shapes=([pltpu.VMEM((band // tq, tq, band), F32), pltpu.VMEM((2, tq, LANES), F32)]
                        + DIL_HEADS * _chain_scratch(tq, band // 2)),
        compiler_params=_cparams(("arbitrary", "arbitrary")),
        name="dil_attn",
    )(proj, proj, projv)


def _na_bias_table(rpb):
    c = np.arange(GRID_W)
    cs = np.clip(c - NA_KC // 2, 0, GRID_W - NA_KC)
    colmask = (c[None, :] >= cs[:, None]) & (c[None, :] < cs[:, None] + NA_KC)
    rows = jnp.stack([rpb[:, si:si + NA_KR, :] for si in range(NA_KR)], axis=1).astype(F32)
    edge = GRID_W - NA_KC
    padded = jnp.concatenate([jnp.repeat(rows[..., :1], edge, axis=-1), rows,
                              jnp.repeat(rows[..., -1:], edge, axis=-1)], axis=-1)
    b = jnp.stack([padded[..., GRID_W - 1 - qc:2 * GRID_W - 1 - qc] for qc in range(GRID_W)],
                  axis=2)
    b = jnp.where(colmask[None, None, :, None, :], b * LOG2E, NEG_INF)
    b = b.transpose(1, 0, 2, 3, 4)
    return b.reshape(NA_KR, rpb.shape[0] * GRID_W, NA_KR * GRID_W)


def _na_attn_kernel(q_ref, k_ref, v_ref, tb_ref, o_ref, *, rows, rg):
    g = pl.program_id(1)
    lane = lax.broadcasted_iota(jnp.int32, (1, 256), 1)
    nk = NA_KR * GRID_W
    sels = [(lane >= h * NA_DH) & (lane < (h + 1) * NA_DH) for h in range(NA_HEADS)]

    def row(r, _):
        grow = g * rg + r
        rs = jnp.clip(grow - NA_KR // 2, 0, rows - NA_KR)
        si = rs - grow + (NA_KR - 1)
        q = q_ref[r * GRID_W:(r + 1) * GRID_W, :]
        koff = pl.multiple_of(rs * GRID_W, GRID_W)
        kb = k_ref[pl.ds(koff, nk), :]
        vb = v_ref[pl.ds(koff, nk), :]
        q4 = jnp.concatenate([jnp.where(sel, q, jnp.zeros_like(q)) for sel in sels], axis=0)
        s = lax.dot_general(q4, kb, (((1,), (1,)), ((), ())), preferred_element_type=F32)
        s = s + tb_ref[si]
        m = jnp.max(s, -1, keepdims=True)
        p = jnp.exp2(s - m)
        l = jnp.sum(p, -1, keepdims=True)
        o4 = jnp.dot(p.astype(BF16), vb, preferred_element_type=F32) * (1.0 / l)
        out = jnp.zeros((GRID_W, 256), F32)
        for h, sel in enumerate(sels):
            out = jnp.where(sel, o4[h * GRID_W:(h + 1) * GRID_W], out)
        o_ref[r * GRID_W:(r + 1) * GRID_W, :] = out.astype(o_ref.dtype)
        return 0

    for r in range(rg):
        row(r, 0)


def _na_attention(proj, tb, nb, t):
    rows = t // GRID_W
    assert rows >= NA_KR
    rg = 8
    ng = rows // rg
    tq = rg * GRID_W
    kern = functools.partial(_na_attn_kernel, rows=rows, rg=rg)
    return pl.pallas_call(
        kern,
        grid=(nb, ng),
        in_specs=[pl.BlockSpec((None, tq, 256), lambda b, i: (S_DQ, b * ng + i, 0)),
                  pl.BlockSpec((None, t, 256), lambda b, i: (S_DK, b, 0)),
                  pl.BlockSpec((None, t, 256), lambda b, i: (S_DV, b, 0)),
                  pl.BlockSpec(tb.shape, lambda b, i: (0, 0, 0))],
        out_specs=pl.BlockSpec((tq, 256), lambda b, i: (b * ng + i, 0)),
        out_shape=jax.ShapeDtypeStruct((nb * t, 256), BF16),
        compiler_params=_cparams(("arbitrary", "arbitrary")),
        name="na_attn",
    )(proj, proj, proj, tb)


def _layer_norm(y, g, b):
    mu = jnp.mean(y, -1, keepdims=True)
    yc = y - mu
    var = jnp.mean(yc * yc, -1, keepdims=True)
    return yc * lax.rsqrt(var + LN_EPS) * g + b


def _outproj_kernel(oa_ref, ob_ref, oc_ref, od_ref, x_ref, g1_ref, sc2_ref, sh2_ref, w_ref,
                    lg_ref, lb_ref, wr_ref, x1_ref, h2_ref, aff_ref):
    m = jnp.dot(oa_ref[...], w_ref[0:256, :], preferred_element_type=F32)
    m += jnp.dot(ob_ref[...], w_ref[256:512, :], preferred_element_type=F32)
    m += jnp.dot(oc_ref[...], w_ref[512:768, :], preferred_element_type=F32)
    m += jnp.dot(od_ref[...], w_ref[768:1024, :], preferred_element_type=F32)
    y = DEEPNORM_ALPHA * x_ref[...] + (1.0 + g1_ref[...]) * m
    x1 = _layer_norm(y, lg_ref[...], lb_ref[...])
    x1_ref[...] = x1
    h2 = x1 * (1.0 + sc2_ref[...]) + sh2_ref[...]
    h2_ref[...] = h2.astype(BF16)
    h_hi, h_lo = _split_bf16(h2)
    w_hi, w_lo = _split_bf16(wr_ref[...])
    nt = (((1,), (1,)), ((), ()))
    lg = lax.dot_general(w_hi, h_hi, nt, preferred_element_type=F32)
    lg += lax.dot_general(w_hi, h_lo, nt, preferred_element_type=F32)
    lg += lax.dot_general(w_lo, h_hi, nt, preferred_element_type=F32)
    lg = lg - jnp.max(lg, 0, keepdims=True)
    e = jnp.exp(lg)
    aff_ref[...] = e / jnp.sum(e, 0, keepdims=True)


def _outproj(oa, ob, oc, od, x, g1, sc2, sh2, w_out, ln_g, ln_b, w_router_t, nb, t):
    tm = 512
    nt = t // tm
    ntok = nb * t
    tok = lambda w: pl.BlockSpec((tm, w), lambda i: (i, 0))
    per_b = pl.BlockSpec((None, 1, D_MODEL), lambda i: (i // nt, 0, 0))
    full = lambda shape: pl.BlockSpec(shape, lambda i: tuple(0 for _ in shape))
    return pl.pallas_call(
        _outproj_kernel,
        grid=(ntok // tm,),
        in_specs=[tok(256), tok(256), tok(256), tok(256), tok(D_MODEL), per_b, per_b, per_b,
                  full((D_MODEL, D_MODEL)), full((1, D_MODEL)), full((1, D_MODEL)),
                  full((N_EXPERTS, D_MODEL))],
        out_specs=[tok(D_MODEL), tok(D_MODEL),
                   pl.BlockSpec((N_EXPERTS, tm), lambda i: (0, i))],
        out_shape=[jax.ShapeDtypeStruct((ntok, D_MODEL), F32),
                   jax.ShapeDtypeStruct((ntok, D_MODEL), BF16),
                   jax.ShapeDtypeStruct((N_EXPERTS, ntok), F32)],
        compiler_params=_cparams(("arbitrary",)),
        name="out_proj",
    )(oa, ob, oc, od, x, g1, sc2, sh2, w_out, ln_g, ln_b, w_router_t)


EXPERT_TF = 256


def _expert_kernel(x_ref, wg_ref, wu_ref, wd_ref, o_ref, acc_ref):
    x = x_ref[...]
    for c in range(D_FF // EXPERT_TF):
        cols = slice(c * EXPERT_TF, (c + 1) * EXPERT_TF)
        g = jnp.dot(x, wg_ref[:, cols], preferred_element_type=F32)
        u = jnp.dot(x, wu_ref[:, cols], preferred_element_type=F32)
        hmid = (g * (1.0 / (1.0 + jnp.exp(-g))) * u).astype(BF16)
        part = jnp.dot(hmid, wd_ref[cols, :], preferred_element_type=F32)
        if c == 0:
            acc_ref[...] = part
        else:
            acc_ref[...] += part
    o_ref[...] = acc_ref[...].astype(o_ref.dtype)


def _experts(xe, slots, wg, wu, wd):
    ne, _, d = xe.shape
    tm = math.gcd(slots, 1024)
    return pl.pallas_call(
        _expert_kernel,
        grid=(ne, slots // tm),
        in_specs=[pl.BlockSpec((None, tm, d), lambda e, m: (e, m, 0)),
                  pl.BlockSpec((None, d, D_FF), lambda e, m: (e, 0, 0)),
                  pl.BlockSpec((None, d, D_FF), lambda e, m: (e, 0, 0)),
                  pl.BlockSpec((None, D_FF, d), lambda e, m: (e, 0, 0))],
        out_specs=pl.BlockSpec((None, tm, d), lambda e, m: (e, m, 0)),
        out_shape=jax.ShapeDtypeStruct((ne, slots, d), BF16),
        scratch_shapes=[pltpu.VMEM((tm, d), F32)],
        compiler_params=_cparams(("arbitrary", "arbitrary")),
        name="expert_ffn",
    )(xe, wg, wu, wd)


MOE_TM = 512
MOE_WIN = 128
ROW_ALIGN = 16


def _select_kernel(aff_ref, o_ref, *, cap):
    a = aff_ref[...]
    keys = lax.bitcast_convert_type(a, jnp.int32)
    ne, n = a.shape
    capf = float(cap)

    def count(mask):
        return jnp.sum(jnp.where(mask, 1.0, 0.0), axis=1, keepdims=True)

    def key_bit(b, thr):
        cand = thr | lax.shift_left(jnp.int32(1), 30 - b)
        return jnp.where(count(keys >= cand) >= capf, cand, thr)

    thr = lax.fori_loop(0, 31, key_bit, jnp.zeros((ne, 1), jnp.int32))
    above = keys > thr
    need = capf - count(above)
    idx = lax.broadcasted_iota(jnp.int32, (ne, n), 1)
    tie_idx = jnp.where(keys == thr, idx, jnp.int32(2 ** 30))
    nbits = max(1, (n - 1).bit_length())

    def idx_bit(b, j):
        cand = j | lax.shift_left(jnp.int32(1), nbits - 1 - b)
        return jnp.where(count(tie_idx < cand) < need, cand, j)

    j = lax.fori_loop(0, nbits, idx_bit, jnp.zeros((ne, 1), jnp.int32))
    sel = above | (tie_idx <= j)
    o_ref[...] = jnp.where(sel, a, -1.0)


def _select(aff, cap):
    ne, n = aff.shape
    return pl.pallas_call(
        functools.partial(_select_kernel, cap=cap),
        out_shape=jax.ShapeDtypeStruct((ne, n), F32),
        compiler_params=pltpu.CompilerParams(vmem_limit_bytes=VMEM_LIMIT),
        name="ec_select",
    )(aff)


def _routing_tables(gs, group_tokens):
    ne = gs.shape[0]
    a_l, off_l, cnt_l, lim_l = [], [], [], []
    tok0, slot0 = 0, 0
    for n in group_tokens:
        cap = EC_FACTOR * n // N_EXPERTS
        nt = n // MOE_TM
        sel = lax.slice_in_dim(gs, tok0, tok0 + n, axis=1) >= 0
        counts = jnp.sum(sel.reshape(ne, nt, MOE_TM), axis=-1, dtype=jnp.int32)
        s0 = slot0 + jnp.cumsum(counts, axis=1) - counts
        a = (s0 // ROW_ALIGN) * ROW_ALIGN
        a_l.append(a)
        off_l.append(s0 - a)
        cnt_l.append(counts)
        lim_l.append(jnp.full((nt,), slot0 + cap - MOE_WIN, jnp.int32))
        tok0 += n
        slot0 += cap
    a = jnp.concatenate(a_l, axis=1).T
    off = jnp.concatenate(off_l, axis=1).T
    end = off + jnp.concatenate(cnt_l, axis=1).T
    rounds = (end + MOE_WIN - 1) // MOE_WIN
    gp = (end // ROW_ALIGN) * ROW_ALIGN
    return dict(a=a.reshape(-1).astype(jnp.int32), nr=rounds.reshape(-1).astype(jnp.int32),
                nrounds=jnp.max(rounds, axis=1).astype(jnp.int32), lim=jnp.concatenate(lim_l),
                off_col=off.astype(F32)[:, :, None], off_row=off.astype(F32)[:, None, :],
                off16=jnp.repeat(off.astype(F32), ROW_ALIGN, axis=1)[:, :, None],
                gp16=jnp.repeat(gp.astype(F32), ROW_ALIGN, axis=1)[:, :, None],
                total=slot0)


def _dispatch_kernel(a_tab, nrounds, nr_tab, gs_ref, off_ref, off16_ref, gp16_ref, x_ref, u_ref, xe_ref,
                     pos_ref, c_ref, c2_ref, stage_ref, carry_ref, sem, rc_ref):
    j = pl.program_id(0)
    ne, tm = gs_ref.shape
    d = x_ref.shape[1]
    win = MOE_WIN

    @pl.when(j == 0)
    def _():
        carry_ref[...] = jnp.zeros_like(carry_ref)
        rc_ref[0] = 0

    sel = gs_ref[...] >= 0.0
    rank = jnp.dot(jnp.where(sel, 1.0, 0.0).astype(BF16), u_ref[...], preferred_element_type=F32)
    pos_ref[...] = jnp.where(sel, rank + off_ref[...], -1.0)

    def wait_round(slot):
        def one(_, c):
            pltpu.make_async_copy(stage_ref.at[slot, pl.ds(0, win)], xe_ref.at[0, pl.ds(0, win)],
                                  sem.at[slot]).wait()
            return c
        lax.fori_loop(0, rc_ref[1 + slot], one, 0)

    def round_body(r, _):
        k = lax.broadcasted_iota(jnp.int32, (win, tm), 0).astype(F32) + (r * win).astype(F32)
        for e in range(ne):
            c_ref[e * win:(e + 1) * win, :] = jnp.where(pos_ref[e:e + 1, :] == k, 1.0, 0.0).astype(BF16)
        slot = rc_ref[0] % 2
        for nb in range(d // MXU_DIM):
            cols = slice(nb * MXU_DIM, (nb + 1) * MXU_DIM)
            stage_ref[slot, :, cols] = jnp.dot(c_ref[...], x_ref[:, cols],
                                               preferred_element_type=F32).astype(BF16)

        @pl.when(r == 0)
        def _():
            k16 = lax.broadcasted_iota(jnp.int32, (ROW_ALIGN, 1), 0).astype(F32)
            for e in range(ne):
                keep = k16 < off16_ref[e * ROW_ALIGN:(e + 1) * ROW_ALIGN, :]
                rows = pl.ds(e * win, ROW_ALIGN)
                stage_ref[slot, rows, :] = jnp.where(
                    keep, carry_ref[e * ROW_ALIGN:(e + 1) * ROW_ALIGN, :], stage_ref[slot, rows, :])

        @pl.when(rc_ref[0] > 0)
        def _():
            wait_round(1 - slot)

        rc_ref[1 + slot] = 0
        for e in range(ne):
            @pl.when(r < nr_tab[j * ne + e])
            def _():
                dst = pl.multiple_of(a_tab[j * ne + e] + r * win, ROW_ALIGN)
                pltpu.make_async_copy(stage_ref.at[slot, pl.ds(e * win, win)],
                                      xe_ref.at[e, pl.ds(dst, win)], sem.at[slot]).start()
                rc_ref[1 + slot] = rc_ref[1 + slot] + 1
        rc_ref[0] = rc_ref[0] + 1
        return 0

    lax.fori_loop(0, nrounds[j], round_body, 0)

    k16 = lax.broadcasted_iota(jnp.int32, (ROW_ALIGN, 1), 0).astype(F32)
    for e in range(ne):
        rows = slice(e * ROW_ALIGN, (e + 1) * ROW_ALIGN)
        c2_ref[rows, :] = jnp.where(pos_ref[e:e + 1, :] == gp16_ref[rows, :] + k16, 1.0, 0.0).astype(BF16)
    kk = jnp.concatenate([k16] * ne, axis=0)
    keep_old = (gp16_ref[...] == 0.0) & (kk < off16_ref[...])
    for nb in range(d // MXU_DIM):
        cols = slice(nb * MXU_DIM, (nb + 1) * MXU_DIM)
        new = jnp.dot(c2_ref[...], x_ref[:, cols], preferred_element_type=F32).astype(BF16)
        carry_ref[:, cols] = jnp.where(keep_old, carry_ref[:, cols], new)

    @pl.when((j == pl.num_programs(0) - 1) & (rc_ref[0] > 0))
    def _():
        wait_round((rc_ref[0] - 1) % 2)


def _dispatch(gs, h2, tabs, u_mat):
    ne, ntok = gs.shape
    d = h2.shape[1]
    nt = ntok // MOE_TM
    rows = tabs['total'] + MOE_WIN
    grid_spec = pltpu.PrefetchScalarGridSpec(
        num_scalar_prefetch=3,
        grid=(nt,),
        in_specs=[pl.BlockSpec((ne, MOE_TM), lambda j, *_: (0, j)),
                  pl.BlockSpec((None, ne, 1), lambda j, *_: (j, 0, 0)),
                  pl.BlockSpec((None, ne * ROW_ALIGN, 1), lambda j, *_: (j, 0, 0)),
                  pl.BlockSpec((None, ne * ROW_ALIGN, 1), lambda j, *_: (j, 0, 0)),
                  pl.BlockSpec((MOE_TM, d), lambda j, *_: (j, 0)),
                  pl.BlockSpec((MOE_TM, MOE_TM), lambda j, *_: (0, 0))],
        out_specs=pl.BlockSpec(memory_space=pl.ANY),
        scratch_shapes=[pltpu.VMEM((ne, MOE_TM), F32),
                        pltpu.VMEM((ne * MOE_WIN, MOE_TM), BF16),
                        pltpu.VMEM((ne * ROW_ALIGN, MOE_TM), BF16),
                        pltpu.VMEM((2, ne * MOE_WIN, d), BF16),
                        pltpu.VMEM((ne * ROW_ALIGN, d), BF16),
                        pltpu.SemaphoreType.DMA((2,)),
                        pltpu.SMEM((3,), jnp.int32)])
    return pl.pallas_call(
        _dispatch_kernel,
        grid_spec=grid_spec,
        out_shape=jax.ShapeDtypeStruct((ne, rows, d), BF16),
        compiler_params=_cparams(("arbitrary",)),
        name="ec_dispatch",
    )(tabs['a'], tabs['nrounds'], tabs['nr'], gs, tabs['off_col'], tabs['off16'], tabs['gp16'], h2,
      u_mat)


def _combine_kernel(a_tab, nrounds, lim_tab, gs_ref, off_ref, l_ref, ye_ref, x1_ref, g2_ref, lg_ref,
                    lb_ref, *refs, split_tiles):
    outs, (p_ref, y_ref, acc_ref, sem) = refs[:-4], refs[-4:]
    j = pl.program_id(0)
    tm, ne = gs_ref.shape
    d = x1_ref.shape[1]
    win = MOE_WIN
    gs = gs_ref[...]
    sel = gs >= 0.0
    rank = jnp.dot(l_ref[...], jnp.where(sel, 1.0, 0.0).astype(BF16), preferred_element_type=F32)
    pos = jnp.where(sel, rank + off_ref[...], -1.0)
    gate = jnp.where(sel, gs, 0.0)
    acc_ref[...] = jnp.zeros_like(acc_ref)
    n_tiles = pl.num_programs(0)
    slot = j % 2

    def window(tile, r, e):
        want = a_tab[tile * ne + e] + r * win
        src = pl.multiple_of(jnp.minimum(want, lim_tab[tile]), ROW_ALIGN)
        return src, want - src

    def start_round(tile, r, sl):
        for e in range(ne):
            src, _ = window(tile, r, e)
            pltpu.make_async_copy(ye_ref.at[e, pl.ds(src, win)], y_ref.at[sl, pl.ds(e * win, win)],
                                  sem.at[sl]).start()

    def wait_round(sl):
        for e in range(ne):
            pltpu.make_async_copy(ye_ref.at[e, pl.ds(0, win)], y_ref.at[sl, pl.ds(e * win, win)],
                                  sem.at[sl]).wait()

    @pl.when((j == 0) & (nrounds[0] > 0))
    def _():
        start_round(0, 0, 0)

    nxt = jnp.minimum(j + 1, n_tiles - 1)

    @pl.when((j + 1 < n_tiles) & (nrounds[nxt] > 0))
    def _():
        start_round(nxt, 0, 1 - slot)

    half = ne // 2

    def accumulate(r):
        base = lax.convert_element_type(r * win, F32)
        k = lax.broadcasted_iota(jnp.int32, (tm, win), 1).astype(F32) + base
        for grp in range(2):
            for e in range(grp * half, (grp + 1) * half):
                pe = pos[:, e:e + 1]
                pe = jnp.where(pe >= base, pe + window(j, r, e)[1].astype(F32), -1.0)
                pcol = jnp.broadcast_to(pe, (tm, win))
                gcol = jnp.broadcast_to(gate[:, e:e + 1], (tm, win))
                p_ref[:, e * win:(e + 1) * win] = jnp.where(pcol == k, gcol, 0.0).astype(BF16)
            rows = slice(grp * half * win, (grp + 1) * half * win)
            for nb in range(d // MXU_DIM):
                cols = slice(nb * MXU_DIM, (nb + 1) * MXU_DIM)
                acc_ref[:, cols] += jnp.dot(p_ref[:, rows], y_ref[slot, rows, cols],
                                            preferred_element_type=F32)

    @pl.when(nrounds[j] > 0)
    def _():
        wait_round(slot)
        accumulate(0)

    def extra_round(r, _):
        start_round(j, r, slot)
        wait_round(slot)
        accumulate(r)
        return 0

    lax.fori_loop(1, nrounds[j], extra_round, 0)
    y = DEEPNORM_ALPHA * x1_ref[...] + (1.0 + g2_ref[...]) * acc_ref[...]
    res = _layer_norm(y, lg_ref[...], lb_ref[...])
    if split_tiles is None:
        outs[0][...] = res
    else:
        @pl.when(j < split_tiles)
        def _():
            outs[0][...] = res

        @pl.when(j >= split_tiles)
        def _():
            outs[1][...] = res


def _combine_postnorm(gs_tok, ye, tabs, l_mat, x1, g2, ln_g, ln_b, t, split_tiles=None):
    ntok, ne = gs_tok.shape
    d = x1.shape[1]
    nt = ntok // MOE_TM
    tiles_per_seq = t // MOE_TM
    if split_tiles is None:
        out_specs = pl.BlockSpec((MOE_TM, d), lambda j, *_: (j, 0))
        out_shape = jax.ShapeDtypeStruct((ntok, d), F32)
    else:
        out_specs = [pl.BlockSpec((MOE_TM, d), lambda j, *_: (jnp.minimum(j, split_tiles - 1), 0)),
                     pl.BlockSpec((MOE_TM, d), lambda j, *_: (jnp.maximum(j - split_tiles, 0), 0))]
        out_shape = [jax.ShapeDtypeStruct((split_tiles * MOE_TM, d), F32),
                     jax.ShapeDtypeStruct((ntok - split_tiles * MOE_TM, d), F32)]
    grid_spec = pltpu.PrefetchScalarGridSpec(
        num_scalar_prefetch=3,
        grid=(nt,),
        in_specs=[pl.BlockSpec((MOE_TM, ne), lambda j, *_: (j, 0)),
                  pl.BlockSpec((None, 1, ne), lambda j, *_: (j, 0, 0)),
                  pl.BlockSpec((MOE_TM, MOE_TM), lambda j, *_: (0, 0)),
                  pl.BlockSpec(memory_space=pl.ANY),
                  pl.BlockSpec((MOE_TM, d), lambda j, *_: (j, 0)),
                  pl.BlockSpec((None, 1, d), lambda j, *_: (j // tiles_per_seq, 0, 0)),
                  pl.BlockSpec((1, d), lambda j, *_: (0, 0)),
                  pl.BlockSpec((1, d), lambda j, *_: (0, 0))],
        out_specs=out_specs,
        scratch_shapes=[pltpu.VMEM((MOE_TM, ne * MOE_WIN), BF16),
                        pltpu.VMEM((2, ne * MOE_WIN, d), BF16),
                        pltpu.VMEM((MOE_TM, d), F32),
                        pltpu.SemaphoreType.DMA((2,))])
    return pl.pallas_call(
        functools.partial(_combine_kernel, split_tiles=split_tiles),
        grid_spec=grid_spec,
        out_shape=out_shape,
        compiler_params=_cparams(("arbitrary",)),
        name="ec_combine",
    )(tabs['a'], tabs['nrounds'], tabs['lim'], gs_tok, tabs['off_row'], l_mat, ye, x1, g2, ln_g, ln_b)


def _prep_w_in(w_in_l):
    sizes = (256, 256, 256, MLA_Q_RANK, MLA_KV_RANK, MLA_ROPE, 256, 256, 256, 256, 256, 256)
    offs = np.concatenate([[0], np.cumsum(sizes)])
    part = [w_in_l[:, offs[i]:offs[i + 1]] for i in range(len(sizes))]
    a_q, a_k, a_v, b_cq, b_ckv, b_kr, c_q, c_k, c_v, d_q, d_k, d_v = part
    d = w_in_l.shape[0]
    zeros = lambda n: jnp.zeros((d, n), w_in_l.dtype)
    def per_head(v):
        out = []
        for h in range(4):
            out += [v[:, h * 64:(h + 1) * 64], zeros(LANES - 64)]
        return out

    cols = ([a_q, a_k] + per_head(a_v) + [c_q, c_k] + per_head(c_v)
            + [d_q, d_k, d_v, b_cq, zeros(256 - MLA_Q_RANK), b_ckv, b_kr, b_kr,
               zeros(128 - 2 * MLA_ROPE)])
    return jnp.concatenate(cols, axis=1).astype(BF16)


def _prep_w_uq(w_uq_l):
    hd = MLA_NOPE + MLA_ROPE
    nope = [w_uq_l[:, h * hd:h * hd + MLA_NOPE] for h in range(MLA_HEADS)]
    rope = [w_uq_l[:, h * hd + MLA_NOPE:(h + 1) * hd] for h in range(MLA_HEADS)]
    z = jnp.zeros((w_uq_l.shape[0], 256 - 2 * hd), w_uq_l.dtype)
    cols = []
    for p in range(2):
        cols += [nope[2 * p], nope[2 * p + 1], rope[2 * p], rope[2 * p + 1], z]
    w = jnp.concatenate(cols, axis=1)
    w = jnp.concatenate([w, jnp.zeros((256 - MLA_Q_RANK, w.shape[1]), w.dtype)], axis=0)
    return w.astype(BF16)


def _prep_w_ukv(w_ukv_l):
    hd = MLA_NOPE + MLA_DV
    kn = [w_ukv_l[:, h * hd:h * hd + MLA_NOPE] for h in range(MLA_HEADS)]
    z = jnp.zeros((w_ukv_l.shape[0], LANES - MLA_DV), w_ukv_l.dtype)
    vv = []
    for h in range(MLA_HEADS):
        vv += [w_ukv_l[:, h * hd + MLA_NOPE:(h + 1) * hd], z]
    return jnp.concatenate(kn + vv, axis=1).astype(BF16)


def _select_groups(aff_t, group_tokens):
    parts, off = [], 0
    for n in group_tokens:
        parts.append(_select(lax.slice_in_dim(aff_t, off, off + n, axis=1),
                             EC_FACTOR * n // N_EXPERTS))
        off += n
    return jnp.concatenate(parts, axis=1)


def _trunk(x, c, group_tokens, nb, t, p):
    ntok = nb * t
    tabs = (_rope_tables(t, 256, DA_DQK, DA_DQK // ROPE_FRACTION, ROPE_THETA)
            + _rope_tables(t, 256, DIL_DH, DIL_DH // ROPE_FRACTION, ROPE_THETA)
            + _rope_tables(t, 128, MLA_ROPE, MLA_ROPE, MLA_ROPE_THETA))
    gmat = jnp.asarray(np.kron(np.eye(4), np.full((64, 64), 1.0 / 64)), BF16)
    ti = jnp.arange(MOE_TM)
    u_mat = (ti[:, None] < ti[None, :]).astype(BF16)
    l_mat = (ti[None, :] < ti[:, None]).astype(BF16)
    for l in range(DEPTH):
        mod = _modulation(c, p['w_ada'][l], p['b_ada'][l])
        sh1, sc1, g1, sh2, sc2, g2 = [m.reshape(nb, 1, D_MODEL) for m in jnp.split(mod, 6, axis=-1)]
        gq = jnp.concatenate([p['q_norm_g'][l], jnp.zeros((256 - MLA_Q_RANK,), F32)]).reshape(1, 256)
        gkv = p['kv_norm_g'][l].reshape(1, 128)
        proj, projv = _inproj(x, sc1, sh1, _prep_w_in(p['w_in'][l]), _prep_w_uq(p['w_uq'][l]),
                              _prep_w_ukv(p['w_ukv'][l]), gq, gkv, tabs, nb, t)
        lam_init = 0.8 - 0.6 * math.exp(-0.3 * l)
        lam = (jnp.exp(jnp.sum(p['da_lq1'][l] * p['da_lk1'][l]))
               - jnp.exp(jnp.sum(p['da_lq2'][l] * p['da_lk2'][l])) + lam_init).reshape(1)
        g_sub = jnp.tile(p['da_subln_g'][l], DA_HEADS).reshape(1, 256)
        oa = _diff_attention(proj, projv, lam, g_sub, gmat, nb, t, 1.0 - lam_init)
        ob = _mla_attention(proj, projv, nb, t)
        oc = _dil_attention(proj, projv, nb, t)
        od = _na_attention(proj, _na_bias_table(p['na_rpb'][l]), nb, t)
        x1, h2, aff_t = _outproj(oa, ob, oc, od, x, g1, sc2, sh2, p['w_out'][l].astype(BF16),
                                 p['ln1_g'][l].reshape(1, -1), p['ln1_b'][l].reshape(1, -1),
                                 p['w_router'][l].T, nb, t)
        wg = p['w_e_gate'][l].astype(BF16)
        wu = p['w_e_up'][l].astype(BF16)
        wd = p['w_e_down'][l].astype(BF16)
        gs = _select_groups(aff_t, group_tokens)
        rt = _routing_tables(gs, group_tokens)
        xe = _dispatch(gs, h2, rt, u_mat)
        ye = _experts(xe, rt['total'], wg, wu, wd)
        split = group_tokens[0] // MOE_TM if l == DEPTH - 1 else None
        x = _combine_postnorm(gs.T, ye, rt, l_mat, x1, g2, p['ln2_g'][l].reshape(1, -1),
                              p['ln2_b'][l].reshape(1, -1), t, split_tiles=split)
    return x


def kernel(x_prompt, x_sample, c_prompt, c_sample, w_in, w_uq, w_ukv, q_norm_g, kv_norm_g, da_lq1,
           da_lk1, da_lq2, da_lk2, da_subln_g, na_rpb, w_out, w_ada, b_ada, ln1_g, ln1_b, ln2_g,
           ln2_b, w_router, w_e_gate, w_e_up, w_e_down):
    p = dict(w_in=w_in, w_uq=w_uq, w_ukv=w_ukv, q_norm_g=q_norm_g, kv_norm_g=kv_norm_g,
             da_lq1=da_lq1, da_lk1=da_lk1, da_lq2=da_lq2, da_lk2=da_lk2, da_subln_g=da_subln_g,
             na_rpb=na_rpb, w_out=w_out, w_ada=w_ada, b_ada=b_ada, ln1_g=ln1_g, ln1_b=ln1_b,
             ln2_g=ln2_g, ln2_b=ln2_b, w_router=w_router, w_e_gate=w_e_gate, w_e_up=w_e_up,
             w_e_down=w_e_down)
    bp, t, d = x_prompt.shape
    bs = x_sample.shape[0]
    assert x_sample.shape[1] == t
    nb = bp + bs
    x = jnp.concatenate([x_prompt.reshape(bp * t, d), x_sample.reshape(bs * t, d)], axis=0)
    c = jnp.concatenate([c_prompt, c_sample], axis=0)
    y_prompt, y_sample = _trunk(x, c, (bp * t, bs * t), nb, t, p)
    return y_prompt.reshape(bp, t, d), y_sample.reshape(bs, t, d)
```

```python
import functools
import math

import jax
import jax.numpy as jnp
import numpy as np
from jax import lax
from jax.experimental import pallas as pl
from jax.experimental.pallas import tpu as pltpu

F32 = jnp.float32
BF16 = jnp.bfloat16

D_MODEL = 1024
DEPTH = 2
GRID_W = 64
GROUP_W = 256
DA_HEADS, DA_DV, DA_DQK = 4, 64, 32
MLA_HEADS, MLA_Q_RANK, MLA_KV_RANK, MLA_NOPE, MLA_ROPE, MLA_DV = 4, 192, 128, 64, 32, 64
MLA_ROPE_THETA = 10000.0
DIL_HEADS, DIL_DH = 4, 64
DIL_PATTERNS = ((128, 1), (512, 4), (2048, 16))
NA_HEADS, NA_DH, NA_KR, NA_KC = 4, 64, 8, 16
ROPE_THETA = 500000.0
ROPE_FRACTION = 4
N_EXPERTS = 16
EC_FACTOR = 2
D_FF = 2816
DEEPNORM_ALPHA = (2.0 * DEPTH) ** 0.25
NEG_INF = -1e30
LOG2E = math.log2(math.e)
LN_EPS = 1e-5
RMS_EPS = 1e-6

LANES = 128
MXU_DIM = 256
VMEM_LIMIT = 56 * 1024 * 1024

S_AQ, S_AK, S_BQ0, S_BQ1, S_BK0, S_BK1, S_CQ, S_CK, S_DQ, S_DK, S_DV = range(11)
N_SLOTS = 11
V_A, V_B, V_C = 0, 4, 8
N_VSLOTS = 12
V_ONE_LANE = 64
W_ALL_COLS = 13 * 256


def _cparams(sem):
    return pltpu.CompilerParams(dimension_semantics=sem, vmem_limit_bytes=VMEM_LIMIT)


def _split_bf16(a):
    hi = a.astype(BF16)
    lo = (a - hi.astype(F32)).astype(BF16)
    return hi, lo


def _mod_kernel(c_ref, w_ref, b_ref, o_ref):
    c = c_ref[...]
    a = c * (1.0 / (1.0 + jnp.exp(-c)))
    a_hi, a_lo = _split_bf16(a)
    w_hi, w_lo = _split_bf16(w_ref[...])
    acc = jnp.dot(a_hi, w_hi, preferred_element_type=F32)
    acc += jnp.dot(a_hi, w_lo, preferred_element_type=F32)
    acc += jnp.dot(a_lo, w_hi, preferred_element_type=F32)
    o_ref[...] = acc + b_ref[...]


def _modulation(c, w_ada, b_ada):
    nb, d = c.shape
    n_out = w_ada.shape[1]
    tn = 1536
    return pl.pallas_call(
        _mod_kernel,
        grid=(n_out // tn,),
        in_specs=[pl.BlockSpec((nb, d), lambda j: (0, 0)),
                  pl.BlockSpec((d, tn), lambda j: (0, j)),
                  pl.BlockSpec((1, tn), lambda j: (0, j))],
        out_specs=pl.BlockSpec((nb, tn), lambda j: (0, j)),
        out_shape=jax.ShapeDtypeStruct((nb, n_out), F32),
        compiler_params=_cparams(("arbitrary",)),
        name="adaln_mod",
    )(c, w_ada, b_ada.reshape(1, n_out))


def _rope_tables(t, width, group, rot, theta):
    half = rot // 2
    inv = theta ** (-jnp.arange(half, dtype=F32) / half)
    ang = jnp.arange(t, dtype=F32)[:, None] * inv[None, :]
    cos, sin = jnp.cos(ang), jnp.sin(ang)
    ones = jnp.ones((t, group - rot), F32)
    zeros = jnp.zeros((t, group - rot), F32)
    c_g = jnp.concatenate([cos, cos, ones], axis=1)
    s_g = jnp.concatenate([-sin, sin, zeros], axis=1)
    reps = width // group
    return jnp.tile(c_g, (1, reps)), jnp.tile(s_g, (1, reps))


def _apply_rope(x, c_tab, s_tab, group, rot):
    width = x.shape[-1]
    half = rot // 2
    lane = lax.broadcasted_iota(jnp.int32, (1, width), 1)
    first = (lane % group) < half
    fwd = pltpu.roll(x, width - half, 1)
    bwd = pltpu.roll(x, half, 1)
    return x * c_tab + s_tab * jnp.where(first, fwd, bwd)


def _inproj_kernel(x_ref, sc_ref, sh_ref, w_ref, wuq_ref, wukv_ref, gq_ref, gkv_ref,
                   ca_ref, sa_ref, cc_ref, scc_ref, cm_ref, sm_ref, o_ref, ov_ref):
    h = (x_ref[...] * (1.0 + sc_ref[...]) + sh_ref[...]).astype(BF16)

    def proj(col, width):
        return jnp.dot(h, w_ref[:, col:col + width], preferred_element_type=F32)

    one_lane = lax.broadcasted_iota(jnp.int32, (1, LANES), 1) == V_ONE_LANE

    def put_values(first_slot, vals):
        for hh in range(4):
            v = vals[:, LANES * hh:LANES * (hh + 1)]
            ov_ref[first_slot + hh] = jnp.where(one_lane, 1.0, v).astype(BF16)

    sa_scale = DA_DQK ** -0.5 * LOG2E
    sb_scale = (MLA_NOPE + MLA_ROPE) ** -0.5 * LOG2E
    sc_scale = DIL_DH ** -0.5 * LOG2E
    sd_scale = NA_DH ** -0.5 * LOG2E
    a_rot = DA_DQK // ROPE_FRACTION
    c_rot = DIL_DH // ROPE_FRACTION

    cm, sm = cm_ref[...], sm_ref[...]
    cq = proj(2816, 256)
    cq = cq * lax.rsqrt(jnp.sum(cq * cq, -1, keepdims=True) * (1.0 / MLA_Q_RANK) + RMS_EPS)
    cq = (cq * gq_ref[...]).astype(BF16)
    q2 = jnp.dot(cq, wuq_ref[...], preferred_element_type=F32)
    for p in range(2):
        qp = q2[:, 256 * p:256 * (p + 1)]
        o_ref[S_BQ0 + p, :, 0:128] = (qp[:, 0:128] * sb_scale).astype(BF16)
        o_ref[S_BQ0 + p, :, 128:256] = (
            _apply_rope(qp[:, 128:256], cm, sm, MLA_ROPE, MLA_ROPE) * sb_scale).astype(BF16)
    ckv = proj(3072, 128)
    ckv = ckv * lax.rsqrt(jnp.mean(ckv * ckv, -1, keepdims=True) + RMS_EPS)
    ckv = (ckv * gkv_ref[...]).astype(BF16)
    kv = jnp.dot(ckv, wukv_ref[...], preferred_element_type=F32)
    kr = _apply_rope(proj(3200, 128), cm, sm, MLA_ROPE, MLA_ROPE).astype(BF16)
    for p in range(2):
        o_ref[S_BK0 + p, :, 0:128] = kv[:, 128 * p:128 * (p + 1)].astype(BF16)
        o_ref[S_BK0 + p, :, 128:256] = kr
    put_values(V_B, kv[:, 256:768])

    ca, sa = ca_ref[...], sa_ref[...]
    o_ref[S_AQ] = (_apply_rope(proj(0, 256), ca, sa, DA_DQK, a_rot) * sa_scale).astype(BF16)
    o_ref[S_AK] = _apply_rope(proj(256, 256), ca, sa, DA_DQK, a_rot).astype(BF16)
    put_values(V_A, proj(512, 512))
    cc, scc = cc_ref[...], scc_ref[...]
    o_ref[S_CQ] = (_apply_rope(proj(1024, 256), cc, scc, DIL_DH, c_rot) * sc_scale).astype(BF16)
    o_ref[S_CK] = _apply_rope(proj(1280, 256), cc, scc, DIL_DH, c_rot).astype(BF16)
    put_values(V_C, proj(1536, 512))
    o_ref[S_DQ] = (proj(2048, 256) * sd_scale).astype(BF16)
    o_ref[S_DK] = proj(2304, 256).astype(BF16)
    o_ref[S_DV] = proj(2560, 256).astype(BF16)


def _inproj(x, sc, sh, w_all, wuq, wukv, gq, gkv, tabs, nb, t):
    tm = 512
    nt = t // tm
    ntok = nb * t
    ca, sa, cc, scc, cm, sm = tabs
    full = lambda shape: pl.BlockSpec(shape, lambda j, b: tuple(0 for _ in shape))
    tab = lambda w: pl.BlockSpec((tm, w), lambda j, b: (j, 0))
    return pl.pallas_call(
        _inproj_kernel,
        grid=(nt, nb),
        in_specs=[pl.BlockSpec((tm, D_MODEL), lambda j, b: (b * nt + j, 0)),
                  pl.BlockSpec((None, 1, D_MODEL), lambda j, b: (b, 0, 0)),
                  pl.BlockSpec((None, 1, D_MODEL), lambda j, b: (b, 0, 0)),
                  full((D_MODEL, W_ALL_COLS)), full((256, 512)), full((128, 768)),
                  full((1, 256)), full((1, 128)),
                  tab(256), tab(256), tab(256), tab(256), tab(128), tab(128)],
        out_specs=[pl.BlockSpec((N_SLOTS, tm, 256), lambda j, b: (0, b * nt + j, 0)),
                   pl.BlockSpec((N_VSLOTS, tm, LANES), lambda j, b: (0, b * nt + j, 0))],
        out_shape=[jax.ShapeDtypeStruct((N_SLOTS, ntok, 256), BF16),
                   jax.ShapeDtypeStruct((N_VSLOTS, ntok, LANES), BF16)],
        compiler_params=_cparams(("arbitrary", "arbitrary")),
        name="in_proj",
    )(x, sc, sh, w_all, wuq, wukv, gq, gkv, ca, sa, cc, scc, cm, sm)


SOFTMAX_SLAB = 32


def _chain_scratch(tq, kc):
    return [pltpu.VMEM((tq, kc), F32), pltpu.VMEM((tq, kc), F32), pltpu.VMEM((tq, kc), BF16),
            pltpu.VMEM((tq, LANES), F32), pltpu.VMEM((tq, LANES), F32), pltpu.VMEM((tq, LANES), F32)]


CHAIN_REFS = 6


def _attention_steps(n_steps, qms_for, k_for, v_for, finish, tq, t, kc, chains, slab=SOFTMAX_SLAB,
                     bias_ref=None, step_unroll=1):
    n_chunks = t // kc
    assert n_chunks % 2 == 0
    nt = (((1,), (1,)), ((), ()))

    def chunk_rows(chunk):
        if isinstance(chunk, int):
            return pl.ds(chunk * kc, kc)
        return pl.ds(pl.multiple_of(chunk * kc, kc), kc)

    def scores(step, chunk, slot):
        k_refs = k_for(step)
        if not isinstance(k_refs, (list, tuple)):
            k_refs = [k_refs] * len(chains)
        loaded = {}
        for qm, k_ref, chain in zip(qms_for(step), k_refs, chains):
            if id(k_ref) not in loaded:
                loaded[id(k_ref)] = k_ref[chunk_rows(chunk), :]
            chain[slot][...] = lax.dot_general(qm, loaded[id(k_ref)], nt, preferred_element_type=F32)

    def softmax_pv(step, chunk, slot):
        for v_ref, chain in zip(v_for(step), chains):
            v = v_ref[chunk_rows(chunk), :]
            s_ref = chain[slot]
            (p_ref, m_ref, a_ref, acc_ref) = chain[2:]
            for r in range(tq // slab):
                rows = slice(r * slab, (r + 1) * slab)
                s = s_ref[rows, :]
                if bias_ref is not None:
                    s = s + bias_ref[rows, chunk * kc:(chunk + 1) * kc]
                m_prev = m_ref[rows, :]
                m_new = jnp.maximum(m_prev, jnp.max(s, -1, keepdims=True))
                d = s - jnp.concatenate([m_new] * (kc // LANES), axis=1)
                p_ref[rows, :] = jnp.exp2(d.astype(BF16))
                a_ref[rows, :] = jnp.exp2(m_prev - m_new)
                m_ref[rows, :] = m_new
            acc_ref[...] = a_ref[...] * acc_ref[...] + jnp.dot(p_ref[...], v,
                                                              preferred_element_type=F32)

    def step_body(st, _):
        for (_, _, _, m_ref, _, acc_ref) in chains:
            m_ref[...] = jnp.full(m_ref.shape, NEG_INF, F32)
            acc_ref[...] = jnp.zeros(acc_ref.shape, F32)
        nxt = min(st + 1, n_steps - 1) if isinstance(st, int) else jnp.minimum(st + 1, n_steps - 1)

        def body(j, _):
            scores(st, 2 * j + 1, 1)
            softmax_pv(st, 2 * j, 0)
            if 2 * j + 2 < n_chunks:
                scores(st, 2 * j + 2, 0)
            elif n_steps > 1:
                scores(nxt, 0, 0)
            softmax_pv(st, 2 * j + 1, 1)
            return 0

        for j in range(n_chunks // 2):
            body(j, 0)
        outs = []
        for chain in chains:
            acc = chain[5][...]
            den = jnp.broadcast_to(acc[:, V_ONE_LANE:V_ONE_LANE + 1], acc.shape)
            outs.append(acc * (1.0 / den))
        finish(st, outs)
        return 0

    scores(0, 0, 0)
    if step_unroll == n_steps:
        for st in range(n_steps):
            step_body(st, 0)
    else:
        lax.fori_loop(0, n_steps, step_body, 0, unroll=step_unroll)


def _place_head(pair_ref, o, j):
    lane = lax.broadcasted_iota(jnp.int32, (1, LANES), 1)
    low = lane < V_ONE_LANE
    shifted = pltpu.roll(o, V_ONE_LANE, 1)
    cur = pair_ref[...]
    pair_ref[...] = jnp.where(low, jnp.where(j == 0, o, cur), jnp.where(j == 1, shifted, cur))


def _group_mean_sq(x, gmat):
    sq = x * x
    hi, lo = _split_bf16(sq)
    return (jnp.dot(hi, gmat, preferred_element_type=F32)
            + jnp.dot(lo, gmat, preferred_element_type=F32))


def _diff_attn_kernel(lam_ref, q_ref, k_ref, v_ref, g_ref, gmat_ref, o_ref, acc_ref, *scratch, t, kc,
                      out_scale):
    chains = (scratch[:CHAIN_REFS], scratch[CHAIN_REFS:])
    lam = lam_ref[0]
    lane = lax.broadcasted_iota(jnp.int32, (1, 256), 1)

    def qms_for(h):
        q = q_ref[...]
        return [jnp.where((lane >= (2 * h + c) * DA_DQK) & (lane < (2 * h + c + 1) * DA_DQK), q,
                          jnp.zeros_like(q)) for c in range(2)]

    def finish(h, outs):
        _place_head(acc_ref.at[h // 2], outs[0] - lam * outs[1], h % 2)

    acc_ref[...] = jnp.zeros_like(acc_ref)
    _attention_steps(DA_HEADS, qms_for, lambda h: k_ref, lambda h: [v_ref.at[h], v_ref.at[h]],
                     finish, q_ref.shape[0], t, kc, chains)
    o = jnp.concatenate([acc_ref[0], acc_ref[1]], axis=1)
    ms = _group_mean_sq(o, gmat_ref[...])
    o_ref[...] = (o * lax.rsqrt(ms + RMS_EPS) * g_ref[...] * out_scale).astype(o_ref.dtype)


def _diff_attention(proj, projv, lam, g_tiled, gmat, nb, t, out_scale):
    tq, kc = 1024, 512
    nq = t // tq
    kern = functools.partial(_diff_attn_kernel, t=t, kc=kc, out_scale=out_scale)
    return pl.pallas_call(
        kern,
        grid=(nb, nq),
        in_specs=[pl.BlockSpec(memory_space=pltpu.SMEM),
                  pl.BlockSpec((None, tq, 256), lambda b, i: (S_AQ, b * nq + i, 0)),
                  pl.BlockSpec((None, t, 256), lambda b, i: (S_AK, b, 0)),
                  pl.BlockSpec((DA_HEADS, t, LANES), lambda b, i: (V_A // DA_HEADS, b, 0)),
                  pl.BlockSpec((1, 256), lambda b, i: (0, 0)),
                  pl.BlockSpec((256, 256), lambda b, i: (0, 0))],
        out_specs=pl.BlockSpec((tq, 256), lambda b, i: (b * nq + i, 0)),
        out_shape=jax.ShapeDtypeStruct((nb * t, 256), BF16),
        scratch_shapes=[pltpu.VMEM((2, tq, LANES), F32)] + 2 * _chain_scratch(tq, kc),
        compiler_params=_cparams(("arbitrary", "arbitrary")),
        name="diff_attn",
    )(lam, proj, proj, projv, g_tiled, gmat)


def _mla_attn_kernel(q_ref, k_ref, v_ref, o_ref, acc_ref, *scratch, t, kc):
    chains = tuple(scratch[c * CHAIN_REFS:(c + 1) * CHAIN_REFS] for c in range(MLA_HEADS))
    lane = lax.broadcasted_iota(jnp.int32, (1, 256), 1)
    k_pairs = [k_ref.at[0], k_ref.at[1]]

    def qms_for(_):
        qms = []
        for h in range(MLA_HEADS):
            q, j = q_ref[h // 2], h % 2
            nope = (lane >= j * MLA_NOPE) & (lane < (j + 1) * MLA_NOPE)
            rope = (lane >= 128 + j * MLA_ROPE) & (lane < 128 + (j + 1) * MLA_ROPE)
            qms.append(jnp.where(nope | rope, q, jnp.zeros_like(q)))
        return qms

    def finish(_, outs):
        for h in range(MLA_HEADS):
            _place_head(acc_ref.at[h // 2], outs[h], h % 2)

    acc_ref[...] = jnp.zeros_like(acc_ref)
    _attention_steps(1, qms_for, lambda _: [k_pairs[h // 2] for h in range(MLA_HEADS)],
                     lambda _: [v_ref.at[h] for h in range(MLA_HEADS)], finish, q_ref.shape[1], t, kc,
                     chains, step_unroll=1)
    o_ref[...] = jnp.concatenate([acc_ref[0], acc_ref[1]], axis=1).astype(o_ref.dtype)


def _mla_attention(proj, projv, nb, t):
    tq, kc = 512, 512
    nq = t // tq
    kern = functools.partial(_mla_attn_kernel, t=t, kc=kc)
    return pl.pallas_call(
        kern,
        grid=(nb, nq),
        in_specs=[pl.BlockSpec((2, tq, 256), lambda b, i: (S_BQ0 // 2, b * nq + i, 0)),
                  pl.BlockSpec((2, t, 256), lambda b, i: (S_BK0 // 2, b, 0)),
                  pl.BlockSpec((MLA_HEADS, t, LANES), lambda b, i: (V_B // MLA_HEADS, b, 0))],
        out_specs=pl.BlockSpec((tq, 256), lambda b, i: (b * nq + i, 0)),
        out_shape=jax.ShapeDtypeStruct((nb * t, 256), BF16),
        scratch_shapes=[pltpu.VMEM((2, tq, LANES), F32)] + MLA_HEADS * _chain_scratch(tq, kc),
        compiler_params=_cparams(("arbitrary", "arbitrary")),
        name="mla_attn",
    )(proj, proj, projv)


DIL_REACH = max(w // 2 for w, _ in DIL_PATTERNS)


def _dil_attn_kernel(q_ref, k_ref, v_ref, o_ref, tables_ref, acc_ref, *scratch, t, tq, band):
    chains = tuple(scratch[c * CHAIN_REFS:(c + 1) * CHAIN_REFS] for c in range(DIL_HEADS))
    i = pl.program_id(1)

    def band_start(blk):
        return jnp.clip(blk * tq - DIL_REACH, 0, t - band)

    start = pl.multiple_of(band_start(i), tq)
    table = (i * tq - start) // tq

    @pl.when((pl.program_id(0) == 0) & (i == 0))
    def _():
        def build(n, _):
            qi = lax.broadcasted_iota(jnp.int32, (tq, band), 0)
            kj = lax.broadcasted_iota(jnp.int32, (tq, band), 1)
            delta = kj - qi - n * tq
            ad = jnp.abs(delta)
            cnt = jnp.zeros((tq, band), F32)
            for window, dil in DIL_PATTERNS:
                ok = (ad <= window // 2) & ((delta & (dil - 1)) == 0)
                cnt = cnt + jnp.where(ok, 1.0, 0.0)
            tables_ref[n] = jnp.where(cnt > 2.5, math.log2(3.0),
                                      jnp.where(cnt > 1.5, 1.0,
                                                jnp.where(cnt > 0.5, 0.0, NEG_INF)))
            return 0

        lax.fori_loop(0, tables_ref.shape[0], build, 0)

    bias_ref = tables_ref.at[table]

    lane = lax.broadcasted_iota(jnp.int32, (1, 256), 1)

    def qms_for(_):
        q = q_ref[...]
        return [jnp.where((lane >= h * DIL_DH) & (lane < (h + 1) * DIL_DH), q, jnp.zeros_like(q))
                for h in range(DIL_HEADS)]

    def finish(_, outs):
        for h in range(DIL_HEADS):
            _place_head(acc_ref.at[h // 2], outs[h], h % 2)

    acc_ref[...] = jnp.zeros_like(acc_ref)
    _attention_steps(1, qms_for, lambda _: k_ref.at[pl.ds(start, band)],
                     lambda _: [v_ref.at[h, pl.ds(start, band)] for h in range(DIL_HEADS)],
                     finish, tq, band, band // 2, chains, slab=DIL_SLAB, bias_ref=bias_ref,
                     step_unroll=1)
    o_ref[...] = jnp.concatenate([acc_ref[0], acc_ref[1]], axis=1).astype(o_ref.dtype)


DIL_SLAB = 16


def _dil_attention(proj, projv, nb, t):
    tq = 256
    band = min(t, tq + 2 * DIL_REACH)
    nq = t // tq
    kern = functools.partial(_dil_attn_kernel, t=t, tq=tq, band=band)
    return pl.pallas_call(
        kern,
        grid=(nb, nq),
        in_specs=[pl.BlockSpec((None, tq, 256), lambda b, i: (S_CQ, b * nq + i, 0)),
                  pl.BlockSpec((None, t, 256), lambda b, i: (S_CK, b, 0)),
                  pl.BlockSpec((DIL_HEADS, t, LANES), lambda b, i: (V_C // DIL_HEADS, b, 0))],
        out_specs=pl.BlockSpec((tq, 256), lambda b, i: (b * nq + i, 0)),
        out_shape=jax.ShapeDtypeStruct((nb * t, 256), BF16),
        scratch_shapes=([pltpu.VMEM((band // tq, tq, band), F32), pltpu.VMEM((2, tq, LANES), F32)]
                        + DIL_HEADS * _chain_scratch(tq, band // 2)),
        compiler_params=_cparams(("arbitrary", "arbitrary")),
        name="dil_attn",
    )(proj, proj, projv)


def _na_bias_table(rpb):
    c = np.arange(GRID_W)
    cs = np.clip(c - NA_KC // 2, 0, GRID_W - NA_KC)
    colmask = (c[None, :] >= cs[:, None]) & (c[None, :] < cs[:, None] + NA_KC)
    rows = jnp.stack([rpb[:, si:si + NA_KR, :] for si in range(NA_KR)], axis=1).astype(F32)
    edge = GRID_W - NA_KC
    padded = jnp.concatenate([jnp.repeat(rows[..., :1], edge, axis=-1), rows,
                              jnp.repeat(rows[..., -1:], edge, axis=-1)], axis=-1)
    b = jnp.stack([padded[..., GRID_W - 1 - qc:2 * GRID_W - 1 - qc] for qc in range(GRID_W)],
                  axis=2)
    b = jnp.where(colmask[None, None, :, None, :], b * LOG2E, NEG_INF)
    b = b.transpose(1, 0, 2, 3, 4)
    return b.reshape(NA_KR, rpb.shape[0] * GRID_W, NA_KR * GRID_W)


def _na_attn_kernel(q_ref, k_ref, v_ref, tb_ref, o_ref, *, rows, rg):
    g = pl.program_id(1)
    lane = lax.broadcasted_iota(jnp.int32, (1, 256), 1)
    nk = NA_KR * GRID_W
    sels = [(lane >= h * NA_DH) & (lane < (h + 1) * NA_DH) for h in range(NA_HEADS)]

    def row(r, _):
        grow = g * rg + r
        rs = jnp.clip(grow - NA_KR // 2, 0, rows - NA_KR)
        si = rs - grow + (NA_KR - 1)
        q = q_ref[r * GRID_W:(r + 1) * GRID_W, :]
        koff = pl.multiple_of(rs * GRID_W, GRID_W)
        kb = k_ref[pl.ds(koff, nk), :]
        vb = v_ref[pl.ds(koff, nk), :]
        q4 = jnp.concatenate([jnp.where(sel, q, jnp.zeros_like(q)) for sel in sels], axis=0)
        s = lax.dot_general(q4, kb, (((1,), (1,)), ((), ())), preferred_element_type=F32)
        s = s + tb_ref[si]
        m = jnp.max(s, -1, keepdims=True)
        p = jnp.exp2(s - m)
        l = jnp.sum(p, -1, keepdims=True)
        o4 = jnp.dot(p.astype(BF16), vb, preferred_element_type=F32) * (1.0 / l)
        out = jnp.zeros((GRID_W, 256), F32)
        for h, sel in enumerate(sels):
            out = jnp.where(sel, o4[h * GRID_W:(h + 1) * GRID_W], out)
        o_ref[r * GRID_W:(r + 1) * GRID_W, :] = out.astype(o_ref.dtype)
        return 0

    for r in range(rg):
        row(r, 0)


def _na_attention(proj, tb, nb, t):
    rows = t // GRID_W
    assert rows >= NA_KR
    rg = 8
    ng = rows // rg
    tq = rg * GRID_W
    kern = functools.partial(_na_attn_kernel, rows=rows, rg=rg)
    return pl.pallas_call(
        kern,
        grid=(nb, ng),
        in_specs=[pl.BlockSpec((None, tq, 256), lambda b, i: (S_DQ, b * ng + i, 0)),
                  pl.BlockSpec((None, t, 256), lambda b, i: (S_DK, b, 0)),
                  pl.BlockSpec((None, t, 256), lambda b, i: (S_DV, b, 0)),
                  pl.BlockSpec(tb.shape, lambda b, i: (0, 0, 0))],
        out_specs=pl.BlockSpec((tq, 256), lambda b, i: (b * ng + i, 0)),
        out_shape=jax.ShapeDtypeStruct((nb * t, 256), BF16),
        compiler_params=_cparams(("arbitrary", "arbitrary")),
        name="na_attn",
    )(proj, proj, proj, tb)


def _layer_norm(y, g, b):
    mu = jnp.mean(y, -1, keepdims=True)
    yc = y - mu
    var = jnp.mean(yc * yc, -1, keepdims=True)
    return yc * lax.rsqrt(var + LN_EPS) * g + b


def _outproj_kernel(oa_ref, ob_ref, oc_ref, od_ref, x_ref, g1_ref, sc2_ref, sh2_ref, w_ref,
                    lg_ref, lb_ref, wr_ref, x1_ref, h2_ref, aff_ref):
    m = jnp.dot(oa_ref[...], w_ref[0:256, :], preferred_element_type=F32)
    m += jnp.dot(ob_ref[...], w_ref[256:512, :], preferred_element_type=F32)
    m += jnp.dot(oc_ref[...], w_ref[512:768, :], preferred_element_type=F32)
    m += jnp.dot(od_ref[...], w_ref[768:1024, :], preferred_element_type=F32)
    y = DEEPNORM_ALPHA * x_ref[...] + (1.0 + g1_ref[...]) * m
    x1 = _layer_norm(y, lg_ref[...], lb_ref[...])
    x1_ref[...] = x1
    h2 = x1 * (1.0 + sc2_ref[...]) + sh2_ref[...]
    h2_ref[...] = h2.astype(BF16)
    h_hi, h_lo = _split_bf16(h2)
    w_hi, w_lo = _split_bf16(wr_ref[...])
    nt = (((1,), (1,)), ((), ()))
    lg = lax.dot_general(w_hi, h_hi, nt, preferred_element_type=F32)
    lg += lax.dot_general(w_hi, h_lo, nt, preferred_element_type=F32)
    lg += lax.dot_general(w_lo, h_hi, nt, preferred_element_type=F32)
    lg = lg - jnp.max(lg, 0, keepdims=True)
    e = jnp.exp(lg)
    aff_ref[...] = e / jnp.sum(e, 0, keepdims=True)


def _outproj(oa, ob, oc, od, x, g1, sc2, sh2, w_out, ln_g, ln_b, w_router_t, nb, t):
    tm = 512
    nt = t // tm
    ntok = nb * t
    tok = lambda w: pl.BlockSpec((tm, w), lambda i: (i, 0))
    per_b = pl.BlockSpec((None, 1, D_MODEL), lambda i: (i // nt, 0, 0))
    full = lambda shape: pl.BlockSpec(shape, lambda i: tuple(0 for _ in shape))
    return pl.pallas_call(
        _outproj_kernel,
        grid=(ntok // tm,),
        in_specs=[tok(256), tok(256), tok(256), tok(256), tok(D_MODEL), per_b, per_b, per_b,
                  full((D_MODEL, D_MODEL)), full((1, D_MODEL)), full((1, D_MODEL)),
                  full((N_EXPERTS, D_MODEL))],
        out_specs=[tok(D_MODEL), tok(D_MODEL),
                   pl.BlockSpec((N_EXPERTS, tm), lambda i: (0, i))],
        out_shape=[jax.ShapeDtypeStruct((ntok, D_MODEL), F32),
                   jax.ShapeDtypeStruct((ntok, D_MODEL), BF16),
                   jax.ShapeDtypeStruct((N_EXPERTS, ntok), F32)],
        compiler_params=_cparams(("arbitrary",)),
        name="out_proj",
    )(oa, ob, oc, od, x, g1, sc2, sh2, w_out, ln_g, ln_b, w_router_t)


EXPERT_TF = 256


def _expert_kernel(x_ref, wg_ref, wu_ref, wd_ref, o_ref, acc_ref):
    x = x_ref[...]
    for c in range(D_FF // EXPERT_TF):
        cols = slice(c * EXPERT_TF, (c + 1) * EXPERT_TF)
        g = jnp.dot(x, wg_ref[:, cols], preferred_element_type=F32)
        u = jnp.dot(x, wu_ref[:, cols], preferred_element_type=F32)
        hmid = (g * (1.0 / (1.0 + jnp.exp(-g))) * u).astype(BF16)
        part = jnp.dot(hmid, wd_ref[cols, :], preferred_element_type=F32)
        if c == 0:
            acc_ref[...] = part
        else:
            acc_ref[...] += part
    o_ref[...] = acc_ref[...].astype(o_ref.dtype)


def _experts(xe, slots, wg, wu, wd):
    ne, _, d = xe.shape
    tm = math.gcd(slots, 1024)
    return pl.pallas_call(
        _expert_kernel,
        grid=(ne, slots // tm),
        in_specs=[pl.BlockSpec((None, tm, d), lambda e, m: (e, m, 0)),
                  pl.BlockSpec((None, d, D_FF), lambda e, m: (e, 0, 0)),
                  pl.BlockSpec((None, d, D_FF), lambda e, m: (e, 0, 0)),
                  pl.BlockSpec((None, D_FF, d), lambda e, m: (e, 0, 0))],
        out_specs=pl.BlockSpec((None, tm, d), lambda e, m: (e, m, 0)),
        out_shape=jax.ShapeDtypeStruct((ne, slots, d), BF16),
        scratch_shapes=[pltpu.VMEM((tm, d), F32)],
        compiler_params=_cparams(("arbitrary", "arbitrary")),
        name="expert_ffn",
    )(xe, wg, wu, wd)


MOE_TM = 512
MOE_WIN = 128
ROW_ALIGN = 16


def _select_kernel(aff_ref, o_ref, *, cap):
    a = aff_ref[...]
    keys = lax.bitcast_convert_type(a, jnp.int32)
    ne, n = a.shape
    capf = float(cap)

    def count(mask):
        return jnp.sum(jnp.where(mask, 1.0, 0.0), axis=1, keepdims=True)

    def key_bit(b, thr):
        cand = thr | lax.shift_left(jnp.int32(1), 30 - b)
        return jnp.where(count(keys >= cand) >= capf, cand, thr)

    thr = lax.fori_loop(0, 31, key_bit, jnp.zeros((ne, 1), jnp.int32))
    above = keys > thr
    need = capf - count(above)
    idx = lax.broadcasted_iota(jnp.int32, (ne, n), 1)
    tie_idx = jnp.where(keys == thr, idx, jnp.int32(2 ** 30))
    nbits = max(1, (n - 1).bit_length())

    def idx_bit(b, j):
        cand = j | lax.shift_left(jnp.int32(1), nbits - 1 - b)
        return jnp.where(count(tie_idx < cand) < need, cand, j)

    j = lax.fori_loop(0, nbits, idx_bit, jnp.zeros((ne, 1), jnp.int32))
    sel = above | (tie_idx <= j)
    o_ref[...] = jnp.where(sel, a, -1.0)


def _select(aff, cap):
    ne, n = aff.shape
    return pl.pallas_call(
        functools.partial(_select_kernel, cap=cap),
        out_shape=jax.ShapeDtypeStruct((ne, n), F32),
        compiler_params=pltpu.CompilerParams(vmem_limit_bytes=VMEM_LIMIT),
        name="ec_select",
    )(aff)


def _routing_tables(gs, group_tokens):
    ne = gs.shape[0]
    a_l, off_l, cnt_l, lim_l = [], [], [], []
    tok0, slot0 = 0, 0
    for n in group_tokens:
        cap = EC_FACTOR * n // N_EXPERTS
        nt = n // MOE_TM
        sel = lax.slice_in_dim(gs, tok0, tok0 + n, axis=1) >= 0
        counts = jnp.sum(sel.reshape(ne, nt, MOE_TM), axis=-1, dtype=jnp.int32)
        s0 = slot0 + jnp.cumsum(counts, axis=1) - counts
        a = (s0 // ROW_ALIGN) * ROW_ALIGN
        a_l.append(a)
        off_l.append(s0 - a)
        cnt_l.append(counts)
        lim_l.append(jnp.full((nt,), slot0 + cap - MOE_WIN, jnp.int32))
        tok0 += n
        slot0 += cap
    a = jnp.concatenate(a_l, axis=1).T
    off = jnp.concatenate(off_l, axis=1).T
    end = off + jnp.concatenate(cnt_l, axis=1).T
    rounds = (end + MOE_WIN - 1) // MOE_WIN
    gp = (end // ROW_ALIGN) * ROW_ALIGN
    return dict(a=a.reshape(-1).astype(jnp.int32), nr=rounds.reshape(-1).astype(jnp.int32),
                nrounds=jnp.max(rounds, axis=1).astype(jnp.int32), lim=jnp.concatenate(lim_l),
                off_col=off.astype(F32)[:, :, None], off_row=off.astype(F32)[:, None, :],
                off16=jnp.repeat(off.astype(F32), ROW_ALIGN, axis=1)[:, :, None],
                gp16=jnp.repeat(gp.astype(F32), ROW_ALIGN, axis=1)[:, :, None],
                total=slot0)


def _dispatch_kernel(a_tab, nrounds, nr_tab, gs_ref, off_ref, off16_ref, gp16_ref, x_ref, u_ref, xe_ref,
                     pos_ref, c_ref, c2_ref, stage_ref, carry_ref, sem, rc_ref):
    j = pl.program_id(0)
    ne, tm = gs_ref.shape
    d = x_ref.shape[1]
    win = MOE_WIN

    @pl.when(j == 0)
    def _():
        carry_ref[...] = jnp.zeros_like(carry_ref)
        rc_ref[0] = 0

    sel = gs_ref[...] >= 0.0
    rank = jnp.dot(jnp.where(sel, 1.0, 0.0).astype(BF16), u_ref[...], preferred_element_type=F32)
    pos_ref[...] = jnp.where(sel, rank + off_ref[...], -1.0)

    def wait_round(slot):
        def one(_, c):
            pltpu.make_async_copy(stage_ref.at[slot, pl.ds(0, win)], xe_ref.at[0, pl.ds(0, win)],
                                  sem.at[slot]).wait()
            return c
        lax.fori_loop(0, rc_ref[1 + slot], one, 0)

    def round_body(r, _):
        k = lax.broadcasted_iota(jnp.int32, (win, tm), 0).astype(F32) + (r * win).astype(F32)
        for e in range(ne):
            c_ref[e * win:(e + 1) * win, :] = jnp.where(pos_ref[e:e + 1, :] == k, 1.0, 0.0).astype(BF16)
        slot = rc_ref[0] % 2
        for nb in range(d // MXU_DIM):
            cols = slice(nb * MXU_DIM, (nb + 1) * MXU_DIM)
            stage_ref[slot, :, cols] = jnp.dot(c_ref[...], x_ref[:, cols],
                                               preferred_element_type=F32).astype(BF16)

        @pl.when(r == 0)
        def _():
            k16 = lax.broadcasted_iota(jnp.int32, (ROW_ALIGN, 1), 0).astype(F32)
            for e in range(ne):
                keep = k16 < off16_ref[e * ROW_ALIGN:(e + 1) * ROW_ALIGN, :]
                rows = pl.ds(e * win, ROW_ALIGN)
                stage_ref[slot, rows, :] = jnp.where(
                    keep, carry_ref[e * ROW_ALIGN:(e + 1) * ROW_ALIGN, :], stage_ref[slot, rows, :])

        @pl.when(rc_ref[0] > 0)
        def _():
            wait_round(1 - slot)

        rc_ref[1 + slot] = 0
        for e in range(ne):
            @pl.when(r < nr_tab[j * ne + e])
            def _():
                dst = pl.multiple_of(a_tab[j * ne + e] + r * win, ROW_ALIGN)
                pltpu.make_async_copy(stage_ref.at[slot, pl.ds(e * win, win)],
                                      xe_ref.at[e, pl.ds(dst, win)], sem.at[slot]).start()
                rc_ref[1 + slot] = rc_ref[1 + slot] + 1
        rc_ref[0] = rc_ref[0] + 1
        return 0

    lax.fori_loop(0, nrounds[j], round_body, 0)

    k16 = lax.broadcasted_iota(jnp.int32, (ROW_ALIGN, 1), 0).astype(F32)
    for e in range(ne):
        rows = slice(e * ROW_ALIGN, (e + 1) * ROW_ALIGN)
        c2_ref[rows, :] = jnp.where(pos_ref[e:e + 1, :] == gp16_ref[rows, :] + k16, 1.0, 0.0).astype(BF16)
    kk = jnp.concatenate([k16] * ne, axis=0)
    keep_old = (gp16_ref[...] == 0.0) & (kk < off16_ref[...])
    for nb in range(d // MXU_DIM):
        cols = slice(nb * MXU_DIM, (nb + 1) * MXU_DIM)
        new = jnp.dot(c2_ref[...], x_ref[:, cols], preferred_element_type=F32).astype(BF16)
        carry_ref[:, cols] = jnp.where(keep_old, carry_ref[:, cols], new)

    @pl.when((j == pl.num_programs(0) - 1) & (rc_ref[0] > 0))
    def _():
        wait_round((rc_ref[0] - 1) % 2)


def _dispatch(gs, h2, tabs, u_mat):
    ne, ntok = gs.shape
    d = h2.shape[1]
    nt = ntok // MOE_TM
    rows = tabs['total'] + MOE_WIN
    grid_spec = pltpu.PrefetchScalarGridSpec(
        num_scalar_prefetch=3,
        grid=(nt,),
        in_specs=[pl.BlockSpec((ne, MOE_TM), lambda j, *_: (0, j)),
                  pl.BlockSpec((None, ne, 1), lambda j, *_: (j, 0, 0)),
                  pl.BlockSpec((None, ne * ROW_ALIGN, 1), lambda j, *_: (j, 0, 0)),
                  pl.BlockSpec((None, ne * ROW_ALIGN, 1), lambda j, *_: (j, 0, 0)),
                  pl.BlockSpec((MOE_TM, d), lambda j, *_: (j, 0)),
                  pl.BlockSpec((MOE_TM, MOE_TM), lambda j, *_: (0, 0))],
        out_specs=pl.BlockSpec(memory_space=pl.ANY),
        scratch_shapes=[pltpu.VMEM((ne, MOE_TM), F32),
                        pltpu.VMEM((ne * MOE_WIN, MOE_TM), BF16),
                        pltpu.VMEM((ne * ROW_ALIGN, MOE_TM), BF16),
                        pltpu.VMEM((2, ne * MOE_WIN, d), BF16),
                        pltpu.VMEM((ne * ROW_ALIGN, d), BF16),
                        pltpu.SemaphoreType.DMA((2,)),
                        pltpu.SMEM((3,), jnp.int32)])
    return pl.pallas_call(
        _dispatch_kernel,
        grid_spec=grid_spec,
        out_shape=jax.ShapeDtypeStruct((ne, rows, d), BF16),
        compiler_params=_cparams(("arbitrary",)),
        name="ec_dispatch",
    )(tabs['a'], tabs['nrounds'], tabs['nr'], gs, tabs['off_col'], tabs['off16'], tabs['gp16'], h2,
      u_mat)


def _combine_kernel(a_tab, nrounds, lim_tab, gs_ref, off_ref, l_ref, ye_ref, x1_ref, g2_ref, lg_ref,
                    lb_ref, *refs, split_tiles):
    outs, (p_ref, y_ref, acc_ref, sem) = refs[:-4], refs[-4:]
    j = pl.program_id(0)
    tm, ne = gs_ref.shape
    d = x1_ref.shape[1]
    win = MOE_WIN
    gs = gs_ref[...]
    sel = gs >= 0.0
    rank = jnp.dot(l_ref[...], jnp.where(sel, 1.0, 0.0).astype(BF16), preferred_element_type=F32)
    pos = jnp.where(sel, rank + off_ref[...], -1.0)
    gate = jnp.where(sel, gs, 0.0)
    acc_ref[...] = jnp.zeros_like(acc_ref)
    n_tiles = pl.num_programs(0)
    slot = j % 2

    def window(tile, r, e):
        want = a_tab[tile * ne + e] + r * win
        src = pl.multiple_of(jnp.minimum(want, lim_tab[tile]), ROW_ALIGN)
        return src, want - src

    def start_round(tile, r, sl):
        for e in range(ne):
            src, _ = window(tile, r, e)
            pltpu.make_async_copy(ye_ref.at[e, pl.ds(src, win)], y_ref.at[sl, pl.ds(e * win, win)],
                                  sem.at[sl]).start()

    def wait_round(sl):
        for e in range(ne):
            pltpu.make_async_copy(ye_ref.at[e, pl.ds(0, win)], y_ref.at[sl, pl.ds(e * win, win)],
                                  sem.at[sl]).wait()

    @pl.when((j == 0) & (nrounds[0] > 0))
    def _():
        start_round(0, 0, 0)

    nxt = jnp.minimum(j + 1, n_tiles - 1)

    @pl.when((j + 1 < n_tiles) & (nrounds[nxt] > 0))
    def _():
        start_round(nxt, 0, 1 - slot)

    half = ne // 2

    def accumulate(r):
        base = lax.convert_element_type(r * win, F32)
        k = lax.broadcasted_iota(jnp.int32, (tm, win), 1).astype(F32) + base
        for grp in range(2):
            for e in range(grp * half, (grp + 1) * half):
                pe = pos[:, e:e + 1]
                pe = jnp.where(pe >= base, pe + window(j, r, e)[1].astype(F32), -1.0)
                pcol = jnp.broadcast_to(pe, (tm, win))
                gcol = jnp.broadcast_to(gate[:, e:e + 1], (tm, win))
                p_ref[:, e * win:(e + 1) * win] = jnp.where(pcol == k, gcol, 0.0).astype(BF16)
            rows = slice(grp * half * win, (grp + 1) * half * win)
            for nb in range(d // MXU_DIM):
                cols = slice(nb * MXU_DIM, (nb + 1) * MXU_DIM)
                acc_ref[:, cols] += jnp.dot(p_ref[:, rows], y_ref[slot, rows, cols],
                                            preferred_element_type=F32)

    @pl.when(nrounds[j] > 0)
    def _():
        wait_round(slot)
        accumulate(0)

    def extra_round(r, _):
        start_round(j, r, slot)
        wait_round(slot)
        accumulate(r)
        return 0

    lax.fori_loop(1, nrounds[j], extra_round, 0)
    y = DEEPNORM_ALPHA * x1_ref[...] + (1.0 + g2_ref[...]) * acc_ref[...]
    res = _layer_norm(y, lg_ref[...], lb_ref[...])
    if split_tiles is None:
        outs[0][...] = res
    else:
        @pl.when(j < split_tiles)
        def _():
            outs[0][...] = res

        @pl.when(j >= split_tiles)
        def _():
            outs[1][...] = res


def _combine_postnorm(gs_tok, ye, tabs, l_mat, x1, g2, ln_g, ln_b, t, split_tiles=None):
    ntok, ne = gs_tok.shape
    d = x1.shape[1]
    nt = ntok // MOE_TM
    tiles_per_seq = t // MOE_TM
    if split_tiles is None:
        out_specs = pl.BlockSpec((MOE_TM, d), lambda j, *_: (j, 0))
        out_shape = jax.ShapeDtypeStruct((ntok, d), F32)
    else:
        out_specs = [pl.BlockSpec((MOE_TM, d), lambda j, *_: (jnp.minimum(j, split_tiles - 1), 0)),
                     pl.BlockSpec((MOE_TM, d), lambda j, *_: (jnp.maximum(j - split_tiles, 0), 0))]
        out_shape = [jax.ShapeDtypeStruct((split_tiles * MOE_TM, d), F32),
                     jax.ShapeDtypeStruct((ntok - split_tiles * MOE_TM, d), F32)]
    grid_spec = pltpu.PrefetchScalarGridSpec(
        num_scalar_prefetch=3,
        grid=(nt,),
        in_specs=[pl.BlockSpec((MOE_TM, ne), lambda j, *_: (j, 0)),
                  pl.BlockSpec((None, 1, ne), lambda j, *_: (j, 0, 0)),
                  pl.BlockSpec((MOE_TM, MOE_TM), lambda j, *_: (0, 0)),
                  pl.BlockSpec(memory_space=pl.ANY),
                  pl.BlockSpec((MOE_TM, d), lambda j, *_: (j, 0)),
                  pl.BlockSpec((None, 1, d), lambda j, *_: (j // tiles_per_seq, 0, 0)),
                  pl.BlockSpec((1, d), lambda j, *_: (0, 0)),
                  pl.BlockSpec((1, d), lambda j, *_: (0, 0))],
        out_specs=out_specs,
        scratch_shapes=[pltpu.VMEM((MOE_TM, ne * MOE_WIN), BF16),
                        pltpu.VMEM((2, ne * MOE_WIN, d), BF16),
                        pltpu.VMEM((MOE_TM, d), F32),
                        pltpu.SemaphoreType.DMA((2,))])
    return pl.pallas_call(
        functools.partial(_combine_kernel, split_tiles=split_tiles),
        grid_spec=grid_spec,
        out_shape=out_shape,
        compiler_params=_cparams(("arbitrary",)),
        name="ec_combine",
    )(tabs['a'], tabs['nrounds'], tabs['lim'], gs_tok, tabs['off_row'], l_mat, ye, x1, g2, ln_g, ln_b)


def _prep_w_in(w_in_l):
    sizes = (256, 256, 256, MLA_Q_RANK, MLA_KV_RANK, MLA_ROPE, 256, 256, 256, 256, 256, 256)
    offs = np.concatenate([[0], np.cumsum(sizes)])
    part = [w_in_l[:, offs[i]:offs[i + 1]] for i in range(len(sizes))]
    a_q, a_k, a_v, b_cq, b_ckv, b_kr, c_q, c_k, c_v, d_q, d_k, d_v = part
    d = w_in_l.shape[0]
    zeros = lambda n: jnp.zeros((d, n), w_in_l.dtype)
    def per_head(v):
        out = []
        for h in range(4):
            out += [v[:, h * 64:(h + 1) * 64], zeros(LANES - 64)]
        return out

    cols = ([a_q, a_k] + per_head(a_v) + [c_q, c_k] + per_head(c_v)
            + [d_q, d_k, d_v, b_cq, zeros(256 - MLA_Q_RANK), b_ckv, b_kr, b_kr,
               zeros(128 - 2 * MLA_ROPE)])
    return jnp.concatenate(cols, axis=1).astype(BF16)


def _prep_w_uq(w_uq_l):
    hd = MLA_NOPE + MLA_ROPE
    nope = [w_uq_l[:, h * hd:h * hd + MLA_NOPE] for h in range(MLA_HEADS)]
    rope = [w_uq_l[:, h * hd + MLA_NOPE:(h + 1) * hd] for h in range(MLA_HEADS)]
    z = jnp.zeros((w_uq_l.shape[0], 256 - 2 * hd), w_uq_l.dtype)
    cols = []
    for p in range(2):
        cols += [nope[2 * p], nope[2 * p + 1], rope[2 * p], rope[2 * p + 1], z]
    w = jnp.concatenate(cols, axis=1)
    w = jnp.concatenate([w, jnp.zeros((256 - MLA_Q_RANK, w.shape[1]), w.dtype)], axis=0)
    return w.astype(BF16)


def _prep_w_ukv(w_ukv_l):
    hd = MLA_NOPE + MLA_DV
    kn = [w_ukv_l[:, h * hd:h * hd + MLA_NOPE] for h in range(MLA_HEADS)]
    z = jnp.zeros((w_ukv_l.shape[0], LANES - MLA_DV), w_ukv_l.dtype)
    vv = []
    for h in range(MLA_HEADS):
        vv += [w_ukv_l[:, h * hd + MLA_NOPE:(h + 1) * hd], z]
    return jnp.concatenate(kn + vv, axis=1).astype(BF16)


def _select_groups(aff_t, group_tokens):
    parts, off = [], 0
    for n in group_tokens:
        parts.append(_select(lax.slice_in_dim(aff_t, off, off + n, axis=1),
                             EC_FACTOR * n // N_EXPERTS))
        off += n
    return jnp.concatenate(parts, axis=1)


def _trunk(x, c, group_tokens, nb, t, p):
    ntok = nb * t
    tabs = (_rope_tables(t, 256, DA_DQK, DA_DQK // ROPE_FRACTION, ROPE_THETA)
            + _rope_tables(t, 256, DIL_DH, DIL_DH // ROPE_FRACTION, ROPE_THETA)
            + _rope_tables(t, 128, MLA_ROPE, MLA_ROPE, MLA_ROPE_THETA))
    gmat = jnp.asarray(np.kron(np.eye(4), np.full((64, 64), 1.0 / 64)), BF16)
    ti = jnp.arange(MOE_TM)
    u_mat = (ti[:, None] < ti[None, :]).astype(BF16)
    l_mat = (ti[None, :] < ti[:, None]).astype(BF16)
    for l in range(DEPTH):
        mod = _modulation(c, p['w_ada'][l], p['b_ada'][l])
        sh1, sc1, g1, sh2, sc2, g2 = [m.reshape(nb, 1, D_MODEL) for m in jnp.split(mod, 6, axis=-1)]
        gq = jnp.concatenate([p['q_norm_g'][l], jnp.zeros((256 - MLA_Q_RANK,), F32)]).reshape(1, 256)
        gkv = p['kv_norm_g'][l].reshape(1, 128)
        proj, projv = _inproj(x, sc1, sh1, _prep_w_in(p['w_in'][l]), _prep_w_uq(p['w_uq'][l]),
                              _prep_w_ukv(p['w_ukv'][l]), gq, gkv, tabs, nb, t)
        lam_init = 0.8 - 0.6 * math.exp(-0.3 * l)
        lam = (jnp.exp(jnp.sum(p['da_lq1'][l] * p['da_lk1'][l]))
               - jnp.exp(jnp.sum(p['da_lq2'][l] * p['da_lk2'][l])) + lam_init).reshape(1)
        g_sub = jnp.tile(p['da_subln_g'][l], DA_HEADS).reshape(1, 256)
        oa = _diff_attention(proj, projv, lam, g_sub, gmat, nb, t, 1.0 - lam_init)
        ob = _mla_attention(proj, projv, nb, t)
        oc = _dil_attention(proj, projv, nb, t)
        od = _na_attention(proj, _na_bias_table(p['na_rpb'][l]), nb, t)
        x1, h2, aff_t = _outproj(oa, ob, oc, od, x, g1, sc2, sh2, p['w_out'][l].astype(BF16),
                                 p['ln1_g'][l].reshape(1, -1), p['ln1_b'][l].reshape(1, -1),
                                 p['w_router'][l].T, nb, t)
        wg = p['w_e_gate'][l].astype(BF16)
        wu = p['w_e_up'][l].astype(BF16)
        wd = p['w_e_down'][l].astype(BF16)
        gs = _select_groups(aff_t, group_tokens)
        rt = _routing_tables(gs, group_tokens)
        xe = _dispatch(gs, h2, rt, u_mat)
        ye = _experts(xe, rt['total'], wg, wu, wd)
        split = group_tokens[0] // MOE_TM if l == DEPTH - 1 else None
        x = _combine_postnorm(gs.T, ye, rt, l_mat, x1, g2, p['ln2_g'][l].reshape(1, -1),
                              p['ln2_b'][l].reshape(1, -1), t, split_tiles=split)
    return x


def kernel(x_prompt, x_sample, c_prompt, c_sample, w_in, w_uq, w_ukv, q_norm_g, kv_norm_g, da_lq1,
           da_lk1, da_lq2, da_lk2, da_subln_g, na_rpb, w_out, w_ada, b_ada, ln1_g, ln1_b, ln2_g,
           ln2_b, w_router, w_e_gate, w_e_up, w_e_down):
    p = dict(w_in=w_in, w_uq=w_uq, w_ukv=w_ukv, q_norm_g=q_norm_g, kv_norm_g=kv_norm_g,
             da_lq1=da_lq1, da_lk1=da_lk1, da_lq2=da_lq2, da_lk2=da_lk2, da_subln_g=da_subln_g,
             na_rpb=na_rpb, w_out=w_out, w_ada=w_ada, b_ada=b_ada, ln1_g=ln1_g, ln1_b=ln1_b,
             ln2_g=ln2_g, ln2_b=ln2_b, w_router=w_router, w_e_gate=w_e_gate, w_e_up=w_e_up,
             w_e_down=w_e_down)
    bp, t, d = x_prompt.shape
    bs = x_sample.shape[0]
    assert x_sample.shape[1] == t
    nb = bp + bs
    x = jnp.concatenate([x_prompt.reshape(bp * t, d), x_sample.reshape(bs * t, d)], axis=0)
    c = jnp.concatenate([c_prompt, c_sample], axis=0)
    y_prompt, y_sample = _trunk(x, c, (bp * t, bs * t), nb, t, p)
    return y_prompt.reshape(bp, t, d), y_sample.reshape(bs, t, d)
```

```python
import functools
import math

import jax
import jax.numpy as jnp
import numpy as np
from jax import lax
from jax.experimental import pallas as pl
from jax.experimental.pallas import tpu as pltpu

F32 = jnp.float32
BF16 = jnp.bfloat16

D_MODEL = 1024
DEPTH = 2
GRID_W = 64
GROUP_W = 256
DA_HEADS, DA_DV, DA_DQK = 4, 64, 32
MLA_HEADS, MLA_Q_RANK, MLA_KV_RANK, MLA_NOPE, MLA_ROPE, MLA_DV = 4, 192, 128, 64, 32, 64
MLA_ROPE_THETA = 10000.0
DIL_HEADS, DIL_DH = 4, 64
DIL_PATTERNS = ((128, 1), (512, 4), (2048, 16))
NA_HEADS, NA_DH, NA_KR, NA_KC = 4, 64, 8, 16
ROPE_THETA = 500000.0
ROPE_FRACTION = 4
N_EXPERTS = 16
EC_FACTOR = 2
D_FF = 2816
DEEPNORM_ALPHA = (2.0 * DEPTH) ** 0.25
NEG_INF = -1e30
LOG2E = math.log2(math.e)
LN_EPS = 1e-5
RMS_EPS = 1e-6

LANES = 128
MXU_DIM = 256
VMEM_LIMIT = 56 * 1024 * 1024

S_AQ, S_AK, S_BQ0, S_BQ1, S_BK0, S_BK1, S_CQ, S_CK, S_DQ, S_DK, S_DV = range(11)
N_SLOTS = 11
V_A, V_B, V_C = 0, 4, 8
N_VSLOTS = 12
V_ONE_LANE = 64
W_ALL_COLS = 13 * 256


def _cparams(sem):
    return pltpu.CompilerParams(dimension_semantics=sem, vmem_limit_bytes=VMEM_LIMIT)


def _split_bf16(a):
    hi = a.astype(BF16)
    lo = (a - hi.astype(F32)).astype(BF16)
    return hi, lo


def _mod_kernel(c_ref, w_ref, b_ref, o_ref):
    c = c_ref[...]
    a = c * (1.0 / (1.0 + jnp.exp(-c)))
    a_hi, a_lo = _split_bf16(a)
    w_hi, w_lo = _split_bf16(w_ref[...])
    acc = jnp.dot(a_hi, w_hi, preferred_element_type=F32)
    acc += jnp.dot(a_hi, w_lo, preferred_element_type=F32)
    acc += jnp.dot(a_lo, w_hi, preferred_element_type=F32)
    o_ref[...] = acc + b_ref[...]


def _modulation(c, w_ada, b_ada):
    nb, d = c.shape
    n_out = w_ada.shape[1]
    tn = 1536
    return pl.pallas_call(
        _mod_kernel,
        grid=(n_out // tn,),
        in_specs=[pl.BlockSpec((nb, d), lambda j: (0, 0)),
                  pl.BlockSpec((d, tn), lambda j: (0, j)),
                  pl.BlockSpec((1, tn), lambda j: (0, j))],
        out_specs=pl.BlockSpec((nb, tn), lambda j: (0, j)),
        out_shape=jax.ShapeDtypeStruct((nb, n_out), F32),
        compiler_params=_cparams(("arbitrary",)),
        name="adaln_mod",
    )(c, w_ada, b_ada.reshape(1, n_out))


def _rope_tables(t, width, group, rot, theta):
    half = rot // 2
    inv = theta ** (-jnp.arange(half, dtype=F32) / half)
    ang = jnp.arange(t, dtype=F32)[:, None] * inv[None, :]
    cos, sin = jnp.cos(ang), jnp.sin(ang)
    ones = jnp.ones((t, group - rot), F32)
    zeros = jnp.zeros((t, group - rot), F32)
    c_g = jnp.concatenate([cos, cos, ones], axis=1)
    s_g = jnp.concatenate([-sin, sin, zeros], axis=1)
    reps = width // group
    return jnp.tile(c_g, (1, reps)), jnp.tile(s_g, (1, reps))


def _apply_rope(x, c_tab, s_tab, group, rot):
    width = x.shape[-1]
    half = rot // 2
    lane = lax.broadcasted_iota(jnp.int32, (1, width), 1)
    first = (lane % group) < half
    fwd = pltpu.roll(x, width - half, 1)
    bwd = pltpu.roll(x, half, 1)
    return x * c_tab + s_tab * jnp.where(first, fwd, bwd)


def _inproj_kernel(x_ref, sc_ref, sh_ref, w_ref, wuq_ref, wukv_ref, gq_ref, gkv_ref,
                   ca_ref, sa_ref, cc_ref, scc_ref, cm_ref, sm_ref, o_ref, ov_ref):
    h = (x_ref[...] * (1.0 + sc_ref[...]) + sh_ref[...]).astype(BF16)

    def proj(col, width):
        return jnp.dot(h, w_ref[:, col:col + width], preferred_element_type=F32)

    one_lane = lax.broadcasted_iota(jnp.int32, (1, LANES), 1) == V_ONE_LANE

    def put_values(first_slot, vals):
        for hh in range(4):
            v = vals[:, LANES * hh:LANES * (hh + 1)]
            ov_ref[first_slot + hh] = jnp.where(one_lane, 1.0, v).astype(BF16)

    sa_scale = DA_DQK ** -0.5 * LOG2E
    sb_scale = (MLA_NOPE + MLA_ROPE) ** -0.5 * LOG2E
    sc_scale = DIL_DH ** -0.5 * LOG2E
    sd_scale = NA_DH ** -0.5 * LOG2E
    a_rot = DA_DQK // ROPE_FRACTION
    c_rot = DIL_DH // ROPE_FRACTION

    cm, sm = cm_ref[...], sm_ref[...]
    cq = proj(2816, 256)
    cq = cq * lax.rsqrt(jnp.sum(cq * cq, -1, keepdims=True) * (1.0 / MLA_Q_RANK) + RMS_EPS)
    cq = (cq * gq_ref[...]).astype(BF16)
    q2 = jnp.dot(cq, wuq_ref[...], preferred_element_type=F32)
    for p in range(2):
        qp = q2[:, 256 * p:256 * (p + 1)]
        o_ref[S_BQ0 + p, :, 0:128] = (qp[:, 0:128] * sb_scale).astype(BF16)
        o_ref[S_BQ0 + p, :, 128:256] = (
            _apply_rope(qp[:, 128:256], cm, sm, MLA_ROPE, MLA_ROPE) * sb_scale).astype(BF16)
    ckv = proj(3072, 128)
    ckv = ckv * lax.rsqrt(jnp.mean(ckv * ckv, -1, keepdims=True) + RMS_EPS)
    ckv = (ckv * gkv_ref[...]).astype(BF16)
    kv = jnp.dot(ckv, wukv_ref[...], preferred_element_type=F32)
    kr = _apply_rope(proj(3200, 128), cm, sm, MLA_ROPE, MLA_ROPE).astype(BF16)
    for p in range(2):
        o_ref[S_BK0 + p, :, 0:128] = kv[:, 128 * p:128 * (p + 1)].astype(BF16)
        o_ref[S_BK0 + p, :, 128:256] = kr
    put_values(V_B, kv[:, 256:768])

    ca, sa = ca_ref[...], sa_ref[...]
    o_ref[S_AQ] = (_apply_rope(proj(0, 256), ca, sa, DA_DQK, a_rot) * sa_scale).astype(BF16)
    o_ref[S_AK] = _apply_rope(proj(256, 256), ca, sa, DA_DQK, a_rot).astype(BF16)
    put_values(V_A, proj(512, 512))
    cc, scc = cc_ref[...], scc_ref[...]
    o_ref[S_CQ] = (_apply_rope(proj(1024, 256), cc, scc, DIL_DH, c_rot) * sc_scale).astype(BF16)
    o_ref[S_CK] = _apply_rope(proj(1280, 256), cc, scc, DIL_DH, c_rot).astype(BF16)
    put_values(V_C, proj(1536, 512))
    o_ref[S_DQ] = (proj(2048, 256) * sd_scale).astype(BF16)
    o_ref[S_DK] = proj(2304, 256).astype(BF16)
    o_ref[S_DV] = proj(2560, 256).astype(BF16)


def _inproj(x, sc, sh, w_all, wuq, wukv, gq, gkv, tabs, nb, t):
    tm = 512
    nt = t // tm
    ntok = nb * t
    ca, sa, cc, scc, cm, sm = tabs
    full = lambda shape: pl.BlockSpec(shape, lambda j, b: tuple(0 for _ in shape))
    tab = lambda w: pl.BlockSpec((tm, w), lambda j, b: (j, 0))
    return pl.pallas_call(
        _inproj_kernel,
        grid=(nt, nb),
        in_specs=[pl.BlockSpec((tm, D_MODEL), lambda j, b: (b * nt + j, 0)),
                  pl.BlockSpec((None, 1, D_MODEL), lambda j, b: (b, 0, 0)),
                  pl.BlockSpec((None, 1, D_MODEL), lambda j, b: (b, 0, 0)),
                  full((D_MODEL, W_ALL_COLS)), full((256, 512)), full((128, 768)),
                  full((1, 256)), full((1, 128)),
                  tab(256), tab(256), tab(256), tab(256), tab(128), tab(128)],
        out_specs=[pl.BlockSpec((N_SLOTS, tm, 256), lambda j, b: (0, b * nt + j, 0)),
                   pl.BlockSpec((N_VSLOTS, tm, LANES), lambda j, b: (0, b * nt + j, 0))],
        out_shape=[jax.ShapeDtypeStruct((N_SLOTS, ntok, 256), BF16),
                   jax.ShapeDtypeStruct((N_VSLOTS, ntok, LANES), BF16)],
        compiler_params=_cparams(("arbitrary", "arbitrary")),
        name="in_proj",
    )(x, sc, sh, w_all, wuq, wukv, gq, gkv, ca, sa, cc, scc, cm, sm)


SOFTMAX_SLAB = 32


def _chain_scratch(tq, kc):
    return [pltpu.VMEM((tq, kc), F32), pltpu.VMEM((tq, kc), F32), pltpu.VMEM((tq, kc), BF16),
            pltpu.VMEM((tq, LANES), F32), pltpu.VMEM((tq, LANES), F32), pltpu.VMEM((tq, LANES), F32)]


CHAIN_REFS = 6


def _attention_steps(n_steps, qms_for, k_for, v_for, finish, tq, t, kc, chains, slab=SOFTMAX_SLAB,
                     bias_ref=None, step_unroll=1):
    n_chunks = t // kc
    assert n_chunks % 2 == 0
    nt = (((1,), (1,)), ((), ()))

    def chunk_rows(chunk):
        if isinstance(chunk, int):
            return pl.ds(chunk * kc, kc)
        return pl.ds(pl.multiple_of(chunk * kc, kc), kc)

    def scores(step, chunk, slot):
        k_refs = k_for(step)
        if not isinstance(k_refs, (list, tuple)):
            k_refs = [k_refs] * len(chains)
        loaded = {}
        for qm, k_ref, chain in zip(qms_for(step), k_refs, chains):
            if id(k_ref) not in loaded:
                loaded[id(k_ref)] = k_ref[chunk_rows(chunk), :]
            chain[slot][...] = lax.dot_general(qm, loaded[id(k_ref)], nt, preferred_element_type=F32)

    def softmax_pv(step, chunk, slot):
        for v_ref, chain in zip(v_for(step), chains):
            v = v_ref[chunk_rows(chunk), :]
            s_ref = chain[slot]
            (p_ref, m_ref, a_ref, acc_ref) = chain[2:]
            for r in range(tq // slab):
                rows = slice(r * slab, (r + 1) * slab)
                s = s_ref[rows, :]
                if bias_ref is not None:
                    s = s + bias_ref[rows, chunk * kc:(chunk + 1) * kc]
                m_prev = m_ref[rows, :]
                m_new = jnp.maximum(m_prev, jnp.max(s, -1, keepdims=True))
                d = s - jnp.concatenate([m_new] * (kc // LANES), axis=1)
                p_ref[rows, :] = jnp.exp2(d.astype(BF16))
                a_ref[rows, :] = jnp.exp2(m_prev - m_new)
                m_ref[rows, :] = m_new
            acc_ref[...] = a_ref[...] * acc_ref[...] + jnp.dot(p_ref[...], v,
                                                              preferred_element_type=F32)

    def step_body(st, _):
        for (_, _, _, m_ref, _, acc_ref) in chains:
            m_ref[...] = jnp.full(m_ref.shape, NEG_INF, F32)
            acc_ref[...] = jnp.zeros(acc_ref.shape, F32)
        nxt = min(st + 1, n_steps - 1) if isinstance(st, int) else jnp.minimum(st + 1, n_steps - 1)

        def body(j, _):
            scores(st, 2 * j + 1, 1)
            softmax_pv(st, 2 * j, 0)
            if 2 * j + 2 < n_chunks:
                scores(st, 2 * j + 2, 0)
            elif n_steps > 1:
                scores(nxt, 0, 0)
            softmax_pv(st, 2 * j + 1, 1)
            return 0

        for j in range(n_chunks // 2):
            body(j, 0)
        outs = []
        for chain in chains:
            acc = chain[5][...]
            den = jnp.broadcast_to(acc[:, V_ONE_LANE:V_ONE_LANE + 1], acc.shape)
            outs.append(acc * (1.0 / den))
        finish(st, outs)
        return 0

    scores(0, 0, 0)
    if step_unroll == n_steps:
        for st in range(n_steps):
            step_body(st, 0)
    else:
        lax.fori_loop(0, n_steps, step_body, 0, unroll=step_unroll)


def _place_head(pair_ref, o, j):
    lane = lax.broadcasted_iota(jnp.int32, (1, LANES), 1)
    low = lane < V_ONE_LANE
    shifted = pltpu.roll(o, V_ONE_LANE, 1)
    cur = pair_ref[...]
    pair_ref[...] = jnp.where(low, jnp.where(j == 0, o, cur), jnp.where(j == 1, shifted, cur))


def _group_mean_sq(x, gmat):
    sq = x * x
    hi, lo = _split_bf16(sq)
    return (jnp.dot(hi, gmat, preferred_element_type=F32)
            + jnp.dot(lo, gmat, preferred_element_type=F32))


def _diff_attn_kernel(lam_ref, q_ref, k_ref, v_ref, g_ref, gmat_ref, o_ref, acc_ref, *scratch, t, kc,
                      out_scale):
    chains = (scratch[:CHAIN_REFS], scratch[CHAIN_REFS:])
    lam = lam_ref[0]
    lane = lax.broadcasted_iota(jnp.int32, (1, 256), 1)

    def qms_for(h):
        q = q_ref[...]
        return [jnp.where((lane >= (2 * h + c) * DA_DQK) & (lane < (2 * h + c + 1) * DA_DQK), q,
                          jnp.zeros_like(q)) for c in range(2)]

    def finish(h, outs):
        _place_head(acc_ref.at[h // 2], outs[0] - lam * outs[1], h % 2)

    acc_ref[...] = jnp.zeros_like(acc_ref)
    _attention_steps(DA_HEADS, qms_for, lambda h: k_ref, lambda h: [v_ref.at[h], v_ref.at[h]],
                     finish, q_ref.shape[0], t, kc, chains)
    o = jnp.concatenate([acc_ref[0], acc_ref[1]], axis=1)
    ms = _group_mean_sq(o, gmat_ref[...])
    o_ref[...] = (o * lax.rsqrt(ms + RMS_EPS) * g_ref[...] * out_scale).astype(o_ref.dtype)


def _diff_attention(proj, projv, lam, g_tiled, gmat, nb, t, out_scale):
    tq, kc = 1024, 512
    nq = t // tq
    kern = functools.partial(_diff_attn_kernel, t=t, kc=kc, out_scale=out_scale)
    return pl.pallas_call(
        kern,
        grid=(nb, nq),
        in_specs=[pl.BlockSpec(memory_space=pltpu.SMEM),
                  pl.BlockSpec((None, tq, 256), lambda b, i: (S_AQ, b * nq + i, 0)),
                  pl.BlockSpec((None, t, 256), lambda b, i: (S_AK, b, 0)),
                  pl.BlockSpec((DA_HEADS, t, LANES), lambda b, i: (V_A // DA_HEADS, b, 0)),
                  pl.BlockSpec((1, 256), lambda b, i: (0, 0)),
                  pl.BlockSpec((256, 256), lambda b, i: (0, 0))],
        out_specs=pl.BlockSpec((tq, 256), lambda b, i: (b * nq + i, 0)),
        out_shape=jax.ShapeDtypeStruct((nb * t, 256), BF16),
        scratch_shapes=[pltpu.VMEM((2, tq, LANES), F32)] + 2 * _chain_scratch(tq, kc),
        compiler_params=_cparams(("arbitrary", "arbitrary")),
        name="diff_attn",
    )(lam, proj, proj, projv, g_tiled, gmat)


def _mla_attn_kernel(q_ref, k_ref, v_ref, o_ref, acc_ref, *scratch, t, kc):
    chains = tuple(scratch[c * CHAIN_REFS:(c + 1) * CHAIN_REFS] for c in range(MLA_HEADS))
    lane = lax.broadcasted_iota(jnp.int32, (1, 256), 1)
    k_pairs = [k_ref.at[0], k_ref.at[1]]

    def qms_for(_):
        qms = []
        for h in range(MLA_HEADS):
            q, j = q_ref[h // 2], h % 2
            nope = (lane >= j * MLA_NOPE) & (lane < (j + 1) * MLA_NOPE)
            rope = (lane >= 128 + j * MLA_ROPE) & (lane < 128 + (j + 1) * MLA_ROPE)
            qms.append(jnp.where(nope | rope, q, jnp.zeros_like(q)))
        return qms

    def finish(_, outs):
        for h in range(MLA_HEADS):
            _place_head(acc_ref.at[h // 2], outs[h], h % 2)

    acc_ref[...] = jnp.zeros_like(acc_ref)
    _attention_steps(1, qms_for, lambda _: [k_pairs[h // 2] for h in range(MLA_HEADS)],
                     lambda _: [v_ref.at[h] for h in range(MLA_HEADS)], finish, q_ref.shape[1], t, kc,
                     chains, step_unroll=1)
    o_ref[...] = jnp.concatenate([acc_ref[0], acc_ref[1]], axis=1).astype(o_ref.dtype)


def _mla_attention(proj, projv, nb, t):
    tq, kc = 512, 512
    nq = t // tq
    kern = functools.partial(_mla_attn_kernel, t=t, kc=kc)
    return pl.pallas_call(
        kern,
        grid=(nb, nq),
        in_specs=[pl.BlockSpec((2, tq, 256), lambda b, i: (S_BQ0 // 2, b * nq + i, 0)),
                  pl.BlockSpec((2, t, 256), lambda b, i: (S_BK0 // 2, b, 0)),
                  pl.BlockSpec((MLA_HEADS, t, LANES), lambda b, i: (V_B // MLA_HEADS, b, 0))],
        out_specs=pl.BlockSpec((tq, 256), lambda b, i: (b * nq + i, 0)),
        out_shape=jax.ShapeDtypeStruct((nb * t, 256), BF16),
        scratch_shapes=[pltpu.VMEM((2, tq, LANES), F32)] + MLA_HEADS * _chain_scratch(tq, kc),
        compiler_params=_cparams(("arbitrary", "arbitrary")),
        name="mla_attn",
    )(proj, proj, projv)


DIL_REACH = max(w // 2 for w, _ in DIL_PATTERNS)


def _dil_attn_kernel(q_ref, k_ref, v_ref, o_ref, tables_ref, acc_ref, *scratch, t, tq, band):
    chains = tuple(scratch[c * CHAIN_REFS:(c + 1) * CHAIN_REFS] for c in range(DIL_HEADS))
    i = pl.program_id(1)

    def band_start(blk):
        return jnp.clip(blk * tq - DIL_REACH, 0, t - band)

    start = pl.multiple_of(band_start(i), tq)
    table = (i * tq - start) // tq

    @pl.when((pl.program_id(0) == 0) & (i == 0))
    def _():
        def build(n, _):
            qi = lax.broadcasted_iota(jnp.int32, (tq, band), 0)
            kj = lax.broadcasted_iota(jnp.int32, (tq, band), 1)
            delta = kj - qi - n * tq
            ad = jnp.abs(delta)
            cnt = jnp.zeros((tq, band), F32)
            for window, dil in DIL_PATTERNS:
                ok = (ad <= window // 2) & ((delta & (dil - 1)) == 0)
                cnt = cnt + jnp.where(ok, 1.0, 0.0)
            tables_ref[n] = jnp.where(cnt > 2.5, math.log2(3.0),
                                      jnp.where(cnt > 1.5, 1.0,
                                                jnp.where(cnt > 0.5, 0.0, NEG_INF)))
            return 0

        lax.fori_loop(0, tables_ref.shape[0], build, 0)

    bias_ref = tables_ref.at[table]

    lane = lax.broadcasted_iota(jnp.int32, (1, 256), 1)

    def qms_for(_):
        q = q_ref[...]
        return [jnp.where((lane >= h * DIL_DH) & (lane < (h + 1) * DIL_DH), q, jnp.zeros_like(q))
                for h in range(DIL_HEADS)]

    def finish(_, outs):
        for h in range(DIL_HEADS):
            _place_head(acc_ref.at[h // 2], outs[h], h % 2)

    acc_ref[...] = jnp.zeros_like(acc_ref)
    _attention_steps(1, qms_for, lambda _: k_ref.at[pl.ds(start, band)],
                     lambda _: [v_ref.at[h, pl.ds(start, band)] for h in range(DIL_HEADS)],
                     finish, tq, band, band // 2, chains, slab=DIL_SLAB, bias_ref=bias_ref,
                     step_unroll=1)
    o_ref[...] = jnp.concatenate([acc_ref[0], acc_ref[1]], axis=1).astype(o_ref.dtype)


DIL_SLAB = 16


def _dil_attention(proj, projv, nb, t):
    tq = 256
    band = min(t, tq + 2 * DIL_REACH)
    nq = t // tq
    kern = functools.partial(_dil_attn_kernel, t=t, tq=tq, band=band)
    return pl.pallas_call(
        kern,
        grid=(nb, nq),
        in_specs=[pl.BlockSpec((None, tq, 256), lambda b, i: (S_CQ, b * nq + i, 0)),
                  pl.BlockSpec((None, t, 256), lambda b, i: (S_CK, b, 0)),
                  pl.BlockSpec((DIL_HEADS, t, LANES), lambda b, i: (V_C // DIL_HEADS, b, 0))],
        out_specs=pl.BlockSpec((tq, 256), lambda b, i: (b * nq + i, 0)),
        out_shape=jax.ShapeDtypeStruct((nb * t, 256), BF16),
        scratch_shapes=([pltpu.VMEM((band // tq, tq, band), F32), pltpu.VMEM((2, tq, LANES), F32)]
                        + DIL_HEADS * _chain_scratch(tq, band // 2)),
        compiler_params=_cparams(("arbitrary", "arbitrary")),
        name="dil_attn",
    )(proj, proj, projv)


def _na_bias_table(rpb):
    c = np.arange(GRID_W)
    cs = np.clip(c - NA_KC // 2, 0, GRID_W - NA_KC)
    colmask = (c[None, :] >= cs[:, None]) & (c[None, :] < cs[:, None] + NA_KC)
    rows = jnp.stack([rpb[:, si:si + NA_KR, :] for si in range(NA_KR)], axis=1).astype(F32)
    edge = GRID_W - NA_KC
    padded = jnp.concatenate([jnp.repeat(rows[..., :1], edge, axis=-1), rows,
                              jnp.repeat(rows[..., -1:], edge, axis=-1)], axis=-1)
    b = jnp.stack([padded[..., GRID_W - 1 - qc:2 * GRID_W - 1 - qc] for qc in range(GRID_W)],
                  axis=2)
    b = jnp.where(colmask[None, None, :, None, :], b * LOG2E, NEG_INF)
    b = b.transpose(1, 0, 2, 3, 4)
    return b.reshape(NA_KR, rpb.shape[0] * GRID_W, NA_KR * GRID_W)


def _na_attn_kernel(q_ref, k_ref, v_ref, tb_ref, o_ref, *, rows, rg):
    g = pl.program_id(1)
    lane = lax.broadcasted_iota(jnp.int32, (1, 256), 1)
    nk = NA_KR * GRID_W
    sels = [(lane >= h * NA_DH) & (lane < (h + 1) * NA_DH) for h in range(NA_HEADS)]

    def row(r, _):
        grow = g * rg + r
        rs = jnp.clip(grow - NA_KR // 2, 0, rows - NA_KR)
        si = rs - grow + (NA_KR - 1)
        q = q_ref[r * GRID_W:(r + 1) * GRID_W, :]
        koff = pl.multiple_of(rs * GRID_W, GRID_W)
        kb = k_ref[pl.ds(koff, nk), :]
        vb = v_ref[pl.ds(koff, nk), :]
        q4 = jnp.concatenate([jnp.where(sel, q, jnp.zeros_like(q)) for sel in sels], axis=0)
        s = lax.dot_general(q4, kb, (((1,), (1,)), ((), ())), preferred_element_type=F32)
        s = s + tb_ref[si]
        m = jnp.max(s, -1, keepdims=True)
        p = jnp.exp2(s - m)
        l = jnp.sum(p, -1, keepdims=True)
        o4 = jnp.dot(p.astype(BF16), vb, preferred_element_type=F32) * (1.0 / l)
        out = jnp.zeros((GRID_W, 256), F32)
        for h, sel in enumerate(sels):
            out = jnp.where(sel, o4[h * GRID_W:(h + 1) * GRID_W], out)
        o_ref[r * GRID_W:(r + 1) * GRID_W, :] = out.astype(o_ref.dtype)
        return 0

    for r in range(rg):
        row(r, 0)


def _na_attention(proj, tb, nb, t):
    rows = t // GRID_W
    assert rows >= NA_KR
    rg = 8
    ng = rows // rg
    tq = rg * GRID_W
    kern = functools.partial(_na_attn_kernel, rows=rows, rg=rg)
    return pl.pallas_call(
        kern,
        grid=(nb, ng),
        in_specs=[pl.BlockSpec((None, tq, 256), lambda b, i: (S_DQ, b * ng + i, 0)),
                  pl.BlockSpec((None, t, 256), lambda b, i: (S_DK, b, 0)),
                  pl.BlockSpec((None, t, 256), lambda b, i: (S_DV, b, 0)),
                  pl.BlockSpec(tb.shape, lambda b, i: (0, 0, 0))],
        out_specs=pl.BlockSpec((tq, 256), lambda b, i: (b * ng + i, 0)),
        out_shape=jax.ShapeDtypeStruct((nb * t, 256), BF16),
        compiler_params=_cparams(("arbitrary", "arbitrary")),
        name="na_attn",
    )(proj, proj, proj, tb)


def _layer_norm(y, g, b):
    mu = jnp.mean(y, -1, keepdims=True)
    yc = y - mu
    var = jnp.mean(yc * yc, -1, keepdims=True)
    return yc * lax.rsqrt(var + LN_EPS) * g + b


def _outproj_kernel(oa_ref, ob_ref, oc_ref, od_ref, x_ref, g1_ref, sc2_ref, sh2_ref, w_ref,
                    lg_ref, lb_ref, wr_ref, x1_ref, h2_ref, aff_ref):
    m = jnp.dot(oa_ref[...], w_ref[0:256, :], preferred_element_type=F32)
    m += jnp.dot(ob_ref[...], w_ref[256:512, :], preferred_element_type=F32)
    m += jnp.dot(oc_ref[...], w_ref[512:768, :], preferred_element_type=F32)
    m += jnp.dot(od_ref[...], w_ref[768:1024, :], preferred_element_type=F32)
    y = DEEPNORM_ALPHA * x_ref[...] + (1.0 + g1_ref[...]) * m
    x1 = _layer_norm(y, lg_ref[...], lb_ref[...])
    x1_ref[...] = x1
    h2 = x1 * (1.0 + sc2_ref[...]) + sh2_ref[...]
    h2_ref[...] = h2.astype(BF16)
    h_hi, h_lo = _split_bf16(h2)
    w_hi, w_lo = _split_bf16(wr_ref[...])
    nt = (((1,), (1,)), ((), ()))
    lg = lax.dot_general(w_hi, h_hi, nt, preferred_element_type=F32)
    lg += lax.dot_general(w_hi, h_lo, nt, preferred_element_type=F32)
    lg += lax.dot_general(w_lo, h_hi, nt, preferred_element_type=F32)
    lg = lg - jnp.max(lg, 0, keepdims=True)
    e = jnp.exp(lg)
    aff_ref[...] = e / jnp.sum(e, 0, keepdims=True)


def _outproj(oa, ob, oc, od, x, g1, sc2, sh2, w_out, ln_g, ln_b, w_router_t, nb, t):
    tm = 512
    nt = t // tm
    ntok = nb * t
    tok = lambda w: pl.BlockSpec((tm, w), lambda i: (i, 0))
    per_b = pl.BlockSpec((None, 1, D_MODEL), lambda i: (i // nt, 0, 0))
    full = lambda shape: pl.BlockSpec(shape, lambda i: tuple(0 for _ in shape))
    return pl.pallas_call(
        _outproj_kernel,
        grid=(ntok // tm,),
        in_specs=[tok(256), tok(256), tok(256), tok(256), tok(D_MODEL), per_b, per_b, per_b,
                  full((D_MODEL, D_MODEL)), full((1, D_MODEL)), full((1, D_MODEL)),
                  full((N_EXPERTS, D_MODEL))],
        out_specs=[tok(D_MODEL), tok(D_MODEL),
                   pl.BlockSpec((N_EXPERTS, tm), lambda i: (0, i))],
        out_shape=[jax.ShapeDtypeStruct((ntok, D_MODEL), F32),
                   jax.ShapeDtypeStruct((ntok, D_MODEL), BF16),
                   jax.ShapeDtypeStruct((N_EXPERTS, ntok), F32)],
        compiler_params=_cparams(("arbitrary",)),
        name="out_proj",
    )(oa, ob, oc, od, x, g1, sc2, sh2, w_out, ln_g, ln_b, w_router_t)


EXPERT_TF = 256


def _expert_kernel(x_ref, wg_ref, wu_ref, wd_ref, o_ref, acc_ref):
    x = x_ref[...]
    for c in range(D_FF // EXPERT_TF):
        cols = slice(c * EXPERT_TF, (c + 1) * EXPERT_TF)
        g = jnp.dot(x, wg_ref[:, cols], preferred_element_type=F32)
        u = jnp.dot(x, wu_ref[:, cols], preferred_element_type=F32)
        hmid = (g * (1.0 / (1.0 + jnp.exp(-g))) * u).astype(BF16)
        part = jnp.dot(hmid, wd_ref[cols, :], preferred_element_type=F32)
        if c == 0:
            acc_ref[...] = part
        else:
            acc_ref[...] += part
    o_ref[...] = acc_ref[...].astype(o_ref.dtype)


def _experts(xe, slots, wg, wu, wd):
    ne, _, d = xe.shape
    tm = math.gcd(slots, 1024)
    return pl.pallas_call(
        _expert_kernel,
        grid=(ne, slots // tm),
        in_specs=[pl.BlockSpec((None, tm, d), lambda e, m: (e, m, 0)),
                  pl.BlockSpec((None, d, D_FF), lambda e, m: (e, 0, 0)),
                  pl.BlockSpec((None, d, D_FF), lambda e, m: (e, 0, 0)),
                  pl.BlockSpec((None, D_FF, d), lambda e, m: (e, 0, 0))],
        out_specs=pl.BlockSpec((None, tm, d), lambda e, m: (e, m, 0)),
        out_shape=jax.ShapeDtypeStruct((ne, slots, d), BF16),
        scratch_shapes=[pltpu.VMEM((tm, d), F32)],
        compiler_params=_cparams(("arbitrary", "arbitrary")),
        name="expert_ffn",
    )(xe, wg, wu, wd)


MOE_TM = 512
MOE_WIN = 128
ROW_ALIGN = 16


def _select_kernel(aff_ref, o_ref, *, cap):
    a = aff_ref[...]
    keys = lax.bitcast_convert_type(a, jnp.int32)
    ne, n = a.shape
    capf = float(cap)

    def count(mask):
        return jnp.sum(jnp.where(mask, 1.0, 0.0), axis=1, keepdims=True)

    def key_bit(b, thr):
        cand = thr | lax.shift_left(jnp.int32(1), 30 - b)
        return jnp.where(count(keys >= cand) >= capf, cand, thr)

    thr = lax.fori_loop(0, 31, key_bit, jnp.zeros((ne, 1), jnp.int32))
    above = keys > thr
    need = capf - count(above)
    idx = lax.broadcasted_iota(jnp.int32, (ne, n), 1)
    tie_idx = jnp.where(keys == thr, idx, jnp.int32(2 ** 30))
    nbits = max(1, (n - 1).bit_length())

    def idx_bit(b, j):
        cand = j | lax.shift_left(jnp.int32(1), nbits - 1 - b)
        return jnp.where(count(tie_idx < cand) < need, cand, j)

    j = lax.fori_loop(0, nbits, idx_bit, jnp.zeros((ne, 1), jnp.int32))
    sel = above | (tie_idx <= j)
    o_ref[...] = jnp.where(sel, a, -1.0)


def _select(aff, cap):
    ne, n = aff.shape
    return pl.pallas_call(
        functools.partial(_select_kernel, cap=cap),
        out_shape=jax.ShapeDtypeStruct((ne, n), F32),
        compiler_params=pltpu.CompilerParams(vmem_limit_bytes=VMEM_LIMIT),
        name="ec_select",
    )(aff)


def _routing_tables(gs, group_tokens):
    ne = gs.shape[0]
    a_l, off_l, cnt_l, lim_l = [], [], [], []
    tok0, slot0 = 0, 0
    for n in group_tokens:
        cap = EC_FACTOR * n // N_EXPERTS
        nt = n // MOE_TM
        sel = lax.slice_in_dim(gs, tok0, tok0 + n, axis=1) >= 0
        counts = jnp.sum(sel.reshape(ne, nt, MOE_TM), axis=-1, dtype=jnp.int32)
        s0 = slot0 + jnp.cumsum(counts, axis=1) - counts
        a = (s0 // ROW_ALIGN) * ROW_ALIGN
        a_l.append(a)
        off_l.append(s0 - a)
        cnt_l.append(counts)
        lim_l.append(jnp.full((nt,), slot0 + cap - MOE_WIN, jnp.int32))
        tok0 += n
        slot0 += cap
    a = jnp.concatenate(a_l, axis=1).T
    off = jnp.concatenate(off_l, axis=1).T
    end = off + jnp.concatenate(cnt_l, axis=1).T
    rounds = (end + MOE_WIN - 1) // MOE_WIN
    gp = (end // ROW_ALIGN) * ROW_ALIGN
    return dict(a=a.reshape(-1).astype(jnp.int32), nr=rounds.reshape(-1).astype(jnp.int32),
                nrounds=jnp.max(rounds, axis=1).astype(jnp.int32), lim=jnp.concatenate(lim_l),
                off_col=off.astype(F32)[:, :, None], off_row=off.astype(F32)[:, None, :],
                off16=jnp.repeat(off.astype(F32), ROW_ALIGN, axis=1)[:, :, None],
                gp16=jnp.repeat(gp.astype(F32), ROW_ALIGN, axis=1)[:, :, None],
                total=slot0)


def _dispatch_kernel(a_tab, nrounds, nr_tab, gs_ref, off_ref, off16_ref, gp16_ref, x_ref, u_ref, xe_ref,
                     pos_ref, c_ref, c2_ref, stage_ref, carry_ref, sem, rc_ref):
    j = pl.program_id(0)
    ne, tm = gs_ref.shape
    d = x_ref.shape[1]
    win = MOE_WIN

    @pl.when(j == 0)
    def _():
        carry_ref[...] = jnp.zeros_like(carry_ref)
        rc_ref[0] = 0
        stage_ref[0, 0:win, :] = jnp.zeros((win, d), BF16)
        pad = [pltpu.make_async_copy(stage_ref.at[0, pl.ds(0, win)],
                                     xe_ref.at[e, pl.ds(xe_ref.shape[1] - win, win)], sem.at[0])
               for e in range(ne)]
        for cp in pad:
            cp.start()
        for cp in pad:
            cp.wait()

    sel = gs_ref[...] >= 0.0
    rank = jnp.dot(jnp.where(sel, 1.0, 0.0).astype(BF16), u_ref[...], preferred_element_type=F32)
    pos_ref[...] = jnp.where(sel, rank + off_ref[...], -1.0)

    def wait_round(slot):
        def one(_, c):
            pltpu.make_async_copy(stage_ref.at[slot, pl.ds(0, win)], xe_ref.at[0, pl.ds(0, win)],
                                  sem.at[slot]).wait()
            return c
        lax.fori_loop(0, rc_ref[1 + slot], one, 0)

    def round_body(r, _):
        k = lax.broadcasted_iota(jnp.int32, (win, tm), 0).astype(F32) + (r * win).astype(F32)
        for e in range(ne):
            c_ref[e * win:(e + 1) * win, :] = jnp.where(pos_ref[e:e + 1, :] == k, 1.0, 0.0).astype(BF16)
        slot = rc_ref[0] % 2
        for nb in range(d // MXU_DIM):
            cols = slice(nb * MXU_DIM, (nb + 1) * MXU_DIM)
            stage_ref[slot, :, cols] = jnp.dot(c_ref[...], x_ref[:, cols],
                                               preferred_element_type=F32).astype(BF16)

        @pl.when(r == 0)
        def _():
            k16 = lax.broadcasted_iota(jnp.int32, (ROW_ALIGN, 1), 0).astype(F32)
            for e in range(ne):
                keep = k16 < off16_ref[e * ROW_ALIGN:(e + 1) * ROW_ALIGN, :]
                rows = pl.ds(e * win, ROW_ALIGN)
                stage_ref[slot, rows, :] = jnp.where(
                    keep, carry_ref[e * ROW_ALIGN:(e + 1) * ROW_ALIGN, :], stage_ref[slot, rows, :])

        @pl.when(rc_ref[0] > 0)
        def _():
            wait_round(1 - slot)

        rc_ref[1 + slot] = 0
        for e in range(ne):
            @pl.when(r < nr_tab[j * ne + e])
            def _():
                dst = pl.multiple_of(a_tab[j * ne + e] + r * win, ROW_ALIGN)
                pltpu.make_async_copy(stage_ref.at[slot, pl.ds(e * win, win)],
                                      xe_ref.at[e, pl.ds(dst, win)], sem.at[slot]).start()
                rc_ref[1 + slot] = rc_ref[1 + slot] + 1
        rc_ref[0] = rc_ref[0] + 1
        return 0

    lax.fori_loop(0, nrounds[j], round_body, 0)

    k16 = lax.broadcasted_iota(jnp.int32, (ROW_ALIGN, 1), 0).astype(F32)
    for e in range(ne):
        rows = slice(e * ROW_ALIGN, (e + 1) * ROW_ALIGN)
        c2_ref[rows, :] = jnp.where(pos_ref[e:e + 1, :] == gp16_ref[rows, :] + k16, 1.0, 0.0).astype(BF16)
    kk = jnp.concatenate([k16] * ne, axis=0)
    keep_old = (gp16_ref[...] == 0.0) & (kk < off16_ref[...])
    for nb in range(d // MXU_DIM):
        cols = slice(nb * MXU_DIM, (nb + 1) * MXU_DIM)
        new = jnp.dot(c2_ref[...], x_ref[:, cols], preferred_element_type=F32).astype(BF16)
        carry_ref[:, cols] = jnp.where(keep_old, carry_ref[:, cols], new)

    @pl.when((j == pl.num_programs(0) - 1) & (rc_ref[0] > 0))
    def _():
        wait_round((rc_ref[0] - 1) % 2)


def _dispatch(gs, h2, tabs, u_mat):
    ne, ntok = gs.shape
    d = h2.shape[1]
    nt = ntok // MOE_TM
    rows = tabs['total'] + MOE_WIN
    grid_spec = pltpu.PrefetchScalarGridSpec(
        num_scalar_prefetch=3,
        grid=(nt,),
        in_specs=[pl.BlockSpec((ne, MOE_TM), lambda j, *_: (0, j)),
                  pl.BlockSpec((None, ne, 1), lambda j, *_: (j, 0, 0)),
                  pl.BlockSpec((None, ne * ROW_ALIGN, 1), lambda j, *_: (j, 0, 0)),
                  pl.BlockSpec((None, ne * ROW_ALIGN, 1), lambda j, *_: (j, 0, 0)),
                  pl.BlockSpec((MOE_TM, d), lambda j, *_: (j, 0)),
                  pl.BlockSpec((MOE_TM, MOE_TM), lambda j, *_: (0, 0))],
        out_specs=pl.BlockSpec(memory_space=pl.ANY),
        scratch_shapes=[pltpu.VMEM((ne, MOE_TM), F32),
                        pltpu.VMEM((ne * MOE_WIN, MOE_TM), BF16),
                        pltpu.VMEM((ne * ROW_ALIGN, MOE_TM), BF16),
                        pltpu.VMEM((2, ne * MOE_WIN, d), BF16),
                        pltpu.VMEM((ne * ROW_ALIGN, d), BF16),
                        pltpu.SemaphoreType.DMA((2,)),
                        pltpu.SMEM((3,), jnp.int32)])
    return pl.pallas_call(
        _dispatch_kernel,
        grid_spec=grid_spec,
        out_shape=jax.ShapeDtypeStruct((ne, rows, d), BF16),
        compiler_params=_cparams(("arbitrary",)),
        name="ec_dispatch",
    )(tabs['a'], tabs['nrounds'], tabs['nr'], gs, tabs['off_col'], tabs['off16'], tabs['gp16'], h2,
      u_mat)


def _combine_kernel(a_tab, nrounds, lim_tab, gs_ref, off_ref, l_ref, ye_ref, x1_ref, g2_ref, lg_ref,
                    lb_ref, *refs, split_tiles):
    outs, (p_ref, y_ref, acc_ref, sem) = refs[:-4], refs[-4:]
    j = pl.program_id(0)
    tm, ne = gs_ref.shape
    d = x1_ref.shape[1]
    win = MOE_WIN
    gs = gs_ref[...]
    sel = gs >= 0.0
    rank = jnp.dot(l_ref[...], jnp.where(sel, 1.0, 0.0).astype(BF16), preferred_element_type=F32)
    pos = jnp.where(sel, rank + off_ref[...], -1.0)
    gate = jnp.where(sel, gs, 0.0)
    acc_ref[...] = jnp.zeros_like(acc_ref)
    n_tiles = pl.num_programs(0)
    slot = j % 2

    def window(tile, r, e):
        want = a_tab[tile * ne + e] + r * win
        src = pl.multiple_of(jnp.minimum(want, lim_tab[tile]), ROW_ALIGN)
        return src, want - src

    def start_round(tile, r, sl):
        for e in range(ne):
            src, _ = window(tile, r, e)
            pltpu.make_async_copy(ye_ref.at[e, pl.ds(src, win)], y_ref.at[sl, pl.ds(e * win, win)],
                                  sem.at[sl]).start()

    def wait_round(sl):
        for e in range(ne):
            pltpu.make_async_copy(ye_ref.at[e, pl.ds(0, win)], y_ref.at[sl, pl.ds(e * win, win)],
                                  sem.at[sl]).wait()

    @pl.when((j == 0) & (nrounds[0] > 0))
    def _():
        start_round(0, 0, 0)

    nxt = jnp.minimum(j + 1, n_tiles - 1)

    @pl.when((j + 1 < n_tiles) & (nrounds[nxt] > 0))
    def _():
        start_round(nxt, 0, 1 - slot)

    half = ne // 2

    def accumulate(r):
        base = lax.convert_element_type(r * win, F32)
        k = lax.broadcasted_iota(jnp.int32, (tm, win), 1).astype(F32) + base
        for grp in range(2):
            for e in range(grp * half, (grp + 1) * half):
                pe = pos[:, e:e + 1]
                pe = jnp.where(pe >= base, pe + window(j, r, e)[1].astype(F32), -1.0)
                pcol = jnp.broadcast_to(pe, (tm, win))
                gcol = jnp.broadcast_to(gate[:, e:e + 1], (tm, win))
                p_ref[:, e * win:(e + 1) * win] = jnp.where(pcol == k, gcol, 0.0).astype(BF16)
            rows = slice(grp * half * win, (grp + 1) * half * win)
            for nb in range(d // MXU_DIM):
                cols = slice(nb * MXU_DIM, (nb + 1) * MXU_DIM)
                acc_ref[:, cols] += jnp.dot(p_ref[:, rows], y_ref[slot, rows, cols],
                                            preferred_element_type=F32)

    @pl.when(nrounds[j] > 0)
    def _():
        wait_round(slot)
        accumulate(0)

    def extra_round(r, _):
        start_round(j, r, slot)
        wait_round(slot)
        accumulate(r)
        return 0

    lax.fori_loop(1, nrounds[j], extra_round, 0)
    y = DEEPNORM_ALPHA * x1_ref[...] + (1.0 + g2_ref[...]) * acc_ref[...]
    res = _layer_norm(y, lg_ref[...], lb_ref[...])
    if split_tiles is None:
        outs[0][...] = res
    else:
        @pl.when(j < split_tiles)
        def _():
            outs[0][...] = res

        @pl.when(j >= split_tiles)
        def _():
            outs[1][...] = res


def _combine_postnorm(gs_tok, ye, tabs, l_mat, x1, g2, ln_g, ln_b, t, split_tiles=None):
    ntok, ne = gs_tok.shape
    d = x1.shape[1]
    nt = ntok // MOE_TM
    tiles_per_seq = t // MOE_TM
    if split_tiles is None:
        out_specs = pl.BlockSpec((MOE_TM, d), lambda j, *_: (j, 0))
        out_shape = jax.ShapeDtypeStruct((ntok, d), F32)
    else:
        out_specs = [pl.BlockSpec((MOE_TM, d), lambda j, *_: (jnp.minimum(j, split_tiles - 1), 0)),
                     pl.BlockSpec((MOE_TM, d), lambda j, *_: (jnp.maximum(j - split_tiles, 0), 0))]
        out_shape = [jax.ShapeDtypeStruct((split_tiles * MOE_TM, d), F32),
                     jax.ShapeDtypeStruct((ntok - split_tiles * MOE_TM, d), F32)]
    grid_spec = pltpu.PrefetchScalarGridSpec(
        num_scalar_prefetch=3,
        grid=(nt,),
        in_specs=[pl.BlockSpec((MOE_TM, ne), lambda j, *_: (j, 0)),
                  pl.BlockSpec((None, 1, ne), lambda j, *_: (j, 0, 0)),
                  pl.BlockSpec((MOE_TM, MOE_TM), lambda j, *_: (0, 0)),
                  pl.BlockSpec(memory_space=pl.ANY),
                  pl.BlockSpec((MOE_TM, d), lambda j, *_: (j, 0)),
                  pl.BlockSpec((None, 1, d), lambda j, *_: (j // tiles_per_seq, 0, 0)),
                  pl.BlockSpec((1, d), lambda j, *_: (0, 0)),
                  pl.BlockSpec((1, d), lambda j, *_: (0, 0))],
        out_specs=out_specs,
        scratch_shapes=[pltpu.VMEM((MOE_TM, ne * MOE_WIN), BF16),
                        pltpu.VMEM((2, ne * MOE_WIN, d), BF16),
                        pltpu.VMEM((MOE_TM, d), F32),
                        pltpu.SemaphoreType.DMA((2,))])
    return pl.pallas_call(
        functools.partial(_combine_kernel, split_tiles=split_tiles),
        grid_spec=grid_spec,
        out_shape=out_shape,
        compiler_params=_cparams(("arbitrary",)),
        name="ec_combine",
    )(tabs['a'], tabs['nrounds'], tabs['lim'], gs_tok, tabs['off_row'], l_mat, ye, x1, g2, ln_g, ln_b)


def _prep_w_in(w_in_l):
    sizes = (256, 256, 256, MLA_Q_RANK, MLA_KV_RANK, MLA_ROPE, 256, 256, 256, 256, 256, 256)
    offs = np.concatenate([[0], np.cumsum(sizes)])
    part = [w_in_l[:, offs[i]:offs[i + 1]] for i in range(len(sizes))]
    a_q, a_k, a_v, b_cq, b_ckv, b_kr, c_q, c_k, c_v, d_q, d_k, d_v = part
    d = w_in_l.shape[0]
    zeros = lambda n: jnp.zeros((d, n), w_in_l.dtype)
    def per_head(v):
        out = []
        for h in range(4):
            out += [v[:, h * 64:(h + 1) * 64], zeros(LANES - 64)]
        return out

    cols = ([a_q, a_k] + per_head(a_v) + [c_q, c_k] + per_head(c_v)
            + [d_q, d_k, d_v, b_cq, zeros(256 - MLA_Q_RANK), b_ckv, b_kr, b_kr,
               zeros(128 - 2 * MLA_ROPE)])
    return jnp.concatenate(cols, axis=1).astype(BF16)


def _prep_w_uq(w_uq_l):
    hd = MLA_NOPE + MLA_ROPE
    nope = [w_uq_l[:, h * hd:h * hd + MLA_NOPE] for h in range(MLA_HEADS)]
    rope = [w_uq_l[:, h * hd + MLA_NOPE:(h + 1) * hd] for h in range(MLA_HEADS)]
    z = jnp.zeros((w_uq_l.shape[0], 256 - 2 * hd), w_uq_l.dtype)
    cols = []
    for p in range(2):
        cols += [nope[2 * p], nope[2 * p + 1], rope[2 * p], rope[2 * p + 1], z]
    w = jnp.concatenate(cols, axis=1)
    w = jnp.concatenate([w, jnp.zeros((256 - MLA_Q_RANK, w.shape[1]), w.dtype)], axis=0)
    return w.astype(BF16)


def _prep_w_ukv(w_ukv_l):
    hd = MLA_NOPE + MLA_DV
    kn = [w_ukv_l[:, h * hd:h * hd + MLA_NOPE] for h in range(MLA_HEADS)]
    z = jnp.zeros((w_ukv_l.shape[0], LANES - MLA_DV), w_ukv_l.dtype)
    vv = []
    for h in range(MLA_HEADS):
        vv += [w_ukv_l[:, h * hd + MLA_NOPE:(h + 1) * hd], z]
    return jnp.concatenate(kn + vv, axis=1).astype(BF16)


def _select_groups(aff_t, group_tokens):
    parts, off = [], 0
    for n in group_tokens:
        parts.append(_select(lax.slice_in_dim(aff_t, off, off + n, axis=1),
                             EC_FACTOR * n // N_EXPERTS))
        off += n
    return jnp.concatenate(parts, axis=1)


def _trunk(x, c, group_tokens, nb, t, p):
    ntok = nb * t
    tabs = (_rope_tables(t, 256, DA_DQK, DA_DQK // ROPE_FRACTION, ROPE_THETA)
            + _rope_tables(t, 256, DIL_DH, DIL_DH // ROPE_FRACTION, ROPE_THETA)
            + _rope_tables(t, 128, MLA_ROPE, MLA_ROPE, MLA_ROPE_THETA))
    gmat = jnp.asarray(np.kron(np.eye(4), np.full((64, 64), 1.0 / 64)), BF16)
    ti = jnp.arange(MOE_TM)
    u_mat = (ti[:, None] < ti[None, :]).astype(BF16)
    l_mat = (ti[None, :] < ti[:, None]).astype(BF16)
    for l in range(DEPTH):
        mod = _modulation(c, p['w_ada'][l], p['b_ada'][l])
        sh1, sc1, g1, sh2, sc2, g2 = [m.reshape(nb, 1, D_MODEL) for m in jnp.split(mod, 6, axis=-1)]
        gq = jnp.concatenate([p['q_norm_g'][l], jnp.zeros((256 - MLA_Q_RANK,), F32)]).reshape(1, 256)
        gkv = p['kv_norm_g'][l].reshape(1, 128)
        proj, projv = _inproj(x, sc1, sh1, _prep_w_in(p['w_in'][l]), _prep_w_uq(p['w_uq'][l]),
                              _prep_w_ukv(p['w_ukv'][l]), gq, gkv, tabs, nb, t)
        lam_init = 0.8 - 0.6 * math.exp(-0.3 * l)
        lam = (jnp.exp(jnp.sum(p['da_lq1'][l] * p['da_lk1'][l]))
               - jnp.exp(jnp.sum(p['da_lq2'][l] * p['da_lk2'][l])) + lam_init).reshape(1)
        g_sub = jnp.tile(p['da_subln_g'][l], DA_HEADS).reshape(1, 256)
        oa = _diff_attention(proj, projv, lam, g_sub, gmat, nb, t, 1.0 - lam_init)
        ob = _mla_attention(proj, projv, nb, t)
        oc = _dil_attention(proj, projv, nb, t)
        od = _na_attention(proj, _na_bias_table(p['na_rpb'][l]), nb, t)
        x1, h2, aff_t = _outproj(oa, ob, oc, od, x, g1, sc2, sh2, p['w_out'][l].astype(BF16),
                                 p['ln1_g'][l].reshape(1, -1), p['ln1_b'][l].reshape(1, -1),
                                 p['w_router'][l].T, nb, t)
        wg = p['w_e_gate'][l].astype(BF16)
        wu = p['w_e_up'][l].astype(BF16)
        wd = p['w_e_down'][l].astype(BF16)
        gs = _select_groups(aff_t, group_tokens)
        rt = _routing_tables(gs, group_tokens)
        xe = _dispatch(gs, h2, rt, u_mat)
        ye = _experts(xe, rt['total'], wg, wu, wd)
        split = group_tokens[0] // MOE_TM if l == DEPTH - 1 else None
        x = _combine_postnorm(gs.T, ye, rt, l_mat, x1, g2, p['ln2_g'][l].reshape(1, -1),
                              p['ln2_b'][l].reshape(1, -1), t, split_tiles=split)
    return x


def kernel(x_prompt, x_sample, c_prompt, c_sample, w_in, w_uq, w_ukv, q_norm_g, kv_norm_g, da_lq1,
           da_lk1, da_lq2, da_lk2, da_subln_g, na_rpb, w_out, w_ada, b_ada, ln1_g, ln1_b, ln2_g,
           ln2_b, w_router, w_e_gate, w_e_up, w_e_down):
    p = dict(w_in=w_in, w_uq=w_uq, w_ukv=w_ukv, q_norm_g=q_norm_g, kv_norm_g=kv_norm_g,
             da_lq1=da_lq1, da_lk1=da_lk1, da_lq2=da_lq2, da_lk2=da_lk2, da_subln_g=da_subln_g,
             na_rpb=na_rpb, w_out=w_out, w_ada=w_ada, b_ada=b_ada, ln1_g=ln1_g, ln1_b=ln1_b,
             ln2_g=ln2_g, ln2_b=ln2_b, w_router=w_router, w_e_gate=w_e_gate, w_e_up=w_e_up,
             w_e_down=w_e_down)
    bp, t, d = x_prompt.shape
    bs = x_sample.shape[0]
    assert x_sample.shape[1] == t
    nb = bp + bs
    x = jnp.concatenate([x_prompt.reshape(bp * t, d), x_sample.reshape(bs * t, d)], axis=0)
    c = jnp.concatenate([c_prompt, c_sample], axis=0)
    y_prompt, y_sample = _trunk(x, c, (bp * t, bs * t), nb, t, p)
    return y_prompt.reshape(bp, t, d), y_sample.reshape(bs, t, d)
```
